```python
import math
import jax
import jax.numpy as jnp
from jax import lax
import numpy as np

D_MODEL = 1024
BATCH = 16
SEQ = 2048
DEPTH = 2
DEC_BATCH = 32
DEC_SEQ = 64
PAST_LEN = 4096

CHUNK = 64
N_EVEN = (DEPTH + 1) // 2
N_ODD = DEPTH // 2
HEAD_DIM = 64
Q_BLOCK = 128
LN_EPS = 1e-5
ALPHA = (2 * DEPTH) ** 0.25
BETA = (8 * DEPTH) ** -0.25
D_FF = 4 * D_MODEL
A_HEADS = 8
A_KV_HEADS = 2
A_GROUP = A_HEADS // A_KV_HEADS
WINDOW = 128
WIN_CHUNKS = WINDOW // CHUNK
B_HEADS = 8
B_WIDTH = B_HEADS * HEAD_DIM
W_LORA = 64
A_LORA = 64
G_LORA = 128
B_COLS = 3 * B_WIDTH + W_LORA + A_LORA + G_LORA
RWKV_GN_EPS = 64e-5
C_HEADS = 4
C_VDIM = 2 * HEAD_DIM
C_WIDTH = C_HEADS * C_VDIM
D_HEADS = 8
D_KV_HEADS = 2
D_GROUP = D_HEADS // D_KV_HEADS
IDX_HEADS = 8
IDX_DIM = 64
TOPK_MAX = 256

EVEN_SPLIT = [A_HEADS * HEAD_DIM, A_KV_HEADS * HEAD_DIM, A_KV_HEADS * HEAD_DIM, B_COLS]
ODD_SPLIT = [C_HEADS * 2 * HEAD_DIM, C_HEADS * 2 * HEAD_DIM, C_WIDTH,
             D_HEADS * HEAD_DIM, D_KV_HEADS * HEAD_DIM, D_KV_HEADS * HEAD_DIM,
             IDX_HEADS * IDX_DIM, IDX_DIM, IDX_HEADS]
COLS_E = sum(EVEN_SPLIT)
COLS_O = sum(ODD_SPLIT)
D_MIX_E = A_HEADS * HEAD_DIM + B_WIDTH
D_MIX_O = C_WIDTH + D_HEADS * HEAD_DIM

kernel_name = 'chunk_streaming_hybrid_encoder_step'


def split_cols(p, sizes):
    offs = [int(o) for o in np.cumsum(sizes)[:-1]]
    return jnp.split(p, offs, axis=-1)


def alibi_slopes(n):
    return jnp.asarray(2.0 ** (-8.0 * np.arange(1, n + 1) / n), dtype=jnp.float32)


def layer_norm(x, g, b):
    xf = x.astype(jnp.float32)
    mu = jnp.mean(xf, -1, keepdims=True)
    var = jnp.mean(jnp.square(xf - mu), -1, keepdims=True)
    return ((xf - mu) * lax.rsqrt(var + LN_EPS) * g + b).astype(x.dtype)


def softmax_with_sink(logits, sink):
    m = jnp.maximum(jnp.max(logits, -1, keepdims=True), sink)
    p = jnp.exp(logits - m)
    return p / (jnp.sum(p, -1, keepdims=True) + jnp.exp(sink - m))


def swa_sink_prompt(q, k, v, sink):
    B, S = q.shape[:2]
    nc = S // CHUNK
    pad = WIN_CHUNKS * CHUNK
    span = (WIN_CHUNKS + 1) * CHUNK

    def band(t):
        tp = jnp.pad(t, ((0, 0), (pad, 0), (0, 0), (0, 0))).reshape(B, nc + WIN_CHUNKS, CHUNK, A_KV_HEADS, HEAD_DIM)
        return jnp.concatenate([tp[:, j:j + nc] for j in range(WIN_CHUNKS + 1)], axis=2)

    kb, vb = band(k), band(v)
    qb = q.reshape(B, nc, CHUNK, A_KV_HEADS, A_GROUP, HEAD_DIM)
    s = jnp.einsum('bnqkgd,bnskd->bnkgqs', qb, kb).astype(jnp.float32) * HEAD_DIM ** -0.5
    qpos = jnp.arange(S).reshape(nc, CHUNK)
    kpos = jnp.arange(nc)[:, None] * CHUNK - pad + jnp.arange(span)[None, :]
    dist = jnp.abs(qpos[:, :, None] - kpos[:, None, :]).astype(jnp.float32)
    slopes = alibi_slopes(A_HEADS).reshape(A_KV_HEADS, A_GROUP)
    logits = s - slopes[None, None, :, :, None, None] * dist[None, :, None, None]
    logits = jnp.where((kpos >= 0)[None, :, None, None, None, :], logits, -jnp.inf)
    sk = sink.astype(jnp.float32).reshape(A_KV_HEADS, A_GROUP)[None, None, :, :, None, None]
    p = softmax_with_sink(logits, sk)
    o = jnp.einsum('bnkgqs,bnskd->bnqkgd', p.astype(v.dtype), vb)
    return o.reshape(B, S, A_HEADS * HEAD_DIM)


def swa_sink_step(q, k_all, v_all, sink):
    B, T = q.shape[:2]
    L = k_all.shape[1]
    qg = q.reshape(B, T, A_KV_HEADS, A_GROUP, HEAD_DIM)
    s = jnp.einsum('bqkgd,bskd->bkgqs', qg, k_all).astype(jnp.float32) * HEAD_DIM ** -0.5
    qpos = PAST_LEN + jnp.arange(T)
    kpos = PAST_LEN - WINDOW + jnp.arange(L)
    dist = jnp.abs(qpos[:, None] - kpos[None, :]).astype(jnp.float32)
    slopes = alibi_slopes(A_HEADS).reshape(A_KV_HEADS, A_GROUP)
    logits = s - slopes[None, :, :, None, None] * dist
    sk = sink.astype(jnp.float32).reshape(A_KV_HEADS, A_GROUP)[None, :, :, None, None]
    p = softmax_with_sink(logits, sk)
    o = jnp.einsum('bkgqs,bskd->bqkgd', p.astype(v_all.dtype), v_all)
    return o.reshape(B, T, A_HEADS * HEAD_DIM)


def rwkv7(pb, shift_prev, S0, mu, w0, w_up, a0, a_up, g_up, k_k, k_a, r_k, lnx_g, lnx_b):
    B, T = pb.shape[:2]
    prev = jnp.concatenate([shift_prev.astype(pb.dtype), pb[:, :-1]], axis=1)
    xm = (pb + (prev - pb) * mu).astype(jnp.float32)
    r, k, v, wl, al, gl = split_cols(xm, [B_WIDTH, B_WIDTH, B_WIDTH, W_LORA, A_LORA, G_LORA])
    w_log = -jax.nn.softplus(-(w0 + jnp.tanh(wl) @ w_up)) - 0.5
    decay = jnp.exp(-jnp.exp(w_log))
    a = jax.nn.sigmoid(a0 + al @ a_up)
    g = jax.nn.sigmoid(gl) @ g_up
    heads = lambda t: t.reshape(B, T, B_HEADS, HEAD_DIM)
    kk = heads(k * k_k)
    kk = kk * lax.rsqrt(jnp.sum(kk * kk, -1, keepdims=True) + 1e-12)
    k = k * (1.0 + (a - 1.0) * k_a)
    r, k, v, decay, a = heads(r), heads(k), heads(v), heads(decay), heads(a)

    def step(S, inp):
        r_t, w_t, k_t, v_t, kk_t, a_t = inp
        sa = jnp.einsum('bhij,bhj->bhi', S, -kk_t)
        S = S * w_t[:, :, None, :] + sa[..., None] * (kk_t * a_t)[:, :, None, :] + v_t[..., None] * k_t[:, :, None, :]
        return S, jnp.einsum('bhij,bhj->bhi', S, r_t)

    seq = tuple(jnp.moveaxis(t, 1, 0) for t in (r, decay, k, v, kk, a))
    S_T, y = lax.scan(step, S0.astype(jnp.float32), seq)
    y = jnp.moveaxis(y, 0, 1)
    ym = jnp.mean(y, -1, keepdims=True)
    yv = jnp.mean(jnp.square(y - ym), -1, keepdims=True)
    y = ((y - ym) * lax.rsqrt(yv + RWKV_GN_EPS)).reshape(B, T, B_WIDTH) * lnx_g + lnx_b
    bonus = jnp.sum(r * k * r_k, -1, keepdims=True) * v
    y = (y + bonus.reshape(B, T, B_WIDTH)) * g
    return y.astype(pb.dtype), pb[:, -1:], S_T


def diff_attention(q, k, v, q_pos, lam, subln_g, lam_init):
    B, T = q.shape[:2]
    L = k.shape[1]
    qbs = min(Q_BLOCK, T)
    nb = T // qbs
    k_pos = jnp.arange(L)
    slopes = alibi_slopes(C_HEADS)

    def block(args):
        qb, pb = args
        s = jnp.einsum('bqhmd,bshmd->bhmqs', qb, k).astype(jnp.float32) * HEAD_DIM ** -0.5
        dist = jnp.abs(pb[:, None] - k_pos[None, :]).astype(jnp.float32)
        s = s - slopes[None, :, None, None, None] * dist
        allowed = (k_pos[None, :] // CHUNK) <= (pb[:, None] // CHUNK)
        p = jax.nn.softmax(jnp.where(allowed, s, -jnp.inf), axis=-1)
        attn = p[:, :, 0] - lam * p[:, :, 1]
        return jnp.einsum('bhqs,bshe->bqhe', attn.astype(v.dtype), v)

    qs = jnp.moveaxis(q.reshape(B, nb, qbs, C_HEADS, 2, HEAD_DIM), 1, 0)
    o = lax.map(block, (qs, q_pos.reshape(nb, qbs)))
    o = jnp.moveaxis(o, 0, 1).reshape(B, T, C_HEADS, C_VDIM).astype(jnp.float32)
    o = o * lax.rsqrt(jnp.mean(o * o, -1, keepdims=True) + LN_EPS) * subln_g * (1.0 - lam_init)
    return o.reshape(B, T, C_WIDTH).astype(q.dtype)


def dsa_attention(q, k, v, q_idx, k_idx, w_idx, q_pos):
    B, T = q.shape[:2]
    L = k.shape[1]
    n_sel = min(TOPK_MAX, L // 4)
    qbs = min(Q_BLOCK, T)
    nb = T // qbs
    k_pos = jnp.arange(L)
    slopes = alibi_slopes(D_HEADS).reshape(D_KV_HEADS, D_GROUP)
    gather = jax.vmap(lambda t, i: t[i])

    def block(args):
        qb, qib, wb, pb = args
        score = jnp.einsum('bqhd,bsd->bqhs', qib, k_idx).astype(jnp.float32) * IDX_DIM ** -0.5
        index = jnp.einsum('bqhs,bqh->bqs', jax.nn.relu(score), wb.astype(jnp.float32)) * IDX_HEADS ** -0.5
        allowed = (k_pos[None, :] // CHUNK) <= (pb[:, None] // CHUNK)
        index = jnp.where(allowed[None], index, -jnp.inf)
        top_val, top_idx = lax.top_k(index, n_sel)
        ks = gather(k, top_idx)
        vs = gather(v, top_idx)
        qg = qb.reshape(B, qbs, D_KV_HEADS, D_GROUP, HEAD_DIM)
        s = jnp.einsum('bqkgd,bqskd->bqkgs', qg, ks).astype(jnp.float32) * HEAD_DIM ** -0.5
        dist = jnp.abs(pb[None, :, None] - top_idx).astype(jnp.float32)
        s = s - slopes[None, None, :, :, None] * dist[:, :, None, None, :]
        s = jnp.where(jnp.isfinite(top_val)[:, :, None, None, :], s, -jnp.inf)
        p = jax.nn.softmax(s, axis=-1)
        o = jnp.einsum('bqkgs,bqskd->bqkgd', p.astype(v.dtype), vs)
        return o.reshape(B, qbs, D_HEADS * HEAD_DIM)

    blk = lambda t: jnp.moveaxis(t.reshape((B, nb, qbs) + t.shape[2:]), 1, 0)
    o = lax.map(block, (blk(q), blk(q_idx), blk(w_idx), q_pos.reshape(nb, qbs)))
    return jnp.moveaxis(o, 0, 1).reshape(B, T, D_HEADS * HEAD_DIM)


def even_mix(x, w_in, w_out, sink, rw, state=None):
    B, T = x.shape[:2]
    p = x @ w_in
    qa, ka, va, pb = split_cols(p, EVEN_SPLIT)
    qa = qa.reshape(B, T, A_HEADS, HEAD_DIM)
    ka = ka.reshape(B, T, A_KV_HEADS, HEAD_DIM)
    va = va.reshape(B, T, A_KV_HEADS, HEAD_DIM)
    if state is None:
        oa = swa_sink_prompt(qa, ka, va, sink)
        k_buf, v_buf = ka, va
        shift = jnp.zeros((B, 1, B_COLS), p.dtype)
        wkv = jnp.zeros((B, B_HEADS, HEAD_DIM, HEAD_DIM), jnp.float32)
    else:
        cache_k, cache_v, shift, wkv = state
        k_buf = jnp.concatenate([cache_k.astype(ka.dtype), ka], axis=1)
        v_buf = jnp.concatenate([cache_v.astype(va.dtype), va], axis=1)
        oa = swa_sink_step(qa, k_buf, v_buf, sink)
    ob, new_shift, new_wkv = rwkv7(pb, shift, wkv, *rw)
    mix = jnp.concatenate([oa, ob], axis=-1) @ w_out
    return mix, (k_buf[:, -WINDOW:], v_buf[:, -WINDOW:], new_shift, new_wkv)


def odd_mix(x, q_pos, w_in, w_out, lam_q1, lam_k1, lam_q2, lam_k2, subln_g, lam_init, state=None):
    B, T = x.shape[:2]
    p = x @ w_in
    qc, kc, vc, qd, kd, vd, qi, ki, wi = split_cols(p, ODD_SPLIT)
    qc = qc.reshape(B, T, C_HEADS, 2, HEAD_DIM)
    kc = kc.reshape(B, T, C_HEADS, 2, HEAD_DIM)
    vc = vc.reshape(B, T, C_HEADS, C_VDIM)
    qd = qd.reshape(B, T, D_HEADS, HEAD_DIM)
    kd = kd.reshape(B, T, D_KV_HEADS, HEAD_DIM)
    vd = vd.reshape(B, T, D_KV_HEADS, HEAD_DIM)
    qi = qi.reshape(B, T, IDX_HEADS, IDX_DIM)
    new_rows = (kc, vc, kd, vd, ki)
    if state is None:
        kc_all, vc_all, kd_all, vd_all, ki_all = new_rows
    else:
        kc_all, vc_all, kd_all, vd_all, ki_all = [
            jnp.concatenate([c.astype(n.dtype), n], axis=1) for c, n in zip(state, new_rows)]
    lam = (jnp.exp(jnp.sum(lam_q1.astype(jnp.float32) * lam_k1.astype(jnp.float32)))
           - jnp.exp(jnp.sum(lam_q2.astype(jnp.float32) * lam_k2.astype(jnp.float32))) + lam_init)
    oc = diff_attention(qc, kc_all, vc_all, q_pos, lam, subln_g, lam_init)
    od = dsa_attention(qd, kd_all, vd_all, qi, ki_all, wi, q_pos)
    mix = jnp.concatenate([oc, od], axis=-1) @ w_out
    return mix, new_rows


def residual_block(x, mix, g1, b1, g2, b2, w_up, w_down):
    x = layer_norm(ALPHA * x + mix, g1, b1)
    h = jnp.square(jax.nn.relu(x @ w_up)) @ w_down
    return layer_norm(ALPHA * x + h, g2, b2)


def stack_states(states):
    return [jnp.stack(z, axis=0) for z in zip(*states)]


def setup_inputs(seed: int = 0) -> dict:
    key = jax.random.key(seed)
    ks = iter(jax.random.split(key, 48))

    def nrm(shape, scale=1.0):
        return jax.random.normal(next(ks), shape, jnp.float32) * scale

    def unif(shape, lo, hi):
        return jax.random.uniform(next(ks), shape, jnp.float32, lo, hi)

    D = D_MODEL
    return {
        'x_prompt': nrm((BATCH, SEQ, D)),
        'x_sample': nrm((DEC_BATCH, DEC_SEQ, D)),
        'cache_a_k': nrm((N_EVEN, DEC_BATCH, WINDOW, A_KV_HEADS, HEAD_DIM)),
        'cache_a_v': nrm((N_EVEN, DEC_BATCH, WINDOW, A_KV_HEADS, HEAD_DIM)),
        'state_b_shift': nrm((N_EVEN, DEC_BATCH, 1, B_COLS)),
        'state_b_wkv': nrm((N_EVEN, DEC_BATCH, B_HEADS, HEAD_DIM, HEAD_DIM), 0.3),
        'cache_c_k': nrm((N_ODD, DEC_BATCH, PAST_LEN, C_HEADS, 2, HEAD_DIM)),
        'cache_c_v': nrm((N_ODD, DEC_BATCH, PAST_LEN, C_HEADS, C_VDIM)),
        'cache_d_k': nrm((N_ODD, DEC_BATCH, PAST_LEN, D_KV_HEADS, HEAD_DIM)),
        'cache_d_v': nrm((N_ODD, DEC_BATCH, PAST_LEN, D_KV_HEADS, HEAD_DIM)),
        'cache_d_kidx': nrm((N_ODD, DEC_BATCH, PAST_LEN, IDX_DIM)),
        'w_in_even': nrm((N_EVEN, D, COLS_E), D ** -0.5),
        'sink_a': nrm((N_EVEN, A_HEADS), 0.5),
        'mu_b': unif((N_EVEN, B_COLS), 0.0, 1.0),
        'w0_b': unif((N_EVEN, B_WIDTH), -4.0, 1.0),
        'w_up_b': nrm((N_EVEN, W_LORA, B_WIDTH), 0.5 * W_LORA ** -0.5),
        'a0_b': nrm((N_EVEN, B_WIDTH), 0.1),
        'a_up_b': nrm((N_EVEN, A_LORA, B_WIDTH), 0.5 * A_LORA ** -0.5),
        'g_up_b': nrm((N_EVEN, G_LORA, B_WIDTH), G_LORA ** -0.5),
        'k_k_b': 0.85 + nrm((N_EVEN, B_WIDTH), 0.05),
        'k_a_b': 1.0 + nrm((N_EVEN, B_WIDTH), 0.05),
        'r_k_b': nrm((N_EVEN, B_HEADS, HEAD_DIM), 0.1),
        'lnx_g_b': 1.0 + nrm((N_EVEN, B_WIDTH), 0.05),
        'lnx_b_b': nrm((N_EVEN, B_WIDTH), 0.02),
        'w_out_even': nrm((N_EVEN, D_MIX_E, D), BETA * D_MIX_E ** -0.5),
        'w_in_odd': nrm((N_ODD, D, COLS_O), D ** -0.5),
        'lam_q1_c': nrm((N_ODD, HEAD_DIM), 0.1),
        'lam_k1_c': nrm((N_ODD, HEAD_DIM), 0.1),
        'lam_q2_c': nrm((N_ODD, HEAD_DIM), 0.1),
        'lam_k2_c': nrm((N_ODD, HEAD_DIM), 0.1),
        'subln_g_c': 1.0 + nrm((N_ODD, C_VDIM), 0.05),
        'w_out_odd': nrm((N_ODD, D_MIX_O, D), BETA * D_MIX_O ** -0.5),
        'ln_mix_g': 1.0 + nrm((DEPTH, D), 0.05),
        'ln_mix_b': nrm((DEPTH, D), 0.02),
        'ln_ffn_g': 1.0 + nrm((DEPTH, D), 0.05),
        'ln_ffn_b': nrm((DEPTH, D), 0.02),
        'w_ff_up': nrm((DEPTH, D, D_FF), D ** -0.5),
        'w_ff_down': nrm((DEPTH, D_FF, D), BETA * D_FF ** -0.5),
    }


def reference(x_prompt, x_sample, cache_a_k, cache_a_v, state_b_shift, state_b_wkv,
              cache_c_k, cache_c_v, cache_d_k, cache_d_v, cache_d_kidx,
              w_in_even, sink_a, mu_b, w0_b, w_up_b, a0_b, a_up_b, g_up_b, k_k_b, k_a_b,
              r_k_b, lnx_g_b, lnx_b_b, w_out_even,
              w_in_odd, lam_q1_c, lam_k1_c, lam_q2_c, lam_k2_c, subln_g_c, w_out_odd,
              ln_mix_g, ln_mix_b, ln_ffn_g, ln_ffn_b, w_ff_up, w_ff_down):
    xp, xs = x_prompt, x_sample
    pos_p = jnp.arange(xp.shape[1])
    pos_s = PAST_LEN + jnp.arange(xs.shape[1])
    even_p, even_s, odd_p, odd_s = [], [], [], []
    for layer in range(DEPTH):
        i = layer // 2
        if layer % 2 == 0:
            rw = (mu_b[i], w0_b[i], w_up_b[i], a0_b[i], a_up_b[i], g_up_b[i],
                  k_k_b[i], k_a_b[i], r_k_b[i], lnx_g_b[i], lnx_b_b[i])
            mp, st_p = even_mix(xp, w_in_even[i], w_out_even[i], sink_a[i], rw)
            ms, st_s = even_mix(xs, w_in_even[i], w_out_even[i], sink_a[i], rw,
                                (cache_a_k[i], cache_a_v[i], state_b_shift[i], state_b_wkv[i]))
            even_p.append(st_p)
            even_s.append(st_s)
        else:
            lam_init = 0.8 - 0.6 * math.exp(-0.3 * layer)
            cp = (w_in_odd[i], w_out_odd[i], lam_q1_c[i], lam_k1_c[i], lam_q2_c[i], lam_k2_c[i],
                  subln_g_c[i], lam_init)
            mp, st_p = odd_mix(xp, pos_p, *cp)
            ms, st_s = odd_mix(xs, pos_s, *cp, state=(cache_c_k[i], cache_c_v[i], cache_d_k[i],
                                                      cache_d_v[i], cache_d_kidx[i]))
            odd_p.append(st_p)
            odd_s.append(st_s)
        ffn = (ln_mix_g[layer], ln_mix_b[layer], ln_ffn_g[layer], ln_ffn_b[layer],
               w_ff_up[layer], w_ff_down[layer])
        xp = residual_block(xp, mp, *ffn)
        xs = residual_block(xs, ms, *ffn)
    a_k_p, a_v_p, b_shift_p, b_wkv_p = stack_states(even_p)
    a_k_s, a_v_s, b_shift_s, b_wkv_s = stack_states(even_s)
    c_k_p, c_v_p, d_k_p, d_v_p, d_kidx_p = stack_states(odd_p)
    c_k_s, c_v_s, d_k_s, d_v_s, d_kidx_s = stack_states(odd_s)
    return (xp, xs, a_k_p, a_v_p, b_shift_p, b_wkv_p, c_k_p, c_v_p, d_k_p, d_v_p, d_kidx_p,
            a_k_s, a_v_s, b_shift_s, b_wkv_s, c_k_s, c_v_s, d_k_s, d_v_s, d_kidx_s)
```

```python
import functools
import math

import numpy as np
import jax
import jax.numpy as jnp
from jax import lax
from jax.experimental import pallas as pl
from jax.experimental.pallas import tpu as pltpu

F32 = jnp.float32
BF16 = jnp.bfloat16
I32 = jnp.int32

LANES = 128
HEAD_DIM = 64
CHUNK = 64
LN_EPS = 1e-5
RWKV_GN_EPS = 64e-5
WINDOW = 128
TOPK_MAX = 256
NEG_INF = float("-inf")
INT_MIN = -(2 ** 31)
VMEM_LIMIT = 56 * 1024 * 1024


def _cparams(sem):
    return pltpu.CompilerParams(dimension_semantics=sem, vmem_limit_bytes=VMEM_LIMIT)


def _alibi(n):
    return [float(2.0 ** (-8.0 * (i + 1) / n)) for i in range(n)]


def _dot(a, b):
    return jnp.dot(a, b, preferred_element_type=F32)


def _dot_nt(a, b):
    return lax.dot_general(a, b, (((1,), (1,)), ((), ())), preferred_element_type=F32)


def _lane_left(shape):
    return lax.broadcasted_iota(I32, shape, len(shape) - 1) < HEAD_DIM


def _segsum64(x):
    left = _lane_left(x.shape)
    sl = jnp.sum(jnp.where(left, x, 0.0), axis=-1, keepdims=True)
    sr = jnp.sum(jnp.where(left, 0.0, x), axis=-1, keepdims=True)
    return jnp.where(left, sl, sr)


def _proj_kernel(x_ref, w_ref, *out_refs, groups):
    xb = x_ref[...].astype(BF16)
    for off, width, outs in groups:
        r = _dot(xb, w_ref[:, off:off + width])
        for idx, ow, scale in outs:
            v = r if ow == width else r[:, :ow]
            if scale != 1.0:
                v = v * scale
            out_refs[idx][...] = v.astype(out_refs[idx].dtype)


def _project(x, w_bf, groups, out_defs, tm):
    n, d = x.shape
    cols = w_bf.shape[1]
    out_shape = [jax.ShapeDtypeStruct((n, ow), dt) for ow, dt in out_defs]
    out_specs = [pl.BlockSpec((tm, ow), lambda i: (i, 0)) for ow, _ in out_defs]
    return pl.pallas_call(
        functools.partial(_proj_kernel, groups=groups),
        grid=(n // tm,),
        in_specs=[pl.BlockSpec((tm, d), lambda i: (i, 0)),
                  pl.BlockSpec((d, cols), lambda i: (0, 0))],
        out_specs=out_specs,
        out_shape=out_shape,
        compiler_params=_cparams(("parallel",)),
        name="in_proj",
    )(x, w_bf)


def _swa_kernel(sink_ref, q_ref, kp_ref, kc_ref, vp_ref, vc_ref, o_ref, *, qb, pos_base, slopes):
    i = pl.program_id(1)
    q0 = pos_base + i * qb
    kp = kp_ref[0].astype(BF16)
    kc = kc_ref[0].astype(BF16)
    vp = vp_ref[0].astype(BF16)
    vc = vc_ref[0].astype(BF16)
    qpos = q0 + lax.broadcasted_iota(I32, (qb, 1), 0)
    kpos_p = q0 - WINDOW + lax.broadcasted_iota(I32, (1, WINDOW), 1)
    kpos_c = q0 + lax.broadcasted_iota(I32, (1, qb), 1)
    qch = qpos // CHUNK

    def allowed(kpos):
        kch = kpos // CHUNK
        return (kpos >= 0) & (kch >= qch - WINDOW // CHUNK) & (kch <= qch)

    al_p, al_c = allowed(kpos_p), allowed(kpos_c)
    dist_p = jnp.abs(qpos - kpos_p).astype(F32)
    dist_c = jnp.abs(qpos - kpos_c).astype(F32)
    for s in range(4):
        slab = q_ref[0, :, s * LANES:(s + 1) * LANES]
        left = _lane_left(slab.shape)
        res = []
        for side in range(2):
            h = 4 * side + s
            qx = jnp.where(left if side == 0 else jnp.logical_not(left), slab, jnp.zeros_like(slab))
            lp = jnp.where(al_p, _dot_nt(qx, kp) - slopes[h] * dist_p, NEG_INF)
            lc = jnp.where(al_c, _dot_nt(qx, kc) - slopes[h] * dist_c, NEG_INF)
            sk = sink_ref[h]
            m = jnp.maximum(jnp.maximum(jnp.max(lp, -1, keepdims=True), jnp.max(lc, -1, keepdims=True)), sk)
            pp = jnp.exp(lp - m)
            pc = jnp.exp(lc - m)
            den = jnp.sum(pp, -1, keepdims=True) + jnp.sum(pc, -1, keepdims=True) + jnp.exp(sk - m)
            o = _dot(pp.astype(BF16), vp) + _dot(pc.astype(BF16), vc)
            res.append(o / den)
        o_ref[0, :, s * LANES:(s + 1) * LANES] = jnp.where(left, res[0], res[1]).astype(o_ref.dtype)


def _swa(q, k_prev_src, k_cur, v_prev_src, v_cur, sink, *, qb, prev_is_cache, pos_base):
    b, t, _ = q.shape
    nq = t // qb
    if prev_is_cache:
        prev_map = lambda bi, i: (bi, 0, 0)
    else:
        r = qb // WINDOW
        prev_map = lambda bi, i: (bi, jnp.maximum(i * r - 1, 0), 0)
    cur_map = lambda bi, i: (bi, i, 0)
    return pl.pallas_call(
        functools.partial(_swa_kernel, qb=qb, pos_base=pos_base, slopes=_alibi(8)),
        grid=(b, nq),
        in_specs=[pl.BlockSpec(memory_space=pltpu.SMEM),
                  pl.BlockSpec((1, qb, 512), cur_map),
                  pl.BlockSpec((1, WINDOW, LANES), prev_map),
                  pl.BlockSpec((1, qb, LANES), cur_map),
                  pl.BlockSpec((1, WINDOW, LANES), prev_map),
                  pl.BlockSpec((1, qb, LANES), cur_map)],
        out_specs=pl.BlockSpec((1, qb, 512), cur_map),
        out_shape=jax.ShapeDtypeStruct((b, t, 512), BF16),
        compiler_params=_cparams(("parallel", "parallel")),
        name="swa_sink",
    )(sink, q, k_prev_src, k_cur, v_prev_src, v_cur)


def _sigmoid(x):
    return 1.0 / (1.0 + jnp.exp(-x))


def _softplus(x):
    return jnp.maximum(x, 0.0) + jnp.log(1.0 + jnp.exp(-jnp.abs(x)))


def _rwkv_prep_kernel(pb_ref, pbprev_ref, shift_ref, mu_ref, w0_ref, wup_ref, a0_ref, aup_ref, gup_ref,
                      kk_ref, ka_ref, rk_ref,
                      r_out, w_out, k_out, v_out, kk_out, b_out, g_out, bonus_out):
    i = pl.program_id(1)
    pb = pb_ref[0]
    tm = pb.shape[0]
    prev_row = jnp.where(i == 0, shift_ref[0], pbprev_ref[0, 7:8, :])
    rolled = pltpu.roll(pb, 1, axis=0)
    row = lax.broadcasted_iota(I32, (tm, 1), 0)
    prev = jnp.where(row == 0, prev_row, rolled)
    xm = pb + (prev - pb) * mu_ref[...]
    r = xm[:, 0:512]
    k = xm[:, 512:1024]
    v = xm[:, 1024:1536]
    wa = xm[:, 1536:1664]
    gl = xm[:, 1664:1792]
    lw = _dot(jnp.tanh(wa).astype(BF16), wup_ref[...])
    la = _dot(wa.astype(BF16), aup_ref[...])
    w_log = -_softplus(-(w0_ref[...] + lw)) - 0.5
    decay = jnp.exp(-jnp.exp(w_log))
    a = _sigmoid(a0_ref[...] + la)
    g = _dot(_sigmoid(gl).astype(BF16), gup_ref[...])
    kk = k * kk_ref[...]
    k2 = k * (1.0 + (a - 1.0) * ka_ref[...])
    rkk = r * k2 * rk_ref[...]
    for p in range(4):
        sl = slice(p * LANES, (p + 1) * LANES)
        kkp = kk[:, sl]
        kkn = kkp * lax.rsqrt(_segsum64(kkp * kkp) + 1e-12)
        kk_out[0, :, sl] = kkn
        b_out[0, :, sl] = kkn * a[:, sl]
        bonus_out[0, :, sl] = _segsum64(rkk[:, sl]) * v[:, sl]
    r_out[0] = r
    w_out[0] = decay
    k_out[0] = k2
    v_out[0] = v
    g_out[0] = g


def _rwkv_prep(pb, shift, mu, w0, wup, a0, aup, gup, k_k, k_a, r_k, tm):
    b, t, c = pb.shape
    nt = t // tm
    cur = lambda bi, i: (bi, i, 0)
    vec = lambda n: pl.BlockSpec((1, n), lambda bi, i: (0, 0))
    outs = [jax.ShapeDtypeStruct((b, t, 512), F32)] * 8
    return pl.pallas_call(
        _rwkv_prep_kernel,
        grid=(b, nt),
        in_specs=[pl.BlockSpec((1, tm, c), cur),
                  pl.BlockSpec((1, 8, c), lambda bi, i: (bi, jnp.maximum(i * (tm // 8) - 1, 0), 0)),
                  pl.BlockSpec((1, 1, c), lambda bi, i: (bi, 0, 0)),
                  vec(c), vec(512),
                  pl.BlockSpec((LANES, 512), lambda bi, i: (0, 0)),
                  vec(512),
                  pl.BlockSpec((LANES, 512), lambda bi, i: (0, 0)),
                  pl.BlockSpec((LANES, 512), lambda bi, i: (0, 0)),
                  vec(512), vec(512), vec(512)],
        out_specs=[pl.BlockSpec((1, tm, 512), cur)] * 8,
        out_shape=outs,
        compiler_params=_cparams(("parallel", "parallel")),
        name="rwkv_prep",
    )(pb, pb, shift, mu, w0, wup, a0, aup, gup, k_k, k_a, r_k)


def _rwkv_scan_kernel(r_ref, w_ref, k_ref, v_ref, kk_ref, b_ref, s0_ref, y_ref, sT_ref, st_scr, *, tb):
    ti = pl.program_id(1)

    @pl.when(ti == 0)
    def _():
        for p in range(4):
            st_scr[p] = jnp.concatenate([s0_ref[0, 2 * p], s0_ref[0, 2 * p + 1]], axis=1)

    eye = (lax.broadcasted_iota(I32, (HEAD_DIM, LANES), 0)
           == lax.broadcasted_iota(I32, (HEAD_DIM, LANES), 1) % HEAD_DIM)

    def step(g, carry):
        t0 = pl.multiple_of(g * 8, 8)
        for p in range(4):
            sl = slice(p * LANES, (p + 1) * LANES)
            blk = lambda ref: ref[0, pl.ds(t0, 8), sl]
            r8, w8, k8, v8, kk8, b8 = blk(r_ref), blk(w_ref), blk(k_ref), blk(v_ref), blk(kk_ref), blk(b_ref)
            br8 = _segsum64(b8 * r8)
            kr8 = _segsum64(k8 * r8)
            wr8 = w8 * r8
            st = st_scr[p]
            ys = []
            for s in range(8):
                row = lambda a: a[s:s + 1]
                sa = _segsum64(st * (-row(kk8)))
                z = _segsum64(st * row(wr8))
                vcol = _segsum64(jnp.where(eye, row(v8), 0.0))
                st = st * row(w8) + sa * row(b8) + vcol * row(k8)
                ycol = z + sa * row(br8) + vcol * row(kr8)
                ys.append(jnp.sum(jnp.where(eye, ycol, 0.0), axis=0, keepdims=True))
            st_scr[p] = st
            y_ref[0, pl.ds(t0, 8), sl] = jnp.concatenate(ys, axis=0)
        return carry

    lax.fori_loop(0, tb // 8, step, 0)

    @pl.when(ti == pl.num_programs(1) - 1)
    def _():
        for p in range(4):
            st = st_scr[p]
            sT_ref[0, 2 * p] = st[:, :HEAD_DIM]
            sT_ref[0, 2 * p + 1] = st[:, HEAD_DIM:]


def _rwkv_scan(r, w, k, v, kk, bb, s0, tb):
    b, t, _ = r.shape
    cur = lambda bi, i: (bi, i, 0)
    st_spec = pl.BlockSpec((1, 8, HEAD_DIM, HEAD_DIM), lambda bi, i: (bi, 0, 0, 0))
    return pl.pallas_call(
        functools.partial(_rwkv_scan_kernel, tb=tb),
        grid=(b, t // tb),
        in_specs=[pl.BlockSpec((1, tb, 512), cur)] * 6 + [st_spec],
        out_specs=[pl.BlockSpec((1, tb, 512), cur), st_spec],
        out_shape=[jax.ShapeDtypeStruct((b, t, 512), F32),
                   jax.ShapeDtypeStruct((b, 8, HEAD_DIM, HEAD_DIM), F32)],
        scratch_shapes=[pltpu.VMEM((4, HEAD_DIM, LANES), F32)],
        compiler_params=_cparams(("parallel", "arbitrary")),
        name="rwkv_scan",
    )(r, w, k, v, kk, bb, s0)


def _rwkv_post_kernel(y_ref, bonus_ref, g_ref, lg_ref, lb_ref, o_ref):
    for p in range(4):
        sl = slice(p * LANES, (p + 1) * LANES)
        y = y_ref[:, sl]
        d = y - _segsum64(y) * (1.0 / HEAD_DIM)
        var = _segsum64(d * d) * (1.0 / HEAD_DIM)
        yn = d * lax.rsqrt(var + RWKV_GN_EPS) * lg_ref[:, sl] + lb_ref[:, sl]
        o_ref[:, sl] = ((yn + bonus_ref[:, sl]) * g_ref[:, sl]).astype(o_ref.dtype)


def _rwkv_post(y, bonus, g, lnx_g, lnx_b, tm):
    n = y.shape[0]
    tile = pl.BlockSpec((tm, 512), lambda i: (i, 0))
    vec = pl.BlockSpec((1, 512), lambda i: (0, 0))
    return pl.pallas_call(
        _rwkv_post_kernel,
        grid=(n // tm,),
        in_specs=[tile, tile, tile, vec, vec],
        out_specs=tile,
        out_shape=jax.ShapeDtypeStruct((n, 512), BF16),
        compiler_params=_cparams(("parallel",)),
        name="rwkv_post",
    )(y, bonus, g, lnx_g, lnx_b)


def _layer_norm(x, g, b):
    mu = jnp.mean(x, -1, keepdims=True)
    d = x - mu
    var = jnp.mean(d * d, -1, keepdims=True)
    return d * lax.rsqrt(var + LN_EPS) * g + b


def _ffn_kernel(x_ref, m1_ref, m2_ref, wo_ref, g1_ref, b1_ref, wu_ref, wd_ref, g2_ref, b2_ref, o_ref,
                *, alpha, ff_chunk):
    half = m1_ref.shape[1]
    mix = _dot(m1_ref[...], wo_ref[0:half, :]) + _dot(m2_ref[...], wo_ref[half:, :])
    x1 = _layer_norm(alpha * x_ref[...] + mix, g1_ref[...], b1_ref[...])
    x1b = x1.astype(BF16)
    d_ff = wu_ref.shape[1]
    h = jnp.zeros_like(x1)
    for c in range(d_ff // ff_chunk):
        u = jnp.maximum(_dot(x1b, wu_ref[:, c * ff_chunk:(c + 1) * ff_chunk]), 0.0)
        h = h + _dot((u * u).astype(BF16), wd_ref[c * ff_chunk:(c + 1) * ff_chunk, :])
    o_ref[...] = _layer_norm(alpha * x1 + h, g2_ref[...], b2_ref[...])


def _const_spec(shape):
    return pl.BlockSpec(shape, lambda i: (0,) * len(shape), pipeline_mode=pl.Buffered(1))


def _out_ffn(x, m1, m2, wo, g1, b1, wu, wd, g2, b2, alpha, tm):
    n, d = x.shape
    dm = m1.shape[1]
    d_ff = wu.shape[1]
    tile = lambda w: pl.BlockSpec((tm, w), lambda i: (i, 0))
    return pl.pallas_call(
        functools.partial(_ffn_kernel, alpha=alpha, ff_chunk=1024),
        grid=(n // tm,),
        in_specs=[tile(d), tile(dm), tile(dm),
                  _const_spec((2 * dm, d)), _const_spec((1, d)), _const_spec((1, d)),
                  _const_spec((d, d_ff)), _const_spec((d_ff, d)), _const_spec((1, d)), _const_spec((1, d))],
        out_specs=tile(d),
        out_shape=jax.ShapeDtypeStruct((n, d), F32),
        compiler_params=_cparams(("parallel",)),
        name="out_ffn",
    )(x, m1, m2, wo, g1, b1, wu, wd, g2, b2)


def _diff_lambda(lam_ref, lam_init):
    lv = lam_ref[...]
    s1 = jnp.sum(lv[0:1] * lv[1:2], axis=-1, keepdims=True)
    s2 = jnp.sum(lv[2:3] * lv[3:4], axis=-1, keepdims=True)
    return jnp.exp(s1) - jnp.exp(s2) + lam_init


def _split_halves(qh):
    left = _lane_left(qh.shape)
    zero = jnp.zeros_like(qh)
    return jnp.concatenate([jnp.where(left, qh, zero), jnp.where(left, zero, qh)], axis=0)


def _flash_update(carry, s3, vblk):
    m, l, acc = carry
    g, rws, kb = s3.shape
    m_new = jnp.maximum(m, jnp.max(s3, -1, keepdims=True))
    m_use = jnp.where(m_new == NEG_INF, 0.0, m_new)
    corr = jnp.exp(m - m_use)
    p = jnp.exp(s3 - m_use)
    l = l * corr + jnp.sum(p, -1, keepdims=True)
    pv = _dot(p.reshape(g * rws, kb).astype(BF16), vblk).reshape(g, rws, vblk.shape[1])
    return m_new, l, acc * corr + pv


def _flash_init(g, rws, e):
    return (jnp.full((g, rws, 1), NEG_INF, F32), jnp.zeros((g, rws, 1), F32), jnp.zeros((g, rws, e), F32))


def _diff_finish(carry, lam, gain, lam_init):
    m, l, acc = carry
    o = acc[0] / l[0] - lam * (acc[1] / l[1])
    o = o * lax.rsqrt(jnp.mean(o * o, -1, keepdims=True) + LN_EPS) * gain * (1.0 - lam_init)
    return o


def _diff_prompt_kernel(q_ref, k_ref, v_ref, lam_ref, gain_ref, o_ref, *, qb, slopes, lam_init):
    i = pl.program_id(1)
    lam = _diff_lambda(lam_ref, lam_init)
    qpos = i * qb + lax.broadcasted_iota(I32, (qb, 1), 0)
    for h in range(4):
        sl = slice(h * LANES, (h + 1) * LANES)
        q2 = _split_halves(q_ref[0, :, sl])

        def body(kb, carry, sl=sl, q2=q2, h=h):
            kblk = k_ref[0, pl.ds(pl.multiple_of(kb * qb, qb), qb), sl]
            vblk = v_ref[0, pl.ds(pl.multiple_of(kb * qb, qb), qb), sl]
            kpos = kb * qb + lax.broadcasted_iota(I32, (1, qb), 1)
            bias = -slopes[h] * jnp.abs(qpos - kpos).astype(F32)
            bias = jnp.where(kpos // CHUNK <= qpos // CHUNK, bias, NEG_INF)
            s3 = _dot_nt(q2, kblk).reshape(2, qb, qb) + bias[None]
            return _flash_update(carry, s3, vblk)

        carry = lax.fori_loop(0, i + 1, body, _flash_init(2, qb, LANES))
        o_ref[0, :, sl] = _diff_finish(carry, lam, gain_ref[...], lam_init).astype(o_ref.dtype)


def _diff_prompt(q, k, v, lamv, gain, lam_init, qb):
    b, t, _ = q.shape
    full = pl.BlockSpec((1, t, 512), lambda bi, i: (bi, 0, 0))
    tile = pl.BlockSpec((1, qb, 512), lambda bi, i: (bi, i, 0))
    return pl.pallas_call(
        functools.partial(_diff_prompt_kernel, qb=qb, slopes=_alibi(4), lam_init=lam_init),
        grid=(b, t // qb),
        in_specs=[tile, full, full,
                  pl.BlockSpec((4, HEAD_DIM), lambda bi, i: (0, 0)),
                  pl.BlockSpec((1, LANES), lambda bi, i: (0, 0))],
        out_specs=tile,
        out_shape=jax.ShapeDtypeStruct((b, t, 512), BF16),
        compiler_params=_cparams(("parallel", "arbitrary")),
        name="diff_prompt",
    )(q, k, v, lamv, gain)


def _diff_sample_kernel(q_ref, kc_ref, vc_ref, kn_ref, vn_ref, lam_ref, gain_ref, o_ref,
                        m_scr, l_scr, acc_scr, *, t, kb_size, n_past, slopes, lam_init):
    j = pl.program_id(1)
    past_len = n_past * kb_size
    qpos = past_len + lax.broadcasted_iota(I32, (t, 1), 0)

    @pl.when(j == 0)
    def _():
        m_scr[...] = jnp.full(m_scr.shape, NEG_INF, F32)
        l_scr[...] = jnp.zeros(l_scr.shape, F32)
        acc_scr[...] = jnp.zeros(acc_scr.shape, F32)

    def run(kfull, vfull, kpos):
        dist = jnp.abs(qpos - kpos).astype(F32)
        for h in range(4):
            sl = slice(h * LANES, (h + 1) * LANES)
            q2 = _split_halves(q_ref[0, :, sl])
            s3 = _dot_nt(q2, kfull[:, sl]).reshape(2, t, kfull.shape[0]) - slopes[h] * dist[None]
            carry = (m_scr[h], l_scr[h], acc_scr[h])
            m, l, acc = _flash_update(carry, s3, vfull[:, sl])
            m_scr[h], l_scr[h], acc_scr[h] = m, l, acc

    @pl.when(j < n_past)
    def _():
        kpos = j * kb_size + lax.broadcasted_iota(I32, (1, kb_size), 1)
        run(kc_ref[0].astype(BF16), vc_ref[0].astype(BF16), kpos)

    @pl.when(j == n_past)
    def _():
        kpos = past_len + lax.broadcasted_iota(I32, (1, t), 1)
        run(kn_ref[0].astype(BF16), vn_ref[0].astype(BF16), kpos)
        lam = _diff_lambda(lam_ref, lam_init)
        for h in range(4):
            sl = slice(h * LANES, (h + 1) * LANES)
            carry = (m_scr[h], l_scr[h], acc_scr[h])
            o_ref[0, :, sl] = _diff_finish(carry, lam, gain_ref[...], lam_init).astype(o_ref.dtype)


def _diff_sample(q, k_cache, v_cache, k_new, v_new, lamv, gain, lam_init, kb_size):
    b, t, _ = q.shape
    n_past = k_cache.shape[1] // kb_size
    cache = pl.BlockSpec((1, kb_size, 512), lambda bi, j: (bi, jnp.minimum(j, n_past - 1), 0))
    new = pl.BlockSpec((1, t, 512), lambda bi, j: (bi, 0, 0))
    return pl.pallas_call(
        functools.partial(_diff_sample_kernel, t=t, kb_size=kb_size, n_past=n_past,
                          slopes=_alibi(4), lam_init=lam_init),
        grid=(b, n_past + 1),
        in_specs=[new, cache, cache, new, new,
                  pl.BlockSpec((4, HEAD_DIM), lambda bi, j: (0, 0)),
                  pl.BlockSpec((1, LANES), lambda bi, j: (0, 0))],
        out_specs=new,
        out_shape=jax.ShapeDtypeStruct((b, t, 512), BF16),
        scratch_shapes=[pltpu.VMEM((4, 2, t, 1), F32), pltpu.VMEM((4, 2, t, 1), F32),
                        pltpu.VMEM((4, 2, t, LANES), F32)],
        compiler_params=_cparams(("parallel", "arbitrary")),
        name="diff_sample",
    )(q, k_cache, v_cache, k_new, v_new, lamv, gain)


def _index_keys(qi, wi, ki_blk, allowed):
    acc = None
    for h in range(8):
        sc = _dot_nt(qi[:, h * HEAD_DIM:(h + 1) * HEAD_DIM], ki_blk)
        term = jnp.maximum(sc, 0.0) * wi[:, h:h + 1]
        acc = term if acc is None else acc + term
    idx = acc * (8.0 ** -0.5) + 0.0
    bits = pltpu.bitcast(idx, I32)
    key = jnp.where(bits < 0, bits ^ 0x7FFFFFFF, bits)
    return jnp.where(allowed, key, INT_MIN)


def _count(pred):
    return jnp.sum(pred.astype(I32), axis=-1, keepdims=True)


def _kth_largest(count_ge, rows, n_sel):
    def body(it, lo):
        bit = 31 - it
        cand = lo + jnp.left_shift(jnp.int32(1), bit)
        return jnp.where(count_ge(cand) >= n_sel, cand, lo)
    return lax.fori_loop(0, 32, body, jnp.full((rows, 1), INT_MIN, I32))


def _tie_position(count_eq_le, need, rows, nbits):
    def body(it, lo):
        bit = nbits - 1 - it
        cand = lo + jnp.left_shift(jnp.int32(1), bit)
        return jnp.where(count_eq_le(cand - 1) < need, cand, lo)
    return lax.fori_loop(0, nbits, body, jnp.zeros((rows, 1), I32))


def _dsa_attend(carry, q8, kblk, vblk, dist, sel, slopes):
    r, kb = dist.shape
    s_list = []
    for h in range(8):
        s = _dot_nt(q8[h], kblk) - slopes[h] * dist
        s_list.append(jnp.where(sel, s, NEG_INF))
    s3 = jnp.stack(s_list, axis=0)
    return _flash_update(carry, s3, vblk)


def _dsa_q8(q_ref):
    q8 = [None] * 8
    for s in range(4):
        slab = q_ref[0, :, s * LANES:(s + 1) * LANES]
        left = _lane_left(slab.shape)
        zero = jnp.zeros_like(slab)
        q8[s] = jnp.where(left, slab, zero)
        q8[4 + s] = jnp.where(left, zero, slab)
    return q8


def _dsa_finish(carry, o_ref):
    m, l, acc = carry
    o = acc / l
    for s in range(4):
        left = _lane_left(o[s].shape)
        o_ref[0, :, s * LANES:(s + 1) * LANES] = jnp.where(left, o[s], o[4 + s]).astype(o_ref.dtype)


def _dsa_prompt_kernel(q_ref, qi_ref, wi_ref, k_ref, v_ref, ki_ref, o_ref, key_scr, *, qb, n_sel, slopes):
    i = pl.program_id(1)
    nkb = i + 1
    qpos = i * qb + lax.broadcasted_iota(I32, (qb, 1), 0)
    qi = qi_ref[0]
    wi = wi_ref[0]

    def kpos_of(kb):
        return kb * qb + lax.broadcasted_iota(I32, (1, qb), 1)

    def fill(kb, c):
        kpos = kpos_of(kb)
        allowed = kpos // CHUNK <= qpos // CHUNK
        ki_blk = ki_ref[0, pl.ds(pl.multiple_of(kb * qb, qb), qb), :]
        key_scr[kb] = _index_keys(qi, wi, ki_blk, allowed)
        return c
    lax.fori_loop(0, nkb, fill, 0)

    def count_over(pred_fn):
        def body(kb, c):
            return c + _count(pred_fn(key_scr[kb], kpos_of(kb)))
        return lax.fori_loop(0, nkb, body, jnp.zeros((qb, 1), I32))

    thr = _kth_largest(lambda cand: count_over(lambda key, kpos: key >= cand), qb, n_sel)
    need = n_sel - count_over(lambda key, kpos: key > thr)
    n_eq = count_over(lambda key, kpos: key == thr)
    has_tie = jnp.max(jnp.where((n_eq > need) & (thr > INT_MIN), 1, 0)) > 0
    pos_thr = lax.cond(
        has_tie,
        lambda: _tie_position(
            lambda p: count_over(lambda key, kpos: (key == thr) & (kpos <= p)), need, qb, 12),
        lambda: jnp.full((qb, 1), 2 ** 30, I32))

    q8 = _dsa_q8(q_ref)

    def attend(kb, carry):
        kpos = kpos_of(kb)
        key = key_scr[kb]
        sel = (key > thr) | ((key == thr) & (kpos <= pos_thr))
        sel = sel & (kpos // CHUNK <= qpos // CHUNK)
        dist = jnp.abs(qpos - kpos).astype(F32)
        kblk = k_ref[0, pl.ds(pl.multiple_of(kb * qb, qb), qb), :]
        vblk = v_ref[0, pl.ds(pl.multiple_of(kb * qb, qb), qb), :]
        return _dsa_attend(carry, q8, kblk, vblk, dist, sel, slopes)

    carry = lax.fori_loop(0, nkb, attend, _flash_init(8, qb, LANES))
    _dsa_finish(carry, o_ref)


def _dsa_prompt(q, qi, wi, k, v, ki, qb):
    b, t, _ = q.shape
    n_sel = min(TOPK_MAX, t // 4)
    tile = lambda w: pl.BlockSpec((1, qb, w), lambda bi, i: (bi, i, 0))
    full = lambda w: pl.BlockSpec((1, t, w), lambda bi, i: (bi, 0, 0))
    return pl.pallas_call(
        functools.partial(_dsa_prompt_kernel, qb=qb, n_sel=n_sel, slopes=_alibi(8)),
        grid=(b, t // qb),
        in_specs=[tile(512), tile(512), tile(LANES), full(LANES), full(LANES), full(HEAD_DIM)],
        out_specs=tile(512),
        out_shape=jax.ShapeDtypeStruct((b, t, 512), BF16),
        scratch_shapes=[pltpu.VMEM((t // qb, qb, qb), I32)],
        compiler_params=_cparams(("parallel", "arbitrary")),
        name="dsa_prompt",
    )(q, qi, wi, k, v, ki)


def _dsa_sample_kernel(q_ref, qi_ref, wi_ref, kc_ref, vc_ref, kic_ref, kn_ref, vn_ref, kin_ref, o_ref,
                       key_scr, keyn_scr, *, t, kb_size, n_past, n_sel, slopes):
    past_len = n_past * kb_size
    qpos = past_len + lax.broadcasted_iota(I32, (t, 1), 0)
    qi = qi_ref[0]
    wi = wi_ref[0]
    kpos_n = past_len + lax.broadcasted_iota(I32, (1, t), 1)

    def kpos_of(kb):
        return kb * kb_size + lax.broadcasted_iota(I32, (1, kb_size), 1)

    for kb in range(n_past):
        ki_blk = kic_ref[0, kb * kb_size:(kb + 1) * kb_size, :].astype(BF16)
        key_scr[kb] = _index_keys(qi, wi, ki_blk, True)
    keyn_scr[...] = _index_keys(qi, wi, kin_ref[0].astype(BF16), True)

    def count_over(pred_fn):
        c = _count(pred_fn(keyn_scr[...], kpos_n))
        for kb in range(n_past):
            c = c + _count(pred_fn(key_scr[kb], kpos_of(kb)))
        return c

    thr = _kth_largest(lambda cand: count_over(lambda key, kpos: key >= cand), t, n_sel)
    need = n_sel - count_over(lambda key, kpos: key > thr)
    n_eq = count_over(lambda key, kpos: key == thr)
    has_tie = jnp.max(jnp.where((n_eq > need) & (thr > INT_MIN), 1, 0)) > 0
    pos_thr = lax.cond(
        has_tie,
        lambda: _tie_position(
            lambda p: count_over(lambda key, kpos: (key == thr) & (kpos <= p)), need, t, 13),
        lambda: jnp.full((t, 1), 2 ** 30, I32))

    q8 = _dsa_q8(q_ref)
    carry = _flash_init(8, t, LANES)

    def sel_of(key, kpos):
        return (key > thr) | ((key == thr) & (kpos <= pos_thr))

    for kb in range(n_past):
        kpos = kpos_of(kb)
        dist = jnp.abs(qpos - kpos).astype(F32)
        kblk = kc_ref[0, kb * kb_size:(kb + 1) * kb_size, :].astype(BF16)
        vblk = vc_ref[0, kb * kb_size:(kb + 1) * kb_size, :].astype(BF16)
        carry = _dsa_attend(carry, q8, kblk, vblk, dist, sel_of(key_scr[kb], kpos), slopes)
    dist = jnp.abs(qpos - kpos_n).astype(F32)
    carry = _dsa_attend(carry, q8, kn_ref[0].astype(BF16), vn_ref[0].astype(BF16), dist,
                        sel_of(keyn_scr[...], kpos_n), slopes)
    _dsa_finish(carry, o_ref)


def _dsa_sample(q, qi, wi, k_cache, v_cache, ki_cache, k_new, v_new, ki_new, kb_size):
    b, t, _ = q.shape
    past = k_cache.shape[1]
    n_past = past // kb_size
    n_sel = min(TOPK_MAX, (past + t) // 4)
    per_b = lambda rows, w: pl.BlockSpec((1, rows, w), lambda bi: (bi, 0, 0))
    return pl.pallas_call(
        functools.partial(_dsa_sample_kernel, t=t, kb_size=kb_size, n_past=n_past, n_sel=n_sel,
                          slopes=_alibi(8)),
        grid=(b,),
        in_specs=[per_b(t, 512), per_b(t, 512), per_b(t, LANES),
                  per_b(past, LANES), per_b(past, LANES), per_b(past, HEAD_DIM),
                  per_b(t, LANES), per_b(t, LANES), per_b(t, HEAD_DIM)],
        out_specs=per_b(t, 512),
        out_shape=jax.ShapeDtypeStruct((b, t, 512), BF16),
        scratch_shapes=[pltpu.VMEM((n_past, t, kb_size), I32), pltpu.VMEM((t, t), I32)],
        compiler_params=_cparams(("parallel",)),
        name="dsa_sample",
    )(q, qi, wi, k_cache, v_cache, ki_cache, k_new, v_new, ki_new)


def _pad_cols(w, total):
    return jnp.pad(w, ((0, 0), (0, total - w.shape[1])))


def _q_perm():
    idx = []
    for s in range(4):
        idx.extend(range(s * HEAD_DIM, (s + 1) * HEAD_DIM))
        idx.extend(range((4 + s) * HEAD_DIM, (5 + s) * HEAD_DIM))
    return np.asarray(idx, np.int32)


def _even_layer(xp, xs, cache_k, cache_v, shift_s, wkv_s, w_in, sink, rw, w_out, ffn, alpha, sizes):
    (mu, w0, w_up, a0, a_up, g_up, k_k, k_a, r_k, lnx_g, lnx_b) = rw
    perm = _q_perm()
    w_in_p = jnp.concatenate([w_in[:, :512][:, perm], w_in[:, 512:]], axis=1).astype(BF16)
    w_out_p = jnp.concatenate([w_out[:512][perm], w_out[512:]], axis=0).astype(BF16)
    groups = ((0, 512, ((0, 512, HEAD_DIM ** -0.5),)),
              (512, 128, ((1, 128, 1.0), (2, 128, 1.0))),
              (640, 128, ((3, 128, 1.0), (4, 128, 1.0))),
              (768, 1792, ((5, 1792, 1.0),)))
    out_defs = ((512, BF16), (128, F32), (128, BF16), (128, F32), (128, BF16), (1792, F32))
    wup_p = jnp.concatenate([w_up, jnp.zeros_like(w_up)], axis=0).astype(BF16)
    aup_p = jnp.concatenate([jnp.zeros_like(a_up), a_up], axis=0).astype(BF16)
    gup = g_up.astype(BF16)
    row = lambda v: v.reshape(1, -1)
    g1, b1, g2, b2, wu, wd = ffn

    def run(x, state, tm, qb, scan_tb, prep_tm):
        b, t, d = x.shape
        n = b * t
        qa, ka, ka_bf, va, va_bf, pb = _project(x.reshape(n, d), w_in_p, groups, out_defs, tm)
        r3 = lambda a: a.reshape(b, t, a.shape[-1])
        if state is None:
            oa = _swa(r3(qa), r3(ka_bf), r3(ka_bf), r3(va_bf), r3(va_bf), sink,
                      qb=qb, prev_is_cache=False, pos_base=0)
            k_buf = r3(ka)[:, -WINDOW:]
            v_buf = r3(va)[:, -WINDOW:]
            shift = jnp.zeros((b, 1, pb.shape[-1]), F32)
            s0 = jnp.zeros((b, 8, HEAD_DIM, HEAD_DIM), F32)
        else:
            ck, cv, shift, s0 = state
            ck2 = ck.reshape(b, WINDOW, LANES)
            cv2 = cv.reshape(b, WINDOW, LANES)
            oa = _swa(r3(qa), ck2, r3(ka_bf), cv2, r3(va_bf), sink,
                      qb=qb, prev_is_cache=True, pos_base=sizes["past"])
            k_buf = jnp.concatenate([ck2, r3(ka)], axis=1)[:, -WINDOW:]
            v_buf = jnp.concatenate([cv2, r3(va)], axis=1)[:, -WINDOW:]
        pb3 = r3(pb)
        rr, ww, kk2, vv, kkn, bb, gg, bonus = _rwkv_prep(
            pb3, shift, row(mu), row(w0), wup_p, row(a0), aup_p, gup, row(k_k), row(k_a), row(r_k), prep_tm)
        y, s_t = _rwkv_scan(rr, ww, kk2, vv, kkn, bb, s0, scan_tb)
        ob = _rwkv_post(y.reshape(n, 512), bonus.reshape(n, 512), gg.reshape(n, 512),
                         row(lnx_g), row(lnx_b), tm)
        xo = _out_ffn(x.reshape(n, d), oa.reshape(n, 512), ob, w_out_p, row(g1), row(b1),
                      wu.astype(BF16), wd.astype(BF16), row(g2), row(b2), alpha, tm)
        st = (k_buf.reshape(b, WINDOW, 2, HEAD_DIM), v_buf.reshape(b, WINDOW, 2, HEAD_DIM),
              pb3[:, -1:], s_t)
        return xo.reshape(b, t, d), st

    xp2, st_p = run(xp, None, sizes["tm_p"], sizes["swa_qb"], sizes["scan_tb_p"], sizes["prep_tm_p"])
    xs2, st_s = run(xs, (cache_k, cache_v, shift_s, wkv_s), sizes["tm_s"], xs.shape[1], xs.shape[1],
                    xs.shape[1])
    return xp2, xs2, st_p, st_s


def _odd_layer(xp, xs, caches, w_in, lam_vecs, subln_g, w_out, ffn, alpha, lam_init, sizes):
    perm = _q_perm()
    scale = HEAD_DIM ** -0.5
    w_p = jnp.concatenate([
        w_in[:, :1536], w_in[:, 1536:2048][:, perm], w_in[:, 2048:2816],
        _pad_cols(w_in[:, 2816:2880], LANES), _pad_cols(w_in[:, 2880:2888], LANES)], axis=1).astype(BF16)
    w_out_p = jnp.concatenate([w_out[:512], w_out[512:][perm]], axis=0).astype(BF16)
    groups = ((0, 512, ((0, 512, scale),)),
              (512, 512, ((1, 512, 1.0), (2, 512, 1.0))),
              (1024, 512, ((3, 512, 1.0), (4, 512, 1.0))),
              (1536, 512, ((5, 512, scale),)),
              (2048, 128, ((6, 128, 1.0), (7, 128, 1.0))),
              (2176, 128, ((8, 128, 1.0), (9, 128, 1.0))),
              (2304, 512, ((10, 512, scale),)),
              (2816, 128, ((11, 64, 1.0), (12, 64, 1.0))),
              (2944, 128, ((13, 128, 1.0),)))
    out_defs = ((512, BF16), (512, F32), (512, BF16), (512, F32), (512, BF16), (512, BF16),
                (128, F32), (128, BF16), (128, F32), (128, BF16), (512, BF16),
                (64, F32), (64, BF16), (128, F32))
    row = lambda v: v.reshape(1, -1)
    g1, b1, g2, b2, wu, wd = ffn
    lamv = jnp.stack(lam_vecs, axis=0)
    gain = row(subln_g)

    def run(x, state, tm):
        b, t, d = x.shape
        n = b * t
        (qc, kc, kc_bf, vc, vc_bf, qd, kd, kd_bf, vd, vd_bf, qi, ki, ki_bf, wi) = _project(
            x.reshape(n, d), w_p, groups, out_defs, tm)
        r3 = lambda a: a.reshape(b, t, a.shape[-1])
        if state is None:
            oc = _diff_prompt(r3(qc), r3(kc_bf), r3(vc_bf), lamv, gain, lam_init, sizes["diff_qb"])
            od = _dsa_prompt(r3(qd), r3(qi), r3(wi), r3(kd_bf), r3(vd_bf), r3(ki_bf), sizes["dsa_qb"])
        else:
            c_k, c_v, d_k, d_v, d_ki = state
            past = c_k.shape[1]
            oc = _diff_sample(r3(qc), c_k.reshape(b, past, 512), c_v.reshape(b, past, 512),
                              r3(kc_bf), r3(vc_bf), lamv, gain, lam_init, sizes["diff_kb_s"])
            od = _dsa_sample(r3(qd), r3(qi), r3(wi), d_k.reshape(b, past, LANES), d_v.reshape(b, past, LANES),
                             d_ki, r3(kd_bf), r3(vd_bf), r3(ki_bf), sizes["dsa_kb_s"])
        xo = _out_ffn(x.reshape(n, d), oc.reshape(n, 512), od.reshape(n, 512), w_out_p, row(g1), row(b1),
                      wu.astype(BF16), wd.astype(BF16), row(g2), row(b2), alpha, tm)
        rows = (kc.reshape(b, t, 4, 2, HEAD_DIM), vc.reshape(b, t, 4, 2 * HEAD_DIM),
                kd.reshape(b, t, 2, HEAD_DIM), vd.reshape(b, t, 2, HEAD_DIM), ki.reshape(b, t, HEAD_DIM))
        return xo.reshape(b, t, d), rows

    xp2, st_p = run(xp, None, sizes["tm_p"])
    xs2, st_s = run(xs, caches, sizes["tm_s"])
    return xp2, xs2, st_p, st_s


def _sizes(xp, xs, past):
    return dict(tm_p=min(512, xp.shape[0] * xp.shape[1]), tm_s=min(512, xs.shape[0] * xs.shape[1]),
                swa_qb=min(256, xp.shape[1]), scan_tb_p=min(256, xp.shape[1]),
                prep_tm_p=min(256, xp.shape[1]), diff_qb=min(256, xp.shape[1]),
                dsa_qb=min(256, xp.shape[1]), diff_kb_s=min(1024, past), dsa_kb_s=min(512, past),
                past=past)


def kernel(x_prompt, x_sample, cache_a_k, cache_a_v, state_b_shift, state_b_wkv, cache_c_k, cache_c_v, cache_d_k, cache_d_v, cache_d_kidx, w_in_even, sink_a, mu_b, w0_b, w_up_b, a0_b, a_up_b, g_up_b, k_k_b, k_a_b, r_k_b, lnx_g_b, lnx_b_b, w_out_even, w_in_odd, lam_q1_c, lam_k1_c, lam_q2_c, lam_k2_c, subln_g_c, w_out_odd, ln_mix_g, ln_mix_b, ln_ffn_g, ln_ffn_b, w_ff_up, w_ff_down):
    depth = ln_mix_g.shape[0]
    alpha = (2 * depth) ** 0.25
    past = cache_c_k.shape[2]
    sizes = _sizes(x_prompt, x_sample, past)
    xp, xs = x_prompt, x_sample
    even_p, even_s, odd_p, odd_s = [], [], [], []
    for layer in range(depth):
        i = layer // 2
        ffn = (ln_mix_g[layer], ln_mix_b[layer], ln_ffn_g[layer], ln_ffn_b[layer],
               w_ff_up[layer], w_ff_down[layer])
        if layer % 2 == 0:
            rw = (mu_b[i], w0_b[i], w_up_b[i], a0_b[i], a_up_b[i], g_up_b[i], k_k_b[i], k_a_b[i],
                  r_k_b[i].reshape(-1), lnx_g_b[i], lnx_b_b[i])
            xp, xs, st_p, st_s = _even_layer(
                xp, xs, cache_a_k[i], cache_a_v[i], state_b_shift[i], state_b_wkv[i],
                w_in_even[i], sink_a[i], rw, w_out_even[i], ffn, alpha, sizes)
            even_p.append(st_p)
            even_s.append(st_s)
        else:
            lam_init = 0.8 - 0.6 * math.exp(-0.3 * layer)
            xp, xs, st_p, st_s = _odd_layer(
                xp, xs, (cache_c_k[i], cache_c_v[i], cache_d_k[i], cache_d_v[i], cache_d_kidx[i]),
                w_in_odd[i], (lam_q1_c[i], lam_k1_c[i], lam_q2_c[i], lam_k2_c[i]), subln_g_c[i],
                w_out_odd[i], ffn, alpha, lam_init, sizes)
            odd_p.append(st_p)
            odd_s.append(st_s)
    stack = lambda states: [jnp.stack(z, axis=0) for z in zip(*states)]
    a_k_p, a_v_p, b_shift_p, b_wkv_p = stack(even_p)
    a_k_s, a_v_s, b_shift_s, b_wkv_s = stack(even_s)
    c_k_p, c_v_p, d_k_p, d_v_p, d_kidx_p = stack(odd_p)
    c_k_s, c_v_s, d_k_s, d_v_s, d_kidx_s = stack(odd_s)
    return (xp, xs, a_k_p, a_v_p, b_shift_p, b_wkv_p, c_k_p, c_v_p, d_k_p, d_v_p, d_kidx_p,
            a_k_s, a_v_s, b_shift_s, b_wkv_s, c_k_s, c_v_s, d_k_s, d_v_s, d_kidx_s)
```

```python
import functools
import math

import numpy as np
import jax
import jax.numpy as jnp
from jax import lax
from jax.experimental import pallas as pl
from jax.experimental.pallas import tpu as pltpu

F32 = jnp.float32
BF16 = jnp.bfloat16
I32 = jnp.int32

LANES = 128
HEAD_DIM = 64
CHUNK = 64
LN_EPS = 1e-5
RWKV_GN_EPS = 64e-5
WINDOW = 128
TOPK_MAX = 256
NEG_INF = float("-inf")
INT_MIN = -(2 ** 31)
VMEM_LIMIT = 56 * 1024 * 1024


def _cparams(sem):
    return pltpu.CompilerParams(dimension_semantics=sem, vmem_limit_bytes=VMEM_LIMIT)


def _alibi(n):
    return [float(2.0 ** (-8.0 * (i + 1) / n)) for i in range(n)]


def _dot(a, b):
    return jnp.dot(a, b, preferred_element_type=F32)


def _dot_nt(a, b):
    return lax.dot_general(a, b, (((1,), (1,)), ((), ())), preferred_element_type=F32)


def _lane_left(shape):
    return lax.broadcasted_iota(I32, shape, len(shape) - 1) < HEAD_DIM


def _segsum64(x):
    left = _lane_left(x.shape)
    sl = jnp.sum(jnp.where(left, x, 0.0), axis=-1, keepdims=True)
    sr = jnp.sum(jnp.where(left, 0.0, x), axis=-1, keepdims=True)
    return jnp.where(left, sl, sr)


def _proj_kernel(x_ref, w_ref, *out_refs, groups):
    xb = x_ref[...].astype(BF16)
    for off, width, outs in groups:
        r = _dot(xb, w_ref[:, off:off + width])
        for idx, ow, scale in outs:
            v = r if ow == width else r[:, :ow]
            if scale != 1.0:
                v = v * scale
            out_refs[idx][...] = v.astype(out_refs[idx].dtype)


def _project(x, w_bf, groups, out_defs, tm):
    n, d = x.shape
    cols = w_bf.shape[1]
    out_shape = [jax.ShapeDtypeStruct((n, ow), dt) for ow, dt in out_defs]
    out_specs = [pl.BlockSpec((tm, ow), lambda i: (i, 0)) for ow, _ in out_defs]
    return pl.pallas_call(
        functools.partial(_proj_kernel, groups=groups),
        grid=(n // tm,),
        in_specs=[pl.BlockSpec((tm, d), lambda i: (i, 0)),
                  pl.BlockSpec((d, cols), lambda i: (0, 0))],
        out_specs=out_specs,
        out_shape=out_shape,
        compiler_params=_cparams(("parallel",)),
        name="in_proj",
    )(x, w_bf)


def _swa_kernel(sink_ref, q_ref, kp_ref, kc_ref, vp_ref, vc_ref, o_ref, *, qb, pos_base, slopes):
    i = pl.program_id(1)
    q0 = pos_base + i * qb
    kp = kp_ref[0].astype(BF16)
    kc = kc_ref[0].astype(BF16)
    vp = vp_ref[0].astype(BF16)
    vc = vc_ref[0].astype(BF16)
    qpos = q0 + lax.broadcasted_iota(I32, (qb, 1), 0)
    kpos_p = q0 - WINDOW + lax.broadcasted_iota(I32, (1, WINDOW), 1)
    kpos_c = q0 + lax.broadcasted_iota(I32, (1, qb), 1)
    qch = qpos // CHUNK

    def allowed(kpos):
        kch = kpos // CHUNK
        return (kpos >= 0) & (kch >= qch - WINDOW // CHUNK) & (kch <= qch)

    al_p, al_c = allowed(kpos_p), allowed(kpos_c)
    dist_p = jnp.abs(qpos - kpos_p).astype(F32)
    dist_c = jnp.abs(qpos - kpos_c).astype(F32)
    for s in range(4):
        slab = q_ref[0, :, s * LANES:(s + 1) * LANES]
        left = _lane_left(slab.shape)
        res = []
        for side in range(2):
            h = 4 * side + s
            qx = jnp.where(left if side == 0 else jnp.logical_not(left), slab, jnp.zeros_like(slab))
            lp = jnp.where(al_p, _dot_nt(qx, kp) - slopes[h] * dist_p, NEG_INF)
            lc = jnp.where(al_c, _dot_nt(qx, kc) - slopes[h] * dist_c, NEG_INF)
            sk = sink_ref[h]
            m = jnp.maximum(jnp.maximum(jnp.max(lp, -1, keepdims=True), jnp.max(lc, -1, keepdims=True)), sk)
            pp = jnp.exp(lp - m)
            pc = jnp.exp(lc - m)
            den = jnp.sum(pp, -1, keepdims=True) + jnp.sum(pc, -1, keepdims=True) + jnp.exp(sk - m)
            o = _dot(pp.astype(BF16), vp) + _dot(pc.astype(BF16), vc)
            res.append(o / den)
        o_ref[0, :, s * LANES:(s + 1) * LANES] = jnp.where(left, res[0], res[1]).astype(o_ref.dtype)


def _swa(q, k_prev_src, k_cur, v_prev_src, v_cur, sink, *, qb, prev_is_cache, pos_base):
    b, t, _ = q.shape
    nq = t // qb
    if prev_is_cache:
        prev_map = lambda bi, i: (bi, 0, 0)
    else:
        r = qb // WINDOW
        prev_map = lambda bi, i: (bi, jnp.maximum(i * r - 1, 0), 0)
    cur_map = lambda bi, i: (bi, i, 0)
    return pl.pallas_call(
        functools.partial(_swa_kernel, qb=qb, pos_base=pos_base, slopes=_alibi(8)),
        grid=(b, nq),
        in_specs=[pl.BlockSpec(memory_space=pltpu.SMEM),
                  pl.BlockSpec((1, qb, 512), cur_map),
                  pl.BlockSpec((1, WINDOW, LANES), prev_map),
                  pl.BlockSpec((1, qb, LANES), cur_map),
                  pl.BlockSpec((1, WINDOW, LANES), prev_map),
                  pl.BlockSpec((1, qb, LANES), cur_map)],
        out_specs=pl.BlockSpec((1, qb, 512), cur_map),
        out_shape=jax.ShapeDtypeStruct((b, t, 512), BF16),
        compiler_params=_cparams(("parallel", "parallel")),
        name="swa_sink",
    )(sink, q, k_prev_src, k_cur, v_prev_src, v_cur)


def _sigmoid(x):
    return 1.0 / (1.0 + jnp.exp(-x))


def _softplus(x):
    return jnp.maximum(x, 0.0) + jnp.log(1.0 + jnp.exp(-jnp.abs(x)))


def _rwkv_prep_kernel(pb_ref, pbprev_ref, shift_ref, mu_ref, w0_ref, wup_ref, a0_ref, aup_ref, gup_ref,
                      kk_ref, ka_ref, rk_ref,
                      r_out, w_out, k_out, v_out, kk_out, b_out, g_out, bonus_out):
    i = pl.program_id(1)
    pb = pb_ref[0]
    tm = pb.shape[0]
    prev_row = jnp.where(i == 0, shift_ref[0], pbprev_ref[0, 7:8, :])
    rolled = pltpu.roll(pb, 1, axis=0)
    row = lax.broadcasted_iota(I32, (tm, 1), 0)
    prev = jnp.where(row == 0, prev_row, rolled)
    xm = pb + (prev - pb) * mu_ref[...]
    r = xm[:, 0:512]
    k = xm[:, 512:1024]
    v = xm[:, 1024:1536]
    wa = xm[:, 1536:1664]
    gl = xm[:, 1664:1792]
    lw = _dot(jnp.tanh(wa).astype(BF16), wup_ref[...])
    la = _dot(wa.astype(BF16), aup_ref[...])
    w_log = -_softplus(-(w0_ref[...] + lw)) - 0.5
    log_decay = -jnp.exp(w_log)
    a = _sigmoid(a0_ref[...] + la)
    g = _dot(_sigmoid(gl).astype(BF16), gup_ref[...])
    kk = k * kk_ref[...]
    k2 = k * (1.0 + (a - 1.0) * ka_ref[...])
    rkk = r * k2 * rk_ref[...]
    for p in range(4):
        sl = slice(p * LANES, (p + 1) * LANES)
        kkp = kk[:, sl]
        kkn = kkp * lax.rsqrt(_segsum64(kkp * kkp) + 1e-12)
        kk_out[0, p] = kkn
        b_out[0, p] = kkn * a[:, sl]
        bonus_out[0, :, sl] = _segsum64(rkk[:, sl]) * v[:, sl]
        r_out[0, p] = r[:, sl]
        w_out[0, p] = log_decay[:, sl]
        k_out[0, p] = k2[:, sl]
        v_out[0, p] = v[:, sl]
    g_out[0] = g


def _rwkv_prep(pb, shift, mu, w0, wup, a0, aup, gup, k_k, k_a, r_k, tm):
    b, t, c = pb.shape
    nt = t // tm
    cur = lambda bi, i: (bi, i, 0)
    vec = lambda n: pl.BlockSpec((1, n), lambda bi, i: (0, 0))
    pair = pl.BlockSpec((1, 4, tm, LANES), lambda bi, i: (bi, 0, i, 0))
    outs = [jax.ShapeDtypeStruct((b, 4, t, LANES), F32)] * 6 + [jax.ShapeDtypeStruct((b, t, 512), F32)] * 2
    return pl.pallas_call(
        _rwkv_prep_kernel,
        grid=(b, nt),
        in_specs=[pl.BlockSpec((1, tm, c), cur),
                  pl.BlockSpec((1, 8, c), lambda bi, i: (bi, jnp.maximum(i * (tm // 8) - 1, 0), 0)),
                  pl.BlockSpec((1, 1, c), lambda bi, i: (bi, 0, 0)),
                  vec(c), vec(512),
                  pl.BlockSpec((LANES, 512), lambda bi, i: (0, 0)),
                  vec(512),
                  pl.BlockSpec((LANES, 512), lambda bi, i: (0, 0)),
                  pl.BlockSpec((LANES, 512), lambda bi, i: (0, 0)),
                  vec(512), vec(512), vec(512)],
        out_specs=[pair] * 6 + [pl.BlockSpec((1, tm, 512), cur)] * 2,
        out_shape=outs,
        compiler_params=_cparams(("parallel", "parallel")),
        name="rwkv_prep",
    )(pb, pb, shift, mu, w0, wup, a0, aup, gup, k_k, k_a, r_k)


RWKV_CHUNK = 16


def _split2(x):
    hi = x.astype(BF16)
    return hi, (x - hi.astype(F32)).astype(BF16)


def _dot3(a, b):
    return _dot(a[0], b[0]) + (_dot(a[0], b[1]) + _dot(a[1], b[0]))


def _rwkv_scan_kernel(r_ref, lw_ref, k_ref, v_ref, kk_ref, b_ref, s0_ref, y_ref, sT_ref,
                      st_scr, lhi_scr, llo_scr, h_scr, y0_scr, *, tb):
    ti = pl.program_id(1)
    c_sz = RWKV_CHUNK
    nc = tb // c_sz

    @pl.when(ti == 0)
    def _():
        st_scr[...] = s0_ref[0]

    row = lax.broadcasted_iota(I32, (tb, tb), 0)
    col = lax.broadcasted_iota(I32, (tb, tb), 1)
    same = (row // c_sz) == (col // c_sz)
    strict = same & (col < row)
    incl = same & (col <= row)
    eye_t = jnp.where(row == col, 1.0, 0.0)
    cum_lhs = jnp.concatenate([jnp.where(incl, 1.0, 0.0), jnp.where(same, 1.0, 0.0)], axis=0).astype(BF16)
    left = _lane_left((tb, LANES))
    r128 = lax.broadcasted_iota(I32, (LANES, LANES), 0)
    c128 = lax.broadcasted_iota(I32, (LANES, LANES), 1)
    blockdiag = (r128 < HEAD_DIM) == (c128 < HEAD_DIM)
    diag128 = r128 == c128
    tcol = lax.broadcasted_iota(I32, (LANES, tb), 1)

    def prepare(p, carry):
        r, lw, k, v, kk, b = (ref[0, p] for ref in (r_ref, lw_ref, k_ref, v_ref, kk_ref, b_ref))
        l1 = lw.astype(BF16)
        rem = lw - l1.astype(F32)
        l2 = rem.astype(BF16)
        l3 = (rem - l2.astype(F32)).astype(BF16)
        cums = _dot(cum_lhs, l1) + (_dot(cum_lhs, l2) + _dot(cum_lhs, l3))
        cs, cse = cums[:tb], cums[tb:]
        p_run = jnp.exp(cs)
        p_inv = jnp.exp(-cs)
        p_end = jnp.exp(cse - cs)
        al = -kk * jnp.exp(cs - lw)
        be = b * p_inv
        kt = k * p_inv
        rt = r * p_run
        zero = jnp.zeros_like(al)
        rhs = _split2(jnp.concatenate([be, kt], axis=0))
        vs, als = _split2(v), _split2(al)
        tm_h, a2_h, b1_h, b2_h = [], [], [], []
        for side in range(2):
            msk = left if side == 0 else jnp.logical_not(left)
            lhs = _split2(jnp.concatenate([jnp.where(msk, al, zero), jnp.where(msk, rt, zero)], axis=0))
            gram = (_dot_nt(lhs[0], rhs[0]) + (_dot_nt(lhs[0], rhs[1]) + _dot_nt(lhs[1], rhs[0])))
            a1 = jnp.where(strict, gram[:tb, :tb], 0.0)
            a2_h.append(_split2(jnp.where(strict, gram[:tb, tb:], 0.0)))
            b1_h.append(_split2(jnp.where(incl, gram[tb:, :tb], 0.0)))
            b2_h.append(_split2(jnp.where(incl, gram[tb:, tb:], 0.0)))
            m = a1
            tm = eye_t + a1
            for _ in range(3):
                ms = _split2(m)
                m = _dot3(ms, ms)
                tm = tm + _dot3(_split2(tm), _split2(m))
            tm_h.append(_split2(tm))
        both = lambda f: jnp.where(left, f(0), f(1))
        xv = _split2(both(lambda s: _dot3(a2_h[s], vs)))
        ah = both(lambda s: _dot3(tm_h[s], als))
        ahs = _split2(ah)
        u0 = both(lambda s: _dot3(tm_h[s], xv))
        u0s = _split2(u0)
        rh = rt + both(lambda s: _dot3(b1_h[s], ahs))
        y0_scr[p] = both(lambda s: _dot3(b1_h[s], u0s) + _dot3(b2_h[s], vs))
        rh_hi, rh_lo = _split2(rh)
        bbt = _split2((b * p_end).T)
        kbt = _split2((k * p_end).T)
        pc_rows = jnp.exp(cse)
        chunk_rows = lambda x: jnp.concatenate(
            [jnp.where((tcol // c_sz) == c, x, jnp.zeros_like(x)) for c in range(nc)], axis=0)
        bsel = tuple(chunk_rows(x) for x in bbt)
        ksel = tuple(chunk_rows(x) for x in kbt)
        bk = tuple(jnp.concatenate([bs, ks], axis=1) for bs, ks in zip(bsel, ksel))
        uv = tuple(jnp.concatenate([us, vv], axis=0) for us, vv in zip(u0s, vs))
        g_all = _dot3(bsel, ahs)
        h_all = _dot3(bk, uv)
        for c in range(nc):
            blk = slice(c * LANES, (c + 1) * LANES)
            pc = pc_rows[c * c_sz:c * c_sz + 1]
            g = jnp.where(diag128, pc, 0.0) + jnp.where(blockdiag, g_all[blk], 0.0)
            h_scr[p, c] = jnp.where(blockdiag, h_all[blk], 0.0)
            g_hi, g_lo = _split2(g)
            rows = slice(c * c_sz, (c + 1) * c_sz)
            lhi_scr[p, c] = jnp.concatenate([g_hi, rh_hi[rows]], axis=0)
            llo_scr[p, c] = jnp.concatenate([g_lo, rh_lo[rows]], axis=0)
        return carry

    lax.fori_loop(0, 4, prepare, 0, unroll=2)

    st = [st_scr[p] for p in range(4)]
    for c in range(nc):
        rows = slice(c * c_sz, (c + 1) * c_sz)
        for p in range(4):
            res = _dot3((lhi_scr[p, c], llo_scr[p, c]), _split2(st[p]))
            y_ref[0, p, rows, :] = res[LANES:] + y0_scr[p, rows, :]
            st[p] = res[:LANES] + h_scr[p, c]
    for p in range(4):
        st_scr[p] = st[p]

    @pl.when(ti == pl.num_programs(1) - 1)
    def _():
        sT_ref[0] = st_scr[...]


def _rwkv_scan(r, lw, k, v, kk, bb, s0_bd, tb):
    b, _, t, _ = r.shape
    nc = tb // RWKV_CHUNK
    cur = pl.BlockSpec((1, 4, tb, LANES), lambda bi, i: (bi, 0, i, 0))
    st_spec = pl.BlockSpec((1, 4, LANES, LANES), lambda bi, i: (bi, 0, 0, 0))
    return pl.pallas_call(
        functools.partial(_rwkv_scan_kernel, tb=tb),
        grid=(b, t // tb),
        in_specs=[cur] * 6 + [st_spec],
        out_specs=[cur, st_spec],
        out_shape=[jax.ShapeDtypeStruct((b, 4, t, LANES), F32),
                   jax.ShapeDtypeStruct((b, 4, LANES, LANES), F32)],
        scratch_shapes=[pltpu.VMEM((4, LANES, LANES), F32),
                        pltpu.VMEM((4, nc, LANES + RWKV_CHUNK, LANES), BF16),
                        pltpu.VMEM((4, nc, LANES + RWKV_CHUNK, LANES), BF16),
                        pltpu.VMEM((4, nc, LANES, LANES), F32),
                        pltpu.VMEM((4, tb, LANES), F32)],
        compiler_params=_cparams(("parallel", "arbitrary")),
        name="rwkv_scan",
    )(r, lw, k, v, kk, bb, s0_bd)


def _state_to_blockdiag(s):
    b = s.shape[0]
    st = jnp.swapaxes(s, -1, -2).reshape(b, 4, 2, HEAD_DIM, HEAD_DIM)
    z = jnp.zeros_like(st[:, :, 0])
    top = jnp.concatenate([st[:, :, 0], z], axis=-1)
    bot = jnp.concatenate([z, st[:, :, 1]], axis=-1)
    return jnp.concatenate([top, bot], axis=-2)


def _state_from_blockdiag(s_bd):
    b = s_bd.shape[0]
    a = s_bd[:, :, :HEAD_DIM, :HEAD_DIM]
    d = s_bd[:, :, HEAD_DIM:, HEAD_DIM:]
    st = jnp.stack([a, d], axis=2).reshape(b, 8, HEAD_DIM, HEAD_DIM)
    return jnp.swapaxes(st, -1, -2)


def _rwkv_post_kernel(y_ref, bonus_ref, g_ref, lg_ref, lb_ref, o_ref):
    for p in range(4):
        sl = slice(p * LANES, (p + 1) * LANES)
        y = y_ref[0, p]
        d = y - _segsum64(y) * (1.0 / HEAD_DIM)
        var = _segsum64(d * d) * (1.0 / HEAD_DIM)
        yn = d * lax.rsqrt(var + RWKV_GN_EPS) * lg_ref[:, sl] + lb_ref[:, sl]
        o_ref[0, :, sl] = ((yn + bonus_ref[0, :, sl]) * g_ref[0, :, sl]).astype(o_ref.dtype)


def _rwkv_post(y, bonus, g, lnx_g, lnx_b, tm):
    b, _, t, _ = y.shape
    tile = pl.BlockSpec((1, tm, 512), lambda bi, i: (bi, i, 0))
    vec = pl.BlockSpec((1, 512), lambda bi, i: (0, 0))
    return pl.pallas_call(
        _rwkv_post_kernel,
        grid=(b, t // tm),
        in_specs=[pl.BlockSpec((1, 4, tm, LANES), lambda bi, i: (bi, 0, i, 0)), tile, tile, vec, vec],
        out_specs=tile,
        out_shape=jax.ShapeDtypeStruct((b, t, 512), BF16),
        compiler_params=_cparams(("parallel", "parallel")),
        name="rwkv_post",
    )(y, bonus, g, lnx_g, lnx_b)


def _layer_norm(x, g, b):
    mu = jnp.mean(x, -1, keepdims=True)
    d = x - mu
    var = jnp.mean(d * d, -1, keepdims=True)
    return d * lax.rsqrt(var + LN_EPS) * g + b


def _ffn_kernel(x_ref, m1_ref, m2_ref, wo_ref, g1_ref, b1_ref, wu_ref, wd_ref, g2_ref, b2_ref, o_ref,
                *, alpha, ff_chunk):
    half = m1_ref.shape[1]
    mix = _dot(m1_ref[...], wo_ref[0:half, :]) + _dot(m2_ref[...], wo_ref[half:, :])
    x1 = _layer_norm(alpha * x_ref[...] + mix, g1_ref[...], b1_ref[...])
    x1b = x1.astype(BF16)
    d_ff = wu_ref.shape[1]
    h = jnp.zeros_like(x1)
    for c in range(d_ff // ff_chunk):
        u = jnp.maximum(_dot(x1b, wu_ref[:, c * ff_chunk:(c + 1) * ff_chunk]), 0.0)
        h = h + _dot((u * u).astype(BF16), wd_ref[c * ff_chunk:(c + 1) * ff_chunk, :])
    o_ref[...] = _layer_norm(alpha * x1 + h, g2_ref[...], b2_ref[...])


def _const_spec(shape):
    return pl.BlockSpec(shape, lambda i: (0,) * len(shape), pipeline_mode=pl.Buffered(1))


def _out_ffn(x, m1, m2, wo, g1, b1, wu, wd, g2, b2, alpha, tm):
    n, d = x.shape
    dm = m1.shape[1]
    d_ff = wu.shape[1]
    tile = lambda w: pl.BlockSpec((tm, w), lambda i: (i, 0))
    return pl.pallas_call(
        functools.partial(_ffn_kernel, alpha=alpha, ff_chunk=1024),
        grid=(n // tm,),
        in_specs=[tile(d), tile(dm), tile(dm),
                  _const_spec((2 * dm, d)), _const_spec((1, d)), _const_spec((1, d)),
                  _const_spec((d, d_ff)), _const_spec((d_ff, d)), _const_spec((1, d)), _const_spec((1, d))],
        out_specs=tile(d),
        out_shape=jax.ShapeDtypeStruct((n, d), F32),
        compiler_params=_cparams(("parallel",)),
        name="out_ffn",
    )(x, m1, m2, wo, g1, b1, wu, wd, g2, b2)


def _diff_lambda(lam_ref, lam_init):
    lv = lam_ref[...]
    s1 = jnp.sum(lv[0:1] * lv[1:2], axis=-1, keepdims=True)
    s2 = jnp.sum(lv[2:3] * lv[3:4], axis=-1, keepdims=True)
    return jnp.exp(s1) - jnp.exp(s2) + lam_init


def _split_halves(qh):
    left = _lane_left(qh.shape)
    zero = jnp.zeros_like(qh)
    return jnp.concatenate([jnp.where(left, qh, zero), jnp.where(left, zero, qh)], axis=0)


def _flash_update(carry, s3, vblk):
    m, l, acc = carry
    g, rws, kb = s3.shape
    m_new = jnp.maximum(m, jnp.max(s3, -1, keepdims=True))
    m_use = jnp.where(m_new == NEG_INF, 0.0, m_new)
    corr = jnp.exp(m - m_use)
    p = jnp.exp(s3 - m_use)
    l = l * corr + jnp.sum(p, -1, keepdims=True)
    pv = _dot(p.reshape(g * rws, kb).astype(BF16), vblk).reshape(g, rws, vblk.shape[1])
    return m_new, l, acc * corr + pv


def _flash_init(g, rws, e):
    return (jnp.full((g, rws, 1), NEG_INF, F32), jnp.zeros((g, rws, 1), F32), jnp.zeros((g, rws, e), F32))


def _diff_finish(carry, lam, gain, lam_init):
    m, l, acc = carry
    o = acc[0] / l[0] - lam * (acc[1] / l[1])
    o = o * lax.rsqrt(jnp.mean(o * o, -1, keepdims=True) + LN_EPS) * gain * (1.0 - lam_init)
    return o


def _diff_prompt_kernel(q_ref, k_ref, v_ref, lam_ref, gain_ref, o_ref, *, qb, slopes, lam_init):
    i = pl.program_id(1)
    lam = _diff_lambda(lam_ref, lam_init)
    qpos = i * qb + lax.broadcasted_iota(I32, (qb, 1), 0)
    for h in range(4):
        sl = slice(h * LANES, (h + 1) * LANES)
        q2 = _split_halves(q_ref[0, :, sl])

        def body(kb, carry, sl=sl, q2=q2, h=h):
            kblk = k_ref[0, pl.ds(pl.multiple_of(kb * qb, qb), qb), sl]
            vblk = v_ref[0, pl.ds(pl.multiple_of(kb * qb, qb), qb), sl]
            kpos = kb * qb + lax.broadcasted_iota(I32, (1, qb), 1)
            bias = -slopes[h] * jnp.abs(qpos - kpos).astype(F32)
            bias = jnp.where(kpos // CHUNK <= qpos // CHUNK, bias, NEG_INF)
            s3 = _dot_nt(q2, kblk).reshape(2, qb, qb) + bias[None]
            return _flash_update(carry, s3, vblk)

        carry = lax.fori_loop(0, i + 1, body, _flash_init(2, qb, LANES))
        o_ref[0, :, sl] = _diff_finish(carry, lam, gain_ref[...], lam_init).astype(o_ref.dtype)


def _diff_prompt(q, k, v, lamv, gain, lam_init, qb):
    b, t, _ = q.shape
    full = pl.BlockSpec((1, t, 512), lambda bi, i: (bi, 0, 0))
    tile = pl.BlockSpec((1, qb, 512), lambda bi, i: (bi, i, 0))
    return pl.pallas_call(
        functools.partial(_diff_prompt_kernel, qb=qb, slopes=_alibi(4), lam_init=lam_init),
        grid=(b, t // qb),
        in_specs=[tile, full, full,
                  pl.BlockSpec((4, HEAD_DIM), lambda bi, i: (0, 0)),
                  pl.BlockSpec((1, LANES), lambda bi, i: (0, 0))],
        out_specs=tile,
        out_shape=jax.ShapeDtypeStruct((b, t, 512), BF16),
        compiler_params=_cparams(("parallel", "arbitrary")),
        name="diff_prompt",
    )(q, k, v, lamv, gain)


def _diff_sample_kernel(q_ref, kc_ref, vc_ref, kn_ref, vn_ref, lam_ref, gain_ref, o_ref,
                        m_scr, l_scr, acc_scr, *, t, kb_size, n_past, slopes, lam_init):
    j = pl.program_id(1)
    past_len = n_past * kb_size
    qpos = past_len + lax.broadcasted_iota(I32, (t, 1), 0)

    @pl.when(j == 0)
    def _():
        m_scr[...] = jnp.full(m_scr.shape, NEG_INF, F32)
        l_scr[...] = jnp.zeros(l_scr.shape, F32)
        acc_scr[...] = jnp.zeros(acc_scr.shape, F32)

    def run(kfull, vfull, kpos):
        dist = jnp.abs(qpos - kpos).astype(F32)
        for h in range(4):
            sl = slice(h * LANES, (h + 1) * LANES)
            q2 = _split_halves(q_ref[0, :, sl])
            s3 = _dot_nt(q2, kfull[:, sl]).reshape(2, t, kfull.shape[0]) - slopes[h] * dist[None]
            carry = (m_scr[h], l_scr[h], acc_scr[h])
            m, l, acc = _flash_update(carry, s3, vfull[:, sl])
            m_scr[h], l_scr[h], acc_scr[h] = m, l, acc

    @pl.when(j < n_past)
    def _():
        kpos = j * kb_size + lax.broadcasted_iota(I32, (1, kb_size), 1)
        run(kc_ref[0].astype(BF16), vc_ref[0].astype(BF16), kpos)

    @pl.when(j == n_past)
    def _():
        kpos = past_len + lax.broadcasted_iota(I32, (1, t), 1)
        run(kn_ref[0].astype(BF16), vn_ref[0].astype(BF16), kpos)
        lam = _diff_lambda(lam_ref, lam_init)
        for h in range(4):
            sl = slice(h * LANES, (h + 1) * LANES)
            carry = (m_scr[h], l_scr[h], acc_scr[h])
            o_ref[0, :, sl] = _diff_finish(carry, lam, gain_ref[...], lam_init).astype(o_ref.dtype)


def _diff_sample(q, k_cache, v_cache, k_new, v_new, lamv, gain, lam_init, kb_size):
    b, t, _ = q.shape
    n_past = k_cache.shape[1] // kb_size
    cache = pl.BlockSpec((1, kb_size, 512), lambda bi, j: (bi, jnp.minimum(j, n_past - 1), 0))
    new = pl.BlockSpec((1, t, 512), lambda bi, j: (bi, 0, 0))
    return pl.pallas_call(
        functools.partial(_diff_sample_kernel, t=t, kb_size=kb_size, n_past=n_past,
                          slopes=_alibi(4), lam_init=lam_init),
        grid=(b, n_past + 1),
        in_specs=[new, cache, cache, new, new,
                  pl.BlockSpec((4, HEAD_DIM), lambda bi, j: (0, 0)),
                  pl.BlockSpec((1, LANES), lambda bi, j: (0, 0))],
        out_specs=new,
        out_shape=jax.ShapeDtypeStruct((b, t, 512), BF16),
        scratch_shapes=[pltpu.VMEM((4, 2, t, 1), F32), pltpu.VMEM((4, 2, t, 1), F32),
                        pltpu.VMEM((4, 2, t, LANES), F32)],
        compiler_params=_cparams(("parallel", "arbitrary")),
        name="diff_sample",
    )(q, k_cache, v_cache, k_new, v_new, lamv, gain)


def _index_keys(qi, wi, ki_blk, allowed):
    acc = None
    for h in range(8):
        sc = _dot_nt(qi[:, h * HEAD_DIM:(h + 1) * HEAD_DIM], ki_blk)
        term = jnp.maximum(sc, 0.0) * wi[:, h:h + 1]
        acc = term if acc is None else acc + term
    idx = acc * (8.0 ** -0.5) + 0.0
    bits = pltpu.bitcast(idx, I32)
    key = jnp.where(bits < 0, bits ^ 0x7FFFFFFF, bits)
    return jnp.where(allowed, key, INT_MIN)


def _lane_fold(x):
    acc = x[:, :LANES]
    for j in range(1, x.shape[1] // LANES):
        acc = acc + x[:, j * LANES:(j + 1) * LANES]
    return acc


def _ones_where(pred):
    return jnp.where(pred, 1, 0).astype(I32)


def _kth_largest(count_ge, rows, n_sel):
    def body(it, lo):
        bit = 31 - it
        cand = lo + jnp.left_shift(jnp.int32(1), bit)
        return jnp.where(count_ge(cand) >= n_sel, cand, lo)
    return lax.fori_loop(0, 32, body, jnp.full((rows, 1), INT_MIN, I32))


def _tie_position(count_eq_le, need, rows, nbits):
    def body(it, lo):
        bit = nbits - 1 - it
        cand = lo + jnp.left_shift(jnp.int32(1), bit)
        return jnp.where(count_eq_le(cand - 1) < need, cand, lo)
    return lax.fori_loop(0, nbits, body, jnp.zeros((rows, 1), I32))


def _select_threshold(count_over, rows, n_sel, pos_bits):
    thr = _kth_largest(lambda cand: count_over(lambda key, kpos: key >= cand), rows, n_sel)
    need = n_sel - count_over(lambda key, kpos: key > thr)
    n_eq = count_over(lambda key, kpos: key == thr)
    has_tie = jnp.max(jnp.where((n_eq > need) & (thr > INT_MIN), 1, 0)) > 0
    pos_thr = lax.cond(
        has_tie,
        lambda: _tie_position(
            lambda p: count_over(lambda key, kpos: (key == thr) & (kpos <= p)), need, rows, pos_bits),
        lambda: jnp.full((rows, 1), 2 ** 30, I32))
    return thr, pos_thr


def _dsa_attend(carry, q_ref, kblk, vblk, dist, sel, slopes):
    neg = jnp.where(sel, 0.0, NEG_INF)
    qs = []
    for s in range(4):
        slab = q_ref[0, :, s * LANES:(s + 1) * LANES]
        left = _lane_left(slab.shape)
        zero = jnp.zeros_like(slab)
        qs.append((jnp.where(left, slab, zero), jnp.where(left, zero, slab)))
    q_all = jnp.concatenate([qs[h % 4][h // 4] for h in range(8)], axis=0)
    r, kb = dist.shape
    s3 = _dot_nt(q_all, kblk).reshape(8, r, kb)
    s3 = jnp.stack([s3[h] - slopes[h] * dist + neg for h in range(8)], axis=0)
    return _flash_update(carry, s3, vblk)


def _dsa_finish(carry, o_ref):
    m, l, acc = carry
    o = acc / l
    for s in range(4):
        left = _lane_left(o[s].shape)
        o_ref[0, :, s * LANES:(s + 1) * LANES] = jnp.where(left, o[s], o[4 + s]).astype(o_ref.dtype)


def _dsa_prompt_kernel(q_ref, qi_ref, wi_ref, k_ref, v_ref, ki_ref, o_ref, key_scr, *, qb, n_sel, slopes):
    i = pl.program_id(1)
    nkb = i + 1
    qpos = i * qb + lax.broadcasted_iota(I32, (qb, 1), 0)
    qi = qi_ref[0]
    wi = wi_ref[0]

    def kpos_of(kb):
        return kb * qb + lax.broadcasted_iota(I32, (1, qb), 1)

    def fill(kb, c):
        kpos = kpos_of(kb)
        allowed = kpos // CHUNK <= qpos // CHUNK
        ki_blk = ki_ref[0, pl.ds(pl.multiple_of(kb * qb, qb), qb), :]
        key_scr[kb] = _index_keys(qi, wi, ki_blk, allowed)
        return c
    lax.fori_loop(0, nkb, fill, 0)

    def count_over(pred_fn):
        def body(kb, c):
            return c + _lane_fold(_ones_where(pred_fn(key_scr[kb], kpos_of(kb))))
        part = lax.fori_loop(0, nkb, body, jnp.zeros((qb, LANES), I32))
        return jnp.sum(part, axis=-1, keepdims=True)

    thr, pos_thr = _select_threshold(count_over, qb, n_sel, 12)

    def attend(kb, carry):
        kpos = kpos_of(kb)
        key = key_scr[kb]
        sel = (key > thr) | ((key == thr) & (kpos <= pos_thr))
        sel = sel & (kpos // CHUNK <= qpos // CHUNK)
        dist = jnp.abs(qpos - kpos).astype(F32)
        kblk = k_ref[0, pl.ds(pl.multiple_of(kb * qb, qb), qb), :]
        vblk = v_ref[0, pl.ds(pl.multiple_of(kb * qb, qb), qb), :]
        return _dsa_attend(carry, q_ref, kblk, vblk, dist, sel, slopes)

    carry = lax.fori_loop(0, nkb, attend, _flash_init(8, qb, LANES))
    _dsa_finish(carry, o_ref)


def _dsa_prompt(q, qi, wi, k, v, ki, qb):
    b, t, _ = q.shape
    n_sel = min(TOPK_MAX, t // 4)
    tile = lambda w: pl.BlockSpec((1, qb, w), lambda bi, i: (bi, i, 0))
    full = lambda w: pl.BlockSpec((1, t, w), lambda bi, i: (bi, 0, 0))
    return pl.pallas_call(
        functools.partial(_dsa_prompt_kernel, qb=qb, n_sel=n_sel, slopes=_alibi(8)),
        grid=(b, t // qb),
        in_specs=[tile(512), tile(512), tile(LANES), full(LANES), full(LANES), full(HEAD_DIM)],
        out_specs=tile(512),
        out_shape=jax.ShapeDtypeStruct((b, t, 512), BF16),
        scratch_shapes=[pltpu.VMEM((t // qb, qb, qb), I32)],
        compiler_params=_cparams(("parallel", "arbitrary")),
        name="dsa_prompt",
    )(q, qi, wi, k, v, ki)


def _dsa_sample_kernel(q_ref, qi_ref, wi_ref, kc_ref, vc_ref, kic_ref, kn_ref, vn_ref, kin_ref, o_ref,
                       key_scr, keyn_scr, *, t, kb_size, n_past, n_sel, slopes):
    past_len = n_past * kb_size
    qpos = past_len + lax.broadcasted_iota(I32, (t, 1), 0)
    qi = qi_ref[0]
    wi = wi_ref[0]
    kpos_n = past_len + lax.broadcasted_iota(I32, (1, t), 1)

    def kpos_of(kb):
        return kb * kb_size + lax.broadcasted_iota(I32, (1, kb_size), 1)

    for kb in range(n_past):
        ki_blk = kic_ref[0, kb * kb_size:(kb + 1) * kb_size, :].astype(BF16)
        key_scr[kb] = _index_keys(qi, wi, ki_blk, True)
    keyn_scr[...] = _index_keys(qi, wi, kin_ref[0].astype(BF16), True)

    def count_over(pred_fn):
        part = _lane_fold(_ones_where(pred_fn(key_scr[0], kpos_of(0))))
        for kb in range(1, n_past):
            part = part + _lane_fold(_ones_where(pred_fn(key_scr[kb], kpos_of(kb))))
        new = jnp.sum(_ones_where(pred_fn(keyn_scr[...], kpos_n)), axis=-1, keepdims=True)
        return jnp.sum(part, axis=-1, keepdims=True) + new

    thr, pos_thr = _select_threshold(count_over, t, n_sel, 13)
    carry = _flash_init(8, t, LANES)

    def sel_of(key, kpos):
        return (key > thr) | ((key == thr) & (kpos <= pos_thr))

    for kb in range(n_past):
        kpos = kpos_of(kb)
        dist = jnp.abs(qpos - kpos).astype(F32)
        kblk = kc_ref[0, kb * kb_size:(kb + 1) * kb_size, :].astype(BF16)
        vblk = vc_ref[0, kb * kb_size:(kb + 1) * kb_size, :].astype(BF16)
        carry = _dsa_attend(carry, q_ref, kblk, vblk, dist, sel_of(key_scr[kb], kpos), slopes)
    dist = jnp.abs(qpos - kpos_n).astype(F32)
    carry = _dsa_attend(carry, q_ref, kn_ref[0].astype(BF16), vn_ref[0].astype(BF16), dist,
                        sel_of(keyn_scr[...], kpos_n), slopes)
    _dsa_finish(carry, o_ref)


def _dsa_sample(q, qi, wi, k_cache, v_cache, ki_cache, k_new, v_new, ki_new, kb_size):
    b, t, _ = q.shape
    past = k_cache.shape[1]
    n_past = past // kb_size
    n_sel = min(TOPK_MAX, (past + t) // 4)
    per_b = lambda rows, w: pl.BlockSpec((1, rows, w), lambda bi: (bi, 0, 0))
    return pl.pallas_call(
        functools.partial(_dsa_sample_kernel, t=t, kb_size=kb_size, n_past=n_past, n_sel=n_sel,
                          slopes=_alibi(8)),
        grid=(b,),
        in_specs=[per_b(t, 512), per_b(t, 512), per_b(t, LANES),
                  per_b(past, LANES), per_b(past, LANES), per_b(past, HEAD_DIM),
                  per_b(t, LANES), per_b(t, LANES), per_b(t, HEAD_DIM)],
        out_specs=per_b(t, 512),
        out_shape=jax.ShapeDtypeStruct((b, t, 512), BF16),
        scratch_shapes=[pltpu.VMEM((n_past, t, kb_size), I32), pltpu.VMEM((t, t), I32)],
        compiler_params=_cparams(("parallel",)),
        name="dsa_sample",
    )(q, qi, wi, k_cache, v_cache, ki_cache, k_new, v_new, ki_new)


def _pad_cols(w, total):
    return jnp.pad(w, ((0, 0), (0, total - w.shape[1])))


def _q_perm():
    idx = []
    for s in range(4):
        idx.extend(range(s * HEAD_DIM, (s + 1) * HEAD_DIM))
        idx.extend(range((4 + s) * HEAD_DIM, (5 + s) * HEAD_DIM))
    return np.asarray(idx, np.int32)


def _even_layer(xp, xs, cache_k, cache_v, shift_s, wkv_s, w_in, sink, rw, w_out, ffn, alpha, sizes):
    (mu, w0, w_up, a0, a_up, g_up, k_k, k_a, r_k, lnx_g, lnx_b) = rw
    perm = _q_perm()
    w_in_p = jnp.concatenate([w_in[:, :512][:, perm], w_in[:, 512:]], axis=1).astype(BF16)
    w_out_p = jnp.concatenate([w_out[:512][perm], w_out[512:]], axis=0).astype(BF16)
    groups = ((0, 512, ((0, 512, HEAD_DIM ** -0.5),)),
              (512, 128, ((1, 128, 1.0), (2, 128, 1.0))),
              (640, 128, ((3, 128, 1.0), (4, 128, 1.0))),
              (768, 1792, ((5, 1792, 1.0),)))
    out_defs = ((512, BF16), (128, F32), (128, BF16), (128, F32), (128, BF16), (1792, F32))
    wup_p = jnp.concatenate([w_up, jnp.zeros_like(w_up)], axis=0).astype(BF16)
    aup_p = jnp.concatenate([jnp.zeros_like(a_up), a_up], axis=0).astype(BF16)
    gup = g_up.astype(BF16)
    row = lambda v: v.reshape(1, -1)
    g1, b1, g2, b2, wu, wd = ffn

    def run(x, state, tm, qb, scan_tb, prep_tm):
        b, t, d = x.shape
        n = b * t
        qa, ka, ka_bf, va, va_bf, pb = _project(x.reshape(n, d), w_in_p, groups, out_defs, tm)
        r3 = lambda a: a.reshape(b, t, a.shape[-1])
        if state is None:
            oa = _swa(r3(qa), r3(ka_bf), r3(ka_bf), r3(va_bf), r3(va_bf), sink,
                      qb=qb, prev_is_cache=False, pos_base=0)
            k_buf = r3(ka)[:, -WINDOW:]
            v_buf = r3(va)[:, -WINDOW:]
            shift = jnp.zeros((b, 1, pb.shape[-1]), F32)
            s0 = jnp.zeros((b, 8, HEAD_DIM, HEAD_DIM), F32)
        else:
            ck, cv, shift, s0 = state
            ck2 = ck.reshape(b, WINDOW, LANES)
            cv2 = cv.reshape(b, WINDOW, LANES)
            oa = _swa(r3(qa), ck2, r3(ka_bf), cv2, r3(va_bf), sink,
                      qb=qb, prev_is_cache=True, pos_base=sizes["past"])
            k_buf = jnp.concatenate([ck2, r3(ka)], axis=1)[:, -WINDOW:]
            v_buf = jnp.concatenate([cv2, r3(va)], axis=1)[:, -WINDOW:]
        pb3 = r3(pb)
        rr, ww, kk2, vv, kkn, bb, gg, bonus = _rwkv_prep(
            pb3, shift, row(mu), row(w0), wup_p, row(a0), aup_p, gup, row(k_k), row(k_a), row(r_k), prep_tm)
        y, s_bd = _rwkv_scan(rr, ww, kk2, vv, kkn, bb, _state_to_blockdiag(s0), scan_tb)
        s_t = _state_from_blockdiag(s_bd)
        ob = _rwkv_post(y, bonus, gg, row(lnx_g), row(lnx_b), prep_tm)
        xo = _out_ffn(x.reshape(n, d), oa.reshape(n, 512), ob.reshape(n, 512), w_out_p, row(g1), row(b1),
                      wu.astype(BF16), wd.astype(BF16), row(g2), row(b2), alpha, tm)
        st = (k_buf.reshape(b, WINDOW, 2, HEAD_DIM), v_buf.reshape(b, WINDOW, 2, HEAD_DIM),
              pb3[:, -1:], s_t)
        return xo.reshape(b, t, d), st

    xp2, st_p = run(xp, None, sizes["tm_p"], sizes["swa_qb"], sizes["scan_tb_p"], sizes["prep_tm_p"])
    xs2, st_s = run(xs, (cache_k, cache_v, shift_s, wkv_s), sizes["tm_s"], xs.shape[1], xs.shape[1],
                    xs.shape[1])
    return xp2, xs2, st_p, st_s


def _odd_layer(xp, xs, caches, w_in, lam_vecs, subln_g, w_out, ffn, alpha, lam_init, sizes):
    perm = _q_perm()
    scale = HEAD_DIM ** -0.5
    w_p = jnp.concatenate([
        w_in[:, :1536], w_in[:, 1536:2048][:, perm], w_in[:, 2048:2816],
        _pad_cols(w_in[:, 2816:2880], LANES), _pad_cols(w_in[:, 2880:2888], LANES)], axis=1).astype(BF16)
    w_out_p = jnp.concatenate([w_out[:512], w_out[512:][perm]], axis=0).astype(BF16)
    groups = ((0, 512, ((0, 512, scale),)),
              (512, 512, ((1, 512, 1.0), (2, 512, 1.0))),
              (1024, 512, ((3, 512, 1.0), (4, 512, 1.0))),
              (1536, 512, ((5, 512, scale),)),
              (2048, 128, ((6, 128, 1.0), (7, 128, 1.0))),
              (2176, 128, ((8, 128, 1.0), (9, 128, 1.0))),
              (2304, 512, ((10, 512, scale),)),
              (2816, 128, ((11, 64, 1.0), (12, 64, 1.0))),
              (2944, 128, ((13, 128, 1.0),)))
    out_defs = ((512, BF16), (512, F32), (512, BF16), (512, F32), (512, BF16), (512, BF16),
                (128, F32), (128, BF16), (128, F32), (128, BF16), (512, BF16),
                (64, F32), (64, BF16), (128, F32))
    row = lambda v: v.reshape(1, -1)
    g1, b1, g2, b2, wu, wd = ffn
    lamv = jnp.stack(lam_vecs, axis=0)
    gain = row(subln_g)

    def run(x, state, tm):
        b, t, d = x.shape
        n = b * t
        (qc, kc, kc_bf, vc, vc_bf, qd, kd, kd_bf, vd, vd_bf, qi, ki, ki_bf, wi) = _project(
            x.reshape(n, d), w_p, groups, out_defs, tm)
        r3 = lambda a: a.reshape(b, t, a.shape[-1])
        if state is None:
            oc = _diff_prompt(r3(qc), r3(kc_bf), r3(vc_bf), lamv, gain, lam_init, sizes["diff_qb"])
            od = _dsa_prompt(r3(qd), r3(qi), r3(wi), r3(kd_bf), r3(vd_bf), r3(ki_bf), sizes["dsa_qb"])
        else:
            c_k, c_v, d_k, d_v, d_ki = state
            past = c_k.shape[1]
            oc = _diff_sample(r3(qc), c_k.reshape(b, past, 512), c_v.reshape(b, past, 512),
                              r3(kc_bf), r3(vc_bf), lamv, gain, lam_init, sizes["diff_kb_s"])
            od = _dsa_sample(r3(qd), r3(qi), r3(wi), d_k.reshape(b, past, LANES), d_v.reshape(b, past, LANES),
                             d_ki, r3(kd_bf), r3(vd_bf), r3(ki_bf), sizes["dsa_kb_s"])
        xo = _out_ffn(x.reshape(n, d), oc.reshape(n, 512), od.reshape(n, 512), w_out_p, row(g1), row(b1),
                      wu.astype(BF16), wd.astype(BF16), row(g2), row(b2), alpha, tm)
        rows = (kc.reshape(b, t, 4, 2, HEAD_DIM), vc.reshape(b, t, 4, 2 * HEAD_DIM),
                kd.reshape(b, t, 2, HEAD_DIM), vd.reshape(b, t, 2, HEAD_DIM), ki.reshape(b, t, HEAD_DIM))
        return xo.reshape(b, t, d), rows

    xp2, st_p = run(xp, None, sizes["tm_p"])
    xs2, st_s = run(xs, caches, sizes["tm_s"])
    return xp2, xs2, st_p, st_s


def _sizes(xp, xs, past):
    return dict(tm_p=min(512, xp.shape[0] * xp.shape[1]), tm_s=min(512, xs.shape[0] * xs.shape[1]),
                swa_qb=min(256, xp.shape[1]), scan_tb_p=min(128, xp.shape[1]),
                prep_tm_p=min(256, xp.shape[1]), diff_qb=min(256, xp.shape[1]),
                dsa_qb=min(256, xp.shape[1]), diff_kb_s=min(1024, past), dsa_kb_s=min(512, past),
                past=past)


def kernel(x_prompt, x_sample, cache_a_k, cache_a_v, state_b_shift, state_b_wkv, cache_c_k, cache_c_v, cache_d_k, cache_d_v, cache_d_kidx, w_in_even, sink_a, mu_b, w0_b, w_up_b, a0_b, a_up_b, g_up_b, k_k_b, k_a_b, r_k_b, lnx_g_b, lnx_b_b, w_out_even, w_in_odd, lam_q1_c, lam_k1_c, lam_q2_c, lam_k2_c, subln_g_c, w_out_odd, ln_mix_g, ln_mix_b, ln_ffn_g, ln_ffn_b, w_ff_up, w_ff_down):
    depth = ln_mix_g.shape[0]
    alpha = (2 * depth) ** 0.25
    past = cache_c_k.shape[2]
    sizes = _sizes(x_prompt, x_sample, past)
    xp, xs = x_prompt, x_sample
    even_p, even_s, odd_p, odd_s = [], [], [], []
    for layer in range(depth):
        i = layer // 2
        ffn = (ln_mix_g[layer], ln_mix_b[layer], ln_ffn_g[layer], ln_ffn_b[layer],
               w_ff_up[layer], w_ff_down[layer])
        if layer % 2 == 0:
            rw = (mu_b[i], w0_b[i], w_up_b[i], a0_b[i], a_up_b[i], g_up_b[i], k_k_b[i], k_a_b[i],
                  r_k_b[i].reshape(-1), lnx_g_b[i], lnx_b_b[i])
            xp, xs, st_p, st_s = _even_layer(
                xp, xs, cache_a_k[i], cache_a_v[i], state_b_shift[i], state_b_wkv[i],
                w_in_even[i], sink_a[i], rw, w_out_even[i], ffn, alpha, sizes)
            even_p.append(st_p)
            even_s.append(st_s)
        else:
            lam_init = 0.8 - 0.6 * math.exp(-0.3 * layer)
            xp, xs, st_p, st_s = _odd_layer(
                xp, xs, (cache_c_k[i], cache_c_v[i], cache_d_k[i], cache_d_v[i], cache_d_kidx[i]),
                w_in_odd[i], (lam_q1_c[i], lam_k1_c[i], lam_q2_c[i], lam_k2_c[i]), subln_g_c[i],
                w_out_odd[i], ffn, alpha, lam_init, sizes)
            odd_p.append(st_p)
            odd_s.append(st_s)
    stack = lambda states: [jnp.stack(z, axis=0) for z in zip(*states)]
    a_k_p, a_v_p, b_shift_p, b_wkv_p = stack(even_p)
    a_k_s, a_v_s, b_shift_s, b_wkv_s = stack(even_s)
    c_k_p, c_v_p, d_k_p, d_v_p, d_kidx_p = stack(odd_p)
    c_k_s, c_v_s, d_k_s, d_v_s, d_kidx_s = stack(odd_s)
    return (xp, xs, a_k_p, a_v_p, b_shift_p, b_wkv_p, c_k_p, c_v_p, d_k_p, d_v_p, d_kidx_p,
            a_k_s, a_v_s, b_shift_s, b_wkv_s, c_k_s, c_v_s, d_k_s, d_v_s, d_kidx_s)
```

```python
import functools
import math

import numpy as np
import jax
import jax.numpy as jnp
from jax import lax
from jax.experimental import pallas as pl
from jax.experimental.pallas import tpu as pltpu

F32 = jnp.float32
BF16 = jnp.bfloat16
I32 = jnp.int32

LANES = 128
HEAD_DIM = 64
CHUNK = 64
LN_EPS = 1e-5
RWKV_GN_EPS = 64e-5
WINDOW = 128
TOPK_MAX = 256
NEG_INF = float("-inf")
INT_MIN = -(2 ** 31)
VMEM_LIMIT = 56 * 1024 * 1024


def _cparams(sem):
    return pltpu.CompilerParams(dimension_semantics=sem, vmem_limit_bytes=VMEM_LIMIT)


def _alibi(n):
    return [float(2.0 ** (-8.0 * (i + 1) / n)) for i in range(n)]


def _dot(a, b):
    return jnp.dot(a, b, preferred_element_type=F32)


def _dot_nt(a, b):
    return lax.dot_general(a, b, (((1,), (1,)), ((), ())), preferred_element_type=F32)


def _lane_left(shape):
    return lax.broadcasted_iota(I32, shape, len(shape) - 1) < HEAD_DIM


def _segsum64(x):
    left = _lane_left(x.shape)
    sl = jnp.sum(jnp.where(left, x, 0.0), axis=-1, keepdims=True)
    sr = jnp.sum(jnp.where(left, 0.0, x), axis=-1, keepdims=True)
    return jnp.where(left, sl, sr)


def _proj_kernel(x_ref, w_ref, *out_refs, groups):
    xb = x_ref[...].astype(BF16)
    for off, width, outs in groups:
        r = _dot(xb, w_ref[:, off:off + width])
        for idx, ow, scale in outs:
            v = r if ow == width else r[:, :ow]
            if scale != 1.0:
                v = v * scale
            out_refs[idx][...] = v.astype(out_refs[idx].dtype)


def _project(x, w_bf, groups, out_defs, tm):
    n, d = x.shape
    cols = w_bf.shape[1]
    out_shape = [jax.ShapeDtypeStruct((n, ow), dt) for ow, dt in out_defs]
    out_specs = [pl.BlockSpec((tm, ow), lambda i: (i, 0)) for ow, _ in out_defs]
    return pl.pallas_call(
        functools.partial(_proj_kernel, groups=groups),
        grid=(n // tm,),
        in_specs=[pl.BlockSpec((tm, d), lambda i: (i, 0)),
                  pl.BlockSpec((d, cols), lambda i: (0, 0))],
        out_specs=out_specs,
        out_shape=out_shape,
        compiler_params=_cparams(("parallel",)),
        name="in_proj",
    )(x, w_bf)


def _swa_kernel(sink_ref, q_ref, kp_ref, kc_ref, vp_ref, vc_ref, o_ref, *, qb, pos_base, slopes):
    i = pl.program_id(1)
    q0 = pos_base + i * qb
    kp = kp_ref[0].astype(BF16)
    kc = kc_ref[0].astype(BF16)
    vp = vp_ref[0].astype(BF16)
    vc = vc_ref[0].astype(BF16)
    qpos = q0 + lax.broadcasted_iota(I32, (qb, 1), 0)
    kpos_p = q0 - WINDOW + lax.broadcasted_iota(I32, (1, WINDOW), 1)
    kpos_c = q0 + lax.broadcasted_iota(I32, (1, qb), 1)
    qch = qpos // CHUNK

    def allowed(kpos):
        kch = kpos // CHUNK
        return (kpos >= 0) & (kch >= qch - WINDOW // CHUNK) & (kch <= qch)

    al_p, al_c = allowed(kpos_p), allowed(kpos_c)
    dist_p = jnp.abs(qpos - kpos_p).astype(F32)
    dist_c = jnp.abs(qpos - kpos_c).astype(F32)
    for s in range(4):
        slab = q_ref[0, :, s * LANES:(s + 1) * LANES]
        left = _lane_left(slab.shape)
        res = []
        for side in range(2):
            h = 4 * side + s
            qx = jnp.where(left if side == 0 else jnp.logical_not(left), slab, jnp.zeros_like(slab))
            lp = jnp.where(al_p, _dot_nt(qx, kp) - slopes[h] * dist_p, NEG_INF)
            lc = jnp.where(al_c, _dot_nt(qx, kc) - slopes[h] * dist_c, NEG_INF)
            sk = sink_ref[h]
            m = jnp.maximum(jnp.maximum(jnp.max(lp, -1, keepdims=True), jnp.max(lc, -1, keepdims=True)), sk)
            pp = jnp.exp(lp - m)
            pc = jnp.exp(lc - m)
            den = jnp.sum(pp, -1, keepdims=True) + jnp.sum(pc, -1, keepdims=True) + jnp.exp(sk - m)
            o = _dot(pp.astype(BF16), vp) + _dot(pc.astype(BF16), vc)
            res.append(o / den)
        o_ref[0, :, s * LANES:(s + 1) * LANES] = jnp.where(left, res[0], res[1]).astype(o_ref.dtype)


def _swa(q, k_prev_src, k_cur, v_prev_src, v_cur, sink, *, qb, prev_is_cache, pos_base):
    b, t, _ = q.shape
    nq = t // qb
    if prev_is_cache:
        prev_map = lambda bi, i: (bi, 0, 0)
    else:
        r = qb // WINDOW
        prev_map = lambda bi, i: (bi, jnp.maximum(i * r - 1, 0), 0)
    cur_map = lambda bi, i: (bi, i, 0)
    return pl.pallas_call(
        functools.partial(_swa_kernel, qb=qb, pos_base=pos_base, slopes=_alibi(8)),
        grid=(b, nq),
        in_specs=[pl.BlockSpec(memory_space=pltpu.SMEM),
                  pl.BlockSpec((1, qb, 512), cur_map),
                  pl.BlockSpec((1, WINDOW, LANES), prev_map),
                  pl.BlockSpec((1, qb, LANES), cur_map),
                  pl.BlockSpec((1, WINDOW, LANES), prev_map),
                  pl.BlockSpec((1, qb, LANES), cur_map)],
        out_specs=pl.BlockSpec((1, qb, 512), cur_map),
        out_shape=jax.ShapeDtypeStruct((b, t, 512), BF16),
        compiler_params=_cparams(("parallel", "parallel")),
        name="swa_sink",
    )(sink, q, k_prev_src, k_cur, v_prev_src, v_cur)


def _sigmoid(x):
    return 1.0 / (1.0 + jnp.exp(-x))


def _softplus(x):
    return jnp.maximum(x, 0.0) + jnp.log(1.0 + jnp.exp(-jnp.abs(x)))


def _rwkv_prep_kernel(pb_ref, pbprev_ref, shift_ref, mu_ref, w0_ref, wup_ref, a0_ref, aup_ref, gup_ref,
                      kk_ref, ka_ref, rk_ref,
                      r_out, w_out, k_out, v_out, kk_out, b_out, g_out, bonus_out):
    i = pl.program_id(1)
    pb = pb_ref[0]
    tm = pb.shape[0]
    prev_row = jnp.where(i == 0, shift_ref[0], pbprev_ref[0, 7:8, :])
    rolled = pltpu.roll(pb, 1, axis=0)
    row = lax.broadcasted_iota(I32, (tm, 1), 0)
    prev = jnp.where(row == 0, prev_row, rolled)
    xm = pb + (prev - pb) * mu_ref[...]
    r = xm[:, 0:512]
    k = xm[:, 512:1024]
    v = xm[:, 1024:1536]
    wa = xm[:, 1536:1664]
    gl = xm[:, 1664:1792]
    lw = _dot(jnp.tanh(wa).astype(BF16), wup_ref[...])
    la = _dot(wa.astype(BF16), aup_ref[...])
    w_log = -_softplus(-(w0_ref[...] + lw)) - 0.5
    log_decay = -jnp.exp(w_log)
    a = _sigmoid(a0_ref[...] + la)
    g = _dot(_sigmoid(gl).astype(BF16), gup_ref[...])
    kk = k * kk_ref[...]
    k2 = k * (1.0 + (a - 1.0) * ka_ref[...])
    rkk = r * k2 * rk_ref[...]
    for p in range(4):
        sl = slice(p * LANES, (p + 1) * LANES)
        kkp = kk[:, sl]
        kkn = kkp * lax.rsqrt(_segsum64(kkp * kkp) + 1e-12)
        kk_out[0, p] = kkn
        b_out[0, p] = kkn * a[:, sl]
        bonus_out[0, :, sl] = _segsum64(rkk[:, sl]) * v[:, sl]
        r_out[0, p] = r[:, sl]
        w_out[0, p] = log_decay[:, sl]
        k_out[0, p] = k2[:, sl]
        v_out[0, p] = v[:, sl]
    g_out[0] = g


def _rwkv_prep(pb, shift, mu, w0, wup, a0, aup, gup, k_k, k_a, r_k, tm):
    b, t, c = pb.shape
    nt = t // tm
    cur = lambda bi, i: (bi, i, 0)
    vec = lambda n: pl.BlockSpec((1, n), lambda bi, i: (0, 0))
    pair = pl.BlockSpec((1, 4, tm, LANES), lambda bi, i: (bi, 0, i, 0))
    outs = [jax.ShapeDtypeStruct((b, 4, t, LANES), F32)] * 6 + [jax.ShapeDtypeStruct((b, t, 512), F32)] * 2
    return pl.pallas_call(
        _rwkv_prep_kernel,
        grid=(b, nt),
        in_specs=[pl.BlockSpec((1, tm, c), cur),
                  pl.BlockSpec((1, 8, c), lambda bi, i: (bi, jnp.maximum(i * (tm // 8) - 1, 0), 0)),
                  pl.BlockSpec((1, 1, c), lambda bi, i: (bi, 0, 0)),
                  vec(c), vec(512),
                  pl.BlockSpec((LANES, 512), lambda bi, i: (0, 0)),
                  vec(512),
                  pl.BlockSpec((LANES, 512), lambda bi, i: (0, 0)),
                  pl.BlockSpec((LANES, 512), lambda bi, i: (0, 0)),
                  vec(512), vec(512), vec(512)],
        out_specs=[pair] * 6 + [pl.BlockSpec((1, tm, 512), cur)] * 2,
        out_shape=outs,
        compiler_params=_cparams(("parallel", "parallel")),
        name="rwkv_prep",
    )(pb, pb, shift, mu, w0, wup, a0, aup, gup, k_k, k_a, r_k)


RWKV_CHUNK = 16
RWKV_PAIRS_PER_STEP = 2


def _split2(x):
    hi = x.astype(BF16)
    return hi, (x - hi.astype(F32)).astype(BF16)


def _dot3(a, b):
    return _dot(a[0], b[0]) + (_dot(a[0], b[1]) + _dot(a[1], b[0]))


def _rwkv_scan_kernel(r_ref, lw_ref, k_ref, v_ref, kk_ref, b_ref, s0_ref, y_ref, sT_ref,
                      st_scr, lhi_scr, llo_scr, h_scr, y0_scr, *, tb):
    ti = pl.program_id(1)
    c_sz = RWKV_CHUNK
    nc = tb // c_sz

    @pl.when(ti == 0)
    def _():
        st_scr[...] = s0_ref[0]

    row = lax.broadcasted_iota(I32, (tb, tb), 0)
    col = lax.broadcasted_iota(I32, (tb, tb), 1)
    same = (row // c_sz) == (col // c_sz)
    strict = same & (col < row)
    incl = same & (col <= row)
    eye_t = jnp.where(row == col, 1.0, 0.0)
    cum_lhs = jnp.concatenate([jnp.where(incl, 1.0, 0.0), jnp.where(same, 1.0, 0.0)], axis=0).astype(BF16)
    left = _lane_left((tb, LANES))
    r128 = lax.broadcasted_iota(I32, (LANES, LANES), 0)
    c128 = lax.broadcasted_iota(I32, (LANES, LANES), 1)
    blockdiag = (r128 < HEAD_DIM) == (c128 < HEAD_DIM)
    diag128 = r128 == c128
    tcol = lax.broadcasted_iota(I32, (LANES, tb), 1)

    left2 = jnp.concatenate([left, left], axis=1)
    npar = RWKV_PAIRS_PER_STEP

    def prepare(i, carry):
        ps = [i * npar + q for q in range(npar)]
        qs = range(npar)
        sides = range(2)
        ins = [[ref[0, p] for ref in (r_ref, lw_ref, k_ref, v_ref, kk_ref, b_ref)] for p in ps]

        def split3(x):
            x1 = x.astype(BF16)
            rem = x - x1.astype(F32)
            x2 = rem.astype(BF16)
            return x1, x2, (rem - x2.astype(F32)).astype(BF16)

        lws = [split3(x[1]) for x in ins]
        cums = [_dot(cum_lhs, l[0]) + (_dot(cum_lhs, l[1]) + _dot(cum_lhs, l[2])) for l in lws]
        al, rt, vs, rhs, lhs, bbt, kbt, pc_rows = [], [], [], [], [], [], [], []
        for q in qs:
            r, lw, k, v, kk, b = ins[q]
            cs, cse = cums[q][:tb], cums[q][tb:]
            p_inv = jnp.exp(-cs)
            p_end = jnp.exp(cse - cs)
            al.append(-kk * jnp.exp(cs - lw))
            rt.append(r * jnp.exp(cs))
            vs.append(_split2(v))
            rhs.append(_split2(jnp.concatenate([b * p_inv, k * p_inv], axis=0)))
            zero = jnp.zeros_like(v)
            half = lambda z, s: jnp.where(left, z, zero) if s == 0 else jnp.where(left, zero, z)
            lhs.append([_split2(jnp.concatenate([half(al[q], s), half(rt[q], s)], axis=0)) for s in sides])
            bbt.append(_split2((b * p_end).T))
            kbt.append(_split2((k * p_end).T))
            pc_rows.append(jnp.exp(cse))
        gram = [[_dot_nt(lhs[q][s][0], rhs[q][0]) + (_dot_nt(lhs[q][s][0], rhs[q][1])
                                                      + _dot_nt(lhs[q][s][1], rhs[q][0]))
                 for s in sides] for q in qs]
        a1 = [[_split2(jnp.where(strict, gram[q][s][:tb, :tb], 0.0)) for s in sides] for q in qs]
        a2 = [[_split2(jnp.where(strict, gram[q][s][:tb, tb:], 0.0)) for s in sides] for q in qs]
        b1 = [[_split2(jnp.where(incl, gram[q][s][tb:, :tb], 0.0)) for s in sides] for q in qs]
        b2 = [[_split2(jnp.where(incl, gram[q][s][tb:, tb:], 0.0)) for s in sides] for q in qs]
        xv = [jnp.where(left, _dot3(a2[q][0], vs[q]), _dot3(a2[q][1], vs[q])) for q in qs]
        pw = [[_dot3(a1[q][s], a1[q][s]) for s in sides] for q in qs]
        x = [jnp.concatenate([al[q], xv[q]], axis=1) for q in qs]
        m = a1
        for step in range(4):
            xs = [_split2(x[q]) for q in qs]
            x = [x[q] + jnp.where(left2, _dot3(m[q][0], xs[q]), _dot3(m[q][1], xs[q])) for q in qs]
            if step < 3:
                m = [[_split2(pw[q][s]) for s in sides] for q in qs]
                if step < 2:
                    pw = [[_dot3(m[q][s], m[q][s]) for s in sides] for q in qs]
        xs = [_split2(x[q]) for q in qs]
        bx = [jnp.where(left2, _dot(b1[q][0][0], xs[q][0]), _dot(b1[q][1][0], xs[q][0])) for q in qs]
        bv = [jnp.where(left, _dot(b2[q][0][0], vs[q][0]), _dot(b2[q][1][0], vs[q][0])) for q in qs]
        chunk_rows = lambda z: jnp.concatenate(
            [jnp.where((tcol // c_sz) == c, z, jnp.zeros_like(z)) for c in range(nc)], axis=0)
        g_all, h_all = [], []
        for q in qs:
            ahs = tuple(z[:, :LANES] for z in xs[q])
            u0s = tuple(z[:, LANES:] for z in xs[q])
            bsel = tuple(chunk_rows(z) for z in bbt[q])
            ksel = tuple(chunk_rows(z) for z in kbt[q])
            bk = tuple(jnp.concatenate([bs, ks], axis=1) for bs, ks in zip(bsel, ksel))
            uv = tuple(jnp.concatenate([us, vv], axis=0) for us, vv in zip(u0s, vs[q]))
            g_all.append(_dot3(bsel, ahs))
            h_all.append(_dot3(bk, uv))
        for q in qs:
            p = ps[q]
            rh_hi, rh_lo = _split2(rt[q] + bx[q][:, :LANES])
            y0_scr[p] = bx[q][:, LANES:] + bv[q]
            for c in range(nc):
                blk = slice(c * LANES, (c + 1) * LANES)
                pc = pc_rows[q][c * c_sz:c * c_sz + 1]
                g = jnp.where(diag128, pc, 0.0) + jnp.where(blockdiag, g_all[q][blk], 0.0)
                h_scr[p, c] = jnp.where(blockdiag, h_all[q][blk], 0.0)
                g_hi, g_lo = _split2(g)
                rows = slice(c * c_sz, (c + 1) * c_sz)
                lhi_scr[p, c] = jnp.concatenate([g_hi, rh_hi[rows]], axis=0)
                llo_scr[p, c] = jnp.concatenate([g_lo, rh_lo[rows]], axis=0)
        return carry

    lax.fori_loop(0, 4 // npar, prepare, 0)

    st = [st_scr[p] for p in range(4)]
    for c in range(nc):
        rows = slice(c * c_sz, (c + 1) * c_sz)
        for p in range(4):
            res = _dot3((lhi_scr[p, c], llo_scr[p, c]), _split2(st[p]))
            y_ref[0, p, rows, :] = res[LANES:] + y0_scr[p, rows, :]
            st[p] = res[:LANES] + h_scr[p, c]
    for p in range(4):
        st_scr[p] = st[p]

    @pl.when(ti == pl.num_programs(1) - 1)
    def _():
        sT_ref[0] = st_scr[...]


def _rwkv_scan(r, lw, k, v, kk, bb, s0_bd, tb):
    b, _, t, _ = r.shape
    nc = tb // RWKV_CHUNK
    cur = pl.BlockSpec((1, 4, tb, LANES), lambda bi, i: (bi, 0, i, 0))
    st_spec = pl.BlockSpec((1, 4, LANES, LANES), lambda bi, i: (bi, 0, 0, 0))
    return pl.pallas_call(
        functools.partial(_rwkv_scan_kernel, tb=tb),
        grid=(b, t // tb),
        in_specs=[cur] * 6 + [st_spec],
        out_specs=[cur, st_spec],
        out_shape=[jax.ShapeDtypeStruct((b, 4, t, LANES), F32),
                   jax.ShapeDtypeStruct((b, 4, LANES, LANES), F32)],
        scratch_shapes=[pltpu.VMEM((4, LANES, LANES), F32),
                        pltpu.VMEM((4, nc, LANES + RWKV_CHUNK, LANES), BF16),
                        pltpu.VMEM((4, nc, LANES + RWKV_CHUNK, LANES), BF16),
                        pltpu.VMEM((4, nc, LANES, LANES), F32),
                        pltpu.VMEM((4, tb, LANES), F32)],
        compiler_params=_cparams(("parallel", "arbitrary")),
        name="rwkv_scan",
    )(r, lw, k, v, kk, bb, s0_bd)


def _state_to_blockdiag(s):
    b = s.shape[0]
    st = jnp.swapaxes(s, -1, -2).reshape(b, 4, 2, HEAD_DIM, HEAD_DIM)
    z = jnp.zeros_like(st[:, :, 0])
    top = jnp.concatenate([st[:, :, 0], z], axis=-1)
    bot = jnp.concatenate([z, st[:, :, 1]], axis=-1)
    return jnp.concatenate([top, bot], axis=-2)


def _state_from_blockdiag(s_bd):
    b = s_bd.shape[0]
    a = s_bd[:, :, :HEAD_DIM, :HEAD_DIM]
    d = s_bd[:, :, HEAD_DIM:, HEAD_DIM:]
    st = jnp.stack([a, d], axis=2).reshape(b, 8, HEAD_DIM, HEAD_DIM)
    return jnp.swapaxes(st, -1, -2)


def _rwkv_post_kernel(y_ref, bonus_ref, g_ref, lg_ref, lb_ref, o_ref):
    for p in range(4):
        sl = slice(p * LANES, (p + 1) * LANES)
        y = y_ref[0, p]
        d = y - _segsum64(y) * (1.0 / HEAD_DIM)
        var = _segsum64(d * d) * (1.0 / HEAD_DIM)
        yn = d * lax.rsqrt(var + RWKV_GN_EPS) * lg_ref[:, sl] + lb_ref[:, sl]
        o_ref[0, :, sl] = ((yn + bonus_ref[0, :, sl]) * g_ref[0, :, sl]).astype(o_ref.dtype)


def _rwkv_post(y, bonus, g, lnx_g, lnx_b, tm):
    b, _, t, _ = y.shape
    tile = pl.BlockSpec((1, tm, 512), lambda bi, i: (bi, i, 0))
    vec = pl.BlockSpec((1, 512), lambda bi, i: (0, 0))
    return pl.pallas_call(
        _rwkv_post_kernel,
        grid=(b, t // tm),
        in_specs=[pl.BlockSpec((1, 4, tm, LANES), lambda bi, i: (bi, 0, i, 0)), tile, tile, vec, vec],
        out_specs=tile,
        out_shape=jax.ShapeDtypeStruct((b, t, 512), BF16),
        compiler_params=_cparams(("parallel", "parallel")),
        name="rwkv_post",
    )(y, bonus, g, lnx_g, lnx_b)


def _layer_norm(x, g, b):
    mu = jnp.mean(x, -1, keepdims=True)
    d = x - mu
    var = jnp.mean(d * d, -1, keepdims=True)
    return d * lax.rsqrt(var + LN_EPS) * g + b


def _ffn_kernel(x_ref, m1_ref, m2_ref, wo_ref, g1_ref, b1_ref, wu_ref, wd_ref, g2_ref, b2_ref, o_ref,
                *, alpha, ff_chunk):
    half = m1_ref.shape[1]
    mix = _dot(m1_ref[...], wo_ref[0:half, :]) + _dot(m2_ref[...], wo_ref[half:, :])
    x1 = _layer_norm(alpha * x_ref[...] + mix, g1_ref[...], b1_ref[...])
    x1b = x1.astype(BF16)
    d_ff = wu_ref.shape[1]
    h = jnp.zeros_like(x1)
    for c in range(d_ff // ff_chunk):
        u = jnp.maximum(_dot(x1b, wu_ref[:, c * ff_chunk:(c + 1) * ff_chunk]), 0.0)
        h = h + _dot((u * u).astype(BF16), wd_ref[c * ff_chunk:(c + 1) * ff_chunk, :])
    o_ref[...] = _layer_norm(alpha * x1 + h, g2_ref[...], b2_ref[...])


def _const_spec(shape):
    return pl.BlockSpec(shape, lambda i: (0,) * len(shape), pipeline_mode=pl.Buffered(1))


def _out_ffn(x, m1, m2, wo, g1, b1, wu, wd, g2, b2, alpha, tm):
    n, d = x.shape
    dm = m1.shape[1]
    d_ff = wu.shape[1]
    tile = lambda w: pl.BlockSpec((tm, w), lambda i: (i, 0))
    return pl.pallas_call(
        functools.partial(_ffn_kernel, alpha=alpha, ff_chunk=1024),
        grid=(n // tm,),
        in_specs=[tile(d), tile(dm), tile(dm),
                  _const_spec((2 * dm, d)), _const_spec((1, d)), _const_spec((1, d)),
                  _const_spec((d, d_ff)), _const_spec((d_ff, d)), _const_spec((1, d)), _const_spec((1, d))],
        out_specs=tile(d),
        out_shape=jax.ShapeDtypeStruct((n, d), F32),
        compiler_params=_cparams(("parallel",)),
        name="out_ffn",
    )(x, m1, m2, wo, g1, b1, wu, wd, g2, b2)


def _diff_lambda(lam_ref, lam_init):
    lv = lam_ref[...]
    s1 = jnp.sum(lv[0:1] * lv[1:2], axis=-1, keepdims=True)
    s2 = jnp.sum(lv[2:3] * lv[3:4], axis=-1, keepdims=True)
    return jnp.exp(s1) - jnp.exp(s2) + lam_init


def _split_halves(qh):
    left = _lane_left(qh.shape)
    zero = jnp.zeros_like(qh)
    return jnp.concatenate([jnp.where(left, qh, zero), jnp.where(left, zero, qh)], axis=0)


def _flash_update(carry, s3, vblk):
    m, l, acc = carry
    g, rws, kb = s3.shape
    m_new = jnp.maximum(m, jnp.max(s3, -1, keepdims=True))
    m_use = jnp.where(m_new == NEG_INF, 0.0, m_new)
    corr = jnp.exp(m - m_use)
    p = jnp.exp(s3 - m_use)
    l = l * corr + jnp.sum(p, -1, keepdims=True)
    pv = _dot(p.reshape(g * rws, kb).astype(BF16), vblk).reshape(g, rws, vblk.shape[1])
    return m_new, l, acc * corr + pv


def _flash_init(g, rws, e):
    return (jnp.full((g, rws, 1), NEG_INF, F32), jnp.zeros((g, rws, 1), F32), jnp.zeros((g, rws, e), F32))


def _diff_finish(carry, lam, gain, lam_init):
    m, l, acc = carry
    o = acc[0] / l[0] - lam * (acc[1] / l[1])
    o = o * lax.rsqrt(jnp.mean(o * o, -1, keepdims=True) + LN_EPS) * gain * (1.0 - lam_init)
    return o


def _diff_prompt_kernel(q_ref, k_ref, vt_ref, lam_ref, gain_ref, o_ref, q2_scr, *, qb, slopes, lam_init):
    i = pl.program_id(1)
    lam = _diff_lambda(lam_ref, lam_init)
    qpos = i * qb + lax.broadcasted_iota(I32, (1, qb), 1)
    for h in range(4):
        q2_scr[2 * h * qb:(2 * h + 2) * qb, :] = _split_halves(q_ref[0, :, h * LANES:(h + 1) * LANES])

    def body(kb, carry):
        m, l, acc = carry
        rows = pl.ds(pl.multiple_of(kb * qb, qb), qb)
        kpos = kb * qb + lax.broadcasted_iota(I32, (qb, 1), 0)
        dist = jnp.abs(qpos - kpos).astype(F32)
        neg = jnp.where(kpos // CHUNK <= qpos // CHUNK, 0.0, NEG_INF)
        st = [_dot_nt(k_ref[0, rows, h * LANES:(h + 1) * LANES], q2_scr[2 * h * qb:(2 * h + 2) * qb, :])
              for h in range(4)]
        s3 = jnp.stack([st[g // 2][:, (g % 2) * qb:(g % 2 + 1) * qb] - slopes[g // 2] * dist + neg
                        for g in range(8)], axis=0)
        m_new = jnp.maximum(m, jnp.max(s3, axis=1, keepdims=True))
        corr = jnp.exp(m - m_new)
        p = jnp.exp(s3 - m_new)
        l = l * corr + jnp.sum(p, axis=1, keepdims=True)
        pv = [_dot(vt_ref[0, kb, h * LANES:(h + 1) * LANES, :],
                   jnp.concatenate([p[2 * h].astype(BF16), p[2 * h + 1].astype(BF16)], axis=1))
              for h in range(4)]
        acc = acc * corr + jnp.stack([pv[g // 2][:, (g % 2) * qb:(g % 2 + 1) * qb] for g in range(8)], axis=0)
        return m_new, l, acc

    init = (jnp.full((8, 1, qb), NEG_INF, F32), jnp.zeros((8, 1, qb), F32), jnp.zeros((8, LANES, qb), F32))
    m, l, acc = lax.fori_loop(0, i + 1, body, init)
    for h in range(4):
        o = acc[2 * h] / l[2 * h] - lam * (acc[2 * h + 1] / l[2 * h + 1])
        o = o * lax.rsqrt(jnp.mean(o * o, axis=0, keepdims=True) + LN_EPS) * gain_ref[...] * (1.0 - lam_init)
        o_ref[0, :, h * LANES:(h + 1) * LANES] = o.T.astype(o_ref.dtype)


def _diff_prompt(q, k, v, lamv, gain, lam_init, qb):
    b, t, _ = q.shape
    nkb = t // qb
    vt = jnp.swapaxes(v.reshape(b, nkb, qb, 512), -1, -2)
    full = pl.BlockSpec((1, t, 512), lambda bi, i: (bi, 0, 0))
    tile = pl.BlockSpec((1, qb, 512), lambda bi, i: (bi, i, 0))
    return pl.pallas_call(
        functools.partial(_diff_prompt_kernel, qb=qb, slopes=_alibi(4), lam_init=lam_init),
        grid=(b, nkb),
        in_specs=[tile, full,
                  pl.BlockSpec((1, nkb, 512, qb), lambda bi, i: (bi, 0, 0, 0)),
                  pl.BlockSpec((4, HEAD_DIM), lambda bi, i: (0, 0)),
                  pl.BlockSpec((LANES, 1), lambda bi, i: (0, 0))],
        out_specs=tile,
        out_shape=jax.ShapeDtypeStruct((b, t, 512), BF16),
        scratch_shapes=[pltpu.VMEM((8 * qb, LANES), BF16)],
        compiler_params=_cparams(("parallel", "arbitrary")),
        name="diff_prompt",
    )(q, k, vt, lamv, gain.reshape(LANES, 1))


def _diff_sample_kernel(q_ref, kc_ref, vc_ref, kn_ref, vn_ref, lam_ref, gain_ref, o_ref,
                        m_scr, l_scr, acc_scr, *, t, kb_size, n_past, slopes, lam_init):
    j = pl.program_id(1)
    past_len = n_past * kb_size
    qpos = past_len + lax.broadcasted_iota(I32, (t, 1), 0)

    @pl.when(j == 0)
    def _():
        m_scr[...] = jnp.full(m_scr.shape, NEG_INF, F32)
        l_scr[...] = jnp.zeros(l_scr.shape, F32)
        acc_scr[...] = jnp.zeros(acc_scr.shape, F32)

    def run(kfull, vfull, kpos):
        dist = jnp.abs(qpos - kpos).astype(F32)
        for h in range(4):
            sl = slice(h * LANES, (h + 1) * LANES)
            q2 = _split_halves(q_ref[0, :, sl])
            s3 = _dot_nt(q2, kfull[:, sl]).reshape(2, t, kfull.shape[0]) - slopes[h] * dist[None]
            carry = (m_scr[h], l_scr[h], acc_scr[h])
            m, l, acc = _flash_update(carry, s3, vfull[:, sl])
            m_scr[h], l_scr[h], acc_scr[h] = m, l, acc

    @pl.when(j < n_past)
    def _():
        kpos = j * kb_size + lax.broadcasted_iota(I32, (1, kb_size), 1)
        run(kc_ref[0].astype(BF16), vc_ref[0].astype(BF16), kpos)

    @pl.when(j == n_past)
    def _():
        kpos = past_len + lax.broadcasted_iota(I32, (1, t), 1)
        run(kn_ref[0].astype(BF16), vn_ref[0].astype(BF16), kpos)
        lam = _diff_lambda(lam_ref, lam_init)
        for h in range(4):
            sl = slice(h * LANES, (h + 1) * LANES)
            carry = (m_scr[h], l_scr[h], acc_scr[h])
            o_ref[0, :, sl] = _diff_finish(carry, lam, gain_ref[...], lam_init).astype(o_ref.dtype)


def _diff_sample(q, k_cache, v_cache, k_new, v_new, lamv, gain, lam_init, kb_size):
    b, t, _ = q.shape
    n_past = k_cache.shape[1] // kb_size
    cache = pl.BlockSpec((1, kb_size, 512), lambda bi, j: (bi, jnp.minimum(j, n_past - 1), 0))
    new = pl.BlockSpec((1, t, 512), lambda bi, j: (bi, 0, 0))
    return pl.pallas_call(
        functools.partial(_diff_sample_kernel, t=t, kb_size=kb_size, n_past=n_past,
                          slopes=_alibi(4), lam_init=lam_init),
        grid=(b, n_past + 1),
        in_specs=[new, cache, cache, new, new,
                  pl.BlockSpec((4, HEAD_DIM), lambda bi, j: (0, 0)),
                  pl.BlockSpec((1, LANES), lambda bi, j: (0, 0))],
        out_specs=new,
        out_shape=jax.ShapeDtypeStruct((b, t, 512), BF16),
        scratch_shapes=[pltpu.VMEM((4, 2, t, 1), F32), pltpu.VMEM((4, 2, t, 1), F32),
                        pltpu.VMEM((4, 2, t, LANES), F32)],
        compiler_params=_cparams(("parallel", "arbitrary")),
        name="diff_sample",
    )(q, k_cache, v_cache, k_new, v_new, lamv, gain)


def _index_keys(qi, wi, ki_blk, allowed):
    acc = None
    for h in range(8):
        sc = _dot_nt(qi[:, h * HEAD_DIM:(h + 1) * HEAD_DIM], ki_blk)
        term = jnp.maximum(sc, 0.0) * wi[:, h:h + 1]
        acc = term if acc is None else acc + term
    idx = acc * (8.0 ** -0.5) + 0.0
    bits = pltpu.bitcast(idx, I32)
    key = jnp.where(bits < 0, bits ^ 0x7FFFFFFF, bits)
    return jnp.where(allowed, key, INT_MIN)


def _lane_fold(x):
    acc = x[:, :LANES]
    for j in range(1, x.shape[1] // LANES):
        acc = acc + x[:, j * LANES:(j + 1) * LANES]
    return acc


def _ones_where(pred):
    return jnp.where(pred, 1, 0).astype(I32)


def _kth_largest(count_ge, shape, n_sel):
    def body(it, lo):
        bit = 31 - it
        cand = lo + jnp.left_shift(jnp.int32(1), bit)
        return jnp.where(count_ge(cand) >= n_sel, cand, lo)
    return lax.fori_loop(0, 32, body, jnp.full(shape, INT_MIN, I32))


def _tie_position(count_eq_le, need, shape, nbits):
    def body(it, lo):
        bit = nbits - 1 - it
        cand = lo + jnp.left_shift(jnp.int32(1), bit)
        return jnp.where(count_eq_le(cand - 1) < need, cand, lo)
    return lax.fori_loop(0, nbits, body, jnp.zeros(shape, I32))


def _select_threshold(count_over, shape, n_sel, pos_bits):
    thr = _kth_largest(lambda cand: count_over(lambda key, kpos: key >= cand), shape, n_sel)
    need = n_sel - count_over(lambda key, kpos: key > thr)
    n_eq = count_over(lambda key, kpos: key == thr)
    has_tie = jnp.max(jnp.where((n_eq > need) & (thr > INT_MIN), 1, 0)) > 0
    pos_thr = lax.cond(
        has_tie,
        lambda: _tie_position(
            lambda p: count_over(lambda key, kpos: (key == thr) & (kpos <= p)), need, shape, pos_bits),
        lambda: jnp.full(shape, 2 ** 30, I32))
    return thr, pos_thr


def _dsa_attend(carry, q_ref, kblk, vblk, dist, sel, slopes):
    neg = jnp.where(sel, 0.0, NEG_INF)
    qs = []
    for s in range(4):
        slab = q_ref[0, :, s * LANES:(s + 1) * LANES]
        left = _lane_left(slab.shape)
        zero = jnp.zeros_like(slab)
        qs.append((jnp.where(left, slab, zero), jnp.where(left, zero, slab)))
    q_all = jnp.concatenate([qs[h % 4][h // 4] for h in range(8)], axis=0)
    r, kb = dist.shape
    s3 = _dot_nt(q_all, kblk).reshape(8, r, kb)
    s3 = jnp.stack([s3[h] - slopes[h] * dist + neg for h in range(8)], axis=0)
    return _flash_update(carry, s3, vblk)


def _dsa_finish(carry, o_ref):
    m, l, acc = carry
    o = acc / l
    for s in range(4):
        left = _lane_left(o[s].shape)
        o_ref[0, :, s * LANES:(s + 1) * LANES] = jnp.where(left, o[s], o[4 + s]).astype(o_ref.dtype)


def _dsa_prompt_kernel(q_ref, qi_ref, wi_ref, k_ref, vt_ref, ki_ref, o_ref, key_scr, qall_scr, qiall_scr,
                       *, qb, n_sel, slopes):
    i = pl.program_id(1)
    nkb = i + 1
    qpos = i * qb + lax.broadcasted_iota(I32, (1, qb), 1)

    for s in range(4):
        slab = q_ref[0, :, s * LANES:(s + 1) * LANES]
        left = _lane_left(slab.shape)
        zero = jnp.zeros_like(slab)
        qall_scr[s * qb:(s + 1) * qb, :] = jnp.where(left, slab, zero)
        qall_scr[(4 + s) * qb:(5 + s) * qb, :] = jnp.where(left, zero, slab)
    for h in range(8):
        qiall_scr[h * qb:(h + 1) * qb, :] = qi_ref[0, :, h * HEAD_DIM:(h + 1) * HEAD_DIM]
    wit = wi_ref[0].T

    def kpos_of(kb):
        return kb * qb + lax.broadcasted_iota(I32, (qb, 1), 0)

    def rows_of(kb):
        return pl.ds(pl.multiple_of(kb * qb, qb), qb)

    def fill(kb, c):
        allowed = kpos_of(kb) // CHUNK <= qpos // CHUNK
        sc = _dot_nt(ki_ref[0, rows_of(kb), :], qiall_scr[...])
        acc = None
        for h in range(8):
            term = jnp.maximum(sc[:, h * qb:(h + 1) * qb], 0.0) * wit[h:h + 1, :]
            acc = term if acc is None else acc + term
        idx = acc * (8.0 ** -0.5) + 0.0
        bits = pltpu.bitcast(idx, I32)
        key = jnp.where(bits < 0, bits ^ 0x7FFFFFFF, bits)
        key_scr[kb] = jnp.where(allowed, key, INT_MIN)
        return c
    lax.fori_loop(0, nkb, fill, 0)

    def count_over(pred_fn):
        def body(kb, c):
            ones = _ones_where(pred_fn(key_scr[kb], kpos_of(kb)))
            return c + jnp.sum(ones.reshape(qb // 8, 8, qb), axis=0)
        part = lax.fori_loop(0, nkb, body, jnp.zeros((8, qb), I32))
        return jnp.sum(part, axis=0, keepdims=True)

    thr, pos_thr = _select_threshold(count_over, (1, qb), n_sel, 12)

    def attend(kb, carry):
        m, l, acc = carry
        kpos = kpos_of(kb)
        key = key_scr[kb]
        sel = (key > thr) | ((key == thr) & (kpos <= pos_thr))
        sel = sel & (kpos // CHUNK <= qpos // CHUNK)
        neg = jnp.where(sel, 0.0, NEG_INF)
        dist = jnp.abs(qpos - kpos).astype(F32)
        st = _dot_nt(k_ref[0, rows_of(kb), :], qall_scr[...])
        s3 = jnp.stack([st[:, h * qb:(h + 1) * qb] - slopes[h] * dist + neg for h in range(8)], axis=0)
        m_new = jnp.maximum(m, jnp.max(s3, axis=1, keepdims=True))
        m_use = jnp.where(m_new == NEG_INF, 0.0, m_new)
        corr = jnp.exp(m - m_use)
        p = jnp.exp(s3 - m_use)
        l = l * corr + jnp.sum(p, axis=1, keepdims=True)
        p_all = jnp.concatenate([p[h].astype(BF16) for h in range(8)], axis=1)
        pv = _dot(vt_ref[0, kb], p_all)
        acc = acc * corr + jnp.stack([pv[:, h * qb:(h + 1) * qb] for h in range(8)], axis=0)
        return m_new, l, acc

    init = (jnp.full((8, 1, qb), NEG_INF, F32), jnp.zeros((8, 1, qb), F32), jnp.zeros((8, LANES, qb), F32))
    m, l, acc = lax.fori_loop(0, nkb, attend, init)
    ot = acc / l
    top = lax.broadcasted_iota(I32, (LANES, qb), 0) < HEAD_DIM
    for s in range(4):
        o_ref[0, :, s * LANES:(s + 1) * LANES] = jnp.where(top, ot[s], ot[4 + s]).T.astype(o_ref.dtype)


def _dsa_prompt(q, qi, wi, k, v, ki, qb):
    b, t, _ = q.shape
    n_sel = min(TOPK_MAX, t // 4)
    nkb = t // qb
    vt = jnp.swapaxes(v.reshape(b, nkb, qb, LANES), -1, -2)
    tile = lambda w: pl.BlockSpec((1, qb, w), lambda bi, i: (bi, i, 0))
    full = lambda w: pl.BlockSpec((1, t, w), lambda bi, i: (bi, 0, 0))
    return pl.pallas_call(
        functools.partial(_dsa_prompt_kernel, qb=qb, n_sel=n_sel, slopes=_alibi(8)),
        grid=(b, nkb),
        in_specs=[tile(512), tile(512), tile(LANES), full(LANES),
                  pl.BlockSpec((1, nkb, LANES, qb), lambda bi, i: (bi, 0, 0, 0)), full(HEAD_DIM)],
        out_specs=tile(512),
        out_shape=jax.ShapeDtypeStruct((b, t, 512), BF16),
        scratch_shapes=[pltpu.VMEM((nkb, qb, qb), I32), pltpu.VMEM((8 * qb, LANES), BF16),
                        pltpu.VMEM((8 * qb, HEAD_DIM), BF16)],
        compiler_params=_cparams(("parallel", "arbitrary")),
        name="dsa_prompt",
    )(q, qi, wi, k, vt, ki)


def _dsa_sample_kernel(q_ref, qi_ref, wi_ref, kc_ref, vc_ref, kic_ref, kn_ref, vn_ref, kin_ref, o_ref,
                       key_scr, keyn_scr, *, t, kb_size, n_past, n_sel, slopes):
    past_len = n_past * kb_size
    qpos = past_len + lax.broadcasted_iota(I32, (t, 1), 0)
    qi = qi_ref[0]
    wi = wi_ref[0]
    kpos_n = past_len + lax.broadcasted_iota(I32, (1, t), 1)

    def kpos_of(kb):
        return kb * kb_size + lax.broadcasted_iota(I32, (1, kb_size), 1)

    for kb in range(n_past):
        ki_blk = kic_ref[0, kb * kb_size:(kb + 1) * kb_size, :].astype(BF16)
        key_scr[kb] = _index_keys(qi, wi, ki_blk, True)
    keyn_scr[...] = _index_keys(qi, wi, kin_ref[0].astype(BF16), True)

    def count_over(pred_fn):
        part = _lane_fold(_ones_where(pred_fn(key_scr[0], kpos_of(0))))
        for kb in range(1, n_past):
            part = part + _lane_fold(_ones_where(pred_fn(key_scr[kb], kpos_of(kb))))
        new = jnp.sum(_ones_where(pred_fn(keyn_scr[...], kpos_n)), axis=-1, keepdims=True)
        return jnp.sum(part, axis=-1, keepdims=True) + new

    thr, pos_thr = _select_threshold(count_over, (t, 1), n_sel, 13)
    carry = _flash_init(8, t, LANES)

    def sel_of(key, kpos):
        return (key > thr) | ((key == thr) & (kpos <= pos_thr))

    for kb in range(n_past):
        kpos = kpos_of(kb)
        dist = jnp.abs(qpos - kpos).astype(F32)
        kblk = kc_ref[0, kb * kb_size:(kb + 1) * kb_size, :].astype(BF16)
        vblk = vc_ref[0, kb * kb_size:(kb + 1) * kb_size, :].astype(BF16)
        carry = _dsa_attend(carry, q_ref, kblk, vblk, dist, sel_of(key_scr[kb], kpos), slopes)
    dist = jnp.abs(qpos - kpos_n).astype(F32)
    carry = _dsa_attend(carry, q_ref, kn_ref[0].astype(BF16), vn_ref[0].astype(BF16), dist,
                        sel_of(keyn_scr[...], kpos_n), slopes)
    _dsa_finish(carry, o_ref)


def _dsa_sample(q, qi, wi, k_cache, v_cache, ki_cache, k_new, v_new, ki_new, kb_size):
    b, t, _ = q.shape
    past = k_cache.shape[1]
    n_past = past // kb_size
    n_sel = min(TOPK_MAX, (past + t) // 4)
    per_b = lambda rows, w: pl.BlockSpec((1, rows, w), lambda bi: (bi, 0, 0))
    return pl.pallas_call(
        functools.partial(_dsa_sample_kernel, t=t, kb_size=kb_size, n_past=n_past, n_sel=n_sel,
                          slopes=_alibi(8)),
        grid=(b,),
        in_specs=[per_b(t, 512), per_b(t, 512), per_b(t, LANES),
                  per_b(past, LANES), per_b(past, LANES), per_b(past, HEAD_DIM),
                  per_b(t, LANES), per_b(t, LANES), per_b(t, HEAD_DIM)],
        out_specs=per_b(t, 512),
        out_shape=jax.ShapeDtypeStruct((b, t, 512), BF16),
        scratch_shapes=[pltpu.VMEM((n_past, t, kb_size), I32), pltpu.VMEM((t, t), I32)],
        compiler_params=_cparams(("parallel",)),
        name="dsa_sample",
    )(q, qi, wi, k_cache, v_cache, ki_cache, k_new, v_new, ki_new)


def _pad_cols(w, total):
    return jnp.pad(w, ((0, 0), (0, total - w.shape[1])))


def _q_perm():
    idx = []
    for s in range(4):
        idx.extend(range(s * HEAD_DIM, (s + 1) * HEAD_DIM))
        idx.extend(range((4 + s) * HEAD_DIM, (5 + s) * HEAD_DIM))
    return np.asarray(idx, np.int32)


def _even_layer(xp, xs, cache_k, cache_v, shift_s, wkv_s, w_in, sink, rw, w_out, ffn, alpha, sizes):
    (mu, w0, w_up, a0, a_up, g_up, k_k, k_a, r_k, lnx_g, lnx_b) = rw
    perm = _q_perm()
    w_in_p = jnp.concatenate([w_in[:, :512][:, perm], w_in[:, 512:]], axis=1).astype(BF16)
    w_out_p = jnp.concatenate([w_out[:512][perm], w_out[512:]], axis=0).astype(BF16)
    groups = ((0, 512, ((0, 512, HEAD_DIM ** -0.5),)),
              (512, 128, ((1, 128, 1.0), (2, 128, 1.0))),
              (640, 128, ((3, 128, 1.0), (4, 128, 1.0))),
              (768, 1792, ((5, 1792, 1.0),)))
    out_defs = ((512, BF16), (128, F32), (128, BF16), (128, F32), (128, BF16), (1792, F32))
    wup_p = jnp.concatenate([w_up, jnp.zeros_like(w_up)], axis=0).astype(BF16)
    aup_p = jnp.concatenate([jnp.zeros_like(a_up), a_up], axis=0).astype(BF16)
    gup = g_up.astype(BF16)
    row = lambda v: v.reshape(1, -1)
    g1, b1, g2, b2, wu, wd = ffn

    def run(x, state, tm, qb, scan_tb, prep_tm):
        b, t, d = x.shape
        n = b * t
        qa, ka, ka_bf, va, va_bf, pb = _project(x.reshape(n, d), w_in_p, groups, out_defs, tm)
        r3 = lambda a: a.reshape(b, t, a.shape[-1])
        if state is None:
            oa = _swa(r3(qa), r3(ka_bf), r3(ka_bf), r3(va_bf), r3(va_bf), sink,
                      qb=qb, prev_is_cache=False, pos_base=0)
            k_buf = r3(ka)[:, -WINDOW:]
            v_buf = r3(va)[:, -WINDOW:]
            shift = jnp.zeros((b, 1, pb.shape[-1]), F32)
            s0 = jnp.zeros((b, 8, HEAD_DIM, HEAD_DIM), F32)
        else:
            ck, cv, shift, s0 = state
            ck2 = ck.reshape(b, WINDOW, LANES)
            cv2 = cv.reshape(b, WINDOW, LANES)
            oa = _swa(r3(qa), ck2, r3(ka_bf), cv2, r3(va_bf), sink,
                      qb=qb, prev_is_cache=True, pos_base=sizes["past"])
            k_buf = jnp.concatenate([ck2, r3(ka)], axis=1)[:, -WINDOW:]
            v_buf = jnp.concatenate([cv2, r3(va)], axis=1)[:, -WINDOW:]
        pb3 = r3(pb)
        rr, ww, kk2, vv, kkn, bb, gg, bonus = _rwkv_prep(
            pb3, shift, row(mu), row(w0), wup_p, row(a0), aup_p, gup, row(k_k), row(k_a), row(r_k), prep_tm)
        y, s_bd = _rwkv_scan(rr, ww, kk2, vv, kkn, bb, _state_to_blockdiag(s0), scan_tb)
        s_t = _state_from_blockdiag(s_bd)
        ob = _rwkv_post(y, bonus, gg, row(lnx_g), row(lnx_b), prep_tm)
        xo = _out_ffn(x.reshape(n, d), oa.reshape(n, 512), ob.reshape(n, 512), w_out_p, row(g1), row(b1),
                      wu.astype(BF16), wd.astype(BF16), row(g2), row(b2), alpha, tm)
        st = (k_buf.reshape(b, WINDOW, 2, HEAD_DIM), v_buf.reshape(b, WINDOW, 2, HEAD_DIM),
              pb3[:, -1:], s_t)
        return xo.reshape(b, t, d), st

    xp2, st_p = run(xp, None, sizes["tm_p"], sizes["swa_qb"], sizes["scan_tb_p"], sizes["prep_tm_p"])
    xs2, st_s = run(xs, (cache_k, cache_v, shift_s, wkv_s), sizes["tm_s"], xs.shape[1], xs.shape[1],
                    xs.shape[1])
    return xp2, xs2, st_p, st_s


def _odd_layer(xp, xs, caches, w_in, lam_vecs, subln_g, w_out, ffn, alpha, lam_init, sizes):
    perm = _q_perm()
    scale = HEAD_DIM ** -0.5
    w_p = jnp.concatenate([
        w_in[:, :1536], w_in[:, 1536:2048][:, perm], w_in[:, 2048:2816],
        _pad_cols(w_in[:, 2816:2880], LANES), _pad_cols(w_in[:, 2880:2888], LANES)], axis=1).astype(BF16)
    w_out_p = jnp.concatenate([w_out[:512], w_out[512:][perm]], axis=0).astype(BF16)
    groups = ((0, 512, ((0, 512, scale),)),
              (512, 512, ((1, 512, 1.0), (2, 512, 1.0))),
              (1024, 512, ((3, 512, 1.0), (4, 512, 1.0))),
              (1536, 512, ((5, 512, scale),)),
              (2048, 128, ((6, 128, 1.0), (7, 128, 1.0))),
              (2176, 128, ((8, 128, 1.0), (9, 128, 1.0))),
              (2304, 512, ((10, 512, scale),)),
              (2816, 128, ((11, 64, 1.0), (12, 64, 1.0))),
              (2944, 128, ((13, 128, 1.0),)))
    out_defs = ((512, BF16), (512, F32), (512, BF16), (512, F32), (512, BF16), (512, BF16),
                (128, F32), (128, BF16), (128, F32), (128, BF16), (512, BF16),
                (64, F32), (64, BF16), (128, F32))
    row = lambda v: v.reshape(1, -1)
    g1, b1, g2, b2, wu, wd = ffn
    lamv = jnp.stack(lam_vecs, axis=0)
    gain = row(subln_g)

    def run(x, state, tm):
        b, t, d = x.shape
        n = b * t
        (qc, kc, kc_bf, vc, vc_bf, qd, kd, kd_bf, vd, vd_bf, qi, ki, ki_bf, wi) = _project(
            x.reshape(n, d), w_p, groups, out_defs, tm)
        r3 = lambda a: a.reshape(b, t, a.shape[-1])
        if state is None:
            oc = _diff_prompt(r3(qc), r3(kc_bf), r3(vc_bf), lamv, gain, lam_init, sizes["diff_qb"])
            od = _dsa_prompt(r3(qd), r3(qi), r3(wi), r3(kd_bf), r3(vd_bf), r3(ki_bf), sizes["dsa_qb"])
        else:
            c_k, c_v, d_k, d_v, d_ki = state
            past = c_k.shape[1]
            oc = _diff_sample(r3(qc), c_k.reshape(b, past, 512), c_v.reshape(b, past, 512),
                              r3(kc_bf), r3(vc_bf), lamv, gain, lam_init, sizes["diff_kb_s"])
            od = _dsa_sample(r3(qd), r3(qi), r3(wi), d_k.reshape(b, past, LANES), d_v.reshape(b, past, LANES),
                             d_ki, r3(kd_bf), r3(vd_bf), r3(ki_bf), sizes["dsa_kb_s"])
        xo = _out_ffn(x.reshape(n, d), oc.reshape(n, 512), od.reshape(n, 512), w_out_p, row(g1), row(b1),
                      wu.astype(BF16), wd.astype(BF16), row(g2), row(b2), alpha, tm)
        rows = (kc.reshape(b, t, 4, 2, HEAD_DIM), vc.reshape(b, t, 4, 2 * HEAD_DIM),
                kd.reshape(b, t, 2, HEAD_DIM), vd.reshape(b, t, 2, HEAD_DIM), ki.reshape(b, t, HEAD_DIM))
        return xo.reshape(b, t, d), rows

    xp2, st_p = run(xp, None, sizes["tm_p"])
    xs2, st_s = run(xs, caches, sizes["tm_s"])
    return xp2, xs2, st_p, st_s


def _sizes(xp, xs, past):
    return dict(tm_p=min(512, xp.shape[0] * xp.shape[1]), tm_s=min(512, xs.shape[0] * xs.shape[1]),
                swa_qb=min(256, xp.shape[1]), scan_tb_p=min(128, xp.shape[1]),
                prep_tm_p=min(256, xp.shape[1]), diff_qb=min(256, xp.shape[1]),
                dsa_qb=min(256, xp.shape[1]), diff_kb_s=min(1024, past), dsa_kb_s=min(512, past),
                past=past)


def kernel(x_prompt, x_sample, cache_a_k, cache_a_v, state_b_shift, state_b_wkv, cache_c_k, cache_c_v, cache_d_k, cache_d_v, cache_d_kidx, w_in_even, sink_a, mu_b, w0_b, w_up_b, a0_b, a_up_b, g_up_b, k_k_b, k_a_b, r_k_b, lnx_g_b, lnx_b_b, w_out_even, w_in_odd, lam_q1_c, lam_k1_c, lam_q2_c, lam_k2_c, subln_g_c, w_out_odd, ln_mix_g, ln_mix_b, ln_ffn_g, ln_ffn_b, w_ff_up, w_ff_down):
    depth = ln_mix_g.shape[0]
    alpha = (2 * depth) ** 0.25
    past = cache_c_k.shape[2]
    sizes = _sizes(x_prompt, x_sample, past)
    xp, xs = x_prompt, x_sample
    even_p, even_s, odd_p, odd_s = [], [], [], []
    for layer in range(depth):
        i = layer // 2
        ffn = (ln_mix_g[layer], ln_mix_b[layer], ln_ffn_g[layer], ln_ffn_b[layer],
               w_ff_up[layer], w_ff_down[layer])
        if layer % 2 == 0:
            rw = (mu_b[i], w0_b[i], w_up_b[i], a0_b[i], a_up_b[i], g_up_b[i], k_k_b[i], k_a_b[i],
                  r_k_b[i].reshape(-1), lnx_g_b[i], lnx_b_b[i])
            xp, xs, st_p, st_s = _even_layer(
                xp, xs, cache_a_k[i], cache_a_v[i], state_b_shift[i], state_b_wkv[i],
                w_in_even[i], sink_a[i], rw, w_out_even[i], ffn, alpha, sizes)
            even_p.append(st_p)
            even_s.append(st_s)
        else:
            lam_init = 0.8 - 0.6 * math.exp(-0.3 * layer)
            xp, xs, st_p, st_s = _odd_layer(
                xp, xs, (cache_c_k[i], cache_c_v[i], cache_d_k[i], cache_d_v[i], cache_d_kidx[i]),
                w_in_odd[i], (lam_q1_c[i], lam_k1_c[i], lam_q2_c[i], lam_k2_c[i]), subln_g_c[i],
                w_out_odd[i], ffn, alpha, lam_init, sizes)
            odd_p.append(st_p)
            odd_s.append(st_s)
    stack = lambda states: [jnp.stack(z, axis=0) for z in zip(*states)]
    a_k_p, a_v_p, b_shift_p, b_wkv_p = stack(even_p)
    a_k_s, a_v_s, b_shift_s, b_wkv_s = stack(even_s)
    c_k_p, c_v_p, d_k_p, d_v_p, d_kidx_p = stack(odd_p)
    c_k_s, c_v_s, d_k_s, d_v_s, d_kidx_s = stack(odd_s)
    return (xp, xs, a_k_p, a_v_p, b_shift_p, b_wkv_p, c_k_p, c_v_p, d_k_p, d_v_p, d_kidx_p,
            a_k_s, a_v_s, b_shift_s, b_wkv_s, c_k_s, c_v_s, d_k_s, d_v_s, d_kidx_s)
```

```python
import functools
import math

import numpy as np
import jax
import jax.numpy as jnp
from jax import lax
from jax.experimental import pallas as pl
from jax.experimental.pallas import tpu as pltpu

F32 = jnp.float32
BF16 = jnp.bfloat16
I32 = jnp.int32

LANES = 128
HEAD_DIM = 64
CHUNK = 64
LN_EPS = 1e-5
RWKV_GN_EPS = 64e-5
WINDOW = 128
TOPK_MAX = 256
NEG_INF = float("-inf")
INT_MIN = -(2 ** 31)
VMEM_LIMIT = 56 * 1024 * 1024


def _cparams(sem):
    return pltpu.CompilerParams(dimension_semantics=sem, vmem_limit_bytes=VMEM_LIMIT)


def _alibi(n):
    return [float(2.0 ** (-8.0 * (i + 1) / n)) for i in range(n)]


def _dot(a, b):
    return jnp.dot(a, b, preferred_element_type=F32)


def _dot_nt(a, b):
    return lax.dot_general(a, b, (((1,), (1,)), ((), ())), preferred_element_type=F32)


def _lane_left(shape):
    return lax.broadcasted_iota(I32, shape, len(shape) - 1) < HEAD_DIM


def _segsum64(x):
    left = _lane_left(x.shape)
    sl = jnp.sum(jnp.where(left, x, 0.0), axis=-1, keepdims=True)
    sr = jnp.sum(jnp.where(left, 0.0, x), axis=-1, keepdims=True)
    return jnp.where(left, sl, sr)


def _proj_kernel(x_ref, w_ref, *out_refs, groups):
    xb = x_ref[...].astype(BF16)
    for off, width, outs in groups:
        r = _dot(xb, w_ref[:, off:off + width])
        for idx, ow, scale in outs:
            v = r if ow == width else r[:, :ow]
            if scale != 1.0:
                v = v * scale
            out_refs[idx][...] = v.astype(out_refs[idx].dtype)


def _project(x, w_bf, groups, out_defs, tm):
    n, d = x.shape
    cols = w_bf.shape[1]
    out_shape = [jax.ShapeDtypeStruct((n, ow), dt) for ow, dt in out_defs]
    out_specs = [pl.BlockSpec((tm, ow), lambda i: (i, 0)) for ow, _ in out_defs]
    return pl.pallas_call(
        functools.partial(_proj_kernel, groups=groups),
        grid=(n // tm,),
        in_specs=[pl.BlockSpec((tm, d), lambda i: (i, 0)),
                  pl.BlockSpec((d, cols), lambda i: (0, 0))],
        out_specs=out_specs,
        out_shape=out_shape,
        compiler_params=_cparams(("parallel",)),
        name="in_proj",
    )(x, w_bf)


def _swa_kernel(sink_ref, q_ref, kp_ref, kc_ref, vp_ref, vc_ref, o_ref, *, qb, pos_base, slopes):
    i = pl.program_id(1)
    q0 = pos_base + i * qb
    kp = kp_ref[0].astype(BF16)
    kc = kc_ref[0].astype(BF16)
    vp = vp_ref[0].astype(BF16)
    vc = vc_ref[0].astype(BF16)
    qpos = q0 + lax.broadcasted_iota(I32, (qb, 1), 0)
    kpos_p = q0 - WINDOW + lax.broadcasted_iota(I32, (1, WINDOW), 1)
    kpos_c = q0 + lax.broadcasted_iota(I32, (1, qb), 1)
    qch = qpos // CHUNK

    def allowed(kpos):
        kch = kpos // CHUNK
        return (kpos >= 0) & (kch >= qch - WINDOW // CHUNK) & (kch <= qch)

    al_p, al_c = allowed(kpos_p), allowed(kpos_c)
    dist_p = jnp.abs(qpos - kpos_p).astype(F32)
    dist_c = jnp.abs(qpos - kpos_c).astype(F32)
    for s in range(4):
        slab = q_ref[0, :, s * LANES:(s + 1) * LANES]
        left = _lane_left(slab.shape)
        res = []
        for side in range(2):
            h = 4 * side + s
            qx = jnp.where(left if side == 0 else jnp.logical_not(left), slab, jnp.zeros_like(slab))
            lp = jnp.where(al_p, _dot_nt(qx, kp) - slopes[h] * dist_p, NEG_INF)
            lc = jnp.where(al_c, _dot_nt(qx, kc) - slopes[h] * dist_c, NEG_INF)
            sk = sink_ref[h]
            m = jnp.maximum(jnp.maximum(jnp.max(lp, -1, keepdims=True), jnp.max(lc, -1, keepdims=True)), sk)
            pp = jnp.exp(lp - m)
            pc = jnp.exp(lc - m)
            den = jnp.sum(pp, -1, keepdims=True) + jnp.sum(pc, -1, keepdims=True) + jnp.exp(sk - m)
            o = _dot(pp.astype(BF16), vp) + _dot(pc.astype(BF16), vc)
            res.append(o / den)
        o_ref[0, :, s * LANES:(s + 1) * LANES] = jnp.where(left, res[0], res[1]).astype(o_ref.dtype)


def _swa(q, k_prev_src, k_cur, v_prev_src, v_cur, sink, *, qb, prev_is_cache, pos_base):
    b, t, _ = q.shape
    nq = t // qb
    if prev_is_cache:
        prev_map = lambda bi, i: (bi, 0, 0)
    else:
        r = qb // WINDOW
        prev_map = lambda bi, i: (bi, jnp.maximum(i * r - 1, 0), 0)
    cur_map = lambda bi, i: (bi, i, 0)
    return pl.pallas_call(
        functools.partial(_swa_kernel, qb=qb, pos_base=pos_base, slopes=_alibi(8)),
        grid=(b, nq),
        in_specs=[pl.BlockSpec(memory_space=pltpu.SMEM),
                  pl.BlockSpec((1, qb, 512), cur_map),
                  pl.BlockSpec((1, WINDOW, LANES), prev_map),
                  pl.BlockSpec((1, qb, LANES), cur_map),
                  pl.BlockSpec((1, WINDOW, LANES), prev_map),
                  pl.BlockSpec((1, qb, LANES), cur_map)],
        out_specs=pl.BlockSpec((1, qb, 512), cur_map),
        out_shape=jax.ShapeDtypeStruct((b, t, 512), BF16),
        compiler_params=_cparams(("parallel", "parallel")),
        name="swa_sink",
    )(sink, q, k_prev_src, k_cur, v_prev_src, v_cur)


def _sigmoid(x):
    return 1.0 / (1.0 + jnp.exp(-x))


def _softplus(x):
    return jnp.maximum(x, 0.0) + jnp.log(1.0 + jnp.exp(-jnp.abs(x)))


def _rwkv_prep_kernel(pb_ref, pbprev_ref, shift_ref, mu_ref, w0_ref, wup_ref, a0_ref, aup_ref, gup_ref,
                      kk_ref, ka_ref, rk_ref,
                      r_out, w_out, k_out, v_out, kk_out, b_out, g_out, bonus_out):
    i = pl.program_id(1)
    pb = pb_ref[0]
    tm = pb.shape[0]
    prev_row = jnp.where(i == 0, shift_ref[0], pbprev_ref[0, 7:8, :])
    rolled = pltpu.roll(pb, 1, axis=0)
    row = lax.broadcasted_iota(I32, (tm, 1), 0)
    prev = jnp.where(row == 0, prev_row, rolled)
    xm = pb + (prev - pb) * mu_ref[...]
    r = xm[:, 0:512]
    k = xm[:, 512:1024]
    v = xm[:, 1024:1536]
    wa = xm[:, 1536:1664]
    gl = xm[:, 1664:1792]
    lw = _dot(jnp.tanh(wa).astype(BF16), wup_ref[...])
    la = _dot(wa.astype(BF16), aup_ref[...])
    w_log = -_softplus(-(w0_ref[...] + lw)) - 0.5
    log_decay = -jnp.exp(w_log)
    a = _sigmoid(a0_ref[...] + la)
    g = _dot(_sigmoid(gl).astype(BF16), gup_ref[...])
    kk = k * kk_ref[...]
    k2 = k * (1.0 + (a - 1.0) * ka_ref[...])
    rkk = r * k2 * rk_ref[...]
    for p in range(4):
        sl = slice(p * LANES, (p + 1) * LANES)
        kkp = kk[:, sl]
        kkn = kkp * lax.rsqrt(_segsum64(kkp * kkp) + 1e-12)
        kk_out[0, p] = kkn
        b_out[0, p] = kkn * a[:, sl]
        bonus_out[0, :, sl] = _segsum64(rkk[:, sl]) * v[:, sl]
        r_out[0, p] = r[:, sl]
        w_out[0, p] = log_decay[:, sl]
        k_out[0, p] = k2[:, sl]
        v_out[0, p] = v[:, sl]
    g_out[0] = g


RWKV_CHUNK = 16
RWKV_PAIRS_PER_STEP = 2


def _split2(x):
    hi = x.astype(BF16)
    return hi, (x - hi.astype(F32)).astype(BF16)


def _dot3(a, b):
    return _dot(a[0], b[0]) + (_dot(a[0], b[1]) + _dot(a[1], b[0]))


def _rwkv_scan_scratch(tb):
    nc, c2 = tb // RWKV_CHUNK, 2 * RWKV_CHUNK
    return [pltpu.VMEM((4, LANES, LANES), F32),
            pltpu.VMEM((4, nc, c2, LANES), BF16), pltpu.VMEM((4, nc, c2, LANES), BF16),
            pltpu.VMEM((4, nc, LANES, c2), BF16), pltpu.VMEM((4, nc, LANES, c2), BF16),
            pltpu.VMEM((4, nc, LANES, LANES), F32),
            pltpu.VMEM((4, tb, LANES), F32), pltpu.VMEM((4, tb, LANES), F32)]


def _rwkv_scan_kernel(r_ref, lw_ref, k_ref, v_ref, kk_ref, b_ref, s0_ref, y_ref, sT_ref,
                      st_scr, l1hi_scr, l1lo_scr, l2hi_scr, l2lo_scr, pcol_scr, u0_scr, y0_scr, *, tb):
    ti = pl.program_id(1)
    c_sz = RWKV_CHUNK
    nc = tb // c_sz

    @pl.when(ti == 0)
    def _():
        st_scr[...] = s0_ref[0]

    row = lax.broadcasted_iota(I32, (tb, tb), 0)
    col = lax.broadcasted_iota(I32, (tb, tb), 1)
    same = (row // c_sz) == (col // c_sz)
    strict = same & (col < row)
    incl = same & (col <= row)
    eye_t = jnp.where(row == col, 1.0, 0.0)
    cum_lhs = jnp.concatenate([jnp.where(incl, 1.0, 0.0), jnp.where(same, 1.0, 0.0)], axis=0).astype(BF16)
    left = _lane_left((tb, LANES))
    r128 = lax.broadcasted_iota(I32, (LANES, LANES), 0)
    c128 = lax.broadcasted_iota(I32, (LANES, LANES), 1)
    blockdiag = (r128 < HEAD_DIM) == (c128 < HEAD_DIM)

    left2 = jnp.concatenate([left, left], axis=1)
    npar = RWKV_PAIRS_PER_STEP

    def prepare(i, carry):
        ps = [i * npar + q for q in range(npar)]
        qs = range(npar)
        sides = range(2)
        ins = [[ref[0, p] for ref in (r_ref, lw_ref, k_ref, v_ref, kk_ref, b_ref)] for p in ps]

        def split3(x):
            x1 = x.astype(BF16)
            rem = x - x1.astype(F32)
            x2 = rem.astype(BF16)
            return x1, x2, (rem - x2.astype(F32)).astype(BF16)

        lws = [split3(x[1]) for x in ins]
        cums = [_dot(cum_lhs, l[0]) + (_dot(cum_lhs, l[1]) + _dot(cum_lhs, l[2])) for l in lws]
        al, rt, vs, rhs, lhs, bbt, kbt, pc_rows = [], [], [], [], [], [], [], []
        for q in qs:
            r, lw, k, v, kk, b = ins[q]
            cs, cse = cums[q][:tb], cums[q][tb:]
            p_inv = jnp.exp(-cs)
            p_end = jnp.exp(cse - cs)
            al.append(-kk * jnp.exp(cs - lw))
            rt.append(r * jnp.exp(cs))
            vs.append(_split2(v))
            rhs.append(_split2(jnp.concatenate([b * p_inv, k * p_inv], axis=0)))
            zero = jnp.zeros_like(v)
            half = lambda z, s: jnp.where(left, z, zero) if s == 0 else jnp.where(left, zero, z)
            lhs.append([_split2(jnp.concatenate([half(al[q], s), half(rt[q], s)], axis=0)) for s in sides])
            bbt.append(_split2((b * p_end).T))
            kbt.append(_split2((k * p_end).T))
            pc_rows.append(jnp.exp(cse))
        gram = [[_dot_nt(lhs[q][s][0], rhs[q][0]) + (_dot_nt(lhs[q][s][0], rhs[q][1])
                                                      + _dot_nt(lhs[q][s][1], rhs[q][0]))
                 for s in sides] for q in qs]
        a1 = [[_split2(jnp.where(strict, gram[q][s][:tb, :tb], 0.0)) for s in sides] for q in qs]
        a2 = [[_split2(jnp.where(strict, gram[q][s][:tb, tb:], 0.0)) for s in sides] for q in qs]
        b1 = [[_split2(jnp.where(incl, gram[q][s][tb:, :tb], 0.0)) for s in sides] for q in qs]
        b2 = [[_split2(jnp.where(incl, gram[q][s][tb:, tb:], 0.0)) for s in sides] for q in qs]
        xv = [jnp.where(left, _dot3(a2[q][0], vs[q]), _dot3(a2[q][1], vs[q])) for q in qs]
        pw = [[_dot3(a1[q][s], a1[q][s]) for s in sides] for q in qs]
        x = [jnp.concatenate([al[q], xv[q]], axis=1) for q in qs]
        m = a1
        for step in range(4):
            xs = [_split2(x[q]) for q in qs]
            x = [x[q] + jnp.where(left2, _dot3(m[q][0], xs[q]), _dot3(m[q][1], xs[q])) for q in qs]
            if step < 3:
                m = [[_split2(pw[q][s]) for s in sides] for q in qs]
                if step < 2:
                    pw = [[_dot3(m[q][s], m[q][s]) for s in sides] for q in qs]
        xs = [_split2(x[q]) for q in qs]
        bx = [jnp.where(left2, _dot(b1[q][0][0], xs[q][0]), _dot(b1[q][1][0], xs[q][0])) for q in qs]
        bv = [jnp.where(left, _dot(b2[q][0][0], vs[q][0]), _dot(b2[q][1][0], vs[q][0])) for q in qs]
        for q in qs:
            p = ps[q]
            ah = x[q][:, :LANES]
            rh = rt[q] + bx[q][:, :LANES]
            u0_scr[p] = x[q][:, LANES:]
            y0_scr[p] = bx[q][:, LANES:] + bv[q]
            pct = pc_rows[q].T
            for c in range(nc):
                rows = slice(c * c_sz, (c + 1) * c_sz)
                l1 = _split2(jnp.concatenate([ah[rows], rh[rows]], axis=0))
                l1hi_scr[p, c], l1lo_scr[p, c] = l1
                l2hi_scr[p, c] = jnp.concatenate([bbt[q][0][:, rows], kbt[q][0][:, rows]], axis=1)
                l2lo_scr[p, c] = jnp.concatenate([bbt[q][1][:, rows], kbt[q][1][:, rows]], axis=1)
                pcol_scr[p, c] = jnp.broadcast_to(pct[:, c * c_sz:c * c_sz + 1], (LANES, LANES))
        return carry

    lax.fori_loop(0, 4 // npar, prepare, 0)

    st = [st_scr[p] for p in range(4)]
    for c in range(nc):
        rows = slice(c * c_sz, (c + 1) * c_sz)
        res = [_dot3((l1hi_scr[p, c], l1lo_scr[p, c]), _split2(st[p])) for p in range(4)]
        for p in range(4):
            y_ref[0, p, rows, :] = res[p][c_sz:] + y0_scr[p, rows, :]
        w2 = [_split2(jnp.concatenate([res[p][:c_sz] + u0_scr[p, rows, :], v_ref[0, p, rows, :]], axis=0))
              for p in range(4)]
        upd = [_dot3((l2hi_scr[p, c], l2lo_scr[p, c]), w2[p]) for p in range(4)]
        st = [st[p] * pcol_scr[p, c] + jnp.where(blockdiag, upd[p], 0.0) for p in range(4)]
    for p in range(4):
        st_scr[p] = st[p]

    @pl.when(ti == pl.num_programs(1) - 1)
    def _():
        sT_ref[0] = st_scr[...]


def _rwkv_scan(r, lw, k, v, kk, bb, s0_bd, tb):
    b, _, t, _ = r.shape
    nc = tb // RWKV_CHUNK
    cur = pl.BlockSpec((1, 4, tb, LANES), lambda bi, i: (bi, 0, i, 0))
    st_spec = pl.BlockSpec((1, 4, LANES, LANES), lambda bi, i: (bi, 0, 0, 0))
    return pl.pallas_call(
        functools.partial(_rwkv_scan_kernel, tb=tb),
        grid=(b, t // tb),
        in_specs=[cur] * 6 + [st_spec],
        out_specs=[cur, st_spec],
        out_shape=[jax.ShapeDtypeStruct((b, 4, t, LANES), F32),
                   jax.ShapeDtypeStruct((b, 4, LANES, LANES), F32)],
        scratch_shapes=_rwkv_scan_scratch(tb),
        compiler_params=_cparams(("parallel", "arbitrary")),
        name="rwkv_scan",
    )(r, lw, k, v, kk, bb, s0_bd)


def _state_to_blockdiag(s):
    b = s.shape[0]
    st = jnp.swapaxes(s, -1, -2).reshape(b, 4, 2, HEAD_DIM, HEAD_DIM)
    z = jnp.zeros_like(st[:, :, 0])
    top = jnp.concatenate([st[:, :, 0], z], axis=-1)
    bot = jnp.concatenate([z, st[:, :, 1]], axis=-1)
    return jnp.concatenate([top, bot], axis=-2)


def _state_from_blockdiag(s_bd):
    b = s_bd.shape[0]
    a = s_bd[:, :, :HEAD_DIM, :HEAD_DIM]
    d = s_bd[:, :, HEAD_DIM:, HEAD_DIM:]
    st = jnp.stack([a, d], axis=2).reshape(b, 8, HEAD_DIM, HEAD_DIM)
    return jnp.swapaxes(st, -1, -2)


def _rwkv_post_kernel(y_ref, bonus_ref, g_ref, lg_ref, lb_ref, o_ref):
    for p in range(4):
        sl = slice(p * LANES, (p + 1) * LANES)
        y = y_ref[0, p]
        d = y - _segsum64(y) * (1.0 / HEAD_DIM)
        var = _segsum64(d * d) * (1.0 / HEAD_DIM)
        yn = d * lax.rsqrt(var + RWKV_GN_EPS) * lg_ref[:, sl] + lb_ref[:, sl]
        o_ref[0, :, sl] = ((yn + bonus_ref[0, :, sl]) * g_ref[0, :, sl]).astype(o_ref.dtype)


def _rwkv_kernel(pb_ref, pbprev_ref, shift_ref, mu_ref, w0_ref, wup_ref, a0_ref, aup_ref, gup_ref,
                 kk_ref, ka_ref, rk_ref, lg_ref, lb_ref, s0_ref, ob_ref, sT_ref,
                 r_s, lw_s, k_s, v_s, kkn_s, b_s, g_s, bonus_s, y_s, *scan_scr, tb):
    _rwkv_prep_kernel(pb_ref, pbprev_ref, shift_ref, mu_ref, w0_ref, wup_ref, a0_ref, aup_ref, gup_ref,
                      kk_ref, ka_ref, rk_ref, r_s, lw_s, k_s, v_s, kkn_s, b_s, g_s, bonus_s)
    _rwkv_scan_kernel(r_s, lw_s, k_s, v_s, kkn_s, b_s, s0_ref, y_s, sT_ref, *scan_scr, tb=tb)
    _rwkv_post_kernel(y_s, bonus_s, g_s, lg_ref, lb_ref, ob_ref)


def _rwkv(pb, shift, mu, w0, wup, a0, aup, gup, k_k, k_a, r_k, lnx_g, lnx_b, s0_bd, tb):
    b, t, c = pb.shape
    nc = tb // RWKV_CHUNK
    cur = lambda bi, i: (bi, i, 0)
    vec = lambda n: pl.BlockSpec((1, n), lambda bi, i: (0, 0))
    lora = pl.BlockSpec((LANES, 512), lambda bi, i: (0, 0))
    st_spec = pl.BlockSpec((1, 4, LANES, LANES), lambda bi, i: (bi, 0, 0, 0))
    pair_scr = pltpu.VMEM((1, 4, tb, LANES), F32)
    return pl.pallas_call(
        functools.partial(_rwkv_kernel, tb=tb),
        grid=(b, t // tb),
        in_specs=[pl.BlockSpec((1, tb, c), cur),
                  pl.BlockSpec((1, 8, c), lambda bi, i: (bi, jnp.maximum(i * (tb // 8) - 1, 0), 0)),
                  pl.BlockSpec((1, 1, c), lambda bi, i: (bi, 0, 0)),
                  vec(c), vec(512), lora, vec(512), lora, lora, vec(512), vec(512), vec(512),
                  vec(512), vec(512), st_spec],
        out_specs=[pl.BlockSpec((1, tb, 512), cur), st_spec],
        out_shape=[jax.ShapeDtypeStruct((b, t, 512), BF16),
                   jax.ShapeDtypeStruct((b, 4, LANES, LANES), F32)],
        scratch_shapes=([pair_scr] * 6 + [pltpu.VMEM((1, tb, 512), F32)] * 2 + [pair_scr]
                        + _rwkv_scan_scratch(tb)),
        compiler_params=_cparams(("parallel", "arbitrary")),
        name="rwkv",
    )(pb, pb, shift, mu, w0, wup, a0, aup, gup, k_k, k_a, r_k, lnx_g, lnx_b, s0_bd)


def _layer_norm(x, g, b):
    mu = jnp.mean(x, -1, keepdims=True)
    d = x - mu
    var = jnp.mean(d * d, -1, keepdims=True)
    return d * lax.rsqrt(var + LN_EPS) * g + b


def _ffn_kernel(x_ref, m1_ref, m2_ref, wo_ref, g1_ref, b1_ref, wu_ref, wd_ref, g2_ref, b2_ref, o_ref,
                *, alpha, ff_chunk):
    half = m1_ref.shape[1]
    mix = _dot(m1_ref[...], wo_ref[0:half, :]) + _dot(m2_ref[...], wo_ref[half:, :])
    x1 = _layer_norm(alpha * x_ref[...] + mix, g1_ref[...], b1_ref[...])
    x1b = x1.astype(BF16)
    d_ff = wu_ref.shape[1]
    h = jnp.zeros_like(x1)
    for c in range(d_ff // ff_chunk):
        u = jnp.maximum(_dot(x1b, wu_ref[:, c * ff_chunk:(c + 1) * ff_chunk]), 0.0)
        h = h + _dot((u * u).astype(BF16), wd_ref[c * ff_chunk:(c + 1) * ff_chunk, :])
    o_ref[...] = _layer_norm(alpha * x1 + h, g2_ref[...], b2_ref[...])


def _const_spec(shape):
    return pl.BlockSpec(shape, lambda i: (0,) * len(shape), pipeline_mode=pl.Buffered(1))


def _out_ffn(x, m1, m2, wo, g1, b1, wu, wd, g2, b2, alpha, tm):
    n, d = x.shape
    dm = m1.shape[1]
    d_ff = wu.shape[1]
    tile = lambda w: pl.BlockSpec((tm, w), lambda i: (i, 0))
    return pl.pallas_call(
        functools.partial(_ffn_kernel, alpha=alpha, ff_chunk=1024),
        grid=(n // tm,),
        in_specs=[tile(d), tile(dm), tile(dm),
                  _const_spec((2 * dm, d)), _const_spec((1, d)), _const_spec((1, d)),
                  _const_spec((d, d_ff)), _const_spec((d_ff, d)), _const_spec((1, d)), _const_spec((1, d))],
        out_specs=tile(d),
        out_shape=jax.ShapeDtypeStruct((n, d), F32),
        compiler_params=_cparams(("parallel",)),
        name="out_ffn",
    )(x, m1, m2, wo, g1, b1, wu, wd, g2, b2)


def _diff_lambda(lam_ref, lam_init):
    lv = lam_ref[...]
    s1 = jnp.sum(lv[0:1] * lv[1:2], axis=-1, keepdims=True)
    s2 = jnp.sum(lv[2:3] * lv[3:4], axis=-1, keepdims=True)
    return jnp.exp(s1) - jnp.exp(s2) + lam_init


def _split_halves(qh):
    left = _lane_left(qh.shape)
    zero = jnp.zeros_like(qh)
    return jnp.concatenate([jnp.where(left, qh, zero), jnp.where(left, zero, qh)], axis=0)


def _flash_update(carry, s3, vblk):
    m, l, acc = carry
    g, rws, kb = s3.shape
    m_new = jnp.maximum(m, jnp.max(s3, -1, keepdims=True))
    m_use = jnp.where(m_new == NEG_INF, 0.0, m_new)
    corr = jnp.exp(m - m_use)
    p = jnp.exp(s3 - m_use)
    l = l * corr + jnp.sum(p, -1, keepdims=True)
    pv = _dot(p.reshape(g * rws, kb).astype(BF16), vblk).reshape(g, rws, vblk.shape[1])
    return m_new, l, acc * corr + pv


def _flash_init(g, rws, e):
    return (jnp.full((g, rws, 1), NEG_INF, F32), jnp.zeros((g, rws, 1), F32), jnp.zeros((g, rws, e), F32))


def _diff_finish(carry, lam, gain, lam_init):
    m, l, acc = carry
    o = acc[0] / l[0] - lam * (acc[1] / l[1])
    o = o * lax.rsqrt(jnp.mean(o * o, -1, keepdims=True) + LN_EPS) * gain * (1.0 - lam_init)
    return o


def _diff_prompt_kernel(q_ref, k_ref, vt_ref, lam_ref, gain_ref, o_ref, q2_scr, *, qb, slopes, lam_init):
    i = pl.program_id(1)
    lam = _diff_lambda(lam_ref, lam_init)
    qpos = i * qb + lax.broadcasted_iota(I32, (1, qb), 1)
    for h in range(4):
        q2_scr[2 * h * qb:(2 * h + 2) * qb, :] = _split_halves(q_ref[0, :, h * LANES:(h + 1) * LANES])

    def body(kb, carry):
        m, l, acc = carry
        rows = pl.ds(pl.multiple_of(kb * qb, qb), qb)
        kpos = kb * qb + lax.broadcasted_iota(I32, (qb, 1), 0)
        dist = jnp.abs(qpos - kpos).astype(F32)
        neg = jnp.where(kpos // CHUNK <= qpos // CHUNK, 0.0, NEG_INF)
        st = [_dot_nt(k_ref[0, rows, h * LANES:(h + 1) * LANES], q2_scr[2 * h * qb:(2 * h + 2) * qb, :])
              for h in range(4)]
        s3 = jnp.stack([st[g // 2][:, (g % 2) * qb:(g % 2 + 1) * qb] - slopes[g // 2] * dist + neg
                        for g in range(8)], axis=0)
        m_new = jnp.maximum(m, jnp.max(s3, axis=1, keepdims=True))
        corr = jnp.exp(m - m_new)
        p = jnp.exp(s3 - m_new)
        l = l * corr + jnp.sum(p, axis=1, keepdims=True)
        pv = [_dot(vt_ref[0, kb, h * LANES:(h + 1) * LANES, :],
                   jnp.concatenate([p[2 * h].astype(BF16), p[2 * h + 1].astype(BF16)], axis=1))
              for h in range(4)]
        acc = acc * corr + jnp.stack([pv[g // 2][:, (g % 2) * qb:(g % 2 + 1) * qb] for g in range(8)], axis=0)
        return m_new, l, acc

    init = (jnp.full((8, 1, qb), NEG_INF, F32), jnp.zeros((8, 1, qb), F32), jnp.zeros((8, LANES, qb), F32))
    m, l, acc = lax.fori_loop(0, i + 1, body, init)
    for h in range(4):
        o = acc[2 * h] / l[2 * h] - lam * (acc[2 * h + 1] / l[2 * h + 1])
        o = o * lax.rsqrt(jnp.mean(o * o, axis=0, keepdims=True) + LN_EPS) * gain_ref[...] * (1.0 - lam_init)
        o_ref[0, :, h * LANES:(h + 1) * LANES] = o.T.astype(o_ref.dtype)


def _diff_prompt(q, k, v, lamv, gain, lam_init, qb):
    b, t, _ = q.shape
    nkb = t // qb
    vt = jnp.swapaxes(v.reshape(b, nkb, qb, 512), -1, -2)
    full = pl.BlockSpec((1, t, 512), lambda bi, i: (bi, 0, 0))
    tile = pl.BlockSpec((1, qb, 512), lambda bi, i: (bi, i, 0))
    return pl.pallas_call(
        functools.partial(_diff_prompt_kernel, qb=qb, slopes=_alibi(4), lam_init=lam_init),
        grid=(b, nkb),
        in_specs=[tile, full,
                  pl.BlockSpec((1, nkb, 512, qb), lambda bi, i: (bi, 0, 0, 0)),
                  pl.BlockSpec((4, HEAD_DIM), lambda bi, i: (0, 0)),
                  pl.BlockSpec((LANES, 1), lambda bi, i: (0, 0))],
        out_specs=tile,
        out_shape=jax.ShapeDtypeStruct((b, t, 512), BF16),
        scratch_shapes=[pltpu.VMEM((8 * qb, LANES), BF16)],
        compiler_params=_cparams(("parallel", "arbitrary")),
        name="diff_prompt",
    )(q, k, vt, lamv, gain.reshape(LANES, 1))


def _diff_sample_kernel(q_ref, kc_ref, vc_ref, kn_ref, vn_ref, lam_ref, gain_ref, o_ref,
                        m_scr, l_scr, acc_scr, *, t, kb_size, n_past, slopes, lam_init):
    j = pl.program_id(1)
    past_len = n_past * kb_size
    qpos = past_len + lax.broadcasted_iota(I32, (t, 1), 0)

    @pl.when(j == 0)
    def _():
        m_scr[...] = jnp.full(m_scr.shape, NEG_INF, F32)
        l_scr[...] = jnp.zeros(l_scr.shape, F32)
        acc_scr[...] = jnp.zeros(acc_scr.shape, F32)

    def run(kfull, v_of, kpos):
        dist = jnp.abs(qpos - kpos).astype(F32)
        nk = kfull.shape[0]
        s3 = jnp.stack(
            [_dot_nt(_split_halves(q_ref[0, :, h * LANES:(h + 1) * LANES]),
                     kfull[:, h * LANES:(h + 1) * LANES]).reshape(2, t, nk) - slopes[h] * dist[None]
             for h in range(4)], axis=0)
        m = m_scr[...]
        m_new = jnp.maximum(m, jnp.max(s3, -1, keepdims=True))
        corr = jnp.exp(m - m_new)
        p = jnp.exp(s3 - m_new)
        l_scr[...] = l_scr[...] * corr + jnp.sum(p, -1, keepdims=True)
        pv = jnp.stack([_dot(p[h].reshape(2 * t, nk).astype(BF16), v_of(h)).reshape(2, t, LANES)
                        for h in range(4)], axis=0)
        acc_scr[...] = acc_scr[...] * corr + pv
        m_scr[...] = m_new

    @pl.when(j < n_past)
    def _():
        kpos = j * kb_size + lax.broadcasted_iota(I32, (1, kb_size), 1)
        run(kc_ref[0].astype(BF16), lambda h: vc_ref[0, :, h * LANES:(h + 1) * LANES].astype(BF16), kpos)

    @pl.when(j == n_past)
    def _():
        kpos = past_len + lax.broadcasted_iota(I32, (1, t), 1)
        run(kn_ref[0].astype(BF16), lambda h: vn_ref[0, :, h * LANES:(h + 1) * LANES].astype(BF16), kpos)
        lam = _diff_lambda(lam_ref, lam_init)
        for h in range(4):
            sl = slice(h * LANES, (h + 1) * LANES)
            carry = (m_scr[h], l_scr[h], acc_scr[h])
            o_ref[0, :, sl] = _diff_finish(carry, lam, gain_ref[...], lam_init).astype(o_ref.dtype)


def _diff_sample(q, k_cache, v_cache, k_new, v_new, lamv, gain, lam_init, kb_size):
    b, t, _ = q.shape
    n_past = k_cache.shape[1] // kb_size
    cache = pl.BlockSpec((1, kb_size, 512), lambda bi, j: (bi, jnp.minimum(j, n_past - 1), 0))
    new = pl.BlockSpec((1, t, 512), lambda bi, j: (bi, 0, 0))
    return pl.pallas_call(
        functools.partial(_diff_sample_kernel, t=t, kb_size=kb_size, n_past=n_past,
                          slopes=_alibi(4), lam_init=lam_init),
        grid=(b, n_past + 1),
        in_specs=[new, cache, cache, new, new,
                  pl.BlockSpec((4, HEAD_DIM), lambda bi, j: (0, 0)),
                  pl.BlockSpec((1, LANES), lambda bi, j: (0, 0))],
        out_specs=new,
        out_shape=jax.ShapeDtypeStruct((b, t, 512), BF16),
        scratch_shapes=[pltpu.VMEM((4, 2, t, 1), F32), pltpu.VMEM((4, 2, t, 1), F32),
                        pltpu.VMEM((4, 2, t, LANES), F32)],
        compiler_params=_cparams(("parallel", "arbitrary")),
        name="diff_sample",
    )(q, k_cache, v_cache, k_new, v_new, lamv, gain)


def _index_keys(qi, wi, ki_blk, allowed):
    acc = None
    for h in range(8):
        sc = _dot_nt(qi[:, h * HEAD_DIM:(h + 1) * HEAD_DIM], ki_blk)
        term = jnp.maximum(sc, 0.0) * wi[:, h:h + 1]
        acc = term if acc is None else acc + term
    idx = acc * (8.0 ** -0.5) + 0.0
    bits = pltpu.bitcast(idx, I32)
    key = jnp.where(bits < 0, bits ^ 0x7FFFFFFF, bits)
    return jnp.where(allowed, key, INT_MIN)


def _lane_fold(x):
    acc = x[:, :LANES]
    for j in range(1, x.shape[1] // LANES):
        acc = acc + x[:, j * LANES:(j + 1) * LANES]
    return acc


def _ones_where(pred):
    return jnp.where(pred, 1, 0).astype(I32)


def _kth_largest(count_ge, shape, n_sel):
    def body(it, lo):
        bit = 31 - it
        cand = lo + jnp.left_shift(jnp.int32(1), bit)
        return jnp.where(count_ge(cand) >= n_sel, cand, lo)
    return lax.fori_loop(0, 32, body, jnp.full(shape, INT_MIN, I32))


def _tie_position(count_eq_le, need, shape, nbits):
    def body(it, lo):
        bit = nbits - 1 - it
        cand = lo + jnp.left_shift(jnp.int32(1), bit)
        return jnp.where(count_eq_le(cand - 1) < need, cand, lo)
    return lax.fori_loop(0, nbits, body, jnp.zeros(shape, I32))


def _select_threshold(count_over, shape, n_sel, pos_bits):
    thr = _kth_largest(lambda cand: count_over(lambda key, kpos: key >= cand), shape, n_sel)
    need = n_sel - count_over(lambda key, kpos: key > thr)
    n_eq = count_over(lambda key, kpos: key == thr)
    has_tie = jnp.max(jnp.where((n_eq > need) & (thr > INT_MIN), 1, 0)) > 0
    pos_thr = lax.cond(
        has_tie,
        lambda: _tie_position(
            lambda p: count_over(lambda key, kpos: (key == thr) & (kpos <= p)), need, shape, pos_bits),
        lambda: jnp.full(shape, 2 ** 30, I32))
    return thr, pos_thr


def _dsa_attend(carry, q_ref, kblk, vblk, dist, sel, slopes):
    neg = jnp.where(sel, 0.0, NEG_INF)
    qs = []
    for s in range(4):
        slab = q_ref[0, :, s * LANES:(s + 1) * LANES]
        left = _lane_left(slab.shape)
        zero = jnp.zeros_like(slab)
        qs.append((jnp.where(left, slab, zero), jnp.where(left, zero, slab)))
    q_all = jnp.concatenate([qs[h % 4][h // 4] for h in range(8)], axis=0)
    r, kb = dist.shape
    s3 = _dot_nt(q_all, kblk).reshape(8, r, kb)
    s3 = jnp.stack([s3[h] - slopes[h] * dist + neg for h in range(8)], axis=0)
    return _flash_update(carry, s3, vblk)


def _dsa_finish(carry, o_ref):
    m, l, acc = carry
    o = acc / l
    for s in range(4):
        left = _lane_left(o[s].shape)
        o_ref[0, :, s * LANES:(s + 1) * LANES] = jnp.where(left, o[s], o[4 + s]).astype(o_ref.dtype)


def _dsa_prompt_kernel(q_ref, qi_ref, wi_ref, k_ref, vt_ref, ki_ref, o_ref, key_scr, qall_scr, qiall_scr,
                       *, qb, n_sel, slopes):
    i = pl.program_id(1)
    nkb = i + 1
    qpos = i * qb + lax.broadcasted_iota(I32, (1, qb), 1)

    for s in range(4):
        slab = q_ref[0, :, s * LANES:(s + 1) * LANES]
        left = _lane_left(slab.shape)
        zero = jnp.zeros_like(slab)
        qall_scr[s * qb:(s + 1) * qb, :] = jnp.where(left, slab, zero)
        qall_scr[(4 + s) * qb:(5 + s) * qb, :] = jnp.where(left, zero, slab)
    for h in range(8):
        qiall_scr[h * qb:(h + 1) * qb, :] = qi_ref[0, :, h * HEAD_DIM:(h + 1) * HEAD_DIM]
    wit = wi_ref[0].T

    def kpos_of(kb):
        return kb * qb + lax.broadcasted_iota(I32, (qb, 1), 0)

    def rows_of(kb):
        return pl.ds(pl.multiple_of(kb * qb, qb), qb)

    def fill(kb, c):
        allowed = kpos_of(kb) // CHUNK <= qpos // CHUNK
        sc = _dot_nt(ki_ref[0, rows_of(kb), :], qiall_scr[...])
        acc = None
        for h in range(8):
            term = jnp.maximum(sc[:, h * qb:(h + 1) * qb], 0.0) * wit[h:h + 1, :]
            acc = term if acc is None else acc + term
        idx = acc * (8.0 ** -0.5) + 0.0
        bits = pltpu.bitcast(idx, I32)
        key = jnp.where(bits < 0, bits ^ 0x7FFFFFFF, bits)
        key_scr[kb] = jnp.where(allowed, key, INT_MIN)
        return c
    lax.fori_loop(0, nkb, fill, 0)

    def count_over(pred_fn):
        def body(kb, c):
            ones = _ones_where(pred_fn(key_scr[kb], kpos_of(kb)))
            return c + jnp.sum(ones.reshape(qb // 8, 8, qb), axis=0)
        part = lax.fori_loop(0, nkb, body, jnp.zeros((8, qb), I32))
        return jnp.sum(part, axis=0, keepdims=True)

    thr, pos_thr = _select_threshold(count_over, (1, qb), n_sel, 12)

    def attend(kb, carry):
        m, l, acc = carry
        kpos = kpos_of(kb)
        key = key_scr[kb]
        sel = (key > thr) | ((key == thr) & (kpos <= pos_thr))
        sel = sel & (kpos // CHUNK <= qpos // CHUNK)
        neg = jnp.where(sel, 0.0, NEG_INF)
        dist = jnp.abs(qpos - kpos).astype(F32)
        st = _dot_nt(k_ref[0, rows_of(kb), :], qall_scr[...])
        s3 = jnp.stack([st[:, h * qb:(h + 1) * qb] - slopes[h] * dist + neg for h in range(8)], axis=0)
        m_new = jnp.maximum(m, jnp.max(s3, axis=1, keepdims=True))
        m_use = jnp.where(m_new == NEG_INF, 0.0, m_new)
        corr = jnp.exp(m - m_use)
        p = jnp.exp(s3 - m_use)
        l = l * corr + jnp.sum(p, axis=1, keepdims=True)
        p_all = jnp.concatenate([p[h].astype(BF16) for h in range(8)], axis=1)
        pv = _dot(vt_ref[0, kb], p_all)
        acc = acc * corr + jnp.stack([pv[:, h * qb:(h + 1) * qb] for h in range(8)], axis=0)
        return m_new, l, acc

    init = (jnp.full((8, 1, qb), NEG_INF, F32), jnp.zeros((8, 1, qb), F32), jnp.zeros((8, LANES, qb), F32))
    m, l, acc = lax.fori_loop(0, nkb, attend, init)
    ot = acc / l
    top = lax.broadcasted_iota(I32, (LANES, qb), 0) < HEAD_DIM
    for s in range(4):
        o_ref[0, :, s * LANES:(s + 1) * LANES] = jnp.where(top, ot[s], ot[4 + s]).T.astype(o_ref.dtype)


def _dsa_prompt(q, qi, wi, k, v, ki, qb):
    b, t, _ = q.shape
    n_sel = min(TOPK_MAX, t // 4)
    nkb = t // qb
    vt = jnp.swapaxes(v.reshape(b, nkb, qb, LANES), -1, -2)
    tile = lambda w: pl.BlockSpec((1, qb, w), lambda bi, i: (bi, i, 0))
    full = lambda w: pl.BlockSpec((1, t, w), lambda bi, i: (bi, 0, 0))
    return pl.pallas_call(
        functools.partial(_dsa_prompt_kernel, qb=qb, n_sel=n_sel, slopes=_alibi(8)),
        grid=(b, nkb),
        in_specs=[tile(512), tile(512), tile(LANES), full(LANES),
                  pl.BlockSpec((1, nkb, LANES, qb), lambda bi, i: (bi, 0, 0, 0)), full(HEAD_DIM)],
        out_specs=tile(512),
        out_shape=jax.ShapeDtypeStruct((b, t, 512), BF16),
        scratch_shapes=[pltpu.VMEM((nkb, qb, qb), I32), pltpu.VMEM((8 * qb, LANES), BF16),
                        pltpu.VMEM((8 * qb, HEAD_DIM), BF16)],
        compiler_params=_cparams(("parallel", "arbitrary")),
        name="dsa_prompt",
    )(q, qi, wi, k, vt, ki)


def _dsa_sample_kernel(q_ref, qi_ref, wi_ref, kc_ref, vc_ref, kic_ref, kn_ref, vn_ref, kin_ref, o_ref,
                       key_scr, keyn_scr, *, t, kb_size, n_past, n_sel, slopes):
    past_len = n_past * kb_size
    qpos = past_len + lax.broadcasted_iota(I32, (t, 1), 0)
    qi = qi_ref[0]
    wi = wi_ref[0]
    kpos_n = past_len + lax.broadcasted_iota(I32, (1, t), 1)

    def kpos_of(kb):
        return kb * kb_size + lax.broadcasted_iota(I32, (1, kb_size), 1)

    for kb in range(n_past):
        ki_blk = kic_ref[0, kb * kb_size:(kb + 1) * kb_size, :].astype(BF16)
        key_scr[kb] = _index_keys(qi, wi, ki_blk, True)
    keyn_scr[...] = _index_keys(qi, wi, kin_ref[0].astype(BF16), True)

    def count_over(pred_fn):
        part = _lane_fold(_ones_where(pred_fn(key_scr[0], kpos_of(0))))
        for kb in range(1, n_past):
            part = part + _lane_fold(_ones_where(pred_fn(key_scr[kb], kpos_of(kb))))
        new = jnp.sum(_ones_where(pred_fn(keyn_scr[...], kpos_n)), axis=-1, keepdims=True)
        return jnp.sum(part, axis=-1, keepdims=True) + new

    thr, pos_thr = _select_threshold(count_over, (t, 1), n_sel, 13)
    carry = _flash_init(8, t, LANES)

    def sel_of(key, kpos):
        return (key > thr) | ((key == thr) & (kpos <= pos_thr))

    for kb in range(n_past):
        kpos = kpos_of(kb)
        dist = jnp.abs(qpos - kpos).astype(F32)
        kblk = kc_ref[0, kb * kb_size:(kb + 1) * kb_size, :].astype(BF16)
        vblk = vc_ref[0, kb * kb_size:(kb + 1) * kb_size, :].astype(BF16)
        carry = _dsa_attend(carry, q_ref, kblk, vblk, dist, sel_of(key_scr[kb], kpos), slopes)
    dist = jnp.abs(qpos - kpos_n).astype(F32)
    carry = _dsa_attend(carry, q_ref, kn_ref[0].astype(BF16), vn_ref[0].astype(BF16), dist,
                        sel_of(keyn_scr[...], kpos_n), slopes)
    _dsa_finish(carry, o_ref)


def _dsa_sample(q, qi, wi, k_cache, v_cache, ki_cache, k_new, v_new, ki_new, kb_size):
    b, t, _ = q.shape
    past = k_cache.shape[1]
    n_past = past // kb_size
    n_sel = min(TOPK_MAX, (past + t) // 4)
    per_b = lambda rows, w: pl.BlockSpec((1, rows, w), lambda bi: (bi, 0, 0))
    return pl.pallas_call(
        functools.partial(_dsa_sample_kernel, t=t, kb_size=kb_size, n_past=n_past, n_sel=n_sel,
                          slopes=_alibi(8)),
        grid=(b,),
        in_specs=[per_b(t, 512), per_b(t, 512), per_b(t, LANES),
                  per_b(past, LANES), per_b(past, LANES), per_b(past, HEAD_DIM),
                  per_b(t, LANES), per_b(t, LANES), per_b(t, HEAD_DIM)],
        out_specs=per_b(t, 512),
        out_shape=jax.ShapeDtypeStruct((b, t, 512), BF16),
        scratch_shapes=[pltpu.VMEM((n_past, t, kb_size), I32), pltpu.VMEM((t, t), I32)],
        compiler_params=_cparams(("parallel",)),
        name="dsa_sample",
    )(q, qi, wi, k_cache, v_cache, ki_cache, k_new, v_new, ki_new)


def _pad_cols(w, total):
    return jnp.pad(w, ((0, 0), (0, total - w.shape[1])))


def _q_perm():
    idx = []
    for s in range(4):
        idx.extend(range(s * HEAD_DIM, (s + 1) * HEAD_DIM))
        idx.extend(range((4 + s) * HEAD_DIM, (5 + s) * HEAD_DIM))
    return np.asarray(idx, np.int32)


def _even_layer(xp, xs, cache_k, cache_v, shift_s, wkv_s, w_in, sink, rw, w_out, ffn, alpha, sizes):
    (mu, w0, w_up, a0, a_up, g_up, k_k, k_a, r_k, lnx_g, lnx_b) = rw
    perm = _q_perm()
    w_in_p = jnp.concatenate([w_in[:, :512][:, perm], w_in[:, 512:]], axis=1).astype(BF16)
    w_out_p = jnp.concatenate([w_out[:512][perm], w_out[512:]], axis=0).astype(BF16)
    groups = ((0, 512, ((0, 512, HEAD_DIM ** -0.5),)),
              (512, 128, ((1, 128, 1.0), (2, 128, 1.0))),
              (640, 128, ((3, 128, 1.0), (4, 128, 1.0))),
              (768, 1792, ((5, 1792, 1.0),)))
    out_defs = ((512, BF16), (128, F32), (128, BF16), (128, F32), (128, BF16), (1792, F32))
    wup_p = jnp.concatenate([w_up, jnp.zeros_like(w_up)], axis=0).astype(BF16)
    aup_p = jnp.concatenate([jnp.zeros_like(a_up), a_up], axis=0).astype(BF16)
    gup = g_up.astype(BF16)
    row = lambda v: v.reshape(1, -1)
    g1, b1, g2, b2, wu, wd = ffn

    def run(x, state, tm, qb, scan_tb):
        b, t, d = x.shape
        n = b * t
        qa, ka, ka_bf, va, va_bf, pb = _project(x.reshape(n, d), w_in_p, groups, out_defs, tm)
        r3 = lambda a: a.reshape(b, t, a.shape[-1])
        if state is None:
            oa = _swa(r3(qa), r3(ka_bf), r3(ka_bf), r3(va_bf), r3(va_bf), sink,
                      qb=qb, prev_is_cache=False, pos_base=0)
            k_buf = r3(ka)[:, -WINDOW:]
            v_buf = r3(va)[:, -WINDOW:]
            shift = jnp.zeros((b, 1, pb.shape[-1]), F32)
            s0 = jnp.zeros((b, 8, HEAD_DIM, HEAD_DIM), F32)
        else:
            ck, cv, shift, s0 = state
            ck2 = ck.reshape(b, WINDOW, LANES)
            cv2 = cv.reshape(b, WINDOW, LANES)
            oa = _swa(r3(qa), ck2, r3(ka_bf), cv2, r3(va_bf), sink,
                      qb=qb, prev_is_cache=True, pos_base=sizes["past"])
            k_buf = jnp.concatenate([ck2, r3(ka)], axis=1)[:, -WINDOW:]
            v_buf = jnp.concatenate([cv2, r3(va)], axis=1)[:, -WINDOW:]
        pb3 = r3(pb)
        ob, s_bd = _rwkv(pb3, shift, row(mu), row(w0), wup_p, row(a0), aup_p, gup, row(k_k), row(k_a),
                         row(r_k), row(lnx_g), row(lnx_b), _state_to_blockdiag(s0), scan_tb)
        s_t = _state_from_blockdiag(s_bd)
        xo = _out_ffn(x.reshape(n, d), oa.reshape(n, 512), ob.reshape(n, 512), w_out_p, row(g1), row(b1),
                      wu.astype(BF16), wd.astype(BF16), row(g2), row(b2), alpha, tm)
        st = (k_buf.reshape(b, WINDOW, 2, HEAD_DIM), v_buf.reshape(b, WINDOW, 2, HEAD_DIM),
              pb3[:, -1:], s_t)
        return xo.reshape(b, t, d), st

    xp2, st_p = run(xp, None, sizes["tm_p"], sizes["swa_qb"], sizes["scan_tb_p"])
    xs2, st_s = run(xs, (cache_k, cache_v, shift_s, wkv_s), sizes["tm_s"], xs.shape[1], xs.shape[1])
    return xp2, xs2, st_p, st_s


def _odd_layer(xp, xs, caches, w_in, lam_vecs, subln_g, w_out, ffn, alpha, lam_init, sizes):
    perm = _q_perm()
    scale = HEAD_DIM ** -0.5
    w_p = jnp.concatenate([
        w_in[:, :1536], w_in[:, 1536:2048][:, perm], w_in[:, 2048:2816],
        _pad_cols(w_in[:, 2816:2880], LANES), _pad_cols(w_in[:, 2880:2888], LANES)], axis=1).astype(BF16)
    w_out_p = jnp.concatenate([w_out[:512], w_out[512:][perm]], axis=0).astype(BF16)
    groups = ((0, 512, ((0, 512, scale),)),
              (512, 512, ((1, 512, 1.0), (2, 512, 1.0))),
              (1024, 512, ((3, 512, 1.0), (4, 512, 1.0))),
              (1536, 512, ((5, 512, scale),)),
              (2048, 128, ((6, 128, 1.0), (7, 128, 1.0))),
              (2176, 128, ((8, 128, 1.0), (9, 128, 1.0))),
              (2304, 512, ((10, 512, scale),)),
              (2816, 128, ((11, 64, 1.0), (12, 64, 1.0))),
              (2944, 128, ((13, 128, 1.0),)))
    out_defs = ((512, BF16), (512, F32), (512, BF16), (512, F32), (512, BF16), (512, BF16),
                (128, F32), (128, BF16), (128, F32), (128, BF16), (512, BF16),
                (64, F32), (64, BF16), (128, F32))
    row = lambda v: v.reshape(1, -1)
    g1, b1, g2, b2, wu, wd = ffn
    lamv = jnp.stack(lam_vecs, axis=0)
    gain = row(subln_g)

    def run(x, state, tm):
        b, t, d = x.shape
        n = b * t
        (qc, kc, kc_bf, vc, vc_bf, qd, kd, kd_bf, vd, vd_bf, qi, ki, ki_bf, wi) = _project(
            x.reshape(n, d), w_p, groups, out_defs, tm)
        r3 = lambda a: a.reshape(b, t, a.shape[-1])
        if state is None:
            oc = _diff_prompt(r3(qc), r3(kc_bf), r3(vc_bf), lamv, gain, lam_init, sizes["diff_qb"])
            od = _dsa_prompt(r3(qd), r3(qi), r3(wi), r3(kd_bf), r3(vd_bf), r3(ki_bf), sizes["dsa_qb"])
        else:
            c_k, c_v, d_k, d_v, d_ki = state
            past = c_k.shape[1]
            oc = _diff_sample(r3(qc), c_k.reshape(b, past, 512), c_v.reshape(b, past, 512),
                              r3(kc_bf), r3(vc_bf), lamv, gain, lam_init, sizes["diff_kb_s"])
            od = _dsa_sample(r3(qd), r3(qi), r3(wi), d_k.reshape(b, past, LANES), d_v.reshape(b, past, LANES),
                             d_ki, r3(kd_bf), r3(vd_bf), r3(ki_bf), sizes["dsa_kb_s"])
        xo = _out_ffn(x.reshape(n, d), oc.reshape(n, 512), od.reshape(n, 512), w_out_p, row(g1), row(b1),
                      wu.astype(BF16), wd.astype(BF16), row(g2), row(b2), alpha, tm)
        rows = (kc.reshape(b, t, 4, 2, HEAD_DIM), vc.reshape(b, t, 4, 2 * HEAD_DIM),
                kd.reshape(b, t, 2, HEAD_DIM), vd.reshape(b, t, 2, HEAD_DIM), ki.reshape(b, t, HEAD_DIM))
        return xo.reshape(b, t, d), rows

    xp2, st_p = run(xp, None, sizes["tm_p"])
    xs2, st_s = run(xs, caches, sizes["tm_s"])
    return xp2, xs2, st_p, st_s


def _sizes(xp, xs, past):
    return dict(tm_p=min(512, xp.shape[0] * xp.shape[1]), tm_s=min(512, xs.shape[0] * xs.shape[1]),
                swa_qb=min(256, xp.shape[1]), scan_tb_p=min(128, xp.shape[1]),
                diff_qb=min(256, xp.shape[1]),
                dsa_qb=min(256, xp.shape[1]), diff_kb_s=min(1024, past), dsa_kb_s=min(512, past),
                past=past)


def kernel(x_prompt, x_sample, cache_a_k, cache_a_v, state_b_shift, state_b_wkv, cache_c_k, cache_c_v, cache_d_k, cache_d_v, cache_d_kidx, w_in_even, sink_a, mu_b, w0_b, w_up_b, a0_b, a_up_b, g_up_b, k_k_b, k_a_b, r_k_b, lnx_g_b, lnx_b_b, w_out_even, w_in_odd, lam_q1_c, lam_k1_c, lam_q2_c, lam_k2_c, subln_g_c, w_out_odd, ln_mix_g, ln_mix_b, ln_ffn_g, ln_ffn_b, w_ff_up, w_ff_down):
    depth = ln_mix_g.shape[0]
    alpha = (2 * depth) ** 0.25
    past = cache_c_k.shape[2]
    sizes = _sizes(x_prompt, x_sample, past)
    xp, xs = x_prompt, x_sample
    even_p, even_s, odd_p, odd_s = [], [], [], []
    for layer in range(depth):
        i = layer // 2
        ffn = (ln_mix_g[layer], ln_mix_b[layer], ln_ffn_g[layer], ln_ffn_b[layer],
               w_ff_up[layer], w_ff_down[layer])
        if layer % 2 == 0:
            rw = (mu_b[i], w0_b[i], w_up_b[i], a0_b[i], a_up_b[i], g_up_b[i], k_k_b[i], k_a_b[i],
                  r_k_b[i].reshape(-1), lnx_g_b[i], lnx_b_b[i])
            xp, xs, st_p, st_s = _even_layer(
                xp, xs, cache_a_k[i], cache_a_v[i], state_b_shift[i], state_b_wkv[i],
                w_in_even[i], sink_a[i], rw, w_out_even[i], ffn, alpha, sizes)
            even_p.append(st_p)
            even_s.append(st_s)
        else:
            lam_init = 0.8 - 0.6 * math.exp(-0.3 * layer)
            xp, xs, st_p, st_s = _odd_layer(
                xp, xs, (cache_c_k[i], cache_c_v[i], cache_d_k[i], cache_d_v[i], cache_d_kidx[i]),
                w_in_odd[i], (lam_q1_c[i], lam_k1_c[i], lam_q2_c[i], lam_k2_c[i]), subln_g_c[i],
                w_out_odd[i], ffn, alpha, lam_init, sizes)
            odd_p.append(st_p)
            odd_s.append(st_s)
    stack = lambda states: [jnp.stack(z, axis=0) for z in zip(*states)]
    a_k_p, a_v_p, b_shift_p, b_wkv_p = stack(even_p)
    a_k_s, a_v_s, b_shift_s, b_wkv_s = stack(even_s)
    c_k_p, c_v_p, d_k_p, d_v_p, d_kidx_p = stack(odd_p)
    c_k_s, c_v_s, d_k_s, d_v_s, d_kidx_s = stack(odd_s)
    return (xp, xs, a_k_p, a_v_p, b_shift_p, b_wkv_p, c_k_p, c_v_p, d_k_p, d_v_p, d_kidx_p,
            a_k_s, a_v_s, b_shift_s, b_wkv_s, c_k_s, c_v_s, d_k_s, d_v_s, d_kidx_s)
```

```python
import functools
import math

import numpy as np
import jax
import jax.numpy as jnp
from jax import lax
from jax.experimental import pallas as pl
from jax.experimental.pallas import tpu as pltpu

F32 = jnp.float32
BF16 = jnp.bfloat16
I32 = jnp.int32

LANES = 128
HEAD_DIM = 64
CHUNK = 64
LN_EPS = 1e-5
RWKV_GN_EPS = 64e-5
WINDOW = 128
TOPK_MAX = 256
NEG_INF = float("-inf")
INT_MIN = -(2 ** 31)
VMEM_LIMIT = 56 * 1024 * 1024


def _cparams(sem):
    return pltpu.CompilerParams(dimension_semantics=sem, vmem_limit_bytes=VMEM_LIMIT)


LOG2E = math.log2(math.e)


def _alibi(n):
    return [float(2.0 ** (-8.0 * (i + 1) / n)) * LOG2E for i in range(n)]


def _bf16_parts(c):
    parts, rem = [], np.float64(c)
    for _ in range(3):
        piece = np.float64(np.asarray(rem, np.float32).astype(BF16).astype(np.float32))
        parts.append(float(piece))
        rem = rem - piece
    return parts


def _alibi_q_features(rows, slope):
    parts = _bf16_parts(slope)
    lane = lax.broadcasted_iota(I32, (rows, LANES), 1)
    out = jnp.zeros((rows, LANES), F32)
    for j, v in enumerate([16.0 * p for p in parts] + parts):
        out = jnp.where(lane == j, v, out)
    return out.astype(BF16)


def _alibi_k_features(kpos):
    lane = lax.broadcasted_iota(I32, (kpos.shape[0], LANES), 1)
    hi = (kpos // 16).astype(F32)
    lo = (kpos % 16).astype(F32)
    return jnp.where(lane < 3, hi, jnp.where(lane < 6, lo, 0.0)).astype(BF16)


def _dot(a, b):
    return jnp.dot(a, b, preferred_element_type=F32)


def _dot_nt(a, b):
    return lax.dot_general(a, b, (((1,), (1,)), ((), ())), preferred_element_type=F32)


def _lane_left(shape):
    return lax.broadcasted_iota(I32, shape, len(shape) - 1) < HEAD_DIM


def _segsum64(x):
    left = _lane_left(x.shape)
    sl = jnp.sum(jnp.where(left, x, 0.0), axis=-1, keepdims=True)
    sr = jnp.sum(jnp.where(left, 0.0, x), axis=-1, keepdims=True)
    return jnp.where(left, sl, sr)


def _proj_kernel(x_ref, w_ref, *out_refs, groups):
    xb = x_ref[...].astype(BF16)
    for off, width, outs in groups:
        r = _dot(xb, w_ref[:, off:off + width])
        for idx, ow, scale in outs:
            v = r if ow == width else r[:, :ow]
            if scale != 1.0:
                v = v * scale
            out_refs[idx][...] = v.astype(out_refs[idx].dtype)


def _project(x, w_bf, groups, out_defs, tm):
    n, d = x.shape
    cols = w_bf.shape[1]
    out_shape = [jax.ShapeDtypeStruct((n, ow), dt) for ow, dt in out_defs]
    out_specs = [pl.BlockSpec((tm, ow), lambda i: (i, 0)) for ow, _ in out_defs]
    return pl.pallas_call(
        functools.partial(_proj_kernel, groups=groups),
        grid=(n // tm,),
        in_specs=[pl.BlockSpec((tm, d), lambda i: (i, 0)),
                  pl.BlockSpec((d, cols), lambda i: (0, 0))],
        out_specs=out_specs,
        out_shape=out_shape,
        compiler_params=_cparams(("parallel",)),
        name="in_proj",
    )(x, w_bf)


def _swa_kernel(sink_ref, q_ref, kp_ref, kc_ref, vp_ref, vc_ref, o_ref, *, qb, pos_base, slopes):
    i = pl.program_id(1)
    q0 = pos_base + i * qb
    kp = kp_ref[0].astype(BF16)
    kc = kc_ref[0].astype(BF16)
    vp = vp_ref[0].astype(BF16)
    vc = vc_ref[0].astype(BF16)
    qpos = q0 + lax.broadcasted_iota(I32, (qb, 1), 0)
    kpos_p = q0 - WINDOW + lax.broadcasted_iota(I32, (1, WINDOW), 1)
    kpos_c = q0 + lax.broadcasted_iota(I32, (1, qb), 1)
    qch = qpos // CHUNK

    def allowed(kpos):
        kch = kpos // CHUNK
        return (kpos >= 0) & (kch >= qch - WINDOW // CHUNK) & (kch <= qch)

    al_p, al_c = allowed(kpos_p), allowed(kpos_c)
    dist_p = jnp.abs(qpos - kpos_p).astype(F32)
    dist_c = jnp.abs(qpos - kpos_c).astype(F32)
    for s in range(4):
        slab = q_ref[0, :, s * LANES:(s + 1) * LANES]
        left = _lane_left(slab.shape)
        res = []
        for side in range(2):
            h = 4 * side + s
            qx = jnp.where(left if side == 0 else jnp.logical_not(left), slab, jnp.zeros_like(slab))
            lp = jnp.where(al_p, _dot_nt(qx, kp) - slopes[h] * dist_p, NEG_INF)
            lc = jnp.where(al_c, _dot_nt(qx, kc) - slopes[h] * dist_c, NEG_INF)
            sk = sink_ref[h] * LOG2E
            m = jnp.maximum(jnp.maximum(jnp.max(lp, -1, keepdims=True), jnp.max(lc, -1, keepdims=True)), sk)
            pp = jnp.exp2(lp - m)
            pc = jnp.exp2(lc - m)
            den = jnp.sum(pp, -1, keepdims=True) + jnp.sum(pc, -1, keepdims=True) + jnp.exp2(sk - m)
            o = _dot(pp.astype(BF16), vp) + _dot(pc.astype(BF16), vc)
            res.append(o / den)
        o_ref[0, :, s * LANES:(s + 1) * LANES] = jnp.where(left, res[0], res[1]).astype(o_ref.dtype)


def _swa(q, k_prev_src, k_cur, v_prev_src, v_cur, sink, *, qb, prev_is_cache, pos_base):
    b, t, _ = q.shape
    nq = t // qb
    if prev_is_cache:
        prev_map = lambda bi, i: (bi, 0, 0)
    else:
        r = qb // WINDOW
        prev_map = lambda bi, i: (bi, jnp.maximum(i * r - 1, 0), 0)
    cur_map = lambda bi, i: (bi, i, 0)
    return pl.pallas_call(
        functools.partial(_swa_kernel, qb=qb, pos_base=pos_base, slopes=_alibi(8)),
        grid=(b, nq),
        in_specs=[pl.BlockSpec(memory_space=pltpu.SMEM),
                  pl.BlockSpec((1, qb, 512), cur_map),
                  pl.BlockSpec((1, WINDOW, LANES), prev_map),
                  pl.BlockSpec((1, qb, LANES), cur_map),
                  pl.BlockSpec((1, WINDOW, LANES), prev_map),
                  pl.BlockSpec((1, qb, LANES), cur_map)],
        out_specs=pl.BlockSpec((1, qb, 512), cur_map),
        out_shape=jax.ShapeDtypeStruct((b, t, 512), BF16),
        compiler_params=_cparams(("parallel", "parallel")),
        name="swa_sink",
    )(sink, q, k_prev_src, k_cur, v_prev_src, v_cur)


def _swa_prompt_kernel(sink_ref, q_ref, kp_ref, kc_ref, vtp_ref, vtc_ref, o_ref, qall_scr, *, qb, slopes):
    i = pl.program_id(1)
    q0 = i * qb
    qpos = q0 + lax.broadcasted_iota(I32, (1, qb), 1)
    for s in range(4):
        slab = q_ref[0, :, s * LANES:(s + 1) * LANES]
        left = _lane_left(slab.shape)
        zero = jnp.zeros_like(slab)
        qall_scr[s * qb:(s + 1) * qb, :LANES] = jnp.where(left, slab, zero)
        qall_scr[(4 + s) * qb:(5 + s) * qb, :LANES] = jnp.where(left, zero, slab)
    for h in range(8):
        qall_scr[h * qb:(h + 1) * qb, LANES:] = _alibi_q_features(qb, slopes[h])
    kpos_p = q0 - WINDOW + lax.broadcasted_iota(I32, (WINDOW, 1), 0)
    kpos_c = q0 + lax.broadcasted_iota(I32, (qb, 1), 0)
    qch = qpos // CHUNK
    first = qch - WINDOW // CHUNK
    neg_p = jnp.where((kpos_p >= 0) & (kpos_p // CHUNK >= first), 0.0, NEG_INF)
    kch_c = kpos_c // CHUNK
    neg_c = jnp.where((kch_c >= first) & (kch_c <= qch), 0.0, NEG_INF)
    ahead = jnp.maximum(kpos_c - qpos, 0).astype(F32)
    qall = qall_scr[...]
    st_p = _dot_nt(jnp.concatenate([kp_ref[0], _alibi_k_features(jnp.maximum(kpos_p, 0))], axis=1), qall)
    st_c = _dot_nt(jnp.concatenate([kc_ref[0], _alibi_k_features(kpos_c)], axis=1), qall)
    qposf = qpos.astype(F32)
    pps, pcs, dens = [], [], []
    for h in range(8):
        sp = st_p[:, h * qb:(h + 1) * qb] + neg_p
        sc = st_c[:, h * qb:(h + 1) * qb] - (2.0 * slopes[h]) * ahead + neg_c
        sk = sink_ref[h] * LOG2E + slopes[h] * qposf
        m = jnp.maximum(jnp.maximum(jnp.max(sp, 0, keepdims=True), jnp.max(sc, 0, keepdims=True)), sk)
        pp = jnp.exp2(sp - m)
        pc = jnp.exp2(sc - m)
        dens.append(jnp.sum(pp, 0, keepdims=True) + jnp.sum(pc, 0, keepdims=True) + jnp.exp2(sk - m))
        pps.append(pp.astype(BF16))
        pcs.append(pc.astype(BF16))
    ot = (_dot(vtp_ref[0], jnp.concatenate(pps, axis=1))
          + _dot(vtc_ref[0], jnp.concatenate(pcs, axis=1)))
    top = lax.broadcasted_iota(I32, (LANES, qb), 0) < HEAD_DIM
    for s in range(4):
        o_l = ot[:, s * qb:(s + 1) * qb] / dens[s]
        o_r = ot[:, (4 + s) * qb:(5 + s) * qb] / dens[4 + s]
        o_ref[0, :, s * LANES:(s + 1) * LANES] = jnp.where(top, o_l, o_r).T.astype(o_ref.dtype)


def _swa_prompt(q, k, v, sink, qb):
    b, t, _ = q.shape
    r = qb // WINDOW
    vt = jnp.swapaxes(v, -1, -2)
    cur_map = lambda bi, i: (bi, i, 0)
    prev_map = lambda bi, i: (bi, jnp.maximum(i * r - 1, 0), 0)
    return pl.pallas_call(
        functools.partial(_swa_prompt_kernel, qb=qb, slopes=_alibi(8)),
        grid=(b, t // qb),
        in_specs=[pl.BlockSpec(memory_space=pltpu.SMEM),
                  pl.BlockSpec((1, qb, 512), cur_map),
                  pl.BlockSpec((1, WINDOW, LANES), prev_map),
                  pl.BlockSpec((1, qb, LANES), cur_map),
                  pl.BlockSpec((1, LANES, WINDOW), lambda bi, i: (bi, 0, jnp.maximum(i * r - 1, 0))),
                  pl.BlockSpec((1, LANES, qb), lambda bi, i: (bi, 0, i))],
        out_specs=pl.BlockSpec((1, qb, 512), cur_map),
        out_shape=jax.ShapeDtypeStruct((b, t, 512), BF16),
        scratch_shapes=[pltpu.VMEM((8 * qb, 2 * LANES), BF16)],
        compiler_params=_cparams(("parallel", "parallel")),
        name="swa_prompt",
    )(sink, q, k, k, vt, vt)


def _sigmoid(x):
    return 1.0 / (1.0 + jnp.exp(-x))


def _softplus(x):
    return jnp.maximum(x, 0.0) + jnp.log(1.0 + jnp.exp(-jnp.abs(x)))


def _rwkv_prep_kernel(pb_ref, pbprev_ref, shift_ref, mu_ref, w0_ref, wup_ref, a0_ref, aup_ref, gup_ref,
                      kk_ref, ka_ref, rk_ref,
                      r_out, w_out, k_out, v_out, kk_out, b_out, g_out, bonus_out):
    i = pl.program_id(1)
    pb = pb_ref[0]
    tm = pb.shape[0]
    prev_row = jnp.where(i == 0, shift_ref[0], pbprev_ref[0, 7:8, :])
    rolled = pltpu.roll(pb, 1, axis=0)
    row = lax.broadcasted_iota(I32, (tm, 1), 0)
    prev = jnp.where(row == 0, prev_row, rolled)
    xm = pb + (prev - pb) * mu_ref[...]
    r = xm[:, 0:512]
    k = xm[:, 512:1024]
    v = xm[:, 1024:1536]
    wa = xm[:, 1536:1664]
    gl = xm[:, 1664:1792]
    lw = _dot(jnp.tanh(wa).astype(BF16), wup_ref[...])
    la = _dot(wa.astype(BF16), aup_ref[...])
    w_log = -_softplus(-(w0_ref[...] + lw)) - 0.5
    log_decay = -jnp.exp(w_log)
    a = _sigmoid(a0_ref[...] + la)
    g = _dot(_sigmoid(gl).astype(BF16), gup_ref[...])
    kk = k * kk_ref[...]
    k2 = k * (1.0 + (a - 1.0) * ka_ref[...])
    rkk = r * k2 * rk_ref[...]
    for p in range(4):
        sl = slice(p * LANES, (p + 1) * LANES)
        kkp = kk[:, sl]
        kkn = kkp * lax.rsqrt(_segsum64(kkp * kkp) + 1e-12)
        kk_out[0, p] = kkn
        b_out[0, p] = kkn * a[:, sl]
        bonus_out[0, :, sl] = _segsum64(rkk[:, sl]) * v[:, sl]
        r_out[0, p] = r[:, sl]
        w_out[0, p] = log_decay[:, sl]
        k_out[0, p] = k2[:, sl]
        v_out[0, p] = v[:, sl]
    g_out[0] = g


RWKV_CHUNK = 16
RWKV_PAIRS_PER_STEP = 4


def _split2(x):
    hi = x.astype(BF16)
    return hi, (x - hi.astype(F32)).astype(BF16)


def _dot3(a, b):
    return _dot(a[0], b[0]) + (_dot(a[0], b[1]) + _dot(a[1], b[0]))


def _rwkv_scan_scratch(tb):
    nc, c2 = tb // RWKV_CHUNK, 2 * RWKV_CHUNK
    return [pltpu.VMEM((4, LANES, LANES), F32),
            pltpu.VMEM((4, nc, c2, LANES), BF16), pltpu.VMEM((4, nc, c2, LANES), BF16),
            pltpu.VMEM((4, nc, LANES, c2), BF16), pltpu.VMEM((4, nc, LANES, c2), BF16),
            pltpu.VMEM((4, nc, LANES, LANES), F32),
            pltpu.VMEM((4, tb, LANES), F32), pltpu.VMEM((4, tb, LANES), F32)]


def _rwkv_scan_kernel(r_ref, lw_ref, k_ref, v_ref, kk_ref, b_ref, s0_ref, y_ref, sT_ref,
                      st_scr, l1hi_scr, l1lo_scr, l2hi_scr, l2lo_scr, pcol_scr, u0_scr, y0_scr, *, tb):
    ti = pl.program_id(1)
    c_sz = RWKV_CHUNK
    nc = tb // c_sz

    @pl.when(ti == 0)
    def _():
        st_scr[...] = s0_ref[0]

    row = lax.broadcasted_iota(I32, (tb, tb), 0)
    col = lax.broadcasted_iota(I32, (tb, tb), 1)
    same = (row // c_sz) == (col // c_sz)
    strict = same & (col < row)
    incl = same & (col <= row)
    eye_t = jnp.where(row == col, 1.0, 0.0)
    cum_lhs = jnp.concatenate([jnp.where(incl, 1.0, 0.0), jnp.where(same, 1.0, 0.0)], axis=0).astype(BF16)
    left = _lane_left((tb, LANES))
    r128 = lax.broadcasted_iota(I32, (LANES, LANES), 0)
    c128 = lax.broadcasted_iota(I32, (LANES, LANES), 1)
    blockdiag = (r128 < HEAD_DIM) == (c128 < HEAD_DIM)

    left2 = jnp.concatenate([left, left], axis=1)
    npar = RWKV_PAIRS_PER_STEP

    def prepare(i, carry):
        ps = [i * npar + q for q in range(npar)]
        qs = range(npar)
        sides = range(2)
        ins = [[ref[0, p] for ref in (r_ref, lw_ref, k_ref, v_ref, kk_ref, b_ref)] for p in ps]

        def split3(x):
            x1 = x.astype(BF16)
            rem = x - x1.astype(F32)
            x2 = rem.astype(BF16)
            return x1, x2, (rem - x2.astype(F32)).astype(BF16)

        lws = [split3(x[1]) for x in ins]
        cums = [_dot(cum_lhs, l[0]) + (_dot(cum_lhs, l[1]) + _dot(cum_lhs, l[2])) for l in lws]
        al, rt, vs, rhs, lhs, bbt, kbt, pc_rows = [], [], [], [], [], [], [], []
        for q in qs:
            r, lw, k, v, kk, b = ins[q]
            cs, cse = cums[q][:tb], cums[q][tb:]
            p_inv = jnp.exp(-cs)
            p_end = jnp.exp(cse - cs)
            al.append(-kk * jnp.exp(cs - lw))
            rt.append(r * jnp.exp(cs))
            vs.append(_split2(v))
            rhs.append(_split2(jnp.concatenate([b * p_inv, k * p_inv], axis=0)))
            zero = jnp.zeros_like(v)
            half = lambda z, s: jnp.where(left, z, zero) if s == 0 else jnp.where(left, zero, z)
            lhs.append([_split2(jnp.concatenate([half(al[q], s), half(rt[q], s)], axis=0)) for s in sides])
            bbt.append(_split2((b * p_end).T))
            kbt.append(_split2((k * p_end).T))
            pc_rows.append(jnp.exp(cse))
        gram = [[_dot_nt(lhs[q][s][0], rhs[q][0]) + (_dot_nt(lhs[q][s][0], rhs[q][1])
                                                      + _dot_nt(lhs[q][s][1], rhs[q][0]))
                 for s in sides] for q in qs]
        a1 = [[_split2(jnp.where(strict, gram[q][s][:tb, :tb], 0.0)) for s in sides] for q in qs]
        a2 = [[_split2(jnp.where(strict, gram[q][s][:tb, tb:], 0.0)) for s in sides] for q in qs]
        b1 = [[_split2(jnp.where(incl, gram[q][s][tb:, :tb], 0.0)) for s in sides] for q in qs]
        b2 = [[_split2(jnp.where(incl, gram[q][s][tb:, tb:], 0.0)) for s in sides] for q in qs]
        xv = [jnp.where(left, _dot3(a2[q][0], vs[q]), _dot3(a2[q][1], vs[q])) for q in qs]
        pw = [[_dot3(a1[q][s], a1[q][s]) for s in sides] for q in qs]
        x = [jnp.concatenate([al[q], xv[q]], axis=1) for q in qs]
        m = a1
        for step in range(4):
            xs = [_split2(x[q]) for q in qs]
            x = [x[q] + jnp.where(left2, _dot3(m[q][0], xs[q]), _dot3(m[q][1], xs[q])) for q in qs]
            if step < 3:
                m = [[_split2(pw[q][s]) for s in sides] for q in qs]
                if step < 2:
                    pw = [[_dot3(m[q][s], m[q][s]) for s in sides] for q in qs]
        xs = [_split2(x[q]) for q in qs]
        bx = [jnp.where(left2, _dot(b1[q][0][0], xs[q][0]), _dot(b1[q][1][0], xs[q][0])) for q in qs]
        bv = [jnp.where(left, _dot(b2[q][0][0], vs[q][0]), _dot(b2[q][1][0], vs[q][0])) for q in qs]
        for q in qs:
            p = ps[q]
            ah = x[q][:, :LANES]
            rh = rt[q] + bx[q][:, :LANES]
            u0_scr[p] = x[q][:, LANES:]
            y0_scr[p] = bx[q][:, LANES:] + bv[q]
            pct = pc_rows[q].T
            for c in range(nc):
                rows = slice(c * c_sz, (c + 1) * c_sz)
                l1 = _split2(jnp.concatenate([ah[rows], rh[rows]], axis=0))
                l1hi_scr[p, c], l1lo_scr[p, c] = l1
                l2hi_scr[p, c] = jnp.concatenate([bbt[q][0][:, rows], kbt[q][0][:, rows]], axis=1)
                l2lo_scr[p, c] = jnp.concatenate([bbt[q][1][:, rows], kbt[q][1][:, rows]], axis=1)
                pcol_scr[p, c] = jnp.broadcast_to(pct[:, c * c_sz:c * c_sz + 1], (LANES, LANES))
        return carry

    lax.fori_loop(0, 4 // npar, prepare, 0)

    st = [st_scr[p] for p in range(4)]
    for c in range(nc):
        rows = slice(c * c_sz, (c + 1) * c_sz)
        res = [_dot3((l1hi_scr[p, c], l1lo_scr[p, c]), _split2(st[p])) for p in range(4)]
        for p in range(4):
            y_ref[0, p, rows, :] = res[p][c_sz:] + y0_scr[p, rows, :]
        w2 = [_split2(jnp.concatenate([res[p][:c_sz] + u0_scr[p, rows, :], v_ref[0, p, rows, :]], axis=0))
              for p in range(4)]
        upd = [_dot3((l2hi_scr[p, c], l2lo_scr[p, c]), w2[p]) for p in range(4)]
        st = [st[p] * pcol_scr[p, c] + jnp.where(blockdiag, upd[p], 0.0) for p in range(4)]
    for p in range(4):
        st_scr[p] = st[p]

    @pl.when(ti == pl.num_programs(1) - 1)
    def _():
        sT_ref[0] = st_scr[...]


def _rwkv_scan(r, lw, k, v, kk, bb, s0_bd, tb):
    b, _, t, _ = r.shape
    nc = tb // RWKV_CHUNK
    cur = pl.BlockSpec((1, 4, tb, LANES), lambda bi, i: (bi, 0, i, 0))
    st_spec = pl.BlockSpec((1, 4, LANES, LANES), lambda bi, i: (bi, 0, 0, 0))
    return pl.pallas_call(
        functools.partial(_rwkv_scan_kernel, tb=tb),
        grid=(b, t // tb),
        in_specs=[cur] * 6 + [st_spec],
        out_specs=[cur, st_spec],
        out_shape=[jax.ShapeDtypeStruct((b, 4, t, LANES), F32),
                   jax.ShapeDtypeStruct((b, 4, LANES, LANES), F32)],
        scratch_shapes=_rwkv_scan_scratch(tb),
        compiler_params=_cparams(("parallel", "arbitrary")),
        name="rwkv_scan",
    )(r, lw, k, v, kk, bb, s0_bd)


def _state_to_blockdiag(s):
    b = s.shape[0]
    st = jnp.swapaxes(s, -1, -2).reshape(b, 4, 2, HEAD_DIM, HEAD_DIM)
    z = jnp.zeros_like(st[:, :, 0])
    top = jnp.concatenate([st[:, :, 0], z], axis=-1)
    bot = jnp.concatenate([z, st[:, :, 1]], axis=-1)
    return jnp.concatenate([top, bot], axis=-2)


def _state_from_blockdiag(s_bd):
    b = s_bd.shape[0]
    a = s_bd[:, :, :HEAD_DIM, :HEAD_DIM]
    d = s_bd[:, :, HEAD_DIM:, HEAD_DIM:]
    st = jnp.stack([a, d], axis=2).reshape(b, 8, HEAD_DIM, HEAD_DIM)
    return jnp.swapaxes(st, -1, -2)


def _rwkv_post_kernel(y_ref, bonus_ref, g_ref, lg_ref, lb_ref, o_ref):
    for p in range(4):
        sl = slice(p * LANES, (p + 1) * LANES)
        y = y_ref[0, p]
        d = y - _segsum64(y) * (1.0 / HEAD_DIM)
        var = _segsum64(d * d) * (1.0 / HEAD_DIM)
        yn = d * lax.rsqrt(var + RWKV_GN_EPS) * lg_ref[:, sl] + lb_ref[:, sl]
        o_ref[0, :, sl] = ((yn + bonus_ref[0, :, sl]) * g_ref[0, :, sl]).astype(o_ref.dtype)


def _rwkv_kernel(pb_ref, pbprev_ref, shift_ref, mu_ref, w0_ref, wup_ref, a0_ref, aup_ref, gup_ref,
                 kk_ref, ka_ref, rk_ref, lg_ref, lb_ref, s0_ref, ob_ref, sT_ref,
                 r_s, lw_s, k_s, v_s, kkn_s, b_s, g_s, bonus_s, y_s, *scan_scr, tb):
    _rwkv_prep_kernel(pb_ref, pbprev_ref, shift_ref, mu_ref, w0_ref, wup_ref, a0_ref, aup_ref, gup_ref,
                      kk_ref, ka_ref, rk_ref, r_s, lw_s, k_s, v_s, kkn_s, b_s, g_s, bonus_s)
    _rwkv_scan_kernel(r_s, lw_s, k_s, v_s, kkn_s, b_s, s0_ref, y_s, sT_ref, *scan_scr, tb=tb)
    _rwkv_post_kernel(y_s, bonus_s, g_s, lg_ref, lb_ref, ob_ref)


def _rwkv(pb, shift, mu, w0, wup, a0, aup, gup, k_k, k_a, r_k, lnx_g, lnx_b, s0_bd, tb):
    b, t, c = pb.shape
    nc = tb // RWKV_CHUNK
    cur = lambda bi, i: (bi, i, 0)
    vec = lambda n: pl.BlockSpec((1, n), lambda bi, i: (0, 0))
    lora = pl.BlockSpec((LANES, 512), lambda bi, i: (0, 0))
    st_spec = pl.BlockSpec((1, 4, LANES, LANES), lambda bi, i: (bi, 0, 0, 0))
    pair_scr = pltpu.VMEM((1, 4, tb, LANES), F32)
    return pl.pallas_call(
        functools.partial(_rwkv_kernel, tb=tb),
        grid=(b, t // tb),
        in_specs=[pl.BlockSpec((1, tb, c), cur),
                  pl.BlockSpec((1, 8, c), lambda bi, i: (bi, jnp.maximum(i * (tb // 8) - 1, 0), 0)),
                  pl.BlockSpec((1, 1, c), lambda bi, i: (bi, 0, 0)),
                  vec(c), vec(512), lora, vec(512), lora, lora, vec(512), vec(512), vec(512),
                  vec(512), vec(512), st_spec],
        out_specs=[pl.BlockSpec((1, tb, 512), cur), st_spec],
        out_shape=[jax.ShapeDtypeStruct((b, t, 512), BF16),
                   jax.ShapeDtypeStruct((b, 4, LANES, LANES), F32)],
        scratch_shapes=([pair_scr] * 6 + [pltpu.VMEM((1, tb, 512), F32)] * 2 + [pair_scr]
                        + _rwkv_scan_scratch(tb)),
        compiler_params=_cparams(("parallel", "arbitrary")),
        name="rwkv",
    )(pb, pb, shift, mu, w0, wup, a0, aup, gup, k_k, k_a, r_k, lnx_g, lnx_b, s0_bd)


def _layer_norm(x, g, b):
    mu = jnp.mean(x, -1, keepdims=True)
    d = x - mu
    var = jnp.mean(d * d, -1, keepdims=True)
    return d * lax.rsqrt(var + LN_EPS) * g + b


def _ffn_kernel(x_ref, m1_ref, m2_ref, wo_ref, g1_ref, b1_ref, wu_ref, wd_ref, g2_ref, b2_ref, o_ref,
                *, alpha, ff_chunk):
    half = m1_ref.shape[1]
    mix = _dot(m1_ref[...], wo_ref[0:half, :]) + _dot(m2_ref[...], wo_ref[half:, :])
    x1 = _layer_norm(alpha * x_ref[...] + mix, g1_ref[...], b1_ref[...])
    x1b = x1.astype(BF16)
    d_ff = wu_ref.shape[1]
    h = jnp.zeros_like(x1)
    for c in range(d_ff // ff_chunk):
        u = jnp.maximum(_dot(x1b, wu_ref[:, c * ff_chunk:(c + 1) * ff_chunk]), 0.0)
        h = h + _dot((u * u).astype(BF16), wd_ref[c * ff_chunk:(c + 1) * ff_chunk, :])
    o_ref[...] = _layer_norm(alpha * x1 + h, g2_ref[...], b2_ref[...])


def _const_spec(shape):
    return pl.BlockSpec(shape, lambda i: (0,) * len(shape), pipeline_mode=pl.Buffered(1))


def _out_ffn(x, m1, m2, wo, g1, b1, wu, wd, g2, b2, alpha, tm):
    n, d = x.shape
    dm = m1.shape[1]
    d_ff = wu.shape[1]
    tile = lambda w: pl.BlockSpec((tm, w), lambda i: (i, 0))
    return pl.pallas_call(
        functools.partial(_ffn_kernel, alpha=alpha, ff_chunk=1024),
        grid=(n // tm,),
        in_specs=[tile(d), tile(dm), tile(dm),
                  _const_spec((2 * dm, d)), _const_spec((1, d)), _const_spec((1, d)),
                  _const_spec((d, d_ff)), _const_spec((d_ff, d)), _const_spec((1, d)), _const_spec((1, d))],
        out_specs=tile(d),
        out_shape=jax.ShapeDtypeStruct((n, d), F32),
        compiler_params=_cparams(("parallel",)),
        name="out_ffn",
    )(x, m1, m2, wo, g1, b1, wu, wd, g2, b2)


def _diff_lambda(lam_ref, lam_init):
    lv = lam_ref[...]
    s1 = jnp.sum(lv[0:1] * lv[1:2], axis=-1, keepdims=True)
    s2 = jnp.sum(lv[2:3] * lv[3:4], axis=-1, keepdims=True)
    return jnp.exp(s1) - jnp.exp(s2) + lam_init


def _split_halves(qh):
    left = _lane_left(qh.shape)
    zero = jnp.zeros_like(qh)
    return jnp.concatenate([jnp.where(left, qh, zero), jnp.where(left, zero, qh)], axis=0)


def _flash_update(carry, s3, vblk):
    m, l, acc = carry
    g, rws, kb = s3.shape
    m_new = jnp.maximum(m, jnp.max(s3, -1, keepdims=True))
    m_use = jnp.where(m_new == NEG_INF, 0.0, m_new)
    corr = jnp.exp2(m - m_use)
    p = jnp.exp2(s3 - m_use)
    l = l * corr + jnp.sum(p, -1, keepdims=True)
    pv = _dot(p.reshape(g * rws, kb).astype(BF16), vblk).reshape(g, rws, vblk.shape[1])
    return m_new, l, acc * corr + pv


def _flash_init(g, rws, e):
    return (jnp.full((g, rws, 1), NEG_INF, F32), jnp.zeros((g, rws, 1), F32), jnp.zeros((g, rws, e), F32))


def _diff_finish(carry, lam, gain, lam_init):
    m, l, acc = carry
    o = acc[0] / l[0] - lam * (acc[1] / l[1])
    o = o * lax.rsqrt(jnp.mean(o * o, -1, keepdims=True) + LN_EPS) * gain * (1.0 - lam_init)
    return o


def _diff_prompt_kernel(q_ref, k_ref, vt_ref, lam_ref, gain_ref, o_ref, q2_scr, *, qb, slopes, lam_init):
    i = pl.program_id(1)
    lam = _diff_lambda(lam_ref, lam_init)
    qpos = i * qb + lax.broadcasted_iota(I32, (1, qb), 1)
    for h in range(4):
        q2_scr[2 * h * qb:(2 * h + 2) * qb, :LANES] = _split_halves(q_ref[0, :, h * LANES:(h + 1) * LANES])
        q2_scr[2 * h * qb:(2 * h + 2) * qb, LANES:] = _alibi_q_features(2 * qb, slopes[h])

    def body(kb, carry, diagonal):
        m, l, acc = carry
        rows = pl.ds(pl.multiple_of(kb * qb, qb), qb)
        kpos = kb * qb + lax.broadcasted_iota(I32, (qb, 1), 0)
        kfeat = _alibi_k_features(kpos)
        st = [_dot_nt(jnp.concatenate([k_ref[0, rows, h * LANES:(h + 1) * LANES], kfeat], axis=1),
                      q2_scr[2 * h * qb:(2 * h + 2) * qb, :])
              for h in range(4)]
        part = lambda g: st[g // 2][:, (g % 2) * qb:(g % 2 + 1) * qb]
        if diagonal:
            ahead = jnp.maximum(kpos - qpos, 0).astype(F32)
            neg = jnp.where(kpos // CHUNK <= qpos // CHUNK, 0.0, NEG_INF)
            s3 = jnp.stack([part(g) - (2.0 * slopes[g // 2]) * ahead + neg for g in range(8)], axis=0)
        else:
            s3 = jnp.stack([part(g) for g in range(8)], axis=0)
        m_new = jnp.maximum(m, jnp.max(s3, axis=1, keepdims=True))
        corr = jnp.exp2(m - m_new)
        p = jnp.exp2(s3 - m_new)
        l = l * corr + jnp.sum(p, axis=1, keepdims=True)
        pv = [_dot(vt_ref[0, kb, h * LANES:(h + 1) * LANES, :],
                   jnp.concatenate([p[2 * h].astype(BF16), p[2 * h + 1].astype(BF16)], axis=1))
              for h in range(4)]
        acc = acc * corr + jnp.stack([pv[g // 2][:, (g % 2) * qb:(g % 2 + 1) * qb] for g in range(8)], axis=0)
        return m_new, l, acc

    init = (jnp.full((8, 1, qb), NEG_INF, F32), jnp.zeros((8, 1, qb), F32), jnp.zeros((8, LANES, qb), F32))
    carry = lax.fori_loop(0, i, functools.partial(body, diagonal=False), init)
    m, l, acc = body(i, carry, diagonal=True)
    for h in range(4):
        o = acc[2 * h] / l[2 * h] - lam * (acc[2 * h + 1] / l[2 * h + 1])
        o = o * lax.rsqrt(jnp.mean(o * o, axis=0, keepdims=True) + LN_EPS) * gain_ref[...] * (1.0 - lam_init)
        o_ref[0, :, h * LANES:(h + 1) * LANES] = o.T.astype(o_ref.dtype)


def _diff_prompt(q, k, v, lamv, gain, lam_init, qb):
    b, t, _ = q.shape
    nkb = t // qb
    vt = jnp.swapaxes(v.reshape(b, nkb, qb, 512), -1, -2)
    full = pl.BlockSpec((1, t, 512), lambda bi, i: (bi, 0, 0))
    tile = pl.BlockSpec((1, qb, 512), lambda bi, i: (bi, i, 0))
    return pl.pallas_call(
        functools.partial(_diff_prompt_kernel, qb=qb, slopes=_alibi(4), lam_init=lam_init),
        grid=(b, nkb),
        in_specs=[tile, full,
                  pl.BlockSpec((1, nkb, 512, qb), lambda bi, i: (bi, 0, 0, 0)),
                  pl.BlockSpec((4, HEAD_DIM), lambda bi, i: (0, 0)),
                  pl.BlockSpec((LANES, 1), lambda bi, i: (0, 0))],
        out_specs=tile,
        out_shape=jax.ShapeDtypeStruct((b, t, 512), BF16),
        scratch_shapes=[pltpu.VMEM((8 * qb, 2 * LANES), BF16)],
        compiler_params=_cparams(("parallel", "arbitrary")),
        name="diff_prompt",
    )(q, k, vt, lamv, gain.reshape(LANES, 1))


def _diff_sample_kernel(q_ref, kc_ref, vc_ref, kn_ref, vn_ref, lam_ref, gain_ref, o_ref,
                        m_scr, l_scr, acc_scr, *, t, kb_size, n_past, slopes, lam_init):
    j = pl.program_id(1)
    past_len = n_past * kb_size
    qpos = past_len + lax.broadcasted_iota(I32, (t, 1), 0)

    @pl.when(j == 0)
    def _():
        m_scr[...] = jnp.full(m_scr.shape, NEG_INF, F32)
        l_scr[...] = jnp.zeros(l_scr.shape, F32)
        acc_scr[...] = jnp.zeros(acc_scr.shape, F32)

    def run(kfull, v_of, kpos):
        dist = jnp.abs(qpos - kpos).astype(F32)
        nk = kfull.shape[0]
        s3 = jnp.stack(
            [_dot_nt(_split_halves(q_ref[0, :, h * LANES:(h + 1) * LANES]),
                     kfull[:, h * LANES:(h + 1) * LANES]).reshape(2, t, nk) - slopes[h] * dist[None]
             for h in range(4)], axis=0)
        m = m_scr[...]
        m_new = jnp.maximum(m, jnp.max(s3, -1, keepdims=True))
        corr = jnp.exp2(m - m_new)
        p = jnp.exp2(s3 - m_new)
        l_scr[...] = l_scr[...] * corr + jnp.sum(p, -1, keepdims=True)
        pv = jnp.stack([_dot(p[h].reshape(2 * t, nk).astype(BF16), v_of(h)).reshape(2, t, LANES)
                        for h in range(4)], axis=0)
        acc_scr[...] = acc_scr[...] * corr + pv
        m_scr[...] = m_new

    @pl.when(j < n_past)
    def _():
        kpos = j * kb_size + lax.broadcasted_iota(I32, (1, kb_size), 1)
        run(kc_ref[0].astype(BF16), lambda h: vc_ref[0, :, h * LANES:(h + 1) * LANES].astype(BF16), kpos)

    @pl.when(j == n_past)
    def _():
        kpos = past_len + lax.broadcasted_iota(I32, (1, t), 1)
        run(kn_ref[0].astype(BF16), lambda h: vn_ref[0, :, h * LANES:(h + 1) * LANES].astype(BF16), kpos)
        lam = _diff_lambda(lam_ref, lam_init)
        for h in range(4):
            sl = slice(h * LANES, (h + 1) * LANES)
            carry = (m_scr[h], l_scr[h], acc_scr[h])
            o_ref[0, :, sl] = _diff_finish(carry, lam, gain_ref[...], lam_init).astype(o_ref.dtype)


def _diff_sample(q, k_cache, v_cache, k_new, v_new, lamv, gain, lam_init, kb_size):
    b, t, _ = q.shape
    n_past = k_cache.shape[1] // kb_size
    cache = pl.BlockSpec((1, kb_size, 512), lambda bi, j: (bi, jnp.minimum(j, n_past - 1), 0))
    new = pl.BlockSpec((1, t, 512), lambda bi, j: (bi, 0, 0))
    return pl.pallas_call(
        functools.partial(_diff_sample_kernel, t=t, kb_size=kb_size, n_past=n_past,
                          slopes=_alibi(4), lam_init=lam_init),
        grid=(b, n_past + 1),
        in_specs=[new, cache, cache, new, new,
                  pl.BlockSpec((4, HEAD_DIM), lambda bi, j: (0, 0)),
                  pl.BlockSpec((1, LANES), lambda bi, j: (0, 0))],
        out_specs=new,
        out_shape=jax.ShapeDtypeStruct((b, t, 512), BF16),
        scratch_shapes=[pltpu.VMEM((4, 2, t, 1), F32), pltpu.VMEM((4, 2, t, 1), F32),
                        pltpu.VMEM((4, 2, t, LANES), F32)],
        compiler_params=_cparams(("parallel", "arbitrary")),
        name="diff_sample",
    )(q, k_cache, v_cache, k_new, v_new, lamv, gain)


def _index_keys(qi, wi, ki_blk, allowed):
    acc = None
    for h in range(8):
        sc = _dot_nt(qi[:, h * HEAD_DIM:(h + 1) * HEAD_DIM], ki_blk)
        term = jnp.maximum(sc, 0.0) * wi[:, h:h + 1]
        acc = term if acc is None else acc + term
    idx = acc * (8.0 ** -0.5) + 0.0
    bits = pltpu.bitcast(idx, I32)
    key = jnp.where(bits < 0, bits ^ 0x7FFFFFFF, bits)
    return jnp.where(allowed, key, INT_MIN)


def _lane_fold(x):
    acc = x[:, :LANES]
    for j in range(1, x.shape[1] // LANES):
        acc = acc + x[:, j * LANES:(j + 1) * LANES]
    return acc


def _ones_where(pred):
    return jnp.where(pred, 1, 0).astype(I32)


def _kth_largest(count_ge, shape, n_sel):
    def body(it, lo):
        bit = 31 - it
        cand = lo + jnp.left_shift(jnp.int32(1), bit)
        return jnp.where(count_ge(cand) >= n_sel, cand, lo)
    return lax.fori_loop(0, 32, body, jnp.full(shape, INT_MIN, I32))


def _tie_position(count_eq_le, need, shape, nbits):
    def body(it, lo):
        bit = nbits - 1 - it
        cand = lo + jnp.left_shift(jnp.int32(1), bit)
        return jnp.where(count_eq_le(cand - 1) < need, cand, lo)
    return lax.fori_loop(0, nbits, body, jnp.zeros(shape, I32))


def _select_threshold(count_over, shape, n_sel, pos_bits):
    thr = _kth_largest(lambda cand: count_over(lambda key, kpos: key >= cand), shape, n_sel)
    need = n_sel - count_over(lambda key, kpos: key > thr)
    n_eq = count_over(lambda key, kpos: key == thr)
    has_tie = jnp.max(jnp.where((n_eq > need) & (thr > INT_MIN), 1, 0)) > 0
    pos_thr = lax.cond(
        has_tie,
        lambda: _tie_position(
            lambda p: count_over(lambda key, kpos: (key == thr) & (kpos <= p)), need, shape, pos_bits),
        lambda: jnp.full(shape, 2 ** 30, I32))
    return thr, pos_thr


def _dsa_attend(carry, q_ref, kblk, vblk, dist, sel, slopes):
    neg = jnp.where(sel, 0.0, NEG_INF)
    qs = []
    for s in range(4):
        slab = q_ref[0, :, s * LANES:(s + 1) * LANES]
        left = _lane_left(slab.shape)
        zero = jnp.zeros_like(slab)
        qs.append((jnp.where(left, slab, zero), jnp.where(left, zero, slab)))
    q_all = jnp.concatenate([qs[h % 4][h // 4] for h in range(8)], axis=0)
    r, kb = dist.shape
    s3 = _dot_nt(q_all, kblk).reshape(8, r, kb)
    s3 = jnp.stack([s3[h] - slopes[h] * dist + neg for h in range(8)], axis=0)
    return _flash_update(carry, s3, vblk)


def _dsa_finish(carry, o_ref):
    m, l, acc = carry
    o = acc / l
    for s in range(4):
        left = _lane_left(o[s].shape)
        o_ref[0, :, s * LANES:(s + 1) * LANES] = jnp.where(left, o[s], o[4 + s]).astype(o_ref.dtype)


def _dsa_prompt_kernel(q_ref, qi_ref, wi_ref, k_ref, vt_ref, ki_ref, o_ref, key_scr, qall_scr, qiall_scr,
                       *, qb, n_sel, slopes):
    i = pl.program_id(1)
    nkb = i + 1
    qpos = i * qb + lax.broadcasted_iota(I32, (1, qb), 1)

    for s in range(4):
        slab = q_ref[0, :, s * LANES:(s + 1) * LANES]
        left = _lane_left(slab.shape)
        zero = jnp.zeros_like(slab)
        qall_scr[s * qb:(s + 1) * qb, :LANES] = jnp.where(left, slab, zero)
        qall_scr[(4 + s) * qb:(5 + s) * qb, :LANES] = jnp.where(left, zero, slab)
    for h in range(8):
        qall_scr[h * qb:(h + 1) * qb, LANES:] = _alibi_q_features(qb, slopes[h])
        qiall_scr[h * qb:(h + 1) * qb, :] = qi_ref[0, :, h * HEAD_DIM:(h + 1) * HEAD_DIM]
    wit = wi_ref[0].T

    def kpos_of(kb):
        return kb * qb + lax.broadcasted_iota(I32, (qb, 1), 0)

    def rows_of(kb):
        return pl.ds(pl.multiple_of(kb * qb, qb), qb)

    def fill(kb, c):
        allowed = kpos_of(kb) // CHUNK <= qpos // CHUNK
        sc = _dot_nt(ki_ref[0, rows_of(kb), :], qiall_scr[...])
        acc = None
        for h in range(8):
            term = jnp.maximum(sc[:, h * qb:(h + 1) * qb], 0.0) * wit[h:h + 1, :]
            acc = term if acc is None else acc + term
        idx = acc * (8.0 ** -0.5) + 0.0
        bits = pltpu.bitcast(idx, I32)
        key = jnp.where(bits < 0, bits ^ 0x7FFFFFFF, bits)
        key_scr[kb] = jnp.where(allowed, key, INT_MIN)
        return c
    lax.fori_loop(0, nkb, fill, 0)

    def count_over(pred_fn):
        def body(kb, c):
            ones = _ones_where(pred_fn(key_scr[kb], kpos_of(kb)))
            return c + jnp.sum(ones.reshape(qb // 8, 8, qb), axis=0)
        part = lax.fori_loop(0, nkb, body, jnp.zeros((8, qb), I32))
        return jnp.sum(part, axis=0, keepdims=True)

    thr, pos_thr = _select_threshold(count_over, (1, qb), n_sel, 12)

    def attend(kb, carry, diagonal):
        m, l, acc = carry
        kpos = kpos_of(kb)
        key = key_scr[kb]
        sel = (key > thr) | ((key == thr) & (kpos <= pos_thr))
        if diagonal:
            sel = sel & (kpos // CHUNK <= qpos // CHUNK)
        neg = jnp.where(sel, 0.0, NEG_INF)
        k_aug = jnp.concatenate([k_ref[0, rows_of(kb), :], _alibi_k_features(kpos)], axis=1)
        st = _dot_nt(k_aug, qall_scr[...])
        if diagonal:
            ahead = jnp.maximum(kpos - qpos, 0).astype(F32)
            s3 = jnp.stack([st[:, h * qb:(h + 1) * qb] - (2.0 * slopes[h]) * ahead + neg for h in range(8)],
                           axis=0)
        else:
            s3 = jnp.stack([st[:, h * qb:(h + 1) * qb] + neg for h in range(8)], axis=0)
        m_new = jnp.maximum(m, jnp.max(s3, axis=1, keepdims=True))
        m_use = jnp.where(m_new == NEG_INF, 0.0, m_new)
        corr = jnp.exp2(m - m_use)
        p = jnp.exp2(s3 - m_use)
        l = l * corr + jnp.sum(p, axis=1, keepdims=True)
        p_all = jnp.concatenate([p[h].astype(BF16) for h in range(8)], axis=1)
        pv = _dot(vt_ref[0, kb], p_all)
        acc = acc * corr + jnp.stack([pv[:, h * qb:(h + 1) * qb] for h in range(8)], axis=0)
        return m_new, l, acc

    init = (jnp.full((8, 1, qb), NEG_INF, F32), jnp.zeros((8, 1, qb), F32), jnp.zeros((8, LANES, qb), F32))
    carry = lax.fori_loop(0, i, functools.partial(attend, diagonal=False), init)
    m, l, acc = attend(i, carry, diagonal=True)
    ot = acc / l
    top = lax.broadcasted_iota(I32, (LANES, qb), 0) < HEAD_DIM
    for s in range(4):
        o_ref[0, :, s * LANES:(s + 1) * LANES] = jnp.where(top, ot[s], ot[4 + s]).T.astype(o_ref.dtype)


def _dsa_prompt(q, qi, wi, k, v, ki, qb):
    b, t, _ = q.shape
    n_sel = min(TOPK_MAX, t // 4)
    nkb = t // qb
    vt = jnp.swapaxes(v.reshape(b, nkb, qb, LANES), -1, -2)
    tile = lambda w: pl.BlockSpec((1, qb, w), lambda bi, i: (bi, i, 0))
    full = lambda w: pl.BlockSpec((1, t, w), lambda bi, i: (bi, 0, 0))
    return pl.pallas_call(
        functools.partial(_dsa_prompt_kernel, qb=qb, n_sel=n_sel, slopes=_alibi(8)),
        grid=(b, nkb),
        in_specs=[tile(512), tile(512), tile(LANES), full(LANES),
                  pl.BlockSpec((1, nkb, LANES, qb), lambda bi, i: (bi, 0, 0, 0)), full(HEAD_DIM)],
        out_specs=tile(512),
        out_shape=jax.ShapeDtypeStruct((b, t, 512), BF16),
        scratch_shapes=[pltpu.VMEM((nkb, qb, qb), I32), pltpu.VMEM((8 * qb, 2 * LANES), BF16),
                        pltpu.VMEM((8 * qb, HEAD_DIM), BF16)],
        compiler_params=_cparams(("parallel", "arbitrary")),
        name="dsa_prompt",
    )(q, qi, wi, k, vt, ki)


def _dsa_sample_kernel(q_ref, qi_ref, wi_ref, kc_ref, vc_ref, kic_ref, kn_ref, vn_ref, kin_ref, o_ref,
                       key_scr, keyn_scr, *, t, kb_size, n_past, n_sel, slopes):
    past_len = n_past * kb_size
    qpos = past_len + lax.broadcasted_iota(I32, (t, 1), 0)
    qi = qi_ref[0]
    wi = wi_ref[0]
    kpos_n = past_len + lax.broadcasted_iota(I32, (1, t), 1)

    def kpos_of(kb):
        return kb * kb_size + lax.broadcasted_iota(I32, (1, kb_size), 1)

    for kb in range(n_past):
        ki_blk = kic_ref[0, kb * kb_size:(kb + 1) * kb_size, :].astype(BF16)
        key_scr[kb] = _index_keys(qi, wi, ki_blk, True)
    keyn_scr[...] = _index_keys(qi, wi, kin_ref[0].astype(BF16), True)

    def count_over(pred_fn):
        part = _lane_fold(_ones_where(pred_fn(key_scr[0], kpos_of(0))))
        for kb in range(1, n_past):
            part = part + _lane_fold(_ones_where(pred_fn(key_scr[kb], kpos_of(kb))))
        new = jnp.sum(_ones_where(pred_fn(keyn_scr[...], kpos_n)), axis=-1, keepdims=True)
        return jnp.sum(part, axis=-1, keepdims=True) + new

    thr, pos_thr = _select_threshold(count_over, (t, 1), n_sel, 13)
    carry = _flash_init(8, t, LANES)

    def sel_of(key, kpos):
        return (key > thr) | ((key == thr) & (kpos <= pos_thr))

    for kb in range(n_past):
        kpos = kpos_of(kb)
        dist = jnp.abs(qpos - kpos).astype(F32)
        kblk = kc_ref[0, kb * kb_size:(kb + 1) * kb_size, :].astype(BF16)
        vblk = vc_ref[0, kb * kb_size:(kb + 1) * kb_size, :].astype(BF16)
        carry = _dsa_attend(carry, q_ref, kblk, vblk, dist, sel_of(key_scr[kb], kpos), slopes)
    dist = jnp.abs(qpos - kpos_n).astype(F32)
    carry = _dsa_attend(carry, q_ref, kn_ref[0].astype(BF16), vn_ref[0].astype(BF16), dist,
                        sel_of(keyn_scr[...], kpos_n), slopes)
    _dsa_finish(carry, o_ref)


def _dsa_sample(q, qi, wi, k_cache, v_cache, ki_cache, k_new, v_new, ki_new, kb_size):
    b, t, _ = q.shape
    past = k_cache.shape[1]
    n_past = past // kb_size
    n_sel = min(TOPK_MAX, (past + t) // 4)
    per_b = lambda rows, w: pl.BlockSpec((1, rows, w), lambda bi: (bi, 0, 0))
    return pl.pallas_call(
        functools.partial(_dsa_sample_kernel, t=t, kb_size=kb_size, n_past=n_past, n_sel=n_sel,
                          slopes=_alibi(8)),
        grid=(b,),
        in_specs=[per_b(t, 512), per_b(t, 512), per_b(t, LANES),
                  per_b(past, LANES), per_b(past, LANES), per_b(past, HEAD_DIM),
                  per_b(t, LANES), per_b(t, LANES), per_b(t, HEAD_DIM)],
        out_specs=per_b(t, 512),
        out_shape=jax.ShapeDtypeStruct((b, t, 512), BF16),
        scratch_shapes=[pltpu.VMEM((n_past, t, kb_size), I32), pltpu.VMEM((t, t), I32)],
        compiler_params=_cparams(("parallel",)),
        name="dsa_sample",
    )(q, qi, wi, k_cache, v_cache, ki_cache, k_new, v_new, ki_new)


def _pad_cols(w, total):
    return jnp.pad(w, ((0, 0), (0, total - w.shape[1])))


def _q_perm():
    idx = []
    for s in range(4):
        idx.extend(range(s * HEAD_DIM, (s + 1) * HEAD_DIM))
        idx.extend(range((4 + s) * HEAD_DIM, (5 + s) * HEAD_DIM))
    return np.asarray(idx, np.int32)


def _even_layer(xp, xs, cache_k, cache_v, shift_s, wkv_s, w_in, sink, rw, w_out, ffn, alpha, sizes):
    (mu, w0, w_up, a0, a_up, g_up, k_k, k_a, r_k, lnx_g, lnx_b) = rw
    perm = _q_perm()
    w_in_p = jnp.concatenate([w_in[:, :512][:, perm], w_in[:, 512:]], axis=1).astype(BF16)
    w_out_p = jnp.concatenate([w_out[:512][perm], w_out[512:]], axis=0).astype(BF16)
    groups = ((0, 512, ((0, 512, HEAD_DIM ** -0.5 * LOG2E),)),
              (512, 128, ((1, 128, 1.0), (2, 128, 1.0))),
              (640, 128, ((3, 128, 1.0), (4, 128, 1.0))),
              (768, 1792, ((5, 1792, 1.0),)))
    out_defs = ((512, BF16), (128, F32), (128, BF16), (128, F32), (128, BF16), (1792, F32))
    wup_p = jnp.concatenate([w_up, jnp.zeros_like(w_up)], axis=0).astype(BF16)
    aup_p = jnp.concatenate([jnp.zeros_like(a_up), a_up], axis=0).astype(BF16)
    gup = g_up.astype(BF16)
    row = lambda v: v.reshape(1, -1)
    g1, b1, g2, b2, wu, wd = ffn

    def run(x, state, tm, qb, scan_tb):
        b, t, d = x.shape
        n = b * t
        qa, ka, ka_bf, va, va_bf, pb = _project(x.reshape(n, d), w_in_p, groups, out_defs, tm)
        r3 = lambda a: a.reshape(b, t, a.shape[-1])
        if state is None:
            oa = _swa_prompt(r3(qa), r3(ka_bf), r3(va_bf), sink, qb)
            k_buf = r3(ka)[:, -WINDOW:]
            v_buf = r3(va)[:, -WINDOW:]
            shift = jnp.zeros((b, 1, pb.shape[-1]), F32)
            s0 = jnp.zeros((b, 8, HEAD_DIM, HEAD_DIM), F32)
        else:
            ck, cv, shift, s0 = state
            ck2 = ck.reshape(b, WINDOW, LANES)
            cv2 = cv.reshape(b, WINDOW, LANES)
            oa = _swa(r3(qa), ck2, r3(ka_bf), cv2, r3(va_bf), sink,
                      qb=qb, prev_is_cache=True, pos_base=sizes["past"])
            k_buf = jnp.concatenate([ck2, r3(ka)], axis=1)[:, -WINDOW:]
            v_buf = jnp.concatenate([cv2, r3(va)], axis=1)[:, -WINDOW:]
        pb3 = r3(pb)
        ob, s_bd = _rwkv(pb3, shift, row(mu), row(w0), wup_p, row(a0), aup_p, gup, row(k_k), row(k_a),
                         row(r_k), row(lnx_g), row(lnx_b), _state_to_blockdiag(s0), scan_tb)
        s_t = _state_from_blockdiag(s_bd)
        xo = _out_ffn(x.reshape(n, d), oa.reshape(n, 512), ob.reshape(n, 512), w_out_p, row(g1), row(b1),
                      wu.astype(BF16), wd.astype(BF16), row(g2), row(b2), alpha, tm)
        st = (k_buf.reshape(b, WINDOW, 2, HEAD_DIM), v_buf.reshape(b, WINDOW, 2, HEAD_DIM),
              pb3[:, -1:], s_t)
        return xo.reshape(b, t, d), st

    xp2, st_p = run(xp, None, sizes["tm_p"], sizes["swa_qb"], sizes["scan_tb_p"])
    xs2, st_s = run(xs, (cache_k, cache_v, shift_s, wkv_s), sizes["tm_s"], xs.shape[1], xs.shape[1])
    return xp2, xs2, st_p, st_s


def _odd_layer(xp, xs, caches, w_in, lam_vecs, subln_g, w_out, ffn, alpha, lam_init, sizes):
    perm = _q_perm()
    scale = HEAD_DIM ** -0.5 * LOG2E
    w_p = jnp.concatenate([
        w_in[:, :1536], w_in[:, 1536:2048][:, perm], w_in[:, 2048:2816],
        _pad_cols(w_in[:, 2816:2880], LANES), _pad_cols(w_in[:, 2880:2888], LANES)], axis=1).astype(BF16)
    w_out_p = jnp.concatenate([w_out[:512], w_out[512:][perm]], axis=0).astype(BF16)
    groups = ((0, 512, ((0, 512, scale),)),
              (512, 512, ((1, 512, 1.0), (2, 512, 1.0))),
              (1024, 512, ((3, 512, 1.0), (4, 512, 1.0))),
              (1536, 512, ((5, 512, scale),)),
              (2048, 128, ((6, 128, 1.0), (7, 128, 1.0))),
              (2176, 128, ((8, 128, 1.0), (9, 128, 1.0))),
              (2304, 512, ((10, 512, HEAD_DIM ** -0.5),)),
              (2816, 128, ((11, 64, 1.0), (12, 64, 1.0))),
              (2944, 128, ((13, 128, 1.0),)))
    out_defs = ((512, BF16), (512, F32), (512, BF16), (512, F32), (512, BF16), (512, BF16),
                (128, F32), (128, BF16), (128, F32), (128, BF16), (512, BF16),
                (64, F32), (64, BF16), (128, F32))
    row = lambda v: v.reshape(1, -1)
    g1, b1, g2, b2, wu, wd = ffn
    lamv = jnp.stack(lam_vecs, axis=0)
    gain = row(subln_g)

    def run(x, state, tm):
        b, t, d = x.shape
        n = b * t
        (qc, kc, kc_bf, vc, vc_bf, qd, kd, kd_bf, vd, vd_bf, qi, ki, ki_bf, wi) = _project(
            x.reshape(n, d), w_p, groups, out_defs, tm)
        r3 = lambda a: a.reshape(b, t, a.shape[-1])
        if state is None:
            oc = _diff_prompt(r3(qc), r3(kc_bf), r3(vc_bf), lamv, gain, lam_init, sizes["diff_qb"])
            od = _dsa_prompt(r3(qd), r3(qi), r3(wi), r3(kd_bf), r3(vd_bf), r3(ki_bf), sizes["dsa_qb"])
        else:
            c_k, c_v, d_k, d_v, d_ki = state
            past = c_k.shape[1]
            oc = _diff_sample(r3(qc), c_k.reshape(b, past, 512), c_v.reshape(b, past, 512),
                              r3(kc_bf), r3(vc_bf), lamv, gain, lam_init, sizes["diff_kb_s"])
            od = _dsa_sample(r3(qd), r3(qi), r3(wi), d_k.reshape(b, past, LANES), d_v.reshape(b, past, LANES),
                             d_ki, r3(kd_bf), r3(vd_bf), r3(ki_bf), sizes["dsa_kb_s"])
        xo = _out_ffn(x.reshape(n, d), oc.reshape(n, 512), od.reshape(n, 512), w_out_p, row(g1), row(b1),
                      wu.astype(BF16), wd.astype(BF16), row(g2), row(b2), alpha, tm)
        rows = (kc.reshape(b, t, 4, 2, HEAD_DIM), vc.reshape(b, t, 4, 2 * HEAD_DIM),
                kd.reshape(b, t, 2, HEAD_DIM), vd.reshape(b, t, 2, HEAD_DIM), ki.reshape(b, t, HEAD_DIM))
        return xo.reshape(b, t, d), rows

    xp2, st_p = run(xp, None, sizes["tm_p"])
    xs2, st_s = run(xs, caches, sizes["tm_s"])
    return xp2, xs2, st_p, st_s


def _sizes(xp, xs, past):
    return dict(tm_p=min(512, xp.shape[0] * xp.shape[1]), tm_s=min(512, xs.shape[0] * xs.shape[1]),
                swa_qb=min(256, xp.shape[1]), scan_tb_p=min(128, xp.shape[1]),
                diff_qb=min(256, xp.shape[1]),
                dsa_qb=min(256, xp.shape[1]), diff_kb_s=min(1024, past), dsa_kb_s=min(512, past),
                past=past)


def kernel(x_prompt, x_sample, cache_a_k, cache_a_v, state_b_shift, state_b_wkv, cache_c_k, cache_c_v, cache_d_k, cache_d_v, cache_d_kidx, w_in_even, sink_a, mu_b, w0_b, w_up_b, a0_b, a_up_b, g_up_b, k_k_b, k_a_b, r_k_b, lnx_g_b, lnx_b_b, w_out_even, w_in_odd, lam_q1_c, lam_k1_c, lam_q2_c, lam_k2_c, subln_g_c, w_out_odd, ln_mix_g, ln_mix_b, ln_ffn_g, ln_ffn_b, w_ff_up, w_ff_down):
    depth = ln_mix_g.shape[0]
    alpha = (2 * depth) ** 0.25
    past = cache_c_k.shape[2]
    sizes = _sizes(x_prompt, x_sample, past)
    xp, xs = x_prompt, x_sample
    even_p, even_s, odd_p, odd_s = [], [], [], []
    for layer in range(depth):
        i = layer // 2
        ffn = (ln_mix_g[layer], ln_mix_b[layer], ln_ffn_g[layer], ln_ffn_b[layer],
               w_ff_up[layer], w_ff_down[layer])
        if layer % 2 == 0:
            rw = (mu_b[i], w0_b[i], w_up_b[i], a0_b[i], a_up_b[i], g_up_b[i], k_k_b[i], k_a_b[i],
                  r_k_b[i].reshape(-1), lnx_g_b[i], lnx_b_b[i])
            xp, xs, st_p, st_s = _even_layer(
                xp, xs, cache_a_k[i], cache_a_v[i], state_b_shift[i], state_b_wkv[i],
                w_in_even[i], sink_a[i], rw, w_out_even[i], ffn, alpha, sizes)
            even_p.append(st_p)
            even_s.append(st_s)
        else:
            lam_init = 0.8 - 0.6 * math.exp(-0.3 * layer)
            xp, xs, st_p, st_s = _odd_layer(
                xp, xs, (cache_c_k[i], cache_c_v[i], cache_d_k[i], cache_d_v[i], cache_d_kidx[i]),
                w_in_odd[i], (lam_q1_c[i], lam_k1_c[i], lam_q2_c[i], lam_k2_c[i]), subln_g_c[i],
                w_out_odd[i], ffn, alpha, lam_init, sizes)
            odd_p.append(st_p)
            odd_s.append(st_s)
    stack = lambda states: [jnp.stack(z, axis=0) for z in zip(*states)]
    a_k_p, a_v_p, b_shift_p, b_wkv_p = stack(even_p)
    a_k_s, a_v_s, b_shift_s, b_wkv_s = stack(even_s)
    c_k_p, c_v_p, d_k_p, d_v_p, d_kidx_p = stack(odd_p)
    c_k_s, c_v_s, d_k_s, d_v_s, d_kidx_s = stack(odd_s)
    return (xp, xs, a_k_p, a_v_p, b_shift_p, b_wkv_p, c_k_p, c_v_p, d_k_p, d_v_p, d_kidx_p,
            a_k_s, a_v_s, b_shift_s, b_wkv_s, c_k_s, c_v_s, d_k_s, d_v_s, d_kidx_s)
```

```python
import functools
import math

import numpy as np
import jax
import jax.numpy as jnp
from jax import lax
from jax.experimental import pallas as pl
from jax.experimental.pallas import tpu as pltpu

F32 = jnp.float32
BF16 = jnp.bfloat16
I32 = jnp.int32

LANES = 128
HEAD_DIM = 64
CHUNK = 64
LN_EPS = 1e-5
RWKV_GN_EPS = 64e-5
WINDOW = 128
TOPK_MAX = 256
NEG_INF = float("-inf")
INT_MIN = -(2 ** 31)
VMEM_LIMIT = 56 * 1024 * 1024


def _cparams(sem):
    return pltpu.CompilerParams(dimension_semantics=sem, vmem_limit_bytes=VMEM_LIMIT)


LOG2E = math.log2(math.e)


def _alibi(n):
    return [float(2.0 ** (-8.0 * (i + 1) / n)) * LOG2E for i in range(n)]


def _bf16_parts(c):
    parts, rem = [], np.float64(c)
    for _ in range(3):
        piece = np.float64(np.asarray(rem, np.float32).astype(BF16).astype(np.float32))
        parts.append(float(piece))
        rem = rem - piece
    return parts


def _alibi_q_features(rows, slope):
    parts = _bf16_parts(slope)
    lane = lax.broadcasted_iota(I32, (rows, LANES), 1)
    out = jnp.zeros((rows, LANES), F32)
    for j, v in enumerate([16.0 * p for p in parts] + parts):
        out = jnp.where(lane == j, v, out)
    return out.astype(BF16)


def _alibi_k_features(kpos):
    lane = lax.broadcasted_iota(I32, (kpos.shape[0], LANES), 1)
    hi = (kpos // 16).astype(F32)
    lo = (kpos % 16).astype(F32)
    return jnp.where(lane < 3, hi, jnp.where(lane < 6, lo, 0.0)).astype(BF16)


def _dot(a, b):
    return jnp.dot(a, b, preferred_element_type=F32)


def _dot_nt(a, b):
    return lax.dot_general(a, b, (((1,), (1,)), ((), ())), preferred_element_type=F32)


def _lane_left(shape):
    return lax.broadcasted_iota(I32, shape, len(shape) - 1) < HEAD_DIM


def _segsum64(x):
    left = _lane_left(x.shape)
    sl = jnp.sum(jnp.where(left, x, 0.0), axis=-1, keepdims=True)
    sr = jnp.sum(jnp.where(left, 0.0, x), axis=-1, keepdims=True)
    return jnp.where(left, sl, sr)


def _proj_kernel(x_ref, w_ref, *out_refs, groups):
    xb = x_ref[...].astype(BF16)
    for off, width, outs in groups:
        r = _dot(xb, w_ref[:, off:off + width])
        for idx, ow, scale in outs:
            v = r if ow == width else r[:, :ow]
            if scale != 1.0:
                v = v * scale
            out_refs[idx][...] = v.astype(out_refs[idx].dtype)


def _project(x, w_bf, groups, out_defs, tm):
    n, d = x.shape
    cols = w_bf.shape[1]
    out_shape = [jax.ShapeDtypeStruct((n, ow), dt) for ow, dt in out_defs]
    out_specs = [pl.BlockSpec((tm, ow), lambda i: (i, 0)) for ow, _ in out_defs]
    return pl.pallas_call(
        functools.partial(_proj_kernel, groups=groups),
        grid=(n // tm,),
        in_specs=[pl.BlockSpec((tm, d), lambda i: (i, 0)),
                  pl.BlockSpec((d, cols), lambda i: (0, 0))],
        out_specs=out_specs,
        out_shape=out_shape,
        compiler_params=_cparams(("parallel",)),
        name="in_proj",
    )(x, w_bf)


def _swa_kernel(sink_ref, q_ref, kp_ref, kc_ref, vp_ref, vc_ref, o_ref, *, qb, pos_base, slopes):
    i = pl.program_id(1)
    q0 = pos_base + i * qb
    kp = kp_ref[0].astype(BF16)
    kc = kc_ref[0].astype(BF16)
    vp = vp_ref[0].astype(BF16)
    vc = vc_ref[0].astype(BF16)
    qpos = q0 + lax.broadcasted_iota(I32, (qb, 1), 0)
    kpos_p = q0 - WINDOW + lax.broadcasted_iota(I32, (1, WINDOW), 1)
    kpos_c = q0 + lax.broadcasted_iota(I32, (1, qb), 1)
    qch = qpos // CHUNK

    def allowed(kpos):
        kch = kpos // CHUNK
        return (kpos >= 0) & (kch >= qch - WINDOW // CHUNK) & (kch <= qch)

    al_p, al_c = allowed(kpos_p), allowed(kpos_c)
    dist_p = jnp.abs(qpos - kpos_p).astype(F32)
    dist_c = jnp.abs(qpos - kpos_c).astype(F32)
    for s in range(4):
        slab = q_ref[0, :, s * LANES:(s + 1) * LANES]
        left = _lane_left(slab.shape)
        res = []
        for side in range(2):
            h = 4 * side + s
            qx = jnp.where(left if side == 0 else jnp.logical_not(left), slab, jnp.zeros_like(slab))
            lp = jnp.where(al_p, _dot_nt(qx, kp) - slopes[h] * dist_p, NEG_INF)
            lc = jnp.where(al_c, _dot_nt(qx, kc) - slopes[h] * dist_c, NEG_INF)
            sk = sink_ref[h] * LOG2E
            m = jnp.maximum(jnp.maximum(jnp.max(lp, -1, keepdims=True), jnp.max(lc, -1, keepdims=True)), sk)
            pp = jnp.exp2(lp - m)
            pc = jnp.exp2(lc - m)
            den = jnp.sum(pp, -1, keepdims=True) + jnp.sum(pc, -1, keepdims=True) + jnp.exp2(sk - m)
            o = _dot(pp.astype(BF16), vp) + _dot(pc.astype(BF16), vc)
            res.append(o / den)
        o_ref[0, :, s * LANES:(s + 1) * LANES] = jnp.where(left, res[0], res[1]).astype(o_ref.dtype)


def _swa(q, k_prev_src, k_cur, v_prev_src, v_cur, sink, *, qb, prev_is_cache, pos_base):
    b, t, _ = q.shape
    nq = t // qb
    if prev_is_cache:
        prev_map = lambda bi, i: (bi, 0, 0)
    else:
        r = qb // WINDOW
        prev_map = lambda bi, i: (bi, jnp.maximum(i * r - 1, 0), 0)
    cur_map = lambda bi, i: (bi, i, 0)
    return pl.pallas_call(
        functools.partial(_swa_kernel, qb=qb, pos_base=pos_base, slopes=_alibi(8)),
        grid=(b, nq),
        in_specs=[pl.BlockSpec(memory_space=pltpu.SMEM),
                  pl.BlockSpec((1, qb, 512), cur_map),
                  pl.BlockSpec((1, WINDOW, LANES), prev_map),
                  pl.BlockSpec((1, qb, LANES), cur_map),
                  pl.BlockSpec((1, WINDOW, LANES), prev_map),
                  pl.BlockSpec((1, qb, LANES), cur_map)],
        out_specs=pl.BlockSpec((1, qb, 512), cur_map),
        out_shape=jax.ShapeDtypeStruct((b, t, 512), BF16),
        compiler_params=_cparams(("parallel", "parallel")),
        name="swa_sink",
    )(sink, q, k_prev_src, k_cur, v_prev_src, v_cur)


def _swa_prompt_kernel(sink_ref, q_ref, kp_ref, kc_ref, vtp_ref, vtc_ref, o_ref, qall_scr, *, qb, slopes):
    i = pl.program_id(1)
    q0 = i * qb
    qpos = q0 + lax.broadcasted_iota(I32, (1, qb), 1)
    for s in range(4):
        slab = q_ref[0, :, s * LANES:(s + 1) * LANES]
        left = _lane_left(slab.shape)
        zero = jnp.zeros_like(slab)
        qall_scr[s * qb:(s + 1) * qb, :LANES] = jnp.where(left, slab, zero)
        qall_scr[(4 + s) * qb:(5 + s) * qb, :LANES] = jnp.where(left, zero, slab)
    for h in range(8):
        qall_scr[h * qb:(h + 1) * qb, LANES:] = _alibi_q_features(qb, slopes[h])
    kpos_p = q0 - WINDOW + lax.broadcasted_iota(I32, (WINDOW, 1), 0)
    kpos_c = q0 + lax.broadcasted_iota(I32, (qb, 1), 0)
    qch = qpos // CHUNK
    first = qch - WINDOW // CHUNK
    neg_p = jnp.where((kpos_p >= 0) & (kpos_p // CHUNK >= first), 0.0, NEG_INF)
    kch_c = kpos_c // CHUNK
    neg_c = jnp.where((kch_c >= first) & (kch_c <= qch), 0.0, NEG_INF)
    ahead = jnp.maximum(kpos_c - qpos, 0).astype(F32)
    qall = qall_scr[...]
    st_p = _dot_nt(jnp.concatenate([kp_ref[0], _alibi_k_features(jnp.maximum(kpos_p, 0))], axis=1), qall)
    st_c = _dot_nt(jnp.concatenate([kc_ref[0], _alibi_k_features(kpos_c)], axis=1), qall)
    qposf = qpos.astype(F32)
    pps, pcs, dens = [], [], []
    for h in range(8):
        sp = st_p[:, h * qb:(h + 1) * qb] + neg_p
        sc = st_c[:, h * qb:(h + 1) * qb] - (2.0 * slopes[h]) * ahead + neg_c
        sk = sink_ref[h] * LOG2E + slopes[h] * qposf
        m = jnp.maximum(jnp.maximum(jnp.max(sp, 0, keepdims=True), jnp.max(sc, 0, keepdims=True)), sk)
        pp = jnp.exp2(sp - m)
        pc = jnp.exp2(sc - m)
        dens.append(jnp.sum(pp, 0, keepdims=True) + jnp.sum(pc, 0, keepdims=True) + jnp.exp2(sk - m))
        pps.append(pp.astype(BF16))
        pcs.append(pc.astype(BF16))
    ot = (_dot(vtp_ref[0], jnp.concatenate(pps, axis=1))
          + _dot(vtc_ref[0], jnp.concatenate(pcs, axis=1)))
    top = lax.broadcasted_iota(I32, (LANES, qb), 0) < HEAD_DIM
    for s in range(4):
        o_l = ot[:, s * qb:(s + 1) * qb] / dens[s]
        o_r = ot[:, (4 + s) * qb:(5 + s) * qb] / dens[4 + s]
        o_ref[0, :, s * LANES:(s + 1) * LANES] = jnp.where(top, o_l, o_r).T.astype(o_ref.dtype)


def _swa_prompt(q, k, v, sink, qb):
    b, t, _ = q.shape
    r = qb // WINDOW
    vt = jnp.swapaxes(v, -1, -2)
    cur_map = lambda bi, i: (bi, i, 0)
    prev_map = lambda bi, i: (bi, jnp.maximum(i * r - 1, 0), 0)
    return pl.pallas_call(
        functools.partial(_swa_prompt_kernel, qb=qb, slopes=_alibi(8)),
        grid=(b, t // qb),
        in_specs=[pl.BlockSpec(memory_space=pltpu.SMEM),
                  pl.BlockSpec((1, qb, 512), cur_map),
                  pl.BlockSpec((1, WINDOW, LANES), prev_map),
                  pl.BlockSpec((1, qb, LANES), cur_map),
                  pl.BlockSpec((1, LANES, WINDOW), lambda bi, i: (bi, 0, jnp.maximum(i * r - 1, 0))),
                  pl.BlockSpec((1, LANES, qb), lambda bi, i: (bi, 0, i))],
        out_specs=pl.BlockSpec((1, qb, 512), cur_map),
        out_shape=jax.ShapeDtypeStruct((b, t, 512), BF16),
        scratch_shapes=[pltpu.VMEM((8 * qb, 2 * LANES), BF16)],
        compiler_params=_cparams(("parallel", "parallel")),
        name="swa_prompt",
    )(sink, q, k, k, vt, vt)


def _sigmoid(x):
    return 1.0 / (1.0 + jnp.exp(-x))


def _softplus(x):
    return jnp.maximum(x, 0.0) + jnp.log(1.0 + jnp.exp(-jnp.abs(x)))


def _rwkv_prep_kernel(pb_ref, pbprev_ref, shift_ref, mu_ref, w0_ref, wup_ref, a0_ref, aup_ref, gup_ref,
                      kk_ref, ka_ref, rk_ref,
                      r_out, w_out, k_out, v_out, kk_out, b_out, g_out, bonus_out):
    i = pl.program_id(1)
    pb = pb_ref[0]
    tm = pb.shape[0]
    prev_row = jnp.where(i == 0, shift_ref[0], pbprev_ref[0, 7:8, :])
    rolled = pltpu.roll(pb, 1, axis=0)
    row = lax.broadcasted_iota(I32, (tm, 1), 0)
    prev = jnp.where(row == 0, prev_row, rolled)
    xm = pb + (prev - pb) * mu_ref[...]
    r = xm[:, 0:512]
    k = xm[:, 512:1024]
    v = xm[:, 1024:1536]
    wa = xm[:, 1536:1664]
    gl = xm[:, 1664:1792]
    lw = _dot(jnp.tanh(wa).astype(BF16), wup_ref[...])
    la = _dot(wa.astype(BF16), aup_ref[...])
    w_log = -_softplus(-(w0_ref[...] + lw)) - 0.5
    log_decay = -jnp.exp(w_log)
    a = _sigmoid(a0_ref[...] + la)
    g = _dot(_sigmoid(gl).astype(BF16), gup_ref[...])
    kk = k * kk_ref[...]
    k2 = k * (1.0 + (a - 1.0) * ka_ref[...])
    rkk = r * k2 * rk_ref[...]
    for p in range(4):
        sl = slice(p * LANES, (p + 1) * LANES)
        kkp = kk[:, sl]
        kkn = kkp * lax.rsqrt(_segsum64(kkp * kkp) + 1e-12)
        kk_out[0, p] = kkn
        b_out[0, p] = kkn * a[:, sl]
        bonus_out[0, :, sl] = _segsum64(rkk[:, sl]) * v[:, sl]
        r_out[0, p] = r[:, sl]
        w_out[0, p] = log_decay[:, sl]
        k_out[0, p] = k2[:, sl]
        v_out[0, p] = v[:, sl]
    g_out[0] = g


RWKV_CHUNK = 16


def _split2(x):
    hi = x.astype(BF16)
    return hi, (x - hi.astype(F32)).astype(BF16)


def _dot3(a, b):
    return _dot(a[0], b[0]) + (_dot(a[0], b[1]) + _dot(a[1], b[0]))


def _rwkv_operator_scratch(tb):
    nc, c2 = tb // RWKV_CHUNK, 2 * RWKV_CHUNK
    return [pltpu.VMEM((4, nc, c2, LANES), BF16), pltpu.VMEM((4, nc, c2, LANES), BF16),
            pltpu.VMEM((4, nc, LANES, c2), BF16), pltpu.VMEM((4, nc, LANES, c2), BF16),
            pltpu.VMEM((4, nc, LANES, LANES), F32),
            pltpu.VMEM((4, tb, LANES), F32), pltpu.VMEM((4, tb, LANES), F32)]


def _rwkv_operator_stages(r_ref, lw_ref, k_ref, v_ref, kk_ref, b_ref,
                          l1hi_scr, l1lo_scr, l2hi_scr, l2lo_scr, pcol_scr, u0_scr, y0_scr, *, tb):
    c_sz = RWKV_CHUNK
    nc = tb // c_sz
    row = lax.broadcasted_iota(I32, (tb, tb), 0)
    col = lax.broadcasted_iota(I32, (tb, tb), 1)
    same = (row // c_sz) == (col // c_sz)
    strict = same & (col < row)
    incl = same & (col <= row)
    cum_lhs = jnp.concatenate([jnp.where(incl, 1.0, 0.0), jnp.where(same, 1.0, 0.0)], axis=0).astype(BF16)
    left = _lane_left((tb, LANES))
    left2 = jnp.concatenate([left, left], axis=1)

    def stages():
        qs = range(4)
        sides = range(2)

        def halves(fn):
            out = [None] * 4
            for group in ((0, 1), (2, 3)):
                for q in group:
                    out[q] = fn(q)
                yield
            return out

        def split3(x):
            x1 = x.astype(BF16)
            rem = x - x1.astype(F32)
            x2 = rem.astype(BF16)
            return x1, x2, (rem - x2.astype(F32)).astype(BF16)

        def cum_of(q):
            l = split3(lw_ref[0, q])
            return _dot(cum_lhs, l[0]) + (_dot(cum_lhs, l[1]) + _dot(cum_lhs, l[2]))
        cums = yield from halves(cum_of)
        al, rt, vs, rhs, lhs, bbt, kbt, pc_rows = [], [], [], [], [], [], [], []
        for q in qs:
            r, lw, k, v, kk, b = (ref[0, q] for ref in (r_ref, lw_ref, k_ref, v_ref, kk_ref, b_ref))
            cs, cse = cums[q][:tb], cums[q][tb:]
            p_inv = jnp.exp(-cs)
            p_end = jnp.exp(cse - cs)
            al.append(-kk * jnp.exp(cs - lw))
            rt.append(r * jnp.exp(cs))
            vs.append(_split2(v))
            rhs.append(_split2(jnp.concatenate([b * p_inv, k * p_inv], axis=0)))
            zero = jnp.zeros_like(v)
            half = lambda z, s: jnp.where(left, z, zero) if s == 0 else jnp.where(left, zero, z)
            lhs.append([_split2(jnp.concatenate([half(al[q], s), half(rt[q], s)], axis=0)) for s in sides])
            bbt.append(_split2((b * p_end).T))
            kbt.append(_split2((k * p_end).T))
            pc_rows.append(jnp.exp(cse))
        gram = yield from halves(lambda q: [
            _dot_nt(lhs[q][s][0], rhs[q][0]) + (_dot_nt(lhs[q][s][0], rhs[q][1]) + _dot_nt(lhs[q][s][1], rhs[q][0]))
            for s in sides])
        a1 =[[_split2(jnp.where(strict, gram[q][s][:tb, :tb], 0.0)) for s in sides] for q in qs]
        a2 = [[_split2(jnp.where(strict, gram[q][s][:tb, tb:], 0.0)) for s in sides] for q in qs]
        b1 = [[_split2(jnp.where(incl, gram[q][s][tb:, :tb], 0.0)) for s in sides] for q in qs]
        b2 = [[_split2(jnp.where(incl, gram[q][s][tb:, tb:], 0.0)) for s in sides] for q in qs]
        xv_pw = yield from halves(lambda q: (
            jnp.where(left, _dot3(a2[q][0], vs[q]), _dot3(a2[q][1], vs[q])),
            [_dot3(a1[q][s], a1[q][s]) for s in sides]))
        x = [jnp.concatenate([al[q], xv_pw[q][0]], axis=1) for q in qs]
        pw = [xv_pw[q][1] for q in qs]
        m = a1
        for step in range(4):
            def apply(q, m=m, pw=pw, x=x, step=step):
                m_new = [_split2(pw[q][s]) for s in sides] if step < 3 else None
                pw_new = [_dot3(m_new[s], m_new[s]) for s in sides] if step < 2 else None
                xq = _split2(x[q])
                x_new = x[q] + jnp.where(left2, _dot3(m[q][0], xq), _dot3(m[q][1], xq))
                return x_new, m_new, pw_new
            res = yield from halves(apply)
            x = [res[q][0] for q in qs]
            m = [res[q][1] for q in qs]
            pw = [res[q][2] for q in qs]
        xs = [_split2(x[q]) for q in qs]
        bxv = yield from halves(lambda q: (
            jnp.where(left2, _dot(b1[q][0][0], xs[q][0]), _dot(b1[q][1][0], xs[q][0])),
            jnp.where(left, _dot(b2[q][0][0], vs[q][0]), _dot(b2[q][1][0], vs[q][0]))))

        def store(q):
            bx, bv = bxv[q]
            ah = x[q][:, :LANES]
            rh = rt[q] + bx[:, :LANES]
            u0_scr[q] = x[q][:, LANES:]
            y0_scr[q] = bx[:, LANES:] + bv
            pct = pc_rows[q].T
            for c in range(nc):
                rows = slice(c * c_sz, (c + 1) * c_sz)
                l1 = _split2(jnp.concatenate([ah[rows], rh[rows]], axis=0))
                l1hi_scr[q, c], l1lo_scr[q, c] = l1
                l2hi_scr[q, c] = jnp.concatenate([bbt[q][0][:, rows], kbt[q][0][:, rows]], axis=1)
                l2lo_scr[q, c] = jnp.concatenate([bbt[q][1][:, rows], kbt[q][1][:, rows]], axis=1)
                pcol_scr[q, c] = jnp.broadcast_to(pct[:, c * c_sz:c * c_sz + 1], (LANES, LANES))
        yield from halves(store)

    yield from stages()


def _rwkv_sequential_steps(v_ref, y_ref, st, l1hi_scr, l1lo_scr, l2hi_scr, l2lo_scr, pcol_scr, u0_scr, y0_scr,
                           *, tb):
    c_sz = RWKV_CHUNK
    r128 = lax.broadcasted_iota(I32, (LANES, LANES), 0)
    c128 = lax.broadcasted_iota(I32, (LANES, LANES), 1)
    blockdiag = (r128 < HEAD_DIM) == (c128 < HEAD_DIM)
    for c in range(tb // c_sz):
        rows = slice(c * c_sz, (c + 1) * c_sz)
        res = [_dot3((l1hi_scr[p, c], l1lo_scr[p, c]), _split2(st[p])) for p in range(4)]
        yield
        for p in range(4):
            y_ref[0, p, rows, :] = res[p][c_sz:] + y0_scr[p, rows, :]
        w2 = [_split2(jnp.concatenate([res[p][:c_sz] + u0_scr[p, rows, :], v_ref[0, p, rows, :]], axis=0))
              for p in range(4)]
        upd = [_dot3((l2hi_scr[p, c], l2lo_scr[p, c]), w2[p]) for p in range(4)]
        for p in range(4):
            st[p] = st[p] * pcol_scr[p, c] + jnp.where(blockdiag, upd[p], 0.0)
        yield


def _interleave(*streams):
    live = list(streams)
    while live:
        for g in list(live):
            try:
                next(g)
            except StopIteration:
                live.remove(g)


def _state_to_blockdiag(s):
    b = s.shape[0]
    st = jnp.swapaxes(s, -1, -2).reshape(b, 4, 2, HEAD_DIM, HEAD_DIM)
    z = jnp.zeros_like(st[:, :, 0])
    top = jnp.concatenate([st[:, :, 0], z], axis=-1)
    bot = jnp.concatenate([z, st[:, :, 1]], axis=-1)
    return jnp.concatenate([top, bot], axis=-2)


def _state_from_blockdiag(s_bd):
    b = s_bd.shape[0]
    a = s_bd[:, :, :HEAD_DIM, :HEAD_DIM]
    d = s_bd[:, :, HEAD_DIM:, HEAD_DIM:]
    st = jnp.stack([a, d], axis=2).reshape(b, 8, HEAD_DIM, HEAD_DIM)
    return jnp.swapaxes(st, -1, -2)


def _rwkv_post_kernel(y_ref, bonus_ref, g_ref, lg_ref, lb_ref, o_ref):
    for p in range(4):
        sl = slice(p * LANES, (p + 1) * LANES)
        y = y_ref[0, p]
        d = y - _segsum64(y) * (1.0 / HEAD_DIM)
        var = _segsum64(d * d) * (1.0 / HEAD_DIM)
        yn = d * lax.rsqrt(var + RWKV_GN_EPS) * lg_ref[:, sl] + lb_ref[:, sl]
        o_ref[0, :, sl] = ((yn + bonus_ref[0, :, sl]) * g_ref[0, :, sl]).astype(o_ref.dtype)


def _rwkv_kernel(pb_ref, pbprev_ref, shift_ref, mu_ref, w0_ref, wup_ref, a0_ref, aup_ref, gup_ref,
                 kk_ref, ka_ref, rk_ref, lg_ref, lb_ref, s0_ref, ob_ref, sT_ref,
                 tok_s, g_s, bonus_s, y_s, st_scr, *op_scr, tb):
    i = pl.program_id(1)

    @pl.when(i == 0)
    def _():
        st_scr[...] = s0_ref[0]

    tok = [tok_s.at[j] for j in range(6)]
    _rwkv_prep_kernel(pb_ref, pbprev_ref, shift_ref, mu_ref, w0_ref, wup_ref, a0_ref, aup_ref, gup_ref,
                      kk_ref, ka_ref, rk_ref, *tok, g_s, bonus_s)
    _interleave(_rwkv_operator_stages(*tok, *op_scr, tb=tb))
    st = [st_scr[p] for p in range(4)]
    _interleave(_rwkv_sequential_steps(tok[3], y_s, st, *op_scr, tb=tb))
    for p in range(4):
        st_scr[p] = st[p]
    _rwkv_post_kernel(y_s, bonus_s, g_s, lg_ref, lb_ref, ob_ref)

    @pl.when(i == pl.num_programs(1) - 1)
    def _():
        sT_ref[0] = st_scr[...]


def _rwkv(pb, shift, mu, w0, wup, a0, aup, gup, k_k, k_a, r_k, lnx_g, lnx_b, s0_bd, tb):
    b, t, c = pb.shape
    cur = lambda bi, i: (bi, i, 0)
    vec = lambda n: pl.BlockSpec((1, n), lambda bi, i: (0, 0))
    lora = pl.BlockSpec((LANES, 512), lambda bi, i: (0, 0))
    st_spec = pl.BlockSpec((1, 4, LANES, LANES), lambda bi, i: (bi, 0, 0, 0))
    return pl.pallas_call(
        functools.partial(_rwkv_kernel, tb=tb),
        grid=(b, t // tb),
        in_specs=[pl.BlockSpec((1, tb, c), cur),
                  pl.BlockSpec((1, 8, c), lambda bi, i: (bi, jnp.maximum(i * (tb // 8) - 1, 0), 0)),
                  pl.BlockSpec((1, 1, c), lambda bi, i: (bi, 0, 0)),
                  vec(c), vec(512), lora, vec(512), lora, lora, vec(512), vec(512), vec(512),
                  vec(512), vec(512), st_spec],
        out_specs=[pl.BlockSpec((1, tb, 512), cur), st_spec],
        out_shape=[jax.ShapeDtypeStruct((b, t, 512), BF16),
                   jax.ShapeDtypeStruct((b, 4, LANES, LANES), F32)],
        scratch_shapes=([pltpu.VMEM((6, 1, 4, tb, LANES), F32)] + [pltpu.VMEM((1, tb, 512), F32)] * 2
                        + [pltpu.VMEM((1, 4, tb, LANES), F32), pltpu.VMEM((4, LANES, LANES), F32)]
                        + _rwkv_operator_scratch(tb)),
        compiler_params=_cparams(("parallel", "arbitrary")),
        name="rwkv",
    )(pb, pb, shift, mu, w0, wup, a0, aup, gup, k_k, k_a, r_k, lnx_g, lnx_b, s0_bd)


def _layer_norm(x, g, b):
    mu = jnp.mean(x, -1, keepdims=True)
    d = x - mu
    var = jnp.mean(d * d, -1, keepdims=True)
    return d * lax.rsqrt(var + LN_EPS) * g + b


def _ffn_kernel(x_ref, m1_ref, m2_ref, wo_ref, g1_ref, b1_ref, wu_ref, wd_ref, g2_ref, b2_ref, o_ref,
                *, alpha, ff_chunk):
    half = m1_ref.shape[1]
    mix = _dot(m1_ref[...], wo_ref[0:half, :]) + _dot(m2_ref[...], wo_ref[half:, :])
    x1 = _layer_norm(alpha * x_ref[...] + mix, g1_ref[...], b1_ref[...])
    x1b = x1.astype(BF16)
    d_ff = wu_ref.shape[1]
    h = jnp.zeros_like(x1)
    for c in range(d_ff // ff_chunk):
        u = jnp.maximum(_dot(x1b, wu_ref[:, c * ff_chunk:(c + 1) * ff_chunk]), 0.0)
        h = h + _dot((u * u).astype(BF16), wd_ref[c * ff_chunk:(c + 1) * ff_chunk, :])
    o_ref[...] = _layer_norm(alpha * x1 + h, g2_ref[...], b2_ref[...])


def _const_spec(shape):
    return pl.BlockSpec(shape, lambda i: (0,) * len(shape), pipeline_mode=pl.Buffered(1))


def _out_ffn(x, m1, m2, wo, g1, b1, wu, wd, g2, b2, alpha, tm):
    n, d = x.shape
    dm = m1.shape[1]
    d_ff = wu.shape[1]
    tile = lambda w: pl.BlockSpec((tm, w), lambda i: (i, 0))
    return pl.pallas_call(
        functools.partial(_ffn_kernel, alpha=alpha, ff_chunk=1024),
        grid=(n // tm,),
        in_specs=[tile(d), tile(dm), tile(dm),
                  _const_spec((2 * dm, d)), _const_spec((1, d)), _const_spec((1, d)),
                  _const_spec((d, d_ff)), _const_spec((d_ff, d)), _const_spec((1, d)), _const_spec((1, d))],
        out_specs=tile(d),
        out_shape=jax.ShapeDtypeStruct((n, d), F32),
        compiler_params=_cparams(("parallel",)),
        name="out_ffn",
    )(x, m1, m2, wo, g1, b1, wu, wd, g2, b2)


def _diff_lambda(lam_ref, lam_init):
    lv = lam_ref[...]
    s1 = jnp.sum(lv[0:1] * lv[1:2], axis=-1, keepdims=True)
    s2 = jnp.sum(lv[2:3] * lv[3:4], axis=-1, keepdims=True)
    return jnp.exp(s1) - jnp.exp(s2) + lam_init


def _split_halves(qh):
    left = _lane_left(qh.shape)
    zero = jnp.zeros_like(qh)
    return jnp.concatenate([jnp.where(left, qh, zero), jnp.where(left, zero, qh)], axis=0)


def _flash_update(carry, s3, vblk):
    m, l, acc = carry
    g, rws, kb = s3.shape
    m_new = jnp.maximum(m, jnp.max(s3, -1, keepdims=True))
    m_use = jnp.where(m_new == NEG_INF, 0.0, m_new)
    corr = jnp.exp2(m - m_use)
    p = jnp.exp2(s3 - m_use)
    l = l * corr + jnp.sum(p, -1, keepdims=True)
    pv = _dot(p.reshape(g * rws, kb).astype(BF16), vblk).reshape(g, rws, vblk.shape[1])
    return m_new, l, acc * corr + pv


def _flash_init(g, rws, e):
    return (jnp.full((g, rws, 1), NEG_INF, F32), jnp.zeros((g, rws, 1), F32), jnp.zeros((g, rws, e), F32))


def _diff_finish(carry, lam, gain, lam_init):
    m, l, acc = carry
    o = acc[0] / l[0] - lam * (acc[1] / l[1])
    o = o * lax.rsqrt(jnp.mean(o * o, -1, keepdims=True) + LN_EPS) * gain * (1.0 - lam_init)
    return o


def _diff_prompt_kernel(q_ref, k_ref, vt_ref, lam_ref, gain_ref, o_ref, q2_scr, *, qb, slopes, lam_init):
    i = pl.program_id(1)
    lam = _diff_lambda(lam_ref, lam_init)
    qpos = i * qb + lax.broadcasted_iota(I32, (1, qb), 1)
    for h in range(4):
        q2_scr[2 * h * qb:(2 * h + 2) * qb, :LANES] = _split_halves(q_ref[0, :, h * LANES:(h + 1) * LANES])
        q2_scr[2 * h * qb:(2 * h + 2) * qb, LANES:] = _alibi_q_features(2 * qb, slopes[h])

    def body(kb, carry, diagonal):
        m, l, acc = carry
        rows = pl.ds(pl.multiple_of(kb * qb, qb), qb)
        kpos = kb * qb + lax.broadcasted_iota(I32, (qb, 1), 0)
        kfeat = _alibi_k_features(kpos)
        st = [_dot_nt(jnp.concatenate([k_ref[0, rows, h * LANES:(h + 1) * LANES], kfeat], axis=1),
                      q2_scr[2 * h * qb:(2 * h + 2) * qb, :])
              for h in range(4)]
        part = lambda g: st[g // 2][:, (g % 2) * qb:(g % 2 + 1) * qb]
        if diagonal:
            ahead = jnp.maximum(kpos - qpos, 0).astype(F32)
            neg = jnp.where(kpos // CHUNK <= qpos // CHUNK, 0.0, NEG_INF)
            s3 = jnp.stack([part(g) - (2.0 * slopes[g // 2]) * ahead + neg for g in range(8)], axis=0)
        else:
            s3 = jnp.stack([part(g) for g in range(8)], axis=0)
        m_new = jnp.maximum(m, jnp.max(s3, axis=1, keepdims=True))
        corr = jnp.exp2(m - m_new)
        p = jnp.exp2(s3 - m_new)
        l = l * corr + jnp.sum(p, axis=1, keepdims=True)
        pv = [_dot(vt_ref[0, kb, h * LANES:(h + 1) * LANES, :],
                   jnp.concatenate([p[2 * h].astype(BF16), p[2 * h + 1].astype(BF16)], axis=1))
              for h in range(4)]
        acc = acc * corr + jnp.stack([pv[g // 2][:, (g % 2) * qb:(g % 2 + 1) * qb] for g in range(8)], axis=0)
        return m_new, l, acc

    init = (jnp.full((8, 1, qb), NEG_INF, F32), jnp.zeros((8, 1, qb), F32), jnp.zeros((8, LANES, qb), F32))
    carry = lax.fori_loop(0, i, functools.partial(body, diagonal=False), init)
    m, l, acc = body(i, carry, diagonal=True)
    for h in range(4):
        o = acc[2 * h] / l[2 * h] - lam * (acc[2 * h + 1] / l[2 * h + 1])
        o = o * lax.rsqrt(jnp.mean(o * o, axis=0, keepdims=True) + LN_EPS) * gain_ref[...] * (1.0 - lam_init)
        o_ref[0, :, h * LANES:(h + 1) * LANES] = o.T.astype(o_ref.dtype)


def _diff_prompt(q, k, v, lamv, gain, lam_init, qb):
    b, t, _ = q.shape
    nkb = t // qb
    vt = jnp.swapaxes(v.reshape(b, nkb, qb, 512), -1, -2)
    full = pl.BlockSpec((1, t, 512), lambda bi, i: (bi, 0, 0))
    tile = pl.BlockSpec((1, qb, 512), lambda bi, i: (bi, i, 0))
    return pl.pallas_call(
        functools.partial(_diff_prompt_kernel, qb=qb, slopes=_alibi(4), lam_init=lam_init),
        grid=(b, nkb),
        in_specs=[tile, full,
                  pl.BlockSpec((1, nkb, 512, qb), lambda bi, i: (bi, 0, 0, 0)),
                  pl.BlockSpec((4, HEAD_DIM), lambda bi, i: (0, 0)),
                  pl.BlockSpec((LANES, 1), lambda bi, i: (0, 0))],
        out_specs=tile,
        out_shape=jax.ShapeDtypeStruct((b, t, 512), BF16),
        scratch_shapes=[pltpu.VMEM((8 * qb, 2 * LANES), BF16)],
        compiler_params=_cparams(("parallel", "arbitrary")),
        name="diff_prompt",
    )(q, k, vt, lamv, gain.reshape(LANES, 1))


def _diff_sample_kernel(q_ref, kc_ref, vc_ref, kn_ref, vn_ref, lam_ref, gain_ref, o_ref,
                        m_scr, l_scr, acc_scr, *, t, kb_size, n_past, slopes, lam_init):
    j = pl.program_id(1)
    past_len = n_past * kb_size
    qpos = past_len + lax.broadcasted_iota(I32, (t, 1), 0)

    @pl.when(j == 0)
    def _():
        m_scr[...] = jnp.full(m_scr.shape, NEG_INF, F32)
        l_scr[...] = jnp.zeros(l_scr.shape, F32)
        acc_scr[...] = jnp.zeros(acc_scr.shape, F32)

    def run(kfull, v_of, kpos):
        dist = jnp.abs(qpos - kpos).astype(F32)
        nk = kfull.shape[0]
        s3 = jnp.stack(
            [_dot_nt(_split_halves(q_ref[0, :, h * LANES:(h + 1) * LANES]),
                     kfull[:, h * LANES:(h + 1) * LANES]).reshape(2, t, nk) - slopes[h] * dist[None]
             for h in range(4)], axis=0)
        m = m_scr[...]
        m_new = jnp.maximum(m, jnp.max(s3, -1, keepdims=True))
        corr = jnp.exp2(m - m_new)
        p = jnp.exp2(s3 - m_new)
        l_scr[...] = l_scr[...] * corr + jnp.sum(p, -1, keepdims=True)
        pv = jnp.stack([_dot(p[h].reshape(2 * t, nk).astype(BF16), v_of(h)).reshape(2, t, LANES)
                        for h in range(4)], axis=0)
        acc_scr[...] = acc_scr[...] * corr + pv
        m_scr[...] = m_new

    @pl.when(j < n_past)
    def _():
        kpos = j * kb_size + lax.broadcasted_iota(I32, (1, kb_size), 1)
        run(kc_ref[0].astype(BF16), lambda h: vc_ref[0, :, h * LANES:(h + 1) * LANES].astype(BF16), kpos)

    @pl.when(j == n_past)
    def _():
        kpos = past_len + lax.broadcasted_iota(I32, (1, t), 1)
        run(kn_ref[0].astype(BF16), lambda h: vn_ref[0, :, h * LANES:(h + 1) * LANES].astype(BF16), kpos)
        lam = _diff_lambda(lam_ref, lam_init)
        for h in range(4):
            sl = slice(h * LANES, (h + 1) * LANES)
            carry = (m_scr[h], l_scr[h], acc_scr[h])
            o_ref[0, :, sl] = _diff_finish(carry, lam, gain_ref[...], lam_init).astype(o_ref.dtype)


def _diff_sample(q, k_cache, v_cache, k_new, v_new, lamv, gain, lam_init, kb_size):
    b, t, _ = q.shape
    n_past = k_cache.shape[1] // kb_size
    cache = pl.BlockSpec((1, kb_size, 512), lambda bi, j: (bi, jnp.minimum(j, n_past - 1), 0))
    new = pl.BlockSpec((1, t, 512), lambda bi, j: (bi, 0, 0))
    return pl.pallas_call(
        functools.partial(_diff_sample_kernel, t=t, kb_size=kb_size, n_past=n_past,
                          slopes=_alibi(4), lam_init=lam_init),
        grid=(b, n_past + 1),
        in_specs=[new, cache, cache, new, new,
                  pl.BlockSpec((4, HEAD_DIM), lambda bi, j: (0, 0)),
                  pl.BlockSpec((1, LANES), lambda bi, j: (0, 0))],
        out_specs=new,
        out_shape=jax.ShapeDtypeStruct((b, t, 512), BF16),
        scratch_shapes=[pltpu.VMEM((4, 2, t, 1), F32), pltpu.VMEM((4, 2, t, 1), F32),
                        pltpu.VMEM((4, 2, t, LANES), F32)],
        compiler_params=_cparams(("parallel", "arbitrary")),
        name="diff_sample",
    )(q, k_cache, v_cache, k_new, v_new, lamv, gain)


def _index_keys(qi, wi, ki_blk, allowed):
    acc = None
    for h in range(8):
        sc = _dot_nt(qi[:, h * HEAD_DIM:(h + 1) * HEAD_DIM], ki_blk)
        term = jnp.maximum(sc, 0.0) * wi[:, h:h + 1]
        acc = term if acc is None else acc + term
    idx = acc * (8.0 ** -0.5) + 0.0
    bits = pltpu.bitcast(idx, I32)
    key = jnp.where(bits < 0, bits ^ 0x7FFFFFFF, bits)
    return jnp.where(allowed, key, INT_MIN)


def _lane_fold(x):
    acc = x[:, :LANES]
    for j in range(1, x.shape[1] // LANES):
        acc = acc + x[:, j * LANES:(j + 1) * LANES]
    return acc


def _ones_where(pred):
    return jnp.where(pred, 1, 0).astype(I32)


def _kth_largest(count_ge, shape, n_sel):
    def body(it, lo):
        bit = 31 - it
        cand = lo + jnp.left_shift(jnp.int32(1), bit)
        return jnp.where(count_ge(cand) >= n_sel, cand, lo)
    return lax.fori_loop(0, 32, body, jnp.full(shape, INT_MIN, I32))


def _tie_position(count_eq_le, need, shape, nbits):
    def body(it, lo):
        bit = nbits - 1 - it
        cand = lo + jnp.left_shift(jnp.int32(1), bit)
        return jnp.where(count_eq_le(cand - 1) < need, cand, lo)
    return lax.fori_loop(0, nbits, body, jnp.zeros(shape, I32))


def _select_threshold(count_over, shape, n_sel, pos_bits):
    thr = _kth_largest(lambda cand: count_over(lambda key, kpos: key >= cand), shape, n_sel)
    need = n_sel - count_over(lambda key, kpos: key > thr)
    n_eq = count_over(lambda key, kpos: key == thr)
    has_tie = jnp.max(jnp.where((n_eq > need) & (thr > INT_MIN), 1, 0)) > 0
    pos_thr = lax.cond(
        has_tie,
        lambda: _tie_position(
            lambda p: count_over(lambda key, kpos: (key == thr) & (kpos <= p)), need, shape, pos_bits),
        lambda: jnp.full(shape, 2 ** 30, I32))
    return thr, pos_thr


def _dsa_attend(carry, q_ref, kblk, vblk, dist, sel, slopes):
    neg = jnp.where(sel, 0.0, NEG_INF)
    qs = []
    for s in range(4):
        slab = q_ref[0, :, s * LANES:(s + 1) * LANES]
        left = _lane_left(slab.shape)
        zero = jnp.zeros_like(slab)
        qs.append((jnp.where(left, slab, zero), jnp.where(left, zero, slab)))
    q_all = jnp.concatenate([qs[h % 4][h // 4] for h in range(8)], axis=0)
    r, kb = dist.shape
    s3 = _dot_nt(q_all, kblk).reshape(8, r, kb)
    s3 = jnp.stack([s3[h] - slopes[h] * dist + neg for h in range(8)], axis=0)
    return _flash_update(carry, s3, vblk)


def _dsa_finish(carry, o_ref):
    m, l, acc = carry
    o = acc / l
    for s in range(4):
        left = _lane_left(o[s].shape)
        o_ref[0, :, s * LANES:(s + 1) * LANES] = jnp.where(left, o[s], o[4 + s]).astype(o_ref.dtype)


def _dsa_prompt_kernel(q_ref, qi_ref, wi_ref, k_ref, vt_ref, ki_ref, o_ref, key_scr, qall_scr, qiall_scr,
                       *, qb, n_sel, slopes):
    i = pl.program_id(1)
    nkb = i + 1
    qpos = i * qb + lax.broadcasted_iota(I32, (1, qb), 1)

    for s in range(4):
        slab = q_ref[0, :, s * LANES:(s + 1) * LANES]
        left = _lane_left(slab.shape)
        zero = jnp.zeros_like(slab)
        qall_scr[s * qb:(s + 1) * qb, :LANES] = jnp.where(left, slab, zero)
        qall_scr[(4 + s) * qb:(5 + s) * qb, :LANES] = jnp.where(left, zero, slab)
    for h in range(8):
        qall_scr[h * qb:(h + 1) * qb, LANES:] = _alibi_q_features(qb, slopes[h])
        qiall_scr[h * qb:(h + 1) * qb, :] = qi_ref[0, :, h * HEAD_DIM:(h + 1) * HEAD_DIM]
    wit = wi_ref[0].T

    def kpos_of(kb):
        return kb * qb + lax.broadcasted_iota(I32, (qb, 1), 0)

    def rows_of(kb):
        return pl.ds(pl.multiple_of(kb * qb, qb), qb)

    def fill(kb, c):
        allowed = kpos_of(kb) // CHUNK <= qpos // CHUNK
        sc = _dot_nt(ki_ref[0, rows_of(kb), :], qiall_scr[...])
        acc = None
        for h in range(8):
            term = jnp.maximum(sc[:, h * qb:(h + 1) * qb], 0.0) * wit[h:h + 1, :]
            acc = term if acc is None else acc + term
        idx = acc * (8.0 ** -0.5) + 0.0
        bits = pltpu.bitcast(idx, I32)
        key = jnp.where(bits < 0, bits ^ 0x7FFFFFFF, bits)
        key_scr[kb] = jnp.where(allowed, key, INT_MIN)
        return c
    lax.fori_loop(0, nkb, fill, 0)

    def count_over(pred_fn):
        def body(kb, c):
            ones = _ones_where(pred_fn(key_scr[kb], kpos_of(kb)))
            return c + jnp.sum(ones.reshape(qb // 8, 8, qb), axis=0)
        part = lax.fori_loop(0, nkb, body, jnp.zeros((8, qb), I32))
        return jnp.sum(part, axis=0, keepdims=True)

    thr, pos_thr = _select_threshold(count_over, (1, qb), n_sel, 12)

    def attend(kb, carry, diagonal):
        m, l, acc = carry
        kpos = kpos_of(kb)
        key = key_scr[kb]
        sel = (key > thr) | ((key == thr) & (kpos <= pos_thr))
        if diagonal:
            sel = sel & (kpos // CHUNK <= qpos // CHUNK)
        neg = jnp.where(sel, 0.0, NEG_INF)
        k_aug = jnp.concatenate([k_ref[0, rows_of(kb), :], _alibi_k_features(kpos)], axis=1)
        st = _dot_nt(k_aug, qall_scr[...])
        if diagonal:
            ahead = jnp.maximum(kpos - qpos, 0).astype(F32)
            s3 = jnp.stack([st[:, h * qb:(h + 1) * qb] - (2.0 * slopes[h]) * ahead + neg for h in range(8)],
                           axis=0)
        else:
            s3 = jnp.stack([st[:, h * qb:(h + 1) * qb] + neg for h in range(8)], axis=0)
        m_new = jnp.maximum(m, jnp.max(s3, axis=1, keepdims=True))
        m_use = jnp.where(m_new == NEG_INF, 0.0, m_new)
        corr = jnp.exp2(m - m_use)
        p = jnp.exp2(s3 - m_use)
        l = l * corr + jnp.sum(p, axis=1, keepdims=True)
        p_all = jnp.concatenate([p[h].astype(BF16) for h in range(8)], axis=1)
        pv = _dot(vt_ref[0, kb], p_all)
        acc = acc * corr + jnp.stack([pv[:, h * qb:(h + 1) * qb] for h in range(8)], axis=0)
        return m_new, l, acc

    init = (jnp.full((8, 1, qb), NEG_INF, F32), jnp.zeros((8, 1, qb), F32), jnp.zeros((8, LANES, qb), F32))
    carry = lax.fori_loop(0, i, functools.partial(attend, diagonal=False), init)
    m, l, acc = attend(i, carry, diagonal=True)
    ot = acc / l
    top = lax.broadcasted_iota(I32, (LANES, qb), 0) < HEAD_DIM
    for s in range(4):
        o_ref[0, :, s * LANES:(s + 1) * LANES] = jnp.where(top, ot[s], ot[4 + s]).T.astype(o_ref.dtype)


def _dsa_prompt(q, qi, wi, k, v, ki, qb):
    b, t, _ = q.shape
    n_sel = min(TOPK_MAX, t // 4)
    nkb = t // qb
    vt = jnp.swapaxes(v.reshape(b, nkb, qb, LANES), -1, -2)
    tile = lambda w: pl.BlockSpec((1, qb, w), lambda bi, i: (bi, i, 0))
    full = lambda w: pl.BlockSpec((1, t, w), lambda bi, i: (bi, 0, 0))
    return pl.pallas_call(
        functools.partial(_dsa_prompt_kernel, qb=qb, n_sel=n_sel, slopes=_alibi(8)),
        grid=(b, nkb),
        in_specs=[tile(512), tile(512), tile(LANES), full(LANES),
                  pl.BlockSpec((1, nkb, LANES, qb), lambda bi, i: (bi, 0, 0, 0)), full(HEAD_DIM)],
        out_specs=tile(512),
        out_shape=jax.ShapeDtypeStruct((b, t, 512), BF16),
        scratch_shapes=[pltpu.VMEM((nkb, qb, qb), I32), pltpu.VMEM((8 * qb, 2 * LANES), BF16),
                        pltpu.VMEM((8 * qb, HEAD_DIM), BF16)],
        compiler_params=_cparams(("parallel", "arbitrary")),
        name="dsa_prompt",
    )(q, qi, wi, k, vt, ki)


def _dsa_sample_kernel(q_ref, qi_ref, wi_ref, kc_ref, vc_ref, kic_ref, kn_ref, vn_ref, kin_ref, o_ref,
                       key_scr, keyn_scr, *, t, kb_size, n_past, n_sel, slopes):
    past_len = n_past * kb_size
    qpos = past_len + lax.broadcasted_iota(I32, (t, 1), 0)
    qi = qi_ref[0]
    wi = wi_ref[0]
    kpos_n = past_len + lax.broadcasted_iota(I32, (1, t), 1)

    def kpos_of(kb):
        return kb * kb_size + lax.broadcasted_iota(I32, (1, kb_size), 1)

    for kb in range(n_past):
        ki_blk = kic_ref[0, kb * kb_size:(kb + 1) * kb_size, :].astype(BF16)
        key_scr[kb] = _index_keys(qi, wi, ki_blk, True)
    keyn_scr[...] = _index_keys(qi, wi, kin_ref[0].astype(BF16), True)

    def count_over(pred_fn):
        part = _lane_fold(_ones_where(pred_fn(key_scr[0], kpos_of(0))))
        for kb in range(1, n_past):
            part = part + _lane_fold(_ones_where(pred_fn(key_scr[kb], kpos_of(kb))))
        new = jnp.sum(_ones_where(pred_fn(keyn_scr[...], kpos_n)), axis=-1, keepdims=True)
        return jnp.sum(part, axis=-1, keepdims=True) + new

    thr, pos_thr = _select_threshold(count_over, (t, 1), n_sel, 13)
    carry = _flash_init(8, t, LANES)

    def sel_of(key, kpos):
        return (key > thr) | ((key == thr) & (kpos <= pos_thr))

    for kb in range(n_past):
        kpos = kpos_of(kb)
        dist = jnp.abs(qpos - kpos).astype(F32)
        kblk = kc_ref[0, kb * kb_size:(kb + 1) * kb_size, :].astype(BF16)
        vblk = vc_ref[0, kb * kb_size:(kb + 1) * kb_size, :].astype(BF16)
        carry = _dsa_attend(carry, q_ref, kblk, vblk, dist, sel_of(key_scr[kb], kpos), slopes)
    dist = jnp.abs(qpos - kpos_n).astype(F32)
    carry = _dsa_attend(carry, q_ref, kn_ref[0].astype(BF16), vn_ref[0].astype(BF16), dist,
                        sel_of(keyn_scr[...], kpos_n), slopes)
    _dsa_finish(carry, o_ref)


def _dsa_sample(q, qi, wi, k_cache, v_cache, ki_cache, k_new, v_new, ki_new, kb_size):
    b, t, _ = q.shape
    past = k_cache.shape[1]
    n_past = past // kb_size
    n_sel = min(TOPK_MAX, (past + t) // 4)
    per_b = lambda rows, w: pl.BlockSpec((1, rows, w), lambda bi: (bi, 0, 0))
    return pl.pallas_call(
        functools.partial(_dsa_sample_kernel, t=t, kb_size=kb_size, n_past=n_past, n_sel=n_sel,
                          slopes=_alibi(8)),
        grid=(b,),
        in_specs=[per_b(t, 512), per_b(t, 512), per_b(t, LANES),
                  per_b(past, LANES), per_b(past, LANES), per_b(past, HEAD_DIM),
                  per_b(t, LANES), per_b(t, LANES), per_b(t, HEAD_DIM)],
        out_specs=per_b(t, 512),
        out_shape=jax.ShapeDtypeStruct((b, t, 512), BF16),
        scratch_shapes=[pltpu.VMEM((n_past, t, kb_size), I32), pltpu.VMEM((t, t), I32)],
        compiler_params=_cparams(("parallel",)),
        name="dsa_sample",
    )(q, qi, wi, k_cache, v_cache, ki_cache, k_new, v_new, ki_new)


def _pad_cols(w, total):
    return jnp.pad(w, ((0, 0), (0, total - w.shape[1])))


def _q_perm():
    idx = []
    for s in range(4):
        idx.extend(range(s * HEAD_DIM, (s + 1) * HEAD_DIM))
        idx.extend(range((4 + s) * HEAD_DIM, (5 + s) * HEAD_DIM))
    return np.asarray(idx, np.int32)


def _even_layer(xp, xs, cache_k, cache_v, shift_s, wkv_s, w_in, sink, rw, w_out, ffn, alpha, sizes):
    (mu, w0, w_up, a0, a_up, g_up, k_k, k_a, r_k, lnx_g, lnx_b) = rw
    perm = _q_perm()
    w_in_p = jnp.concatenate([w_in[:, :512][:, perm], w_in[:, 512:]], axis=1).astype(BF16)
    w_out_p = jnp.concatenate([w_out[:512][perm], w_out[512:]], axis=0).astype(BF16)
    groups = ((0, 512, ((0, 512, HEAD_DIM ** -0.5 * LOG2E),)),
              (512, 128, ((1, 128, 1.0), (2, 128, 1.0))),
              (640, 128, ((3, 128, 1.0), (4, 128, 1.0))),
              (768, 1792, ((5, 1792, 1.0),)))
    out_defs = ((512, BF16), (128, F32), (128, BF16), (128, F32), (128, BF16), (1792, F32))
    wup_p = jnp.concatenate([w_up, jnp.zeros_like(w_up)], axis=0).astype(BF16)
    aup_p = jnp.concatenate([jnp.zeros_like(a_up), a_up], axis=0).astype(BF16)
    gup = g_up.astype(BF16)
    row = lambda v: v.reshape(1, -1)
    g1, b1, g2, b2, wu, wd = ffn

    def run(x, state, tm, qb, scan_tb):
        b, t, d = x.shape
        n = b * t
        qa, ka, ka_bf, va, va_bf, pb = _project(x.reshape(n, d), w_in_p, groups, out_defs, tm)
        r3 = lambda a: a.reshape(b, t, a.shape[-1])
        if state is None:
            oa = _swa_prompt(r3(qa), r3(ka_bf), r3(va_bf), sink, qb)
            k_buf = r3(ka)[:, -WINDOW:]
            v_buf = r3(va)[:, -WINDOW:]
            shift = jnp.zeros((b, 1, pb.shape[-1]), F32)
            s0 = jnp.zeros((b, 8, HEAD_DIM, HEAD_DIM), F32)
        else:
            ck, cv, shift, s0 = state
            ck2 = ck.reshape(b, WINDOW, LANES)
            cv2 = cv.reshape(b, WINDOW, LANES)
            oa = _swa(r3(qa), ck2, r3(ka_bf), cv2, r3(va_bf), sink,
                      qb=qb, prev_is_cache=True, pos_base=sizes["past"])
            k_buf = jnp.concatenate([ck2, r3(ka)], axis=1)[:, -WINDOW:]
            v_buf = jnp.concatenate([cv2, r3(va)], axis=1)[:, -WINDOW:]
        pb3 = r3(pb)
        ob, s_bd = _rwkv(pb3, shift, row(mu), row(w0), wup_p, row(a0), aup_p, gup, row(k_k), row(k_a),
                         row(r_k), row(lnx_g), row(lnx_b), _state_to_blockdiag(s0), scan_tb)
        s_t = _state_from_blockdiag(s_bd)
        xo = _out_ffn(x.reshape(n, d), oa.reshape(n, 512), ob.reshape(n, 512), w_out_p, row(g1), row(b1),
                      wu.astype(BF16), wd.astype(BF16), row(g2), row(b2), alpha, tm)
        st = (k_buf.reshape(b, WINDOW, 2, HEAD_DIM), v_buf.reshape(b, WINDOW, 2, HEAD_DIM),
              pb3[:, -1:], s_t)
        return xo.reshape(b, t, d), st

    xp2, st_p = run(xp, None, sizes["tm_p"], sizes["swa_qb"], sizes["scan_tb_p"])
    xs2, st_s = run(xs, (cache_k, cache_v, shift_s, wkv_s), sizes["tm_s"], xs.shape[1], xs.shape[1])
    return xp2, xs2, st_p, st_s


def _odd_layer(xp, xs, caches, w_in, lam_vecs, subln_g, w_out, ffn, alpha, lam_init, sizes):
    perm = _q_perm()
    scale = HEAD_DIM ** -0.5 * LOG2E
    w_p = jnp.concatenate([
        w_in[:, :1536], w_in[:, 1536:2048][:, perm], w_in[:, 2048:2816],
        _pad_cols(w_in[:, 2816:2880], LANES), _pad_cols(w_in[:, 2880:2888], LANES)], axis=1).astype(BF16)
    w_out_p = jnp.concatenate([w_out[:512], w_out[512:][perm]], axis=0).astype(BF16)
    groups = ((0, 512, ((0, 512, scale),)),
              (512, 512, ((1, 512, 1.0), (2, 512, 1.0))),
              (1024, 512, ((3, 512, 1.0), (4, 512, 1.0))),
              (1536, 512, ((5, 512, scale),)),
              (2048, 128, ((6, 128, 1.0), (7, 128, 1.0))),
              (2176, 128, ((8, 128, 1.0), (9, 128, 1.0))),
              (2304, 512, ((10, 512, HEAD_DIM ** -0.5),)),
              (2816, 128, ((11, 64, 1.0), (12, 64, 1.0))),
              (2944, 128, ((13, 128, 1.0),)))
    out_defs = ((512, BF16), (512, F32), (512, BF16), (512, F32), (512, BF16), (512, BF16),
                (128, F32), (128, BF16), (128, F32), (128, BF16), (512, BF16),
                (64, F32), (64, BF16), (128, F32))
    row = lambda v: v.reshape(1, -1)
    g1, b1, g2, b2, wu, wd = ffn
    lamv = jnp.stack(lam_vecs, axis=0)
    gain = row(subln_g)

    def run(x, state, tm):
        b, t, d = x.shape
        n = b * t
        (qc, kc, kc_bf, vc, vc_bf, qd, kd, kd_bf, vd, vd_bf, qi, ki, ki_bf, wi) = _project(
            x.reshape(n, d), w_p, groups, out_defs, tm)
        r3 = lambda a: a.reshape(b, t, a.shape[-1])
        if state is None:
            oc = _diff_prompt(r3(qc), r3(kc_bf), r3(vc_bf), lamv, gain, lam_init, sizes["diff_qb"])
            od = _dsa_prompt(r3(qd), r3(qi), r3(wi), r3(kd_bf), r3(vd_bf), r3(ki_bf), sizes["dsa_qb"])
        else:
            c_k, c_v, d_k, d_v, d_ki = state
            past = c_k.shape[1]
            oc = _diff_sample(r3(qc), c_k.reshape(b, past, 512), c_v.reshape(b, past, 512),
                              r3(kc_bf), r3(vc_bf), lamv, gain, lam_init, sizes["diff_kb_s"])
            od = _dsa_sample(r3(qd), r3(qi), r3(wi), d_k.reshape(b, past, LANES), d_v.reshape(b, past, LANES),
                             d_ki, r3(kd_bf), r3(vd_bf), r3(ki_bf), sizes["dsa_kb_s"])
        xo = _out_ffn(x.reshape(n, d), oc.reshape(n, 512), od.reshape(n, 512), w_out_p, row(g1), row(b1),
                      wu.astype(BF16), wd.astype(BF16), row(g2), row(b2), alpha, tm)
        rows = (kc.reshape(b, t, 4, 2, HEAD_DIM), vc.reshape(b, t, 4, 2 * HEAD_DIM),
                kd.reshape(b, t, 2, HEAD_DIM), vd.reshape(b, t, 2, HEAD_DIM), ki.reshape(b, t, HEAD_DIM))
        return xo.reshape(b, t, d), rows

    xp2, st_p = run(xp, None, sizes["tm_p"])
    xs2, st_s = run(xs, caches, sizes["tm_s"])
    return xp2, xs2, st_p, st_s


def _sizes(xp, xs, past):
    return dict(tm_p=min(512, xp.shape[0] * xp.shape[1]), tm_s=min(512, xs.shape[0] * xs.shape[1]),
                swa_qb=min(256, xp.shape[1]), scan_tb_p=min(128, xp.shape[1]),
                diff_qb=min(256, xp.shape[1]),
                dsa_qb=min(256, xp.shape[1]), diff_kb_s=min(1024, past), dsa_kb_s=min(512, past),
                past=past)


def kernel(x_prompt, x_sample, cache_a_k, cache_a_v, state_b_shift, state_b_wkv, cache_c_k, cache_c_v, cache_d_k, cache_d_v, cache_d_kidx, w_in_even, sink_a, mu_b, w0_b, w_up_b, a0_b, a_up_b, g_up_b, k_k_b, k_a_b, r_k_b, lnx_g_b, lnx_b_b, w_out_even, w_in_odd, lam_q1_c, lam_k1_c, lam_q2_c, lam_k2_c, subln_g_c, w_out_odd, ln_mix_g, ln_mix_b, ln_ffn_g, ln_ffn_b, w_ff_up, w_ff_down):
    depth = ln_mix_g.shape[0]
    alpha = (2 * depth) ** 0.25
    past = cache_c_k.shape[2]
    sizes = _sizes(x_prompt, x_sample, past)
    xp, xs = x_prompt, x_sample
    even_p, even_s, odd_p, odd_s = [], [], [], []
    for layer in range(depth):
        i = layer // 2
        ffn = (ln_mix_g[layer], ln_mix_b[layer], ln_ffn_g[layer], ln_ffn_b[layer],
               w_ff_up[layer], w_ff_down[layer])
        if layer % 2 == 0:
            rw = (mu_b[i], w0_b[i], w_up_b[i], a0_b[i], a_up_b[i], g_up_b[i], k_k_b[i], k_a_b[i],
                  r_k_b[i].reshape(-1), lnx_g_b[i], lnx_b_b[i])
            xp, xs, st_p, st_s = _even_layer(
                xp, xs, cache_a_k[i], cache_a_v[i], state_b_shift[i], state_b_wkv[i],
                w_in_even[i], sink_a[i], rw, w_out_even[i], ffn, alpha, sizes)
            even_p.append(st_p)
            even_s.append(st_s)
        else:
            lam_init = 0.8 - 0.6 * math.exp(-0.3 * layer)
            xp, xs, st_p, st_s = _odd_layer(
                xp, xs, (cache_c_k[i], cache_c_v[i], cache_d_k[i], cache_d_v[i], cache_d_kidx[i]),
                w_in_odd[i], (lam_q1_c[i], lam_k1_c[i], lam_q2_c[i], lam_k2_c[i]), subln_g_c[i],
                w_out_odd[i], ffn, alpha, lam_init, sizes)
            odd_p.append(st_p)
            odd_s.append(st_s)
    stack = lambda states: [jnp.stack(z, axis=0) for z in zip(*states)]
    a_k_p, a_v_p, b_shift_p, b_wkv_p = stack(even_p)
    a_k_s, a_v_s, b_shift_s, b_wkv_s = stack(even_s)
    c_k_p, c_v_p, d_k_p, d_v_p, d_kidx_p = stack(odd_p)
    c_k_s, c_v_s, d_k_s, d_v_s, d_kidx_s = stack(odd_s)
    return (xp, xs, a_k_p, a_v_p, b_shift_p, b_wkv_p, c_k_p, c_v_p, d_k_p, d_v_p, d_kidx_p,
            a_k_s, a_v_s, b_shift_s, b_wkv_s, c_k_s, c_v_s, d_k_s, d_v_s, d_kidx_s)
```

```python
import functools
import math

import numpy as np
import jax
import jax.numpy as jnp
from jax import lax
from jax.experimental import pallas as pl
from jax.experimental.pallas import tpu as pltpu

F32 = jnp.float32
BF16 = jnp.bfloat16
I32 = jnp.int32

LANES = 128
HEAD_DIM = 64
CHUNK = 64
LN_EPS = 1e-5
RWKV_GN_EPS = 64e-5
WINDOW = 128
TOPK_MAX = 256
NEG_INF = float("-inf")
INT_MIN = -(2 ** 31)
VMEM_LIMIT = 56 * 1024 * 1024


def _cparams(sem):
    return pltpu.CompilerParams(dimension_semantics=sem, vmem_limit_bytes=VMEM_LIMIT)


LOG2E = math.log2(math.e)


def _alibi(n):
    return [float(2.0 ** (-8.0 * (i + 1) / n)) * LOG2E for i in range(n)]


def _bf16_parts(c):
    parts, rem = [], np.float64(c)
    for _ in range(3):
        piece = np.float64(np.asarray(rem, np.float32).astype(BF16).astype(np.float32))
        parts.append(float(piece))
        rem = rem - piece
    return parts


def _alibi_q_features(rows, slope):
    parts = _bf16_parts(slope)
    lane = lax.broadcasted_iota(I32, (rows, LANES), 1)
    out = jnp.zeros((rows, LANES), F32)
    for j, v in enumerate([16.0 * p for p in parts] + parts):
        out = jnp.where(lane == j, v, out)
    return out.astype(BF16)


def _alibi_k_features(kpos):
    lane = lax.broadcasted_iota(I32, (kpos.shape[0], LANES), 1)
    hi = (kpos // 16).astype(F32)
    lo = (kpos % 16).astype(F32)
    return jnp.where(lane < 3, hi, jnp.where(lane < 6, lo, 0.0)).astype(BF16)


def _dot(a, b):
    return jnp.dot(a, b, preferred_element_type=F32)


def _dot_nt(a, b):
    return lax.dot_general(a, b, (((1,), (1,)), ((), ())), preferred_element_type=F32)


def _dot_tn(a, b):
    return lax.dot_general(a, b, (((0,), (0,)), ((), ())), preferred_element_type=F32)


def _lane_left(shape):
    return lax.broadcasted_iota(I32, shape, len(shape) - 1) < HEAD_DIM


def _segsum64(x):
    left = _lane_left(x.shape)
    sl = jnp.sum(jnp.where(left, x, 0.0), axis=-1, keepdims=True)
    sr = jnp.sum(jnp.where(left, 0.0, x), axis=-1, keepdims=True)
    return jnp.where(left, sl, sr)


def _proj_kernel(x_ref, w_ref, *out_refs, groups):
    xb = x_ref[...].astype(BF16)
    for off, width, outs in groups:
        r = _dot(xb, w_ref[:, off:off + width])
        for idx, ow, scale in outs:
            v = r if ow == width else r[:, :ow]
            if scale != 1.0:
                v = v * scale
            out_refs[idx][...] = v.astype(out_refs[idx].dtype)


def _project(x, w_bf, groups, out_defs, tm):
    n, d = x.shape
    cols = w_bf.shape[1]
    out_shape = [jax.ShapeDtypeStruct((n, ow), dt) for ow, dt in out_defs]
    out_specs = [pl.BlockSpec((tm, ow), lambda i: (i, 0)) for ow, _ in out_defs]
    return pl.pallas_call(
        functools.partial(_proj_kernel, groups=groups),
        grid=(n // tm,),
        in_specs=[pl.BlockSpec((tm, d), lambda i: (i, 0)),
                  pl.BlockSpec((d, cols), lambda i: (0, 0))],
        out_specs=out_specs,
        out_shape=out_shape,
        compiler_params=_cparams(("parallel",)),
        name="in_proj",
    )(x, w_bf)


def _swa_kernel(sink_ref, q_ref, kp_ref, kc_ref, vp_ref, vc_ref, o_ref, *, qb, pos_base, slopes):
    i = pl.program_id(1)
    q0 = pos_base + i * qb
    kp = kp_ref[0].astype(BF16)
    kc = kc_ref[0].astype(BF16)
    vp = vp_ref[0].astype(BF16)
    vc = vc_ref[0].astype(BF16)
    qpos = q0 + lax.broadcasted_iota(I32, (qb, 1), 0)
    kpos_p = q0 - WINDOW + lax.broadcasted_iota(I32, (1, WINDOW), 1)
    kpos_c = q0 + lax.broadcasted_iota(I32, (1, qb), 1)
    qch = qpos // CHUNK

    def allowed(kpos):
        kch = kpos // CHUNK
        return (kpos >= 0) & (kch >= qch - WINDOW // CHUNK) & (kch <= qch)

    al_p, al_c = allowed(kpos_p), allowed(kpos_c)
    dist_p = jnp.abs(qpos - kpos_p).astype(F32)
    dist_c = jnp.abs(qpos - kpos_c).astype(F32)
    for s in range(4):
        slab = q_ref[0, :, s * LANES:(s + 1) * LANES]
        left = _lane_left(slab.shape)
        res = []
        for side in range(2):
            h = 4 * side + s
            qx = jnp.where(left if side == 0 else jnp.logical_not(left), slab, jnp.zeros_like(slab))
            lp = jnp.where(al_p, _dot_nt(qx, kp) - slopes[h] * dist_p, NEG_INF)
            lc = jnp.where(al_c, _dot_nt(qx, kc) - slopes[h] * dist_c, NEG_INF)
            sk = sink_ref[h] * LOG2E
            m = jnp.maximum(jnp.maximum(jnp.max(lp, -1, keepdims=True), jnp.max(lc, -1, keepdims=True)), sk)
            pp = jnp.exp2(lp - m)
            pc = jnp.exp2(lc - m)
            den = jnp.sum(pp, -1, keepdims=True) + jnp.sum(pc, -1, keepdims=True) + jnp.exp2(sk - m)
            o = _dot(pp.astype(BF16), vp) + _dot(pc.astype(BF16), vc)
            res.append(o / den)
        o_ref[0, :, s * LANES:(s + 1) * LANES] = jnp.where(left, res[0], res[1]).astype(o_ref.dtype)


def _swa_step(q, k_cache, k_cur, v_cache, v_cur, sink, *, qb, pos_base):
    b, t, _ = q.shape
    nq = t // qb
    prev_map = lambda bi, i: (bi, 0, 0)
    cur_map = lambda bi, i: (bi, i, 0)
    return pl.pallas_call(
        functools.partial(_swa_kernel, qb=qb, pos_base=pos_base, slopes=_alibi(8)),
        grid=(b, nq),
        in_specs=[pl.BlockSpec(memory_space=pltpu.SMEM),
                  pl.BlockSpec((1, qb, 512), cur_map),
                  pl.BlockSpec((1, WINDOW, LANES), prev_map),
                  pl.BlockSpec((1, qb, LANES), cur_map),
                  pl.BlockSpec((1, WINDOW, LANES), prev_map),
                  pl.BlockSpec((1, qb, LANES), cur_map)],
        out_specs=pl.BlockSpec((1, qb, 512), cur_map),
        out_shape=jax.ShapeDtypeStruct((b, t, 512), BF16),
        compiler_params=_cparams(("parallel", "parallel")),
        name="swa_sink",
    )(sink, q, k_cache, k_cur, v_cache, v_cur)


def _swa_prompt_kernel(sink_ref, q_ref, kp_ref, kc_ref, vtp_ref, vtc_ref, o_ref, qall_scr, *, qb, slopes):
    i = pl.program_id(1)
    q0 = i * qb
    qpos = q0 + lax.broadcasted_iota(I32, (1, qb), 1)
    for s in range(4):
        slab = q_ref[0, :, s * LANES:(s + 1) * LANES]
        left = _lane_left(slab.shape)
        zero = jnp.zeros_like(slab)
        qall_scr[s * qb:(s + 1) * qb, :LANES] = jnp.where(left, slab, zero)
        qall_scr[(4 + s) * qb:(5 + s) * qb, :LANES] = jnp.where(left, zero, slab)
    for h in range(8):
        qall_scr[h * qb:(h + 1) * qb, LANES:] = _alibi_q_features(qb, slopes[h])
    kpos_p = q0 - WINDOW + lax.broadcasted_iota(I32, (WINDOW, 1), 0)
    kpos_c = q0 + lax.broadcasted_iota(I32, (qb, 1), 0)
    qch = qpos // CHUNK
    first = qch - WINDOW // CHUNK
    neg_p = jnp.where((kpos_p >= 0) & (kpos_p // CHUNK >= first), 0.0, NEG_INF)
    kch_c = kpos_c // CHUNK
    neg_c = jnp.where((kch_c >= first) & (kch_c <= qch), 0.0, NEG_INF)
    ahead = jnp.maximum(kpos_c - qpos, 0).astype(F32)
    qall = qall_scr[...]
    st_p = _dot_nt(jnp.concatenate([kp_ref[0], _alibi_k_features(jnp.maximum(kpos_p, 0))], axis=1), qall)
    st_c = _dot_nt(jnp.concatenate([kc_ref[0], _alibi_k_features(kpos_c)], axis=1), qall)
    qposf = qpos.astype(F32)
    pps, pcs, dens = [], [], []
    for h in range(8):
        sp = st_p[:, h * qb:(h + 1) * qb] + neg_p
        sc = st_c[:, h * qb:(h + 1) * qb] - (2.0 * slopes[h]) * ahead + neg_c
        sk = sink_ref[h] * LOG2E + slopes[h] * qposf
        m = jnp.maximum(jnp.maximum(jnp.max(sp, 0, keepdims=True), jnp.max(sc, 0, keepdims=True)), sk)
        pp = jnp.exp2(sp - m)
        pc = jnp.exp2(sc - m)
        dens.append(jnp.sum(pp, 0, keepdims=True) + jnp.sum(pc, 0, keepdims=True) + jnp.exp2(sk - m))
        pps.append(pp.astype(BF16))
        pcs.append(pc.astype(BF16))
    ot = (_dot(vtp_ref[0], jnp.concatenate(pps, axis=1))
          + _dot(vtc_ref[0], jnp.concatenate(pcs, axis=1)))
    top = lax.broadcasted_iota(I32, (LANES, qb), 0) < HEAD_DIM
    for s in range(4):
        o_l = ot[:, s * qb:(s + 1) * qb] / dens[s]
        o_r = ot[:, (4 + s) * qb:(5 + s) * qb] / dens[4 + s]
        o_ref[0, :, s * LANES:(s + 1) * LANES] = jnp.where(top, o_l, o_r).T.astype(o_ref.dtype)


def _swa_prompt(q, k, v, sink, qb):
    b, t, _ = q.shape
    r = qb // WINDOW
    vt = jnp.swapaxes(v, -1, -2)
    cur_map = lambda bi, i: (bi, i, 0)
    prev_map = lambda bi, i: (bi, jnp.maximum(i * r - 1, 0), 0)
    return pl.pallas_call(
        functools.partial(_swa_prompt_kernel, qb=qb, slopes=_alibi(8)),
        grid=(b, t // qb),
        in_specs=[pl.BlockSpec(memory_space=pltpu.SMEM),
                  pl.BlockSpec((1, qb, 512), cur_map),
                  pl.BlockSpec((1, WINDOW, LANES), prev_map),
                  pl.BlockSpec((1, qb, LANES), cur_map),
                  pl.BlockSpec((1, LANES, WINDOW), lambda bi, i: (bi, 0, jnp.maximum(i * r - 1, 0))),
                  pl.BlockSpec((1, LANES, qb), lambda bi, i: (bi, 0, i))],
        out_specs=pl.BlockSpec((1, qb, 512), cur_map),
        out_shape=jax.ShapeDtypeStruct((b, t, 512), BF16),
        scratch_shapes=[pltpu.VMEM((8 * qb, 2 * LANES), BF16)],
        compiler_params=_cparams(("parallel", "parallel")),
        name="swa_prompt",
    )(sink, q, k, k, vt, vt)


def _sigmoid(x):
    return 1.0 / (1.0 + jnp.exp(-x))


def _softplus(x):
    return jnp.maximum(x, 0.0) + jnp.log(1.0 + jnp.exp(-jnp.abs(x)))


def _rwkv_prep_kernel(pb_ref, pbprev_ref, shift_ref, mu_ref, w0_ref, wup_ref, a0_ref, aup_ref, gup_ref,
                      kk_ref, ka_ref, rk_ref,
                      r_out, w_out, k_out, v_out, kk_out, b_out, g_out, bonus_out):
    i = pl.program_id(1)
    pb = pb_ref[0]
    tm = pb.shape[0]
    prev_row = jnp.where(i == 0, shift_ref[0], pbprev_ref[0, 7:8, :])
    rolled = pltpu.roll(pb, 1, axis=0)
    row = lax.broadcasted_iota(I32, (tm, 1), 0)
    prev = jnp.where(row == 0, prev_row, rolled)
    xm = pb + (prev - pb) * mu_ref[...]
    r = xm[:, 0:512]
    k = xm[:, 512:1024]
    v = xm[:, 1024:1536]
    wa = xm[:, 1536:1664]
    gl = xm[:, 1664:1792]
    lw = _dot(jnp.tanh(wa).astype(BF16), wup_ref[...])
    la = _dot(wa.astype(BF16), aup_ref[...])
    w_log = -_softplus(-(w0_ref[...] + lw)) - 0.5
    log_decay = -jnp.exp(w_log)
    a = _sigmoid(a0_ref[...] + la)
    g = _dot(_sigmoid(gl).astype(BF16), gup_ref[...])
    kk = k * kk_ref[...]
    k2 = k * (1.0 + (a - 1.0) * ka_ref[...])
    rkk = r * k2 * rk_ref[...]
    for p in range(4):
        sl = slice(p * LANES, (p + 1) * LANES)
        kkp = kk[:, sl]
        kkn = kkp * lax.rsqrt(_segsum64(kkp * kkp) + 1e-12)
        kk_out[0, p] = kkn
        b_out[0, p] = kkn * a[:, sl]
        bonus_out[0, :, sl] = _segsum64(rkk[:, sl]) * v[:, sl]
        r_out[0, p] = r[:, sl]
        w_out[0, p] = log_decay[:, sl]
        k_out[0, p] = k2[:, sl]
        v_out[0, p] = v[:, sl]
    g_out[0] = g


RWKV_CHUNK = 16


def _split2(x):
    hi = x.astype(BF16)
    return hi, (x - hi.astype(F32)).astype(BF16)


def _dot3(a, b):
    m = a[0].shape[0]
    both = _dot(jnp.concatenate([a[0], a[1]], axis=0), b[0])
    return both[:m] + (_dot(a[0], b[1]) + both[m:])


def _rwkv_operator_scratch(tb):
    nc, c2 = tb // RWKV_CHUNK, 2 * RWKV_CHUNK
    return [pltpu.VMEM((4, nc, c2, LANES), BF16), pltpu.VMEM((4, nc, c2, LANES), BF16),
            pltpu.VMEM((4, nc, c2, LANES), BF16), pltpu.VMEM((4, nc, c2, LANES), BF16),
            pltpu.VMEM((4, nc, LANES, LANES), F32),
            pltpu.VMEM((4, tb, LANES), F32), pltpu.VMEM((4, tb, LANES), F32)]


def _rwkv_operator_stages(r_ref, lw_ref, k_ref, v_ref, kk_ref, b_ref,
                          l1hi_scr, l1lo_scr, l2hi_scr, l2lo_scr, pcol_scr, u0_scr, y0_scr, *, tb):
    c_sz = RWKV_CHUNK
    nc = tb // c_sz
    row = lax.broadcasted_iota(I32, (tb, tb), 0)
    col = lax.broadcasted_iota(I32, (tb, tb), 1)
    same = (row // c_sz) == (col // c_sz)
    strict = same & (col < row)
    incl = same & (col <= row)
    cum_lhs = jnp.concatenate([jnp.where(incl, 1.0, 0.0), jnp.where(same, 1.0, 0.0)], axis=0).astype(BF16)
    left = _lane_left((tb, LANES))
    left2 = jnp.concatenate([left, left], axis=1)

    def stages():
        qs = range(4)
        sides = range(2)

        def halves(fn):
            out = [None] * 4
            for group in ((0, 1), (2, 3)):
                for q in group:
                    out[q] = fn(q)
                yield
            return out

        def split3(x):
            x1 = x.astype(BF16)
            rem = x - x1.astype(F32)
            x2 = rem.astype(BF16)
            return x1, x2, (rem - x2.astype(F32)).astype(BF16)

        def cum_of(q):
            l = split3(lw_ref[0, q])
            return _dot(cum_lhs, l[0]) + (_dot(cum_lhs, l[1]) + _dot(cum_lhs, l[2]))
        cums = yield from halves(cum_of)
        al, rt, vs, rhs, lhs, bbt, kbt, pc_rows = [], [], [], [], [], [], [], []
        for q in qs:
            r, lw, k, v, kk, b = (ref[0, q] for ref in (r_ref, lw_ref, k_ref, v_ref, kk_ref, b_ref))
            cs, cse = cums[q][:tb], cums[q][tb:]
            p_inv = jnp.exp(-cs)
            p_end = jnp.exp(cse - cs)
            al.append(-kk * jnp.exp(cs - lw))
            rt.append(r * jnp.exp(cs))
            vs.append(_split2(v))
            rhs.append(_split2(jnp.concatenate([b * p_inv, k * p_inv], axis=0)))
            zero = jnp.zeros_like(v)
            half = lambda z, s: jnp.where(left, z, zero) if s == 0 else jnp.where(left, zero, z)
            lhs.append([_split2(jnp.concatenate([half(al[q], s), half(rt[q], s)], axis=0)) for s in sides])
            bbt.append(_split2(b * p_end))
            kbt.append(_split2(k * p_end))
            pc_rows.append(jnp.exp(cse))
        gram = yield from halves(lambda q: [
            _dot_nt(lhs[q][s][0], rhs[q][0]) + (_dot_nt(lhs[q][s][0], rhs[q][1]) + _dot_nt(lhs[q][s][1], rhs[q][0]))
            for s in sides])
        a1 =[[_split2(jnp.where(strict, gram[q][s][:tb, :tb], 0.0)) for s in sides] for q in qs]
        a2 = [[_split2(jnp.where(strict, gram[q][s][:tb, tb:], 0.0)) for s in sides] for q in qs]
        b1 = [[_split2(jnp.where(incl, gram[q][s][tb:, :tb], 0.0)) for s in sides] for q in qs]
        b2 = [[_split2(jnp.where(incl, gram[q][s][tb:, tb:], 0.0)) for s in sides] for q in qs]
        xv_pw = yield from halves(lambda q: (
            jnp.where(left, _dot3(a2[q][0], vs[q]), _dot3(a2[q][1], vs[q])),
            [_dot3(a1[q][s], a1[q][s]) for s in sides]))
        x = [jnp.concatenate([al[q], xv_pw[q][0]], axis=1) for q in qs]
        pw = [xv_pw[q][1] for q in qs]
        m = a1
        for step in range(4):
            def apply(q, m=m, pw=pw, x=x, step=step):
                m_new = [_split2(pw[q][s]) for s in sides] if step < 3 else None
                pw_new = [_dot3(m_new[s], m_new[s]) for s in sides] if step < 2 else None
                xq = _split2(x[q])
                x_new = x[q] + jnp.where(left2, _dot3(m[q][0], xq), _dot3(m[q][1], xq))
                return x_new, m_new, pw_new
            res = yield from halves(apply)
            x = [res[q][0] for q in qs]
            m = [res[q][1] for q in qs]
            pw = [res[q][2] for q in qs]
        xs = [_split2(x[q]) for q in qs]
        bxv = yield from halves(lambda q: (
            jnp.where(left2, _dot(b1[q][0][0], xs[q][0]), _dot(b1[q][1][0], xs[q][0])),
            jnp.where(left, _dot(b2[q][0][0], vs[q][0]), _dot(b2[q][1][0], vs[q][0]))))

        def store(q):
            bx, bv = bxv[q]
            ah = x[q][:, :LANES]
            rh = rt[q] + bx[:, :LANES]
            u0_scr[q] = x[q][:, LANES:]
            y0_scr[q] = bx[:, LANES:] + bv
            pct = pc_rows[q].T
            for c in range(nc):
                rows = slice(c * c_sz, (c + 1) * c_sz)
                l1 = _split2(jnp.concatenate([ah[rows], rh[rows]], axis=0))
                l1hi_scr[q, c], l1lo_scr[q, c] = l1
                l2hi_scr[q, c] = jnp.concatenate([bbt[q][0][rows], kbt[q][0][rows]], axis=0)
                l2lo_scr[q, c] = jnp.concatenate([bbt[q][1][rows], kbt[q][1][rows]], axis=0)
                pcol_scr[q, c] = jnp.broadcast_to(pct[:, c * c_sz:c * c_sz + 1], (LANES, LANES))
        yield from halves(store)

    yield from stages()


def _rwkv_sequential_steps(v_ref, y_ref, st, l1hi_scr, l1lo_scr, l2hi_scr, l2lo_scr, pcol_scr, u0_scr, y0_scr,
                           *, tb):
    c_sz = RWKV_CHUNK
    r128 = lax.broadcasted_iota(I32, (LANES, LANES), 0)
    c128 = lax.broadcasted_iota(I32, (LANES, LANES), 1)
    blockdiag = (r128 < HEAD_DIM) == (c128 < HEAD_DIM)
    for c in range(tb // c_sz):
        rows = slice(c * c_sz, (c + 1) * c_sz)
        res = [_dot3((l1hi_scr[p, c], l1lo_scr[p, c]), _split2(st[p])) for p in range(4)]
        yield
        for p in range(4):
            y_ref[0, p, rows, :] = res[p][c_sz:] + y0_scr[p, rows, :]
        w2 = [_split2(jnp.concatenate([res[p][:c_sz] + u0_scr[p, rows, :], v_ref[0, p, rows, :]], axis=0))
              for p in range(4)]
        upd = [_dot_tn(l2hi_scr[p, c], w2[p][0]) + (_dot_tn(l2hi_scr[p, c], w2[p][1])
                                                     + _dot_tn(l2lo_scr[p, c], w2[p][0]))
               for p in range(4)]
        for p in range(4):
            st[p] = st[p] * pcol_scr[p, c] + jnp.where(blockdiag, upd[p], 0.0)
        yield


def _interleave(*streams):
    live = list(streams)
    while live:
        for g in list(live):
            try:
                next(g)
            except StopIteration:
                live.remove(g)


def _state_to_blockdiag(s):
    b = s.shape[0]
    st = jnp.swapaxes(s, -1, -2).reshape(b, 4, 2, HEAD_DIM, HEAD_DIM)
    z = jnp.zeros_like(st[:, :, 0])
    top = jnp.concatenate([st[:, :, 0], z], axis=-1)
    bot = jnp.concatenate([z, st[:, :, 1]], axis=-1)
    return jnp.concatenate([top, bot], axis=-2)


def _state_from_blockdiag(s_bd):
    b = s_bd.shape[0]
    a = s_bd[:, :, :HEAD_DIM, :HEAD_DIM]
    d = s_bd[:, :, HEAD_DIM:, HEAD_DIM:]
    st = jnp.stack([a, d], axis=2).reshape(b, 8, HEAD_DIM, HEAD_DIM)
    return jnp.swapaxes(st, -1, -2)


def _rwkv_post_kernel(y_ref, bonus_ref, g_ref, lg_ref, lb_ref, o_ref):
    for p in range(4):
        sl = slice(p * LANES, (p + 1) * LANES)
        y = y_ref[0, p]
        d = y - _segsum64(y) * (1.0 / HEAD_DIM)
        var = _segsum64(d * d) * (1.0 / HEAD_DIM)
        yn = d * lax.rsqrt(var + RWKV_GN_EPS) * lg_ref[:, sl] + lb_ref[:, sl]
        o_ref[0, :, sl] = ((yn + bonus_ref[0, :, sl]) * g_ref[0, :, sl]).astype(o_ref.dtype)


def _rwkv_kernel(pb_ref, pbprev_ref, shift_ref, mu_ref, w0_ref, wup_ref, a0_ref, aup_ref, gup_ref,
                 kk_ref, ka_ref, rk_ref, lg_ref, lb_ref, s0_ref, ob_ref, sT_ref,
                 tok_s, g_s, bonus_s, y_s, st_scr, *op_scr, tb):
    i = pl.program_id(1)

    @pl.when(i == 0)
    def _():
        st_scr[...] = s0_ref[0]

    tok = [tok_s.at[j] for j in range(6)]
    _rwkv_prep_kernel(pb_ref, pbprev_ref, shift_ref, mu_ref, w0_ref, wup_ref, a0_ref, aup_ref, gup_ref,
                      kk_ref, ka_ref, rk_ref, *tok, g_s, bonus_s)
    _interleave(_rwkv_operator_stages(*tok, *op_scr, tb=tb))
    st = [st_scr[p] for p in range(4)]
    _interleave(_rwkv_sequential_steps(tok[3], y_s, st, *op_scr, tb=tb))
    for p in range(4):
        st_scr[p] = st[p]
    _rwkv_post_kernel(y_s, bonus_s, g_s, lg_ref, lb_ref, ob_ref)

    @pl.when(i == pl.num_programs(1) - 1)
    def _():
        sT_ref[0] = st_scr[...]


def _rwkv(pb, shift, mu, w0, wup, a0, aup, gup, k_k, k_a, r_k, lnx_g, lnx_b, s0_bd, tb):
    b, t, c = pb.shape
    cur = lambda bi, i: (bi, i, 0)
    vec = lambda n: pl.BlockSpec((1, n), lambda bi, i: (0, 0))
    lora = pl.BlockSpec((LANES, 512), lambda bi, i: (0, 0))
    st_spec = pl.BlockSpec((1, 4, LANES, LANES), lambda bi, i: (bi, 0, 0, 0))
    return pl.pallas_call(
        functools.partial(_rwkv_kernel, tb=tb),
        grid=(b, t // tb),
        in_specs=[pl.BlockSpec((1, tb, c), cur),
                  pl.BlockSpec((1, 8, c), lambda bi, i: (bi, jnp.maximum(i * (tb // 8) - 1, 0), 0)),
                  pl.BlockSpec((1, 1, c), lambda bi, i: (bi, 0, 0)),
                  vec(c), vec(512), lora, vec(512), lora, lora, vec(512), vec(512), vec(512),
                  vec(512), vec(512), st_spec],
        out_specs=[pl.BlockSpec((1, tb, 512), cur), st_spec],
        out_shape=[jax.ShapeDtypeStruct((b, t, 512), BF16),
                   jax.ShapeDtypeStruct((b, 4, LANES, LANES), F32)],
        scratch_shapes=([pltpu.VMEM((6, 1, 4, tb, LANES), F32)] + [pltpu.VMEM((1, tb, 512), F32)] * 2
                        + [pltpu.VMEM((1, 4, tb, LANES), F32), pltpu.VMEM((4, LANES, LANES), F32)]
                        + _rwkv_operator_scratch(tb)),
        compiler_params=_cparams(("parallel", "arbitrary")),
        name="rwkv",
    )(pb, pb, shift, mu, w0, wup, a0, aup, gup, k_k, k_a, r_k, lnx_g, lnx_b, s0_bd)


def _layer_norm(x, g, b):
    mu = jnp.mean(x, -1, keepdims=True)
    d = x - mu
    var = jnp.mean(d * d, -1, keepdims=True)
    return d * lax.rsqrt(var + LN_EPS) * g + b


def _ffn_kernel(x_ref, m1_ref, m2_ref, wo_ref, g1_ref, b1_ref, wu_ref, wd_ref, g2_ref, b2_ref, o_ref,
                *, alpha, ff_chunk):
    half = m1_ref.shape[1]
    mix = _dot(m1_ref[...], wo_ref[0:half, :]) + _dot(m2_ref[...], wo_ref[half:, :])
    x1 = _layer_norm(alpha * x_ref[...] + mix, g1_ref[...], b1_ref[...])
    x1b = x1.astype(BF16)
    d_ff = wu_ref.shape[1]
    h = jnp.zeros_like(x1)
    for c in range(d_ff // ff_chunk):
        u = jnp.maximum(_dot(x1b, wu_ref[:, c * ff_chunk:(c + 1) * ff_chunk]), 0.0)
        h = h + _dot((u * u).astype(BF16), wd_ref[c * ff_chunk:(c + 1) * ff_chunk, :])
    o_ref[...] = _layer_norm(alpha * x1 + h, g2_ref[...], b2_ref[...])


def _const_spec(shape):
    return pl.BlockSpec(shape, lambda i: (0,) * len(shape), pipeline_mode=pl.Buffered(1))


def _out_ffn(x, m1, m2, wo, g1, b1, wu, wd, g2, b2, alpha, tm):
    n, d = x.shape
    dm = m1.shape[1]
    d_ff = wu.shape[1]
    tile = lambda w: pl.BlockSpec((tm, w), lambda i: (i, 0))
    return pl.pallas_call(
        functools.partial(_ffn_kernel, alpha=alpha, ff_chunk=1024),
        grid=(n // tm,),
        in_specs=[tile(d), tile(dm), tile(dm),
                  _const_spec((2 * dm, d)), _const_spec((1, d)), _const_spec((1, d)),
                  _const_spec((d, d_ff)), _const_spec((d_ff, d)), _const_spec((1, d)), _const_spec((1, d))],
        out_specs=tile(d),
        out_shape=jax.ShapeDtypeStruct((n, d), F32),
        compiler_params=_cparams(("parallel",)),
        name="out_ffn",
    )(x, m1, m2, wo, g1, b1, wu, wd, g2, b2)


def _diff_lambda(lam_ref, lam_init):
    lv = lam_ref[...]
    s1 = jnp.sum(lv[0:1] * lv[1:2], axis=-1, keepdims=True)
    s2 = jnp.sum(lv[2:3] * lv[3:4], axis=-1, keepdims=True)
    return jnp.exp(s1) - jnp.exp(s2) + lam_init


def _split_halves(qh):
    left = _lane_left(qh.shape)
    zero = jnp.zeros_like(qh)
    return jnp.concatenate([jnp.where(left, qh, zero), jnp.where(left, zero, qh)], axis=0)


def _flash_update(carry, s3, vblk):
    m, l, acc = carry
    g, rws, kb = s3.shape
    m_new = jnp.maximum(m, jnp.max(s3, -1, keepdims=True))
    m_use = jnp.where(m_new == NEG_INF, 0.0, m_new)
    corr = jnp.exp2(m - m_use)
    p = jnp.exp2(s3 - m_use)
    l = l * corr + jnp.sum(p, -1, keepdims=True)
    pv = _dot(p.reshape(g * rws, kb).astype(BF16), vblk).reshape(g, rws, vblk.shape[1])
    return m_new, l, acc * corr + pv


def _flash_init(g, rws, e):
    return (jnp.full((g, rws, 1), NEG_INF, F32), jnp.zeros((g, rws, 1), F32), jnp.zeros((g, rws, e), F32))


def _diff_finish(carry, lam, gain, lam_init):
    m, l, acc = carry
    o = acc[0] / l[0] - lam * (acc[1] / l[1])
    o = o * lax.rsqrt(jnp.mean(o * o, -1, keepdims=True) + LN_EPS) * gain * (1.0 - lam_init)
    return o


def _diff_prompt_kernel(q_ref, k_ref, vt_ref, lam_ref, gain_ref, o_ref, q2_scr, *, qb, slopes, lam_init):
    i = pl.program_id(1)
    lam = _diff_lambda(lam_ref, lam_init)
    qpos = i * qb + lax.broadcasted_iota(I32, (1, qb), 1)
    for h in range(4):
        q2_scr[2 * h * qb:(2 * h + 2) * qb, :LANES] = _split_halves(q_ref[0, :, h * LANES:(h + 1) * LANES])
        q2_scr[2 * h * qb:(2 * h + 2) * qb, LANES:] = _alibi_q_features(2 * qb, slopes[h])

    def body(kb, carry, diagonal):
        m, l, acc = carry
        rows = pl.ds(pl.multiple_of(kb * qb, qb), qb)
        kpos = kb * qb + lax.broadcasted_iota(I32, (qb, 1), 0)
        kfeat = _alibi_k_features(kpos)
        st = [_dot_nt(jnp.concatenate([k_ref[0, rows, h * LANES:(h + 1) * LANES], kfeat], axis=1),
                      q2_scr[2 * h * qb:(2 * h + 2) * qb, :])
              for h in range(4)]
        part = lambda g: st[g // 2][:, (g % 2) * qb:(g % 2 + 1) * qb]
        if diagonal:
            ahead = jnp.maximum(kpos - qpos, 0).astype(F32)
            neg = jnp.where(kpos // CHUNK <= qpos // CHUNK, 0.0, NEG_INF)
            s3 = jnp.stack([part(g) - (2.0 * slopes[g // 2]) * ahead + neg for g in range(8)], axis=0)
        else:
            s3 = jnp.stack([part(g) for g in range(8)], axis=0)
        m_new = jnp.maximum(m, jnp.max(s3, axis=1, keepdims=True))
        corr = jnp.exp2(m - m_new)
        p = jnp.exp2(s3 - m_new)
        l = l * corr + jnp.sum(p, axis=1, keepdims=True)
        pv = [_dot(vt_ref[0, kb, h * LANES:(h + 1) * LANES, :],
                   jnp.concatenate([p[2 * h].astype(BF16), p[2 * h + 1].astype(BF16)], axis=1))
              for h in range(4)]
        acc = acc * corr + jnp.stack([pv[g // 2][:, (g % 2) * qb:(g % 2 + 1) * qb] for g in range(8)], axis=0)
        return m_new, l, acc

    init = (jnp.full((8, 1, qb), NEG_INF, F32), jnp.zeros((8, 1, qb), F32), jnp.zeros((8, LANES, qb), F32))
    carry = lax.fori_loop(0, i, functools.partial(body, diagonal=False), init)
    m, l, acc = body(i, carry, diagonal=True)
    for h in range(4):
        o = acc[2 * h] / l[2 * h] - lam * (acc[2 * h + 1] / l[2 * h + 1])
        o = o * lax.rsqrt(jnp.mean(o * o, axis=0, keepdims=True) + LN_EPS) * gain_ref[...] * (1.0 - lam_init)
        o_ref[0, :, h * LANES:(h + 1) * LANES] = o.T.astype(o_ref.dtype)


def _diff_prompt(q, k, v, lamv, gain, lam_init, qb):
    b, t, _ = q.shape
    nkb = t // qb
    vt = jnp.swapaxes(v.reshape(b, nkb, qb, 512), -1, -2)
    full = pl.BlockSpec((1, t, 512), lambda bi, i: (bi, 0, 0))
    tile = pl.BlockSpec((1, qb, 512), lambda bi, i: (bi, i, 0))
    return pl.pallas_call(
        functools.partial(_diff_prompt_kernel, qb=qb, slopes=_alibi(4), lam_init=lam_init),
        grid=(b, nkb),
        in_specs=[tile, full,
                  pl.BlockSpec((1, nkb, 512, qb), lambda bi, i: (bi, 0, 0, 0)),
                  pl.BlockSpec((4, HEAD_DIM), lambda bi, i: (0, 0)),
                  pl.BlockSpec((LANES, 1), lambda bi, i: (0, 0))],
        out_specs=tile,
        out_shape=jax.ShapeDtypeStruct((b, t, 512), BF16),
        scratch_shapes=[pltpu.VMEM((8 * qb, 2 * LANES), BF16)],
        compiler_params=_cparams(("parallel", "arbitrary")),
        name="diff_prompt",
    )(q, k, vt, lamv, gain.reshape(LANES, 1))


def _diff_sample_kernel(q_ref, kc_ref, vc_ref, kn_ref, vn_ref, lam_ref, gain_ref, o_ref,
                        m_scr, l_scr, acc_scr, *, t, kb_size, n_past, slopes, lam_init):
    j = pl.program_id(1)
    past_len = n_past * kb_size
    qpos = past_len + lax.broadcasted_iota(I32, (t, 1), 0)

    @pl.when(j == 0)
    def _():
        m_scr[...] = jnp.full(m_scr.shape, NEG_INF, F32)
        l_scr[...] = jnp.zeros(l_scr.shape, F32)
        acc_scr[...] = jnp.zeros(acc_scr.shape, F32)

    def run(kfull, v_of, kpos):
        dist = jnp.abs(qpos - kpos).astype(F32)
        nk = kfull.shape[0]
        s3 = jnp.stack(
            [_dot_nt(_split_halves(q_ref[0, :, h * LANES:(h + 1) * LANES]),
                     kfull[:, h * LANES:(h + 1) * LANES]).reshape(2, t, nk) - slopes[h] * dist[None]
             for h in range(4)], axis=0)
        m = m_scr[...]
        m_new = jnp.maximum(m, jnp.max(s3, -1, keepdims=True))
        corr = jnp.exp2(m - m_new)
        p = jnp.exp2(s3 - m_new)
        l_scr[...] = l_scr[...] * corr + jnp.sum(p, -1, keepdims=True)
        pv = jnp.stack([_dot(p[h].reshape(2 * t, nk).astype(BF16), v_of(h)).reshape(2, t, LANES)
                        for h in range(4)], axis=0)
        acc_scr[...] = acc_scr[...] * corr + pv
        m_scr[...] = m_new

    @pl.when(j < n_past)
    def _():
        kpos = j * kb_size + lax.broadcasted_iota(I32, (1, kb_size), 1)
        run(kc_ref[0].astype(BF16), lambda h: vc_ref[0, :, h * LANES:(h + 1) * LANES].astype(BF16), kpos)

    @pl.when(j == n_past)
    def _():
        kpos = past_len + lax.broadcasted_iota(I32, (1, t), 1)
        run(kn_ref[0].astype(BF16), lambda h: vn_ref[0, :, h * LANES:(h + 1) * LANES].astype(BF16), kpos)
        lam = _diff_lambda(lam_ref, lam_init)
        for h in range(4):
            sl = slice(h * LANES, (h + 1) * LANES)
            carry = (m_scr[h], l_scr[h], acc_scr[h])
            o_ref[0, :, sl] = _diff_finish(carry, lam, gain_ref[...], lam_init).astype(o_ref.dtype)


def _diff_sample(q, k_cache, v_cache, k_new, v_new, lamv, gain, lam_init, kb_size):
    b, t, _ = q.shape
    n_past = k_cache.shape[1] // kb_size
    cache = pl.BlockSpec((1, kb_size, 512), lambda bi, j: (bi, jnp.minimum(j, n_past - 1), 0))
    new = pl.BlockSpec((1, t, 512), lambda bi, j: (bi, 0, 0))
    return pl.pallas_call(
        functools.partial(_diff_sample_kernel, t=t, kb_size=kb_size, n_past=n_past,
                          slopes=_alibi(4), lam_init=lam_init),
        grid=(b, n_past + 1),
        in_specs=[new, cache, cache, new, new,
                  pl.BlockSpec((4, HEAD_DIM), lambda bi, j: (0, 0)),
                  pl.BlockSpec((1, LANES), lambda bi, j: (0, 0))],
        out_specs=new,
        out_shape=jax.ShapeDtypeStruct((b, t, 512), BF16),
        scratch_shapes=[pltpu.VMEM((4, 2, t, 1), F32), pltpu.VMEM((4, 2, t, 1), F32),
                        pltpu.VMEM((4, 2, t, LANES), F32)],
        compiler_params=_cparams(("parallel", "arbitrary")),
        name="diff_sample",
    )(q, k_cache, v_cache, k_new, v_new, lamv, gain)


def _index_keys(qi, wi, ki_blk, allowed):
    acc = None
    for h in range(8):
        sc = _dot_nt(qi[:, h * HEAD_DIM:(h + 1) * HEAD_DIM], ki_blk)
        term = jnp.maximum(sc, 0.0) * wi[:, h:h + 1]
        acc = term if acc is None else acc + term
    idx = acc * (8.0 ** -0.5) + 0.0
    bits = pltpu.bitcast(idx, I32)
    key = jnp.where(bits < 0, bits ^ 0x7FFFFFFF, bits)
    return jnp.where(allowed, key, INT_MIN)


def _lane_fold(x):
    acc = x[:, :LANES]
    for j in range(1, x.shape[1] // LANES):
        acc = acc + x[:, j * LANES:(j + 1) * LANES]
    return acc


def _ones_where(pred):
    return jnp.where(pred, 1, 0).astype(I32)


def _kth_largest(count_ge, shape, n_sel):
    def body(it, lo):
        bit = 31 - it
        cand = lo + jnp.left_shift(jnp.int32(1), bit)
        return jnp.where(count_ge(cand) >= n_sel, cand, lo)
    return lax.fori_loop(0, 32, body, jnp.full(shape, INT_MIN, I32))


def _tie_position(count_eq_le, need, shape, nbits):
    def body(it, lo):
        bit = nbits - 1 - it
        cand = lo + jnp.left_shift(jnp.int32(1), bit)
        return jnp.where(count_eq_le(cand - 1) < need, cand, lo)
    return lax.fori_loop(0, nbits, body, jnp.zeros(shape, I32))


def _select_threshold(count_over, shape, n_sel, pos_bits):
    thr = _kth_largest(lambda cand: count_over(lambda key, kpos: key >= cand), shape, n_sel)
    need = n_sel - count_over(lambda key, kpos: key > thr)
    n_eq = count_over(lambda key, kpos: key == thr)
    has_tie = jnp.max(jnp.where((n_eq > need) & (thr > INT_MIN), 1, 0)) > 0
    pos_thr = lax.cond(
        has_tie,
        lambda: _tie_position(
            lambda p: count_over(lambda key, kpos: (key == thr) & (kpos <= p)), need, shape, pos_bits),
        lambda: jnp.full(shape, 2 ** 30, I32))
    return thr, pos_thr


def _dsa_attend(carry, q_ref, kblk, vblk, dist, sel, slopes):
    neg = jnp.where(sel, 0.0, NEG_INF)
    qs = []
    for s in range(4):
        slab = q_ref[0, :, s * LANES:(s + 1) * LANES]
        left = _lane_left(slab.shape)
        zero = jnp.zeros_like(slab)
        qs.append((jnp.where(left, slab, zero), jnp.where(left, zero, slab)))
    q_all = jnp.concatenate([qs[h % 4][h // 4] for h in range(8)], axis=0)
    r, kb = dist.shape
    s3 = _dot_nt(q_all, kblk).reshape(8, r, kb)
    s3 = jnp.stack([s3[h] - slopes[h] * dist + neg for h in range(8)], axis=0)
    return _flash_update(carry, s3, vblk)


def _dsa_finish(carry, o_ref):
    m, l, acc = carry
    o = acc / l
    for s in range(4):
        left = _lane_left(o[s].shape)
        o_ref[0, :, s * LANES:(s + 1) * LANES] = jnp.where(left, o[s], o[4 + s]).astype(o_ref.dtype)


def _dsa_prompt_kernel(q_ref, qi_ref, wi_ref, k_ref, vt_ref, ki_ref, o_ref, key_scr, qall_scr, qiall_scr,
                       *, qb, n_sel, slopes):
    i = pl.program_id(1)
    nkb = i + 1
    qpos = i * qb + lax.broadcasted_iota(I32, (1, qb), 1)

    for s in range(4):
        slab = q_ref[0, :, s * LANES:(s + 1) * LANES]
        left = _lane_left(slab.shape)
        zero = jnp.zeros_like(slab)
        qall_scr[s * qb:(s + 1) * qb, :LANES] = jnp.where(left, slab, zero)
        qall_scr[(4 + s) * qb:(5 + s) * qb, :LANES] = jnp.where(left, zero, slab)
    for h in range(8):
        qall_scr[h * qb:(h + 1) * qb, LANES:] = _alibi_q_features(qb, slopes[h])
        qiall_scr[h * qb:(h + 1) * qb, :] = qi_ref[0, :, h * HEAD_DIM:(h + 1) * HEAD_DIM]
    wit = wi_ref[0].T

    def kpos_of(kb):
        return kb * qb + lax.broadcasted_iota(I32, (qb, 1), 0)

    def rows_of(kb):
        return pl.ds(pl.multiple_of(kb * qb, qb), qb)

    def fill(kb, c):
        allowed = kpos_of(kb) // CHUNK <= qpos // CHUNK
        sc = _dot_nt(ki_ref[0, rows_of(kb), :], qiall_scr[...])
        acc = None
        for h in range(8):
            term = jnp.maximum(sc[:, h * qb:(h + 1) * qb], 0.0) * wit[h:h + 1, :]
            acc = term if acc is None else acc + term
        idx = acc * (8.0 ** -0.5) + 0.0
        bits = pltpu.bitcast(idx, I32)
        key = jnp.where(bits < 0, bits ^ 0x7FFFFFFF, bits)
        key_scr[kb] = jnp.where(allowed, key, INT_MIN)
        return c
    lax.fori_loop(0, nkb, fill, 0)

    def count_over(pred_fn):
        def body(kb, c):
            ones = _ones_where(pred_fn(key_scr[kb], kpos_of(kb)))
            return c + jnp.sum(ones.reshape(qb // 8, 8, qb), axis=0)
        part = lax.fori_loop(0, nkb, body, jnp.zeros((8, qb), I32))
        return jnp.sum(part, axis=0, keepdims=True)

    thr, pos_thr = _select_threshold(count_over, (1, qb), n_sel, 12)

    def attend(kb, carry, diagonal):
        m, l, acc = carry
        kpos = kpos_of(kb)
        key = key_scr[kb]
        sel = (key > thr) | ((key == thr) & (kpos <= pos_thr))
        if diagonal:
            sel = sel & (kpos // CHUNK <= qpos // CHUNK)
        neg = jnp.where(sel, 0.0, NEG_INF)
        k_aug = jnp.concatenate([k_ref[0, rows_of(kb), :], _alibi_k_features(kpos)], axis=1)
        st = _dot_nt(k_aug, qall_scr[...])
        if diagonal:
            ahead = jnp.maximum(kpos - qpos, 0).astype(F32)
            s3 = jnp.stack([st[:, h * qb:(h + 1) * qb] - (2.0 * slopes[h]) * ahead + neg for h in range(8)],
                           axis=0)
        else:
            s3 = jnp.stack([st[:, h * qb:(h + 1) * qb] + neg for h in range(8)], axis=0)
        m_new = jnp.maximum(m, jnp.max(s3, axis=1, keepdims=True))
        m_use = jnp.where(m_new == NEG_INF, 0.0, m_new)
        corr = jnp.exp2(m - m_use)
        p = jnp.exp2(s3 - m_use)
        l = l * corr + jnp.sum(p, axis=1, keepdims=True)
        p_all = jnp.concatenate([p[h].astype(BF16) for h in range(8)], axis=1)
        pv = _dot(vt_ref[0, kb], p_all)
        acc = acc * corr + jnp.stack([pv[:, h * qb:(h + 1) * qb] for h in range(8)], axis=0)
        return m_new, l, acc

    init = (jnp.full((8, 1, qb), NEG_INF, F32), jnp.zeros((8, 1, qb), F32), jnp.zeros((8, LANES, qb), F32))
    carry = lax.fori_loop(0, i, functools.partial(attend, diagonal=False), init)
    m, l, acc = attend(i, carry, diagonal=True)
    ot = acc / l
    top = lax.broadcasted_iota(I32, (LANES, qb), 0) < HEAD_DIM
    for s in range(4):
        o_ref[0, :, s * LANES:(s + 1) * LANES] = jnp.where(top, ot[s], ot[4 + s]).T.astype(o_ref.dtype)


def _dsa_prompt(q, qi, wi, k, v, ki, qb):
    b, t, _ = q.shape
    n_sel = min(TOPK_MAX, t // 4)
    nkb = t // qb
    vt = jnp.swapaxes(v.reshape(b, nkb, qb, LANES), -1, -2)
    tile = lambda w: pl.BlockSpec((1, qb, w), lambda bi, i: (bi, i, 0))
    full = lambda w: pl.BlockSpec((1, t, w), lambda bi, i: (bi, 0, 0))
    return pl.pallas_call(
        functools.partial(_dsa_prompt_kernel, qb=qb, n_sel=n_sel, slopes=_alibi(8)),
        grid=(b, nkb),
        in_specs=[tile(512), tile(512), tile(LANES), full(LANES),
                  pl.BlockSpec((1, nkb, LANES, qb), lambda bi, i: (bi, 0, 0, 0)), full(HEAD_DIM)],
        out_specs=tile(512),
        out_shape=jax.ShapeDtypeStruct((b, t, 512), BF16),
        scratch_shapes=[pltpu.VMEM((nkb, qb, qb), I32), pltpu.VMEM((8 * qb, 2 * LANES), BF16),
                        pltpu.VMEM((8 * qb, HEAD_DIM), BF16)],
        compiler_params=_cparams(("parallel", "arbitrary")),
        name="dsa_prompt",
    )(q, qi, wi, k, vt, ki)


def _dsa_sample_kernel(q_ref, qi_ref, wi_ref, kc_ref, vc_ref, kic_ref, kn_ref, vn_ref, kin_ref, o_ref,
                       key_scr, keyn_scr, *, t, kb_size, n_past, n_sel, slopes):
    past_len = n_past * kb_size
    qpos = past_len + lax.broadcasted_iota(I32, (t, 1), 0)
    qi = qi_ref[0]
    wi = wi_ref[0]
    kpos_n = past_len + lax.broadcasted_iota(I32, (1, t), 1)

    def kpos_of(kb):
        return kb * kb_size + lax.broadcasted_iota(I32, (1, kb_size), 1)

    for kb in range(n_past):
        ki_blk = kic_ref[0, kb * kb_size:(kb + 1) * kb_size, :].astype(BF16)
        key_scr[kb] = _index_keys(qi, wi, ki_blk, True)
    keyn_scr[...] = _index_keys(qi, wi, kin_ref[0].astype(BF16), True)

    def count_over(pred_fn):
        part = _lane_fold(_ones_where(pred_fn(key_scr[0], kpos_of(0))))
        for kb in range(1, n_past):
            part = part + _lane_fold(_ones_where(pred_fn(key_scr[kb], kpos_of(kb))))
        new = jnp.sum(_ones_where(pred_fn(keyn_scr[...], kpos_n)), axis=-1, keepdims=True)
        return jnp.sum(part, axis=-1, keepdims=True) + new

    thr, pos_thr = _select_threshold(count_over, (t, 1), n_sel, 13)
    carry = _flash_init(8, t, LANES)

    def sel_of(key, kpos):
        return (key > thr) | ((key == thr) & (kpos <= pos_thr))

    for kb in range(n_past):
        kpos = kpos_of(kb)
        dist = jnp.abs(qpos - kpos).astype(F32)
        kblk = kc_ref[0, kb * kb_size:(kb + 1) * kb_size, :].astype(BF16)
        vblk = vc_ref[0, kb * kb_size:(kb + 1) * kb_size, :].astype(BF16)
        carry = _dsa_attend(carry, q_ref, kblk, vblk, dist, sel_of(key_scr[kb], kpos), slopes)
    dist = jnp.abs(qpos - kpos_n).astype(F32)
    carry = _dsa_attend(carry, q_ref, kn_ref[0].astype(BF16), vn_ref[0].astype(BF16), dist,
                        sel_of(keyn_scr[...], kpos_n), slopes)
    _dsa_finish(carry, o_ref)


def _dsa_sample(q, qi, wi, k_cache, v_cache, ki_cache, k_new, v_new, ki_new, kb_size):
    b, t, _ = q.shape
    past = k_cache.shape[1]
    n_past = past // kb_size
    n_sel = min(TOPK_MAX, (past + t) // 4)
    per_b = lambda rows, w: pl.BlockSpec((1, rows, w), lambda bi: (bi, 0, 0))
    return pl.pallas_call(
        functools.partial(_dsa_sample_kernel, t=t, kb_size=kb_size, n_past=n_past, n_sel=n_sel,
                          slopes=_alibi(8)),
        grid=(b,),
        in_specs=[per_b(t, 512), per_b(t, 512), per_b(t, LANES),
                  per_b(past, LANES), per_b(past, LANES), per_b(past, HEAD_DIM),
                  per_b(t, LANES), per_b(t, LANES), per_b(t, HEAD_DIM)],
        out_specs=per_b(t, 512),
        out_shape=jax.ShapeDtypeStruct((b, t, 512), BF16),
        scratch_shapes=[pltpu.VMEM((n_past, t, kb_size), I32), pltpu.VMEM((t, t), I32)],
        compiler_params=_cparams(("parallel",)),
        name="dsa_sample",
    )(q, qi, wi, k_cache, v_cache, ki_cache, k_new, v_new, ki_new)


def _pad_cols(w, total):
    return jnp.pad(w, ((0, 0), (0, total - w.shape[1])))


def _q_perm():
    idx = []
    for s in range(4):
        idx.extend(range(s * HEAD_DIM, (s + 1) * HEAD_DIM))
        idx.extend(range((4 + s) * HEAD_DIM, (5 + s) * HEAD_DIM))
    return np.asarray(idx, np.int32)


def _even_layer(xp, xs, cache_k, cache_v, shift_s, wkv_s, w_in, sink, rw, w_out, ffn, alpha, sizes):
    (mu, w0, w_up, a0, a_up, g_up, k_k, k_a, r_k, lnx_g, lnx_b) = rw
    perm = _q_perm()
    w_in_p = jnp.concatenate([w_in[:, :512][:, perm], w_in[:, 512:]], axis=1).astype(BF16)
    w_out_p = jnp.concatenate([w_out[:512][perm], w_out[512:]], axis=0).astype(BF16)
    groups = ((0, 512, ((0, 512, HEAD_DIM ** -0.5 * LOG2E),)),
              (512, 128, ((1, 128, 1.0), (2, 128, 1.0))),
              (640, 128, ((3, 128, 1.0), (4, 128, 1.0))),
              (768, 1792, ((5, 1792, 1.0),)))
    out_defs = ((512, BF16), (128, F32), (128, BF16), (128, F32), (128, BF16), (1792, F32))
    wup_p = jnp.concatenate([w_up, jnp.zeros_like(w_up)], axis=0).astype(BF16)
    aup_p = jnp.concatenate([jnp.zeros_like(a_up), a_up], axis=0).astype(BF16)
    gup = g_up.astype(BF16)
    row = lambda v: v.reshape(1, -1)
    g1, b1, g2, b2, wu, wd = ffn

    def run(x, state, tm, qb, scan_tb):
        b, t, d = x.shape
        n = b * t
        qa, ka, ka_bf, va, va_bf, pb = _project(x.reshape(n, d), w_in_p, groups, out_defs, tm)
        r3 = lambda a: a.reshape(b, t, a.shape[-1])
        if state is None:
            oa = _swa_prompt(r3(qa), r3(ka_bf), r3(va_bf), sink, qb)
            k_buf = r3(ka)[:, -WINDOW:]
            v_buf = r3(va)[:, -WINDOW:]
            shift = jnp.zeros((b, 1, pb.shape[-1]), F32)
            s0 = jnp.zeros((b, 8, HEAD_DIM, HEAD_DIM), F32)
        else:
            ck, cv, shift, s0 = state
            ck2 = ck.reshape(b, WINDOW, LANES)
            cv2 = cv.reshape(b, WINDOW, LANES)
            oa = _swa_step(r3(qa), ck2, r3(ka_bf), cv2, r3(va_bf), sink, qb=qb, pos_base=sizes["past"])
            k_buf = jnp.concatenate([ck2, r3(ka)], axis=1)[:, -WINDOW:]
            v_buf = jnp.concatenate([cv2, r3(va)], axis=1)[:, -WINDOW:]
        pb3 = r3(pb)
        ob, s_bd = _rwkv(pb3, shift, row(mu), row(w0), wup_p, row(a0), aup_p, gup, row(k_k), row(k_a),
                         row(r_k), row(lnx_g), row(lnx_b), _state_to_blockdiag(s0), scan_tb)
        s_t = _state_from_blockdiag(s_bd)
        xo = _out_ffn(x.reshape(n, d), oa.reshape(n, 512), ob.reshape(n, 512), w_out_p, row(g1), row(b1),
                      wu.astype(BF16), wd.astype(BF16), row(g2), row(b2), alpha, tm)
        st = (k_buf.reshape(b, WINDOW, 2, HEAD_DIM), v_buf.reshape(b, WINDOW, 2, HEAD_DIM),
              pb3[:, -1:], s_t)
        return xo.reshape(b, t, d), st

    xp2, st_p = run(xp, None, sizes["tm_p"], sizes["swa_qb"], sizes["scan_tb_p"])
    xs2, st_s = run(xs, (cache_k, cache_v, shift_s, wkv_s), sizes["tm_s"], xs.shape[1], xs.shape[1])
    return xp2, xs2, st_p, st_s


def _odd_layer(xp, xs, caches, w_in, lam_vecs, subln_g, w_out, ffn, alpha, lam_init, sizes):
    perm = _q_perm()
    scale = HEAD_DIM ** -0.5 * LOG2E
    w_p = jnp.concatenate([
        w_in[:, :1536], w_in[:, 1536:2048][:, perm], w_in[:, 2048:2816],
        _pad_cols(w_in[:, 2816:2880], LANES), _pad_cols(w_in[:, 2880:2888], LANES)], axis=1).astype(BF16)
    w_out_p = jnp.concatenate([w_out[:512], w_out[512:][perm]], axis=0).astype(BF16)
    groups = ((0, 512, ((0, 512, scale),)),
              (512, 512, ((1, 512, 1.0), (2, 512, 1.0))),
              (1024, 512, ((3, 512, 1.0), (4, 512, 1.0))),
              (1536, 512, ((5, 512, scale),)),
              (2048, 128, ((6, 128, 1.0), (7, 128, 1.0))),
              (2176, 128, ((8, 128, 1.0), (9, 128, 1.0))),
              (2304, 512, ((10, 512, HEAD_DIM ** -0.5),)),
              (2816, 128, ((11, 64, 1.0), (12, 64, 1.0))),
              (2944, 128, ((13, 128, 1.0),)))
    out_defs = ((512, BF16), (512, F32), (512, BF16), (512, F32), (512, BF16), (512, BF16),
                (128, F32), (128, BF16), (128, F32), (128, BF16), (512, BF16),
                (64, F32), (64, BF16), (128, F32))
    row = lambda v: v.reshape(1, -1)
    g1, b1, g2, b2, wu, wd = ffn
    lamv = jnp.stack(lam_vecs, axis=0)
    gain = row(subln_g)

    def run(x, state, tm):
        b, t, d = x.shape
        n = b * t
        (qc, kc, kc_bf, vc, vc_bf, qd, kd, kd_bf, vd, vd_bf, qi, ki, ki_bf, wi) = _project(
            x.reshape(n, d), w_p, groups, out_defs, tm)
        r3 = lambda a: a.reshape(b, t, a.shape[-1])
        if state is None:
            oc = _diff_prompt(r3(qc), r3(kc_bf), r3(vc_bf), lamv, gain, lam_init, sizes["diff_qb"])
            od = _dsa_prompt(r3(qd), r3(qi), r3(wi), r3(kd_bf), r3(vd_bf), r3(ki_bf), sizes["dsa_qb"])
        else:
            c_k, c_v, d_k, d_v, d_ki = state
            past = c_k.shape[1]
            oc = _diff_sample(r3(qc), c_k.reshape(b, past, 512), c_v.reshape(b, past, 512),
                              r3(kc_bf), r3(vc_bf), lamv, gain, lam_init, sizes["diff_kb_s"])
            od = _dsa_sample(r3(qd), r3(qi), r3(wi), d_k.reshape(b, past, LANES), d_v.reshape(b, past, LANES),
                             d_ki, r3(kd_bf), r3(vd_bf), r3(ki_bf), sizes["dsa_kb_s"])
        xo = _out_ffn(x.reshape(n, d), oc.reshape(n, 512), od.reshape(n, 512), w_out_p, row(g1), row(b1),
                      wu.astype(BF16), wd.astype(BF16), row(g2), row(b2), alpha, tm)
        rows = (kc.reshape(b, t, 4, 2, HEAD_DIM), vc.reshape(b, t, 4, 2 * HEAD_DIM),
                kd.reshape(b, t, 2, HEAD_DIM), vd.reshape(b, t, 2, HEAD_DIM), ki.reshape(b, t, HEAD_DIM))
        return xo.reshape(b, t, d), rows

    xp2, st_p = run(xp, None, sizes["tm_p"])
    xs2, st_s = run(xs, caches, sizes["tm_s"])
    return xp2, xs2, st_p, st_s


def _sizes(xp, xs, past):
    return dict(tm_p=min(512, xp.shape[0] * xp.shape[1]), tm_s=min(512, xs.shape[0] * xs.shape[1]),
                swa_qb=min(256, xp.shape[1]), scan_tb_p=min(128, xp.shape[1]),
                diff_qb=min(256, xp.shape[1]),
                dsa_qb=min(256, xp.shape[1]), diff_kb_s=min(1024, past), dsa_kb_s=min(512, past),
                past=past)


def kernel(x_prompt, x_sample, cache_a_k, cache_a_v, state_b_shift, state_b_wkv, cache_c_k, cache_c_v, cache_d_k, cache_d_v, cache_d_kidx, w_in_even, sink_a, mu_b, w0_b, w_up_b, a0_b, a_up_b, g_up_b, k_k_b, k_a_b, r_k_b, lnx_g_b, lnx_b_b, w_out_even, w_in_odd, lam_q1_c, lam_k1_c, lam_q2_c, lam_k2_c, subln_g_c, w_out_odd, ln_mix_g, ln_mix_b, ln_ffn_g, ln_ffn_b, w_ff_up, w_ff_down):
    depth = ln_mix_g.shape[0]
    alpha = (2 * depth) ** 0.25
    past = cache_c_k.shape[2]
    sizes = _sizes(x_prompt, x_sample, past)
    xp, xs = x_prompt, x_sample
    even_p, even_s, odd_p, odd_s = [], [], [], []
    for layer in range(depth):
        i = layer // 2
        ffn = (ln_mix_g[layer], ln_mix_b[layer], ln_ffn_g[layer], ln_ffn_b[layer],
               w_ff_up[layer], w_ff_down[layer])
        if layer % 2 == 0:
            rw = (mu_b[i], w0_b[i], w_up_b[i], a0_b[i], a_up_b[i], g_up_b[i], k_k_b[i], k_a_b[i],
                  r_k_b[i].reshape(-1), lnx_g_b[i], lnx_b_b[i])
            xp, xs, st_p, st_s = _even_layer(
                xp, xs, cache_a_k[i], cache_a_v[i], state_b_shift[i], state_b_wkv[i],
                w_in_even[i], sink_a[i], rw, w_out_even[i], ffn, alpha, sizes)
            even_p.append(st_p)
            even_s.append(st_s)
        else:
            lam_init = 0.8 - 0.6 * math.exp(-0.3 * layer)
            xp, xs, st_p, st_s = _odd_layer(
                xp, xs, (cache_c_k[i], cache_c_v[i], cache_d_k[i], cache_d_v[i], cache_d_kidx[i]),
                w_in_odd[i], (lam_q1_c[i], lam_k1_c[i], lam_q2_c[i], lam_k2_c[i]), subln_g_c[i],
                w_out_odd[i], ffn, alpha, lam_init, sizes)
            odd_p.append(st_p)
            odd_s.append(st_s)
    stack = lambda states: [jnp.stack(z, axis=0) for z in zip(*states)]
    a_k_p, a_v_p, b_shift_p, b_wkv_p = stack(even_p)
    a_k_s, a_v_s, b_shift_s, b_wkv_s = stack(even_s)
    c_k_p, c_v_p, d_k_p, d_v_p, d_kidx_p = stack(odd_p)
    c_k_s, c_v_s, d_k_s, d_v_s, d_kidx_s = stack(odd_s)
    return (xp, xs, a_k_p, a_v_p, b_shift_p, b_wkv_p, c_k_p, c_v_p, d_k_p, d_v_p, d_kidx_p,
            a_k_s, a_v_s, b_shift_s, b_wkv_s, c_k_s, c_v_s, d_k_s, d_v_s, d_kidx_s)
```

```python
import functools
import math

import numpy as np
import jax
import jax.numpy as jnp
from jax import lax
from jax.experimental import pallas as pl
from jax.experimental.pallas import tpu as pltpu

F32 = jnp.float32
BF16 = jnp.bfloat16
I32 = jnp.int32

LANES = 128
HEAD_DIM = 64
CHUNK = 64
LN_EPS = 1e-5
RWKV_GN_EPS = 64e-5
WINDOW = 128
TOPK_MAX = 256
NEG_INF = float("-inf")
INT_MIN = -(2 ** 31)
VMEM_LIMIT = 56 * 1024 * 1024


def _cparams(sem):
    return pltpu.CompilerParams(dimension_semantics=sem, vmem_limit_bytes=VMEM_LIMIT)


LOG2E = math.log2(math.e)


def _alibi(n):
    return [float(2.0 ** (-8.0 * (i + 1) / n)) * LOG2E for i in range(n)]


def _bf16_parts(c):
    parts, rem = [], np.float64(c)
    for _ in range(3):
        piece = np.float64(np.asarray(rem, np.float32).astype(BF16).astype(np.float32))
        parts.append(float(piece))
        rem = rem - piece
    return parts


def _alibi_q_features(rows, slope):
    parts = _bf16_parts(slope)
    lane = lax.broadcasted_iota(I32, (rows, LANES), 1)
    out = jnp.zeros((rows, LANES), F32)
    for j, v in enumerate([16.0 * p for p in parts] + parts):
        out = jnp.where(lane == j, v, out)
    return out.astype(BF16)


def _alibi_k_features(kpos):
    lane = lax.broadcasted_iota(I32, (kpos.shape[0], LANES), 1)
    hi = (kpos // 16).astype(F32)
    lo = (kpos % 16).astype(F32)
    return jnp.where(lane < 3, hi, jnp.where(lane < 6, lo, 0.0)).astype(BF16)


def _dot(a, b):
    return jnp.dot(a, b, preferred_element_type=F32)


def _dot_nt(a, b):
    return lax.dot_general(a, b, (((1,), (1,)), ((), ())), preferred_element_type=F32)


def _dot_tn(a, b):
    return lax.dot_general(a, b, (((0,), (0,)), ((), ())), preferred_element_type=F32)


def _lane_left(shape):
    return lax.broadcasted_iota(I32, shape, len(shape) - 1) < HEAD_DIM


def _segsum64(x):
    left = _lane_left(x.shape)
    sl = jnp.sum(jnp.where(left, x, 0.0), axis=-1, keepdims=True)
    sr = jnp.sum(jnp.where(left, 0.0, x), axis=-1, keepdims=True)
    return jnp.where(left, sl, sr)


def _proj_kernel(x_ref, w_ref, *out_refs, groups):
    xb = x_ref[...].astype(BF16)
    for off, width, outs in groups:
        r = _dot(xb, w_ref[:, off:off + width])
        for idx, ow, scale in outs:
            v = r if ow == width else r[:, :ow]
            if scale != 1.0:
                v = v * scale
            out_refs[idx][...] = v.astype(out_refs[idx].dtype)


def _project(x, w_bf, groups, out_defs, tm):
    n, d = x.shape
    cols = w_bf.shape[1]
    out_shape = [jax.ShapeDtypeStruct((n, ow), dt) for ow, dt in out_defs]
    out_specs = [pl.BlockSpec((tm, ow), lambda i: (i, 0)) for ow, _ in out_defs]
    return pl.pallas_call(
        functools.partial(_proj_kernel, groups=groups),
        grid=(n // tm,),
        in_specs=[pl.BlockSpec((tm, d), lambda i: (i, 0)),
                  pl.BlockSpec((d, cols), lambda i: (0, 0))],
        out_specs=out_specs,
        out_shape=out_shape,
        compiler_params=_cparams(("parallel",)),
        name="in_proj",
    )(x, w_bf)


def _swa_kernel(sink_ref, q_ref, kp_ref, kc_ref, vp_ref, vc_ref, o_ref, *, qb, pos_base, slopes):
    i = pl.program_id(1)
    q0 = pos_base + i * qb
    kp = kp_ref[0].astype(BF16)
    kc = kc_ref[0].astype(BF16)
    vp = vp_ref[0].astype(BF16)
    vc = vc_ref[0].astype(BF16)
    qpos = q0 + lax.broadcasted_iota(I32, (qb, 1), 0)
    kpos_p = q0 - WINDOW + lax.broadcasted_iota(I32, (1, WINDOW), 1)
    kpos_c = q0 + lax.broadcasted_iota(I32, (1, qb), 1)
    qch = qpos // CHUNK

    def allowed(kpos):
        kch = kpos // CHUNK
        return (kpos >= 0) & (kch >= qch - WINDOW // CHUNK) & (kch <= qch)

    al_p, al_c = allowed(kpos_p), allowed(kpos_c)
    dist_p = jnp.abs(qpos - kpos_p).astype(F32)
    dist_c = jnp.abs(qpos - kpos_c).astype(F32)
    for s in range(4):
        slab = q_ref[0, :, s * LANES:(s + 1) * LANES]
        left = _lane_left(slab.shape)
        res = []
        for side in range(2):
            h = 4 * side + s
            qx = jnp.where(left if side == 0 else jnp.logical_not(left), slab, jnp.zeros_like(slab))
            lp = jnp.where(al_p, _dot_nt(qx, kp) - slopes[h] * dist_p, NEG_INF)
            lc = jnp.where(al_c, _dot_nt(qx, kc) - slopes[h] * dist_c, NEG_INF)
            sk = sink_ref[h] * LOG2E
            m = jnp.maximum(jnp.maximum(jnp.max(lp, -1, keepdims=True), jnp.max(lc, -1, keepdims=True)), sk)
            pp = jnp.exp2(lp - m)
            pc = jnp.exp2(lc - m)
            den = jnp.sum(pp, -1, keepdims=True) + jnp.sum(pc, -1, keepdims=True) + jnp.exp2(sk - m)
            o = _dot(pp.astype(BF16), vp) + _dot(pc.astype(BF16), vc)
            res.append(o / den)
        o_ref[0, :, s * LANES:(s + 1) * LANES] = jnp.where(left, res[0], res[1]).astype(o_ref.dtype)


def _swa_step(q, k_cache, k_cur, v_cache, v_cur, sink, *, qb, pos_base):
    b, t, _ = q.shape
    nq = t // qb
    prev_map = lambda bi, i: (bi, 0, 0)
    cur_map = lambda bi, i: (bi, i, 0)
    return pl.pallas_call(
        functools.partial(_swa_kernel, qb=qb, pos_base=pos_base, slopes=_alibi(8)),
        grid=(b, nq),
        in_specs=[pl.BlockSpec(memory_space=pltpu.SMEM),
                  pl.BlockSpec((1, qb, 512), cur_map),
                  pl.BlockSpec((1, WINDOW, LANES), prev_map),
                  pl.BlockSpec((1, qb, LANES), cur_map),
                  pl.BlockSpec((1, WINDOW, LANES), prev_map),
                  pl.BlockSpec((1, qb, LANES), cur_map)],
        out_specs=pl.BlockSpec((1, qb, 512), cur_map),
        out_shape=jax.ShapeDtypeStruct((b, t, 512), BF16),
        compiler_params=_cparams(("parallel", "parallel")),
        name="swa_sink",
    )(sink, q, k_cache, k_cur, v_cache, v_cur)


def _swa_prompt_kernel(sink_ref, q_ref, kp_ref, kc_ref, vtp_ref, vtc_ref, o_ref, qall_scr, *, qb, slopes):
    i = pl.program_id(1)
    q0 = i * qb
    qpos = q0 + lax.broadcasted_iota(I32, (1, qb), 1)
    for s in range(4):
        slab = q_ref[0, :, s * LANES:(s + 1) * LANES]
        left = _lane_left(slab.shape)
        zero = jnp.zeros_like(slab)
        qall_scr[s * qb:(s + 1) * qb, :LANES] = jnp.where(left, slab, zero)
        qall_scr[(4 + s) * qb:(5 + s) * qb, :LANES] = jnp.where(left, zero, slab)
    for h in range(8):
        qall_scr[h * qb:(h + 1) * qb, LANES:] = _alibi_q_features(qb, slopes[h])
    kpos_p = q0 - WINDOW + lax.broadcasted_iota(I32, (WINDOW, 1), 0)
    kpos_c = q0 + lax.broadcasted_iota(I32, (qb, 1), 0)
    qch = qpos // CHUNK
    first = qch - WINDOW // CHUNK
    neg_p = jnp.where((kpos_p >= 0) & (kpos_p // CHUNK >= first), 0.0, NEG_INF)
    kch_c = kpos_c // CHUNK
    neg_c = jnp.where((kch_c >= first) & (kch_c <= qch), 0.0, NEG_INF)
    ahead = jnp.maximum(kpos_c - qpos, 0).astype(F32)
    qall = qall_scr[...]
    st_p = _dot_nt(jnp.concatenate([kp_ref[0], _alibi_k_features(jnp.maximum(kpos_p, 0))], axis=1), qall)
    st_c = _dot_nt(jnp.concatenate([kc_ref[0], _alibi_k_features(kpos_c)], axis=1), qall)
    qposf = qpos.astype(F32)
    pps, pcs, dens = [], [], []
    for h in range(8):
        sp = st_p[:, h * qb:(h + 1) * qb] + neg_p
        sc = st_c[:, h * qb:(h + 1) * qb] - (2.0 * slopes[h]) * ahead + neg_c
        sk = sink_ref[h] * LOG2E + slopes[h] * qposf
        m = jnp.maximum(jnp.maximum(jnp.max(sp, 0, keepdims=True), jnp.max(sc, 0, keepdims=True)), sk)
        pp = jnp.exp2(sp - m)
        pc = jnp.exp2(sc - m)
        dens.append(jnp.sum(pp, 0, keepdims=True) + jnp.sum(pc, 0, keepdims=True) + jnp.exp2(sk - m))
        pps.append(pp.astype(BF16))
        pcs.append(pc.astype(BF16))
    ot = (_dot(vtp_ref[0], jnp.concatenate(pps, axis=1))
          + _dot(vtc_ref[0], jnp.concatenate(pcs, axis=1)))
    top = lax.broadcasted_iota(I32, (LANES, qb), 0) < HEAD_DIM
    for s in range(4):
        o_l = ot[:, s * qb:(s + 1) * qb] / dens[s]
        o_r = ot[:, (4 + s) * qb:(5 + s) * qb] / dens[4 + s]
        o_ref[0, :, s * LANES:(s + 1) * LANES] = jnp.where(top, o_l, o_r).T.astype(o_ref.dtype)


def _swa_prompt(q, k, v, sink, qb):
    b, t, _ = q.shape
    r = qb // WINDOW
    vt = jnp.swapaxes(v, -1, -2)
    cur_map = lambda bi, i: (bi, i, 0)
    prev_map = lambda bi, i: (bi, jnp.maximum(i * r - 1, 0), 0)
    return pl.pallas_call(
        functools.partial(_swa_prompt_kernel, qb=qb, slopes=_alibi(8)),
        grid=(b, t // qb),
        in_specs=[pl.BlockSpec(memory_space=pltpu.SMEM),
                  pl.BlockSpec((1, qb, 512), cur_map),
                  pl.BlockSpec((1, WINDOW, LANES), prev_map),
                  pl.BlockSpec((1, qb, LANES), cur_map),
                  pl.BlockSpec((1, LANES, WINDOW), lambda bi, i: (bi, 0, jnp.maximum(i * r - 1, 0))),
                  pl.BlockSpec((1, LANES, qb), lambda bi, i: (bi, 0, i))],
        out_specs=pl.BlockSpec((1, qb, 512), cur_map),
        out_shape=jax.ShapeDtypeStruct((b, t, 512), BF16),
        scratch_shapes=[pltpu.VMEM((8 * qb, 2 * LANES), BF16)],
        compiler_params=_cparams(("parallel", "parallel")),
        name="swa_prompt",
    )(sink, q, k, k, vt, vt)


def _sigmoid(x):
    return 1.0 / (1.0 + jnp.exp(-x))


def _softplus(x):
    return jnp.maximum(x, 0.0) + jnp.log(1.0 + jnp.exp(-jnp.abs(x)))


def _rwkv_prep_kernel(pb_ref, pbprev_ref, shift_ref, mu_ref, w0_ref, wup_ref, a0_ref, aup_ref, gup_ref,
                      kk_ref, ka_ref, rk_ref,
                      r_out, w_out, k_out, v_out, kk_out, b_out, g_out, bonus_out):
    i = pl.program_id(1)
    pb = pb_ref[0]
    tm = pb.shape[0]
    prev_row = jnp.where(i == 0, shift_ref[0], pbprev_ref[0, 7:8, :])
    rolled = pltpu.roll(pb, 1, axis=0)
    row = lax.broadcasted_iota(I32, (tm, 1), 0)
    prev = jnp.where(row == 0, prev_row, rolled)
    xm = pb + (prev - pb) * mu_ref[...]
    r = xm[:, 0:512]
    k = xm[:, 512:1024]
    v = xm[:, 1024:1536]
    wa = xm[:, 1536:1664]
    gl = xm[:, 1664:1792]
    lw = _dot(jnp.tanh(wa).astype(BF16), wup_ref[...])
    la = _dot(wa.astype(BF16), aup_ref[...])
    w_log = -_softplus(-(w0_ref[...] + lw)) - 0.5
    log_decay = -jnp.exp(w_log)
    a = _sigmoid(a0_ref[...] + la)
    g = _dot(_sigmoid(gl).astype(BF16), gup_ref[...])
    kk = k * kk_ref[...]
    k2 = k * (1.0 + (a - 1.0) * ka_ref[...])
    rkk = r * k2 * rk_ref[...]
    for p in range(4):
        sl = slice(p * LANES, (p + 1) * LANES)
        kkp = kk[:, sl]
        kkn = kkp * lax.rsqrt(_segsum64(kkp * kkp) + 1e-12)
        kk_out[0, p] = kkn
        b_out[0, p] = kkn * a[:, sl]
        bonus_out[0, :, sl] = _segsum64(rkk[:, sl]) * v[:, sl]
        r_out[0, p] = r[:, sl]
        w_out[0, p] = log_decay[:, sl]
        k_out[0, p] = k2[:, sl]
        v_out[0, p] = v[:, sl]
    g_out[0] = g


RWKV_CHUNK = 16


def _split2(x):
    hi = x.astype(BF16)
    return hi, (x - hi.astype(F32)).astype(BF16)


def _dot3(a, b):
    return _dot(a[0], b[0]) + (_dot(a[0], b[1]) + _dot(a[1], b[0]))


def _rwkv_operator_scratch(tb):
    nc, c2 = tb // RWKV_CHUNK, 2 * RWKV_CHUNK
    return [pltpu.VMEM((4, nc, c2, LANES), BF16), pltpu.VMEM((4, nc, c2, LANES), BF16),
            pltpu.VMEM((4, nc, c2, LANES), BF16), pltpu.VMEM((4, nc, c2, LANES), BF16),
            pltpu.VMEM((4, nc, LANES, LANES), F32),
            pltpu.VMEM((4, tb, LANES), F32), pltpu.VMEM((4, tb, LANES), F32)]


def _rwkv_operator_stages(r_ref, lw_ref, k_ref, v_ref, kk_ref, b_ref,
                          l1hi_scr, l1lo_scr, l2hi_scr, l2lo_scr, pcol_scr, u0_scr, y0_scr, *, tb):
    c_sz = RWKV_CHUNK
    nc = tb // c_sz
    row = lax.broadcasted_iota(I32, (tb, tb), 0)
    col = lax.broadcasted_iota(I32, (tb, tb), 1)
    same = (row // c_sz) == (col // c_sz)
    strict = same & (col < row)
    incl = same & (col <= row)
    cum_lhs = jnp.concatenate([jnp.where(incl, 1.0, 0.0), jnp.where(same, 1.0, 0.0)], axis=0).astype(BF16)
    left = _lane_left((tb, LANES))
    left2 = jnp.concatenate([left, left], axis=1)

    def stages():
        qs = range(4)
        sides = range(2)

        def halves(fn):
            out = [None] * 4
            for group in ((0, 1), (2, 3)):
                for q in group:
                    out[q] = fn(q)
                yield
            return out

        def split3(x):
            x1 = x.astype(BF16)
            rem = x - x1.astype(F32)
            x2 = rem.astype(BF16)
            return x1, x2, (rem - x2.astype(F32)).astype(BF16)

        def cum_of(q):
            l = split3(lw_ref[0, q])
            return _dot(cum_lhs, l[0]) + (_dot(cum_lhs, l[1]) + _dot(cum_lhs, l[2]))
        cums = yield from halves(cum_of)
        al, rt, vs, rhs, lhs, bbt, kbt, pc_rows = [], [], [], [], [], [], [], []
        for q in qs:
            r, lw, k, v, kk, b = (ref[0, q] for ref in (r_ref, lw_ref, k_ref, v_ref, kk_ref, b_ref))
            cs, cse = cums[q][:tb], cums[q][tb:]
            p_inv = jnp.exp(-cs)
            p_end = jnp.exp(cse - cs)
            al.append(-kk * jnp.exp(cs - lw))
            rt.append(r * jnp.exp(cs))
            vs.append(_split2(v))
            rhs.append(_split2(jnp.concatenate([b * p_inv, k * p_inv], axis=0)))
            zero = jnp.zeros_like(v)
            half = lambda z, s: jnp.where(left, z, zero) if s == 0 else jnp.where(left, zero, z)
            lhs.append([_split2(jnp.concatenate([half(al[q], s), half(rt[q], s)], axis=0)) for s in sides])
            bbt.append(_split2(b * p_end))
            kbt.append(_split2(k * p_end))
            pc_rows.append(jnp.exp(cse))
        gram = yield from halves(lambda q: [
            _dot_nt(lhs[q][s][0], rhs[q][0]) + (_dot_nt(lhs[q][s][0], rhs[q][1]) + _dot_nt(lhs[q][s][1], rhs[q][0]))
            for s in sides])
        a1 =[[_split2(jnp.where(strict, gram[q][s][:tb, :tb], 0.0)) for s in sides] for q in qs]
        a2 = [[_split2(jnp.where(strict, gram[q][s][:tb, tb:], 0.0)) for s in sides] for q in qs]
        b1 = [[_split2(jnp.where(incl, gram[q][s][tb:, :tb], 0.0)) for s in sides] for q in qs]
        b2 = [[_split2(jnp.where(incl, gram[q][s][tb:, tb:], 0.0)) for s in sides] for q in qs]
        xv_pw = yield from halves(lambda q: (
            jnp.where(left, _dot3(a2[q][0], vs[q]), _dot3(a2[q][1], vs[q])),
            [_dot3(a1[q][s], a1[q][s]) for s in sides]))
        x = [jnp.concatenate([al[q], xv_pw[q][0]], axis=1) for q in qs]
        pw = [xv_pw[q][1] for q in qs]
        m = a1
        for step in range(4):
            def apply(q, m=m, pw=pw, x=x, step=step):
                m_new = [_split2(pw[q][s]) for s in sides] if step < 3 else None
                pw_new = [_dot3(m_new[s], m_new[s]) for s in sides] if step < 2 else None
                xq = _split2(x[q])
                x_new = x[q] + jnp.where(left2, _dot3(m[q][0], xq), _dot3(m[q][1], xq))
                return x_new, m_new, pw_new
            res = yield from halves(apply)
            x = [res[q][0] for q in qs]
            m = [res[q][1] for q in qs]
            pw = [res[q][2] for q in qs]
        xs = [_split2(x[q]) for q in qs]
        bxv = yield from halves(lambda q: (
            jnp.where(left2, _dot(b1[q][0][0], xs[q][0]), _dot(b1[q][1][0], xs[q][0])),
            jnp.where(left, _dot(b2[q][0][0], vs[q][0]), _dot(b2[q][1][0], vs[q][0]))))

        def store(q):
            bx, bv = bxv[q]
            ah = x[q][:, :LANES]
            rh = rt[q] + bx[:, :LANES]
            u0_scr[q] = x[q][:, LANES:]
            y0_scr[q] = bx[:, LANES:] + bv
            pct = pc_rows[q].T
            for c in range(nc):
                rows = slice(c * c_sz, (c + 1) * c_sz)
                l1 = _split2(jnp.concatenate([ah[rows], rh[rows]], axis=0))
                l1hi_scr[q, c], l1lo_scr[q, c] = l1
                l2hi_scr[q, c] = jnp.concatenate([bbt[q][0][rows], kbt[q][0][rows]], axis=0)
                l2lo_scr[q, c] = jnp.concatenate([bbt[q][1][rows], kbt[q][1][rows]], axis=0)
                pcol_scr[q, c] = jnp.broadcast_to(pct[:, c * c_sz:c * c_sz + 1], (LANES, LANES))
        yield from halves(store)

    yield from stages()


def _rwkv_sequential_steps(v_ref, y_ref, st, l1hi_scr, l1lo_scr, l2hi_scr, l2lo_scr, pcol_scr, u0_scr, y0_scr,
                           *, tb):
    c_sz = RWKV_CHUNK
    r128 = lax.broadcasted_iota(I32, (LANES, LANES), 0)
    c128 = lax.broadcasted_iota(I32, (LANES, LANES), 1)
    blockdiag = (r128 < HEAD_DIM) == (c128 < HEAD_DIM)
    for c in range(tb // c_sz):
        rows = slice(c * c_sz, (c + 1) * c_sz)
        res = [_dot3((l1hi_scr[p, c], l1lo_scr[p, c]), _split2(st[p])) for p in range(4)]
        yield
        for p in range(4):
            y_ref[0, p, rows, :] = res[p][c_sz:] + y0_scr[p, rows, :]
        w2 = [_split2(jnp.concatenate([res[p][:c_sz] + u0_scr[p, rows, :], v_ref[0, p, rows, :]], axis=0))
              for p in range(4)]
        upd = [_dot_tn(l2hi_scr[p, c], w2[p][0]) + (_dot_tn(l2hi_scr[p, c], w2[p][1])
                                                     + _dot_tn(l2lo_scr[p, c], w2[p][0]))
               for p in range(4)]
        for p in range(4):
            st[p] = st[p] * pcol_scr[p, c] + jnp.where(blockdiag, upd[p], 0.0)
        yield


def _interleave(*streams):
    live = list(streams)
    while live:
        for g in list(live):
            try:
                next(g)
            except StopIteration:
                live.remove(g)


def _state_to_blockdiag(s):
    b = s.shape[0]
    st = jnp.swapaxes(s, -1, -2).reshape(b, 4, 2, HEAD_DIM, HEAD_DIM)
    z = jnp.zeros_like(st[:, :, 0])
    top = jnp.concatenate([st[:, :, 0], z], axis=-1)
    bot = jnp.concatenate([z, st[:, :, 1]], axis=-1)
    return jnp.concatenate([top, bot], axis=-2)


def _state_from_blockdiag(s_bd):
    b = s_bd.shape[0]
    a = s_bd[:, :, :HEAD_DIM, :HEAD_DIM]
    d = s_bd[:, :, HEAD_DIM:, HEAD_DIM:]
    st = jnp.stack([a, d], axis=2).reshape(b, 8, HEAD_DIM, HEAD_DIM)
    return jnp.swapaxes(st, -1, -2)


def _rwkv_post_kernel(y_ref, bonus_ref, g_ref, lg_ref, lb_ref, o_ref):
    for p in range(4):
        sl = slice(p * LANES, (p + 1) * LANES)
        y = y_ref[0, p]
        d = y - _segsum64(y) * (1.0 / HEAD_DIM)
        var = _segsum64(d * d) * (1.0 / HEAD_DIM)
        yn = d * lax.rsqrt(var + RWKV_GN_EPS) * lg_ref[:, sl] + lb_ref[:, sl]
        o_ref[0, :, sl] = ((yn + bonus_ref[0, :, sl]) * g_ref[0, :, sl]).astype(o_ref.dtype)


def _rwkv_kernel(pb_ref, pbprev_ref, shift_ref, mu_ref, w0_ref, wup_ref, a0_ref, aup_ref, gup_ref,
                 kk_ref, ka_ref, rk_ref, lg_ref, lb_ref, s0_ref, ob_ref, sT_ref,
                 tok_s, g_s, bonus_s, y_s, st_scr, *op_scr, tb):
    i = pl.program_id(1)

    @pl.when(i == 0)
    def _():
        st_scr[...] = s0_ref[0]

    tok = [tok_s.at[j] for j in range(6)]
    _rwkv_prep_kernel(pb_ref, pbprev_ref, shift_ref, mu_ref, w0_ref, wup_ref, a0_ref, aup_ref, gup_ref,
                      kk_ref, ka_ref, rk_ref, *tok, g_s, bonus_s)
    _interleave(_rwkv_operator_stages(*tok, *op_scr, tb=tb))
    st = [st_scr[p] for p in range(4)]
    _interleave(_rwkv_sequential_steps(tok[3], y_s, st, *op_scr, tb=tb))
    for p in range(4):
        st_scr[p] = st[p]
    _rwkv_post_kernel(y_s, bonus_s, g_s, lg_ref, lb_ref, ob_ref)

    @pl.when(i == pl.num_programs(1) - 1)
    def _():
        sT_ref[0] = st_scr[...]


def _rwkv(pb, shift, mu, w0, wup, a0, aup, gup, k_k, k_a, r_k, lnx_g, lnx_b, s0_bd, tb):
    b, t, c = pb.shape
    cur = lambda bi, i: (bi, i, 0)
    vec = lambda n: pl.BlockSpec((1, n), lambda bi, i: (0, 0))
    lora = pl.BlockSpec((LANES, 512), lambda bi, i: (0, 0))
    st_spec = pl.BlockSpec((1, 4, LANES, LANES), lambda bi, i: (bi, 0, 0, 0))
    return pl.pallas_call(
        functools.partial(_rwkv_kernel, tb=tb),
        grid=(b, t // tb),
        in_specs=[pl.BlockSpec((1, tb, c), cur),
                  pl.BlockSpec((1, 8, c), lambda bi, i: (bi, jnp.maximum(i * (tb // 8) - 1, 0), 0)),
                  pl.BlockSpec((1, 1, c), lambda bi, i: (bi, 0, 0)),
                  vec(c), vec(512), lora, vec(512), lora, lora, vec(512), vec(512), vec(512),
                  vec(512), vec(512), st_spec],
        out_specs=[pl.BlockSpec((1, tb, 512), cur), st_spec],
        out_shape=[jax.ShapeDtypeStruct((b, t, 512), BF16),
                   jax.ShapeDtypeStruct((b, 4, LANES, LANES), F32)],
        scratch_shapes=([pltpu.VMEM((6, 1, 4, tb, LANES), F32)] + [pltpu.VMEM((1, tb, 512), F32)] * 2
                        + [pltpu.VMEM((1, 4, tb, LANES), F32), pltpu.VMEM((4, LANES, LANES), F32)]
                        + _rwkv_operator_scratch(tb)),
        compiler_params=_cparams(("parallel", "arbitrary")),
        name="rwkv",
    )(pb, pb, shift, mu, w0, wup, a0, aup, gup, k_k, k_a, r_k, lnx_g, lnx_b, s0_bd)


def _layer_norm(x, g, b):
    mu = jnp.mean(x, -1, keepdims=True)
    d = x - mu
    var = jnp.mean(d * d, -1, keepdims=True)
    return d * lax.rsqrt(var + LN_EPS) * g + b


def _ffn_kernel(x_ref, m1_ref, m2_ref, wo_ref, g1_ref, b1_ref, wu_ref, wd_ref, g2_ref, b2_ref, o_ref,
                *, alpha, ff_chunk):
    half = m1_ref.shape[1]
    mix = _dot(m1_ref[...], wo_ref[0:half, :]) + _dot(m2_ref[...], wo_ref[half:, :])
    x1 = _layer_norm(alpha * x_ref[...] + mix, g1_ref[...], b1_ref[...])
    x1b = x1.astype(BF16)
    d_ff = wu_ref.shape[1]
    h = jnp.zeros_like(x1)
    for c in range(d_ff // ff_chunk):
        u = jnp.maximum(_dot(x1b, wu_ref[:, c * ff_chunk:(c + 1) * ff_chunk]), 0.0)
        h = h + _dot((u * u).astype(BF16), wd_ref[c * ff_chunk:(c + 1) * ff_chunk, :])
    o_ref[...] = _layer_norm(alpha * x1 + h, g2_ref[...], b2_ref[...])


def _const_spec(shape):
    return pl.BlockSpec(shape, lambda i: (0,) * len(shape), pipeline_mode=pl.Buffered(1))


def _out_ffn(x, m1, m2, wo, g1, b1, wu, wd, g2, b2, alpha, tm):
    n, d = x.shape
    dm = m1.shape[1]
    d_ff = wu.shape[1]
    tile = lambda w: pl.BlockSpec((tm, w), lambda i: (i, 0))
    return pl.pallas_call(
        functools.partial(_ffn_kernel, alpha=alpha, ff_chunk=1024),
        grid=(n // tm,),
        in_specs=[tile(d), tile(dm), tile(dm),
                  _const_spec((2 * dm, d)), _const_spec((1, d)), _const_spec((1, d)),
                  _const_spec((d, d_ff)), _const_spec((d_ff, d)), _const_spec((1, d)), _const_spec((1, d))],
        out_specs=tile(d),
        out_shape=jax.ShapeDtypeStruct((n, d), F32),
        compiler_params=_cparams(("parallel",)),
        name="out_ffn",
    )(x, m1, m2, wo, g1, b1, wu, wd, g2, b2)


def _diff_lambda(lam_ref, lam_init):
    lv = lam_ref[...]
    s1 = jnp.sum(lv[0:1] * lv[1:2], axis=-1, keepdims=True)
    s2 = jnp.sum(lv[2:3] * lv[3:4], axis=-1, keepdims=True)
    return jnp.exp(s1) - jnp.exp(s2) + lam_init


def _split_halves(qh):
    left = _lane_left(qh.shape)
    zero = jnp.zeros_like(qh)
    return jnp.concatenate([jnp.where(left, qh, zero), jnp.where(left, zero, qh)], axis=0)


def _flash_update(carry, s3, vblk):
    m, l, acc = carry
    g, rws, kb = s3.shape
    m_new = jnp.maximum(m, jnp.max(s3, -1, keepdims=True))
    m_use = jnp.where(m_new == NEG_INF, 0.0, m_new)
    corr = jnp.exp2(m - m_use)
    p = jnp.exp2(s3 - m_use)
    l = l * corr + jnp.sum(p, -1, keepdims=True)
    pv = _dot(p.reshape(g * rws, kb).astype(BF16), vblk).reshape(g, rws, vblk.shape[1])
    return m_new, l, acc * corr + pv


def _flash_init(g, rws, e):
    return (jnp.full((g, rws, 1), NEG_INF, F32), jnp.zeros((g, rws, 1), F32), jnp.zeros((g, rws, e), F32))


def _diff_finish(carry, lam, gain, lam_init):
    m, l, acc = carry
    o = acc[0] / l[0] - lam * (acc[1] / l[1])
    o = o * lax.rsqrt(jnp.mean(o * o, -1, keepdims=True) + LN_EPS) * gain * (1.0 - lam_init)
    return o


def _diff_prompt_kernel(q_ref, k_ref, vt_ref, lam_ref, gain_ref, o_ref, q2_scr, *, qb, slopes, lam_init):
    i = pl.program_id(1)
    lam = _diff_lambda(lam_ref, lam_init)
    qpos = i * qb + lax.broadcasted_iota(I32, (1, qb), 1)
    for h in range(4):
        q2_scr[2 * h * qb:(2 * h + 2) * qb, :LANES] = _split_halves(q_ref[0, :, h * LANES:(h + 1) * LANES])
        q2_scr[2 * h * qb:(2 * h + 2) * qb, LANES:] = _alibi_q_features(2 * qb, slopes[h])

    def body(kb, carry, diagonal):
        m, l, acc = carry
        rows = pl.ds(pl.multiple_of(kb * qb, qb), qb)
        kpos = kb * qb + lax.broadcasted_iota(I32, (qb, 1), 0)
        kfeat = _alibi_k_features(kpos)
        st = [_dot_nt(jnp.concatenate([k_ref[0, rows, h * LANES:(h + 1) * LANES], kfeat], axis=1),
                      q2_scr[2 * h * qb:(2 * h + 2) * qb, :])
              for h in range(4)]
        part = lambda g: st[g // 2][:, (g % 2) * qb:(g % 2 + 1) * qb]
        if diagonal:
            ahead = jnp.maximum(kpos - qpos, 0).astype(F32)
            neg = jnp.where(kpos // CHUNK <= qpos // CHUNK, 0.0, NEG_INF)
            s3 = jnp.stack([part(g) - (2.0 * slopes[g // 2]) * ahead + neg for g in range(8)], axis=0)
        else:
            s3 = jnp.stack([part(g) for g in range(8)], axis=0)
        m_new = jnp.maximum(m, jnp.max(s3, axis=1, keepdims=True))
        corr = jnp.exp2(m - m_new)
        p = jnp.exp2(s3 - m_new)
        l = l * corr + jnp.sum(p, axis=1, keepdims=True)
        pv = [_dot(vt_ref[0, kb, h * LANES:(h + 1) * LANES, :],
                   jnp.concatenate([p[2 * h].astype(BF16), p[2 * h + 1].astype(BF16)], axis=1))
              for h in range(4)]
        acc = acc * corr + jnp.stack([pv[g // 2][:, (g % 2) * qb:(g % 2 + 1) * qb] for g in range(8)], axis=0)
        return m_new, l, acc

    init = (jnp.full((8, 1, qb), NEG_INF, F32), jnp.zeros((8, 1, qb), F32), jnp.zeros((8, LANES, qb), F32))
    carry = lax.fori_loop(0, i, functools.partial(body, diagonal=False), init)
    m, l, acc = body(i, carry, diagonal=True)
    for h in range(4):
        o = acc[2 * h] / l[2 * h] - lam * (acc[2 * h + 1] / l[2 * h + 1])
        o = o * lax.rsqrt(jnp.mean(o * o, axis=0, keepdims=True) + LN_EPS) * gain_ref[...] * (1.0 - lam_init)
        o_ref[0, :, h * LANES:(h + 1) * LANES] = o.T.astype(o_ref.dtype)


def _diff_prompt(q, k, v, lamv, gain, lam_init, qb):
    b, t, _ = q.shape
    nkb = t // qb
    vt = jnp.swapaxes(v.reshape(b, nkb, qb, 512), -1, -2)
    full = pl.BlockSpec((1, t, 512), lambda bi, i: (bi, 0, 0))
    tile = pl.BlockSpec((1, qb, 512), lambda bi, i: (bi, i, 0))
    return pl.pallas_call(
        functools.partial(_diff_prompt_kernel, qb=qb, slopes=_alibi(4), lam_init=lam_init),
        grid=(b, nkb),
        in_specs=[tile, full,
                  pl.BlockSpec((1, nkb, 512, qb), lambda bi, i: (bi, 0, 0, 0)),
                  pl.BlockSpec((4, HEAD_DIM), lambda bi, i: (0, 0)),
                  pl.BlockSpec((LANES, 1), lambda bi, i: (0, 0))],
        out_specs=tile,
        out_shape=jax.ShapeDtypeStruct((b, t, 512), BF16),
        scratch_shapes=[pltpu.VMEM((8 * qb, 2 * LANES), BF16)],
        compiler_params=_cparams(("parallel", "arbitrary")),
        name="diff_prompt",
    )(q, k, vt, lamv, gain.reshape(LANES, 1))


def _diff_sample_kernel(q_ref, kc_ref, vc_ref, kn_ref, vn_ref, lam_ref, gain_ref, o_ref,
                        m_scr, l_scr, acc_scr, *, t, kb_size, n_past, slopes, lam_init):
    j = pl.program_id(1)
    past_len = n_past * kb_size
    qpos = past_len + lax.broadcasted_iota(I32, (t, 1), 0)

    @pl.when(j == 0)
    def _():
        m_scr[...] = jnp.full(m_scr.shape, NEG_INF, F32)
        l_scr[...] = jnp.zeros(l_scr.shape, F32)
        acc_scr[...] = jnp.zeros(acc_scr.shape, F32)

    def run(kfull, v_of, kpos):
        dist = jnp.abs(qpos - kpos).astype(F32)
        nk = kfull.shape[0]
        s3 = jnp.stack(
            [_dot_nt(_split_halves(q_ref[0, :, h * LANES:(h + 1) * LANES]),
                     kfull[:, h * LANES:(h + 1) * LANES]).reshape(2, t, nk) - slopes[h] * dist[None]
             for h in range(4)], axis=0)
        m = m_scr[...]
        m_new = jnp.maximum(m, jnp.max(s3, -1, keepdims=True))
        corr = jnp.exp2(m - m_new)
        p = jnp.exp2(s3 - m_new)
        l_scr[...] = l_scr[...] * corr + jnp.sum(p, -1, keepdims=True)
        pv = jnp.stack([_dot(p[h].reshape(2 * t, nk).astype(BF16), v_of(h)).reshape(2, t, LANES)
                        for h in range(4)], axis=0)
        acc_scr[...] = acc_scr[...] * corr + pv
        m_scr[...] = m_new

    @pl.when(j < n_past)
    def _():
        kpos = j * kb_size + lax.broadcasted_iota(I32, (1, kb_size), 1)
        run(kc_ref[0].astype(BF16), lambda h: vc_ref[0, :, h * LANES:(h + 1) * LANES].astype(BF16), kpos)

    @pl.when(j == n_past)
    def _():
        kpos = past_len + lax.broadcasted_iota(I32, (1, t), 1)
        run(kn_ref[0].astype(BF16), lambda h: vn_ref[0, :, h * LANES:(h + 1) * LANES].astype(BF16), kpos)
        lam = _diff_lambda(lam_ref, lam_init)
        for h in range(4):
            sl = slice(h * LANES, (h + 1) * LANES)
            carry = (m_scr[h], l_scr[h], acc_scr[h])
            o_ref[0, :, sl] = _diff_finish(carry, lam, gain_ref[...], lam_init).astype(o_ref.dtype)


def _diff_sample(q, k_cache, v_cache, k_new, v_new, lamv, gain, lam_init, kb_size):
    b, t, _ = q.shape
    n_past = k_cache.shape[1] // kb_size
    cache = pl.BlockSpec((1, kb_size, 512), lambda bi, j: (bi, jnp.minimum(j, n_past - 1), 0))
    new = pl.BlockSpec((1, t, 512), lambda bi, j: (bi, 0, 0))
    return pl.pallas_call(
        functools.partial(_diff_sample_kernel, t=t, kb_size=kb_size, n_past=n_past,
                          slopes=_alibi(4), lam_init=lam_init),
        grid=(b, n_past + 1),
        in_specs=[new, cache, cache, new, new,
                  pl.BlockSpec((4, HEAD_DIM), lambda bi, j: (0, 0)),
                  pl.BlockSpec((1, LANES), lambda bi, j: (0, 0))],
        out_specs=new,
        out_shape=jax.ShapeDtypeStruct((b, t, 512), BF16),
        scratch_shapes=[pltpu.VMEM((4, 2, t, 1), F32), pltpu.VMEM((4, 2, t, 1), F32),
                        pltpu.VMEM((4, 2, t, LANES), F32)],
        compiler_params=_cparams(("parallel", "arbitrary")),
        name="diff_sample",
    )(q, k_cache, v_cache, k_new, v_new, lamv, gain)


def _index_keys(qi, wi, ki_blk, allowed):
    acc = None
    for h in range(8):
        sc = _dot_nt(qi[:, h * HEAD_DIM:(h + 1) * HEAD_DIM], ki_blk)
        term = jnp.maximum(sc, 0.0) * wi[:, h:h + 1]
        acc = term if acc is None else acc + term
    idx = acc * (8.0 ** -0.5) + 0.0
    bits = pltpu.bitcast(idx, I32)
    key = jnp.where(bits < 0, bits ^ 0x7FFFFFFF, bits)
    return jnp.where(allowed, key, INT_MIN)


def _lane_fold(x):
    acc = x[:, :LANES]
    for j in range(1, x.shape[1] // LANES):
        acc = acc + x[:, j * LANES:(j + 1) * LANES]
    return acc


def _ones_where(pred):
    return jnp.where(pred, 1, 0).astype(I32)


def _kth_largest(count_ge, shape, n_sel):
    def body(it, lo):
        bit = 31 - it
        cand = lo + jnp.left_shift(jnp.int32(1), bit)
        return jnp.where(count_ge(cand) >= n_sel, cand, lo)
    return lax.fori_loop(0, 32, body, jnp.full(shape, INT_MIN, I32))


def _tie_position(count_eq_le, need, shape, nbits):
    def body(it, lo):
        bit = nbits - 1 - it
        cand = lo + jnp.left_shift(jnp.int32(1), bit)
        return jnp.where(count_eq_le(cand - 1) < need, cand, lo)
    return lax.fori_loop(0, nbits, body, jnp.zeros(shape, I32))


def _select_threshold(count_over, shape, n_sel, pos_bits):
    thr = _kth_largest(lambda cand: count_over(lambda key, kpos: key >= cand), shape, n_sel)
    need = n_sel - count_over(lambda key, kpos: key > thr)
    n_eq = count_over(lambda key, kpos: key == thr)
    has_tie = jnp.max(jnp.where((n_eq > need) & (thr > INT_MIN), 1, 0)) > 0
    pos_thr = lax.cond(
        has_tie,
        lambda: _tie_position(
            lambda p: count_over(lambda key, kpos: (key == thr) & (kpos <= p)), need, shape, pos_bits),
        lambda: jnp.full(shape, 2 ** 30, I32))
    return thr, pos_thr


def _dsa_attend(carry, q_ref, kblk, vblk, dist, sel, slopes):
    neg = jnp.where(sel, 0.0, NEG_INF)
    qs = []
    for s in range(4):
        slab = q_ref[0, :, s * LANES:(s + 1) * LANES]
        left = _lane_left(slab.shape)
        zero = jnp.zeros_like(slab)
        qs.append((jnp.where(left, slab, zero), jnp.where(left, zero, slab)))
    q_all = jnp.concatenate([qs[h % 4][h // 4] for h in range(8)], axis=0)
    r, kb = dist.shape
    s3 = _dot_nt(q_all, kblk).reshape(8, r, kb)
    s3 = jnp.stack([s3[h] - slopes[h] * dist + neg for h in range(8)], axis=0)
    return _flash_update(carry, s3, vblk)


def _dsa_finish(carry, o_ref):
    m, l, acc = carry
    o = acc / l
    for s in range(4):
        left = _lane_left(o[s].shape)
        o_ref[0, :, s * LANES:(s + 1) * LANES] = jnp.where(left, o[s], o[4 + s]).astype(o_ref.dtype)


def _dsa_prompt_kernel(q_ref, qi_ref, wi_ref, k_ref, vt_ref, ki_ref, o_ref, key_scr, qall_scr, qiall_scr,
                       *, qb, n_sel, slopes):
    i = pl.program_id(1)
    nkb = i + 1
    qpos = i * qb + lax.broadcasted_iota(I32, (1, qb), 1)

    for s in range(4):
        slab = q_ref[0, :, s * LANES:(s + 1) * LANES]
        left = _lane_left(slab.shape)
        zero = jnp.zeros_like(slab)
        qall_scr[s * qb:(s + 1) * qb, :LANES] = jnp.where(left, slab, zero)
        qall_scr[(4 + s) * qb:(5 + s) * qb, :LANES] = jnp.where(left, zero, slab)
    for h in range(8):
        qall_scr[h * qb:(h + 1) * qb, LANES:] = _alibi_q_features(qb, slopes[h])
        qiall_scr[h * qb:(h + 1) * qb, :] = qi_ref[0, :, h * HEAD_DIM:(h + 1) * HEAD_DIM]
    wit = wi_ref[0].T

    def kpos_of(kb):
        return kb * qb + lax.broadcasted_iota(I32, (qb, 1), 0)

    def rows_of(kb):
        return pl.ds(pl.multiple_of(kb * qb, qb), qb)

    def fill(kb, c):
        allowed = kpos_of(kb) // CHUNK <= qpos // CHUNK
        sc = _dot_nt(ki_ref[0, rows_of(kb), :], qiall_scr[...])
        acc = None
        for h in range(8):
            term = jnp.maximum(sc[:, h * qb:(h + 1) * qb], 0.0) * wit[h:h + 1, :]
            acc = term if acc is None else acc + term
        idx = acc * (8.0 ** -0.5) + 0.0
        bits = pltpu.bitcast(idx, I32)
        key = jnp.where(bits < 0, bits ^ 0x7FFFFFFF, bits)
        key_scr[kb] = jnp.where(allowed, key, INT_MIN)
        return c
    lax.fori_loop(0, nkb, fill, 0)

    def count_over(pred_fn):
        def body(kb, c):
            ones = _ones_where(pred_fn(key_scr[kb], kpos_of(kb)))
            return c + jnp.sum(ones.reshape(qb // 8, 8, qb), axis=0)
        part = lax.fori_loop(0, nkb, body, jnp.zeros((8, qb), I32))
        return jnp.sum(part, axis=0, keepdims=True)

    thr, pos_thr = _select_threshold(count_over, (1, qb), n_sel, 12)

    def attend(kb, carry, diagonal):
        m, l, acc = carry
        kpos = kpos_of(kb)
        key = key_scr[kb]
        sel = (key > thr) | ((key == thr) & (kpos <= pos_thr))
        if diagonal:
            sel = sel & (kpos // CHUNK <= qpos // CHUNK)
        neg = jnp.where(sel, 0.0, NEG_INF)
        k_aug = jnp.concatenate([k_ref[0, rows_of(kb), :], _alibi_k_features(kpos)], axis=1)
        st = _dot_nt(k_aug, qall_scr[...])
        if diagonal:
            ahead = jnp.maximum(kpos - qpos, 0).astype(F32)
            s3 = jnp.stack([st[:, h * qb:(h + 1) * qb] - (2.0 * slopes[h]) * ahead + neg for h in range(8)],
                           axis=0)
        else:
            s3 = jnp.stack([st[:, h * qb:(h + 1) * qb] + neg for h in range(8)], axis=0)
        m_new = jnp.maximum(m, jnp.max(s3, axis=1, keepdims=True))
        m_use = jnp.where(m_new == NEG_INF, 0.0, m_new)
        corr = jnp.exp2(m - m_use)
        p = jnp.exp2(s3 - m_use)
        l = l * corr + jnp.sum(p, axis=1, keepdims=True)
        p_all = jnp.concatenate([p[h].astype(BF16) for h in range(8)], axis=1)
        pv = _dot(vt_ref[0, kb], p_all)
        acc = acc * corr + jnp.stack([pv[:, h * qb:(h + 1) * qb] for h in range(8)], axis=0)
        return m_new, l, acc

    init = (jnp.full((8, 1, qb), NEG_INF, F32), jnp.zeros((8, 1, qb), F32), jnp.zeros((8, LANES, qb), F32))
    carry = lax.fori_loop(0, i, functools.partial(attend, diagonal=False), init)
    m, l, acc = attend(i, carry, diagonal=True)
    ot = acc / l
    top = lax.broadcasted_iota(I32, (LANES, qb), 0) < HEAD_DIM
    for s in range(4):
        o_ref[0, :, s * LANES:(s + 1) * LANES] = jnp.where(top, ot[s], ot[4 + s]).T.astype(o_ref.dtype)


def _dsa_prompt(q, qi, wi, k, v, ki, qb):
    b, t, _ = q.shape
    n_sel = min(TOPK_MAX, t // 4)
    nkb = t // qb
    vt = jnp.swapaxes(v.reshape(b, nkb, qb, LANES), -1, -2)
    tile = lambda w: pl.BlockSpec((1, qb, w), lambda bi, i: (bi, i, 0))
    full = lambda w: pl.BlockSpec((1, t, w), lambda bi, i: (bi, 0, 0))
    return pl.pallas_call(
        functools.partial(_dsa_prompt_kernel, qb=qb, n_sel=n_sel, slopes=_alibi(8)),
        grid=(b, nkb),
        in_specs=[tile(512), tile(512), tile(LANES), full(LANES),
                  pl.BlockSpec((1, nkb, LANES, qb), lambda bi, i: (bi, 0, 0, 0)), full(HEAD_DIM)],
        out_specs=tile(512),
        out_shape=jax.ShapeDtypeStruct((b, t, 512), BF16),
        scratch_shapes=[pltpu.VMEM((nkb, qb, qb), I32), pltpu.VMEM((8 * qb, 2 * LANES), BF16),
                        pltpu.VMEM((8 * qb, HEAD_DIM), BF16)],
        compiler_params=_cparams(("parallel", "arbitrary")),
        name="dsa_prompt",
    )(q, qi, wi, k, vt, ki)


def _dsa_sample_kernel(q_ref, qi_ref, wi_ref, kc_ref, vc_ref, kic_ref, kn_ref, vn_ref, kin_ref, o_ref,
                       key_scr, keyn_scr, *, t, kb_size, n_past, n_sel, slopes):
    past_len = n_past * kb_size
    qpos = past_len + lax.broadcasted_iota(I32, (t, 1), 0)
    qi = qi_ref[0]
    wi = wi_ref[0]
    kpos_n = past_len + lax.broadcasted_iota(I32, (1, t), 1)

    def kpos_of(kb):
        return kb * kb_size + lax.broadcasted_iota(I32, (1, kb_size), 1)

    for kb in range(n_past):
        ki_blk = kic_ref[0, kb * kb_size:(kb + 1) * kb_size, :].astype(BF16)
        key_scr[kb] = _index_keys(qi, wi, ki_blk, True)
    keyn_scr[...] = _index_keys(qi, wi, kin_ref[0].astype(BF16), True)

    def count_over(pred_fn):
        part = _lane_fold(_ones_where(pred_fn(key_scr[0], kpos_of(0))))
        for kb in range(1, n_past):
            part = part + _lane_fold(_ones_where(pred_fn(key_scr[kb], kpos_of(kb))))
        new = jnp.sum(_ones_where(pred_fn(keyn_scr[...], kpos_n)), axis=-1, keepdims=True)
        return jnp.sum(part, axis=-1, keepdims=True) + new

    thr, pos_thr = _select_threshold(count_over, (t, 1), n_sel, 13)
    carry = _flash_init(8, t, LANES)

    def sel_of(key, kpos):
        return (key > thr) | ((key == thr) & (kpos <= pos_thr))

    for kb in range(n_past):
        kpos = kpos_of(kb)
        dist = jnp.abs(qpos - kpos).astype(F32)
        kblk = kc_ref[0, kb * kb_size:(kb + 1) * kb_size, :].astype(BF16)
        vblk = vc_ref[0, kb * kb_size:(kb + 1) * kb_size, :].astype(BF16)
        carry = _dsa_attend(carry, q_ref, kblk, vblk, dist, sel_of(key_scr[kb], kpos), slopes)
    dist = jnp.abs(qpos - kpos_n).astype(F32)
    carry = _dsa_attend(carry, q_ref, kn_ref[0].astype(BF16), vn_ref[0].astype(BF16), dist,
                        sel_of(keyn_scr[...], kpos_n), slopes)
    _dsa_finish(carry, o_ref)


def _dsa_sample(q, qi, wi, k_cache, v_cache, ki_cache, k_new, v_new, ki_new, kb_size):
    b, t, _ = q.shape
    past = k_cache.shape[1]
    n_past = past // kb_size
    n_sel = min(TOPK_MAX, (past + t) // 4)
    per_b = lambda rows, w: pl.BlockSpec((1, rows, w), lambda bi: (bi, 0, 0))
    return pl.pallas_call(
        functools.partial(_dsa_sample_kernel, t=t, kb_size=kb_size, n_past=n_past, n_sel=n_sel,
                          slopes=_alibi(8)),
        grid=(b,),
        in_specs=[per_b(t, 512), per_b(t, 512), per_b(t, LANES),
                  per_b(past, LANES), per_b(past, LANES), per_b(past, HEAD_DIM),
                  per_b(t, LANES), per_b(t, LANES), per_b(t, HEAD_DIM)],
        out_specs=per_b(t, 512),
        out_shape=jax.ShapeDtypeStruct((b, t, 512), BF16),
        scratch_shapes=[pltpu.VMEM((n_past, t, kb_size), I32), pltpu.VMEM((t, t), I32)],
        compiler_params=_cparams(("parallel",)),
        name="dsa_sample",
    )(q, qi, wi, k_cache, v_cache, ki_cache, k_new, v_new, ki_new)


def _pad_cols(w, total):
    return jnp.pad(w, ((0, 0), (0, total - w.shape[1])))


def _q_perm():
    idx = []
    for s in range(4):
        idx.extend(range(s * HEAD_DIM, (s + 1) * HEAD_DIM))
        idx.extend(range((4 + s) * HEAD_DIM, (5 + s) * HEAD_DIM))
    return np.asarray(idx, np.int32)


def _even_layer(xp, xs, cache_k, cache_v, shift_s, wkv_s, w_in, sink, rw, w_out, ffn, alpha, sizes):
    (mu, w0, w_up, a0, a_up, g_up, k_k, k_a, r_k, lnx_g, lnx_b) = rw
    perm = _q_perm()
    w_in_p = jnp.concatenate([w_in[:, :512][:, perm], w_in[:, 512:]], axis=1).astype(BF16)
    w_out_p = jnp.concatenate([w_out[:512][perm], w_out[512:]], axis=0).astype(BF16)
    groups = ((0, 512, ((0, 512, HEAD_DIM ** -0.5 * LOG2E),)),
              (512, 128, ((1, 128, 1.0), (2, 128, 1.0))),
              (640, 128, ((3, 128, 1.0), (4, 128, 1.0))),
              (768, 1792, ((5, 1792, 1.0),)))
    out_defs = ((512, BF16), (128, F32), (128, BF16), (128, F32), (128, BF16), (1792, F32))
    wup_p = jnp.concatenate([w_up, jnp.zeros_like(w_up)], axis=0).astype(BF16)
    aup_p = jnp.concatenate([jnp.zeros_like(a_up), a_up], axis=0).astype(BF16)
    gup = g_up.astype(BF16)
    row = lambda v: v.reshape(1, -1)
    g1, b1, g2, b2, wu, wd = ffn

    def run(x, state, tm, qb, scan_tb):
        b, t, d = x.shape
        n = b * t
        qa, ka, ka_bf, va, va_bf, pb = _project(x.reshape(n, d), w_in_p, groups, out_defs, tm)
        r3 = lambda a: a.reshape(b, t, a.shape[-1])
        if state is None:
            oa = _swa_prompt(r3(qa), r3(ka_bf), r3(va_bf), sink, qb)
            k_buf = r3(ka)[:, -WINDOW:]
            v_buf = r3(va)[:, -WINDOW:]
            shift = jnp.zeros((b, 1, pb.shape[-1]), F32)
            s0 = jnp.zeros((b, 8, HEAD_DIM, HEAD_DIM), F32)
        else:
            ck, cv, shift, s0 = state
            ck2 = ck.reshape(b, WINDOW, LANES)
            cv2 = cv.reshape(b, WINDOW, LANES)
            oa = _swa_step(r3(qa), ck2, r3(ka_bf), cv2, r3(va_bf), sink, qb=qb, pos_base=sizes["past"])
            k_buf = jnp.concatenate([ck2, r3(ka)], axis=1)[:, -WINDOW:]
            v_buf = jnp.concatenate([cv2, r3(va)], axis=1)[:, -WINDOW:]
        pb3 = r3(pb)
        ob, s_bd = _rwkv(pb3, shift, row(mu), row(w0), wup_p, row(a0), aup_p, gup, row(k_k), row(k_a),
                         row(r_k), row(lnx_g), row(lnx_b), _state_to_blockdiag(s0), scan_tb)
        s_t = _state_from_blockdiag(s_bd)
        xo = _out_ffn(x.reshape(n, d), oa.reshape(n, 512), ob.reshape(n, 512), w_out_p, row(g1), row(b1),
                      wu.astype(BF16), wd.astype(BF16), row(g2), row(b2), alpha, tm)
        st = (k_buf.reshape(b, WINDOW, 2, HEAD_DIM), v_buf.reshape(b, WINDOW, 2, HEAD_DIM),
              pb3[:, -1:], s_t)
        return xo.reshape(b, t, d), st

    xp2, st_p = run(xp, None, sizes["tm_p"], sizes["swa_qb"], sizes["scan_tb_p"])
    xs2, st_s = run(xs, (cache_k, cache_v, shift_s, wkv_s), sizes["tm_s"], xs.shape[1], xs.shape[1])
    return xp2, xs2, st_p, st_s


def _odd_layer(xp, xs, caches, w_in, lam_vecs, subln_g, w_out, ffn, alpha, lam_init, sizes):
    perm = _q_perm()
    scale = HEAD_DIM ** -0.5 * LOG2E
    w_p = jnp.concatenate([
        w_in[:, :1536], w_in[:, 1536:2048][:, perm], w_in[:, 2048:2816],
        _pad_cols(w_in[:, 2816:2880], LANES), _pad_cols(w_in[:, 2880:2888], LANES)], axis=1).astype(BF16)
    w_out_p = jnp.concatenate([w_out[:512], w_out[512:][perm]], axis=0).astype(BF16)
    groups = ((0, 512, ((0, 512, scale),)),
              (512, 512, ((1, 512, 1.0), (2, 512, 1.0))),
              (1024, 512, ((3, 512, 1.0), (4, 512, 1.0))),
              (1536, 512, ((5, 512, scale),)),
              (2048, 128, ((6, 128, 1.0), (7, 128, 1.0))),
              (2176, 128, ((8, 128, 1.0), (9, 128, 1.0))),
              (2304, 512, ((10, 512, HEAD_DIM ** -0.5),)),
              (2816, 128, ((11, 64, 1.0), (12, 64, 1.0))),
              (2944, 128, ((13, 128, 1.0),)))
    out_defs = ((512, BF16), (512, F32), (512, BF16), (512, F32), (512, BF16), (512, BF16),
                (128, F32), (128, BF16), (128, F32), (128, BF16), (512, BF16),
                (64, F32), (64, BF16), (128, F32))
    row = lambda v: v.reshape(1, -1)
    g1, b1, g2, b2, wu, wd = ffn
    lamv = jnp.stack(lam_vecs, axis=0)
    gain = row(subln_g)

    def run(x, state, tm):
        b, t, d = x.shape
        n = b * t
        (qc, kc, kc_bf, vc, vc_bf, qd, kd, kd_bf, vd, vd_bf, qi, ki, ki_bf, wi) = _project(
            x.reshape(n, d), w_p, groups, out_defs, tm)
        r3 = lambda a: a.reshape(b, t, a.shape[-1])
        if state is None:
            oc = _diff_prompt(r3(qc), r3(kc_bf), r3(vc_bf), lamv, gain, lam_init, sizes["diff_qb"])
            od = _dsa_prompt(r3(qd), r3(qi), r3(wi), r3(kd_bf), r3(vd_bf), r3(ki_bf), sizes["dsa_qb"])
        else:
            c_k, c_v, d_k, d_v, d_ki = state
            past = c_k.shape[1]
            oc = _diff_sample(r3(qc), c_k.reshape(b, past, 512), c_v.reshape(b, past, 512),
                              r3(kc_bf), r3(vc_bf), lamv, gain, lam_init, sizes["diff_kb_s"])
            od = _dsa_sample(r3(qd), r3(qi), r3(wi), d_k.reshape(b, past, LANES), d_v.reshape(b, past, LANES),
                             d_ki, r3(kd_bf), r3(vd_bf), r3(ki_bf), sizes["dsa_kb_s"])
        xo = _out_ffn(x.reshape(n, d), oc.reshape(n, 512), od.reshape(n, 512), w_out_p, row(g1), row(b1),
                      wu.astype(BF16), wd.astype(BF16), row(g2), row(b2), alpha, tm)
        rows = (kc.reshape(b, t, 4, 2, HEAD_DIM), vc.reshape(b, t, 4, 2 * HEAD_DIM),
                kd.reshape(b, t, 2, HEAD_DIM), vd.reshape(b, t, 2, HEAD_DIM), ki.reshape(b, t, HEAD_DIM))
        return xo.reshape(b, t, d), rows

    xp2, st_p = run(xp, None, sizes["tm_p"])
    xs2, st_s = run(xs, caches, sizes["tm_s"])
    return xp2, xs2, st_p, st_s


def _sizes(xp, xs, past):
    return dict(tm_p=min(512, xp.shape[0] * xp.shape[1]), tm_s=min(512, xs.shape[0] * xs.shape[1]),
                swa_qb=min(256, xp.shape[1]), scan_tb_p=min(128, xp.shape[1]),
                diff_qb=min(256, xp.shape[1]),
                dsa_qb=min(256, xp.shape[1]), diff_kb_s=min(1024, past), dsa_kb_s=min(512, past),
                past=past)


def kernel(x_prompt, x_sample, cache_a_k, cache_a_v, state_b_shift, state_b_wkv, cache_c_k, cache_c_v, cache_d_k, cache_d_v, cache_d_kidx, w_in_even, sink_a, mu_b, w0_b, w_up_b, a0_b, a_up_b, g_up_b, k_k_b, k_a_b, r_k_b, lnx_g_b, lnx_b_b, w_out_even, w_in_odd, lam_q1_c, lam_k1_c, lam_q2_c, lam_k2_c, subln_g_c, w_out_odd, ln_mix_g, ln_mix_b, ln_ffn_g, ln_ffn_b, w_ff_up, w_ff_down):
    depth = ln_mix_g.shape[0]
    alpha = (2 * depth) ** 0.25
    past = cache_c_k.shape[2]
    sizes = _sizes(x_prompt, x_sample, past)
    xp, xs = x_prompt, x_sample
    even_p, even_s, odd_p, odd_s = [], [], [], []
    for layer in range(depth):
        i = layer // 2
        ffn = (ln_mix_g[layer], ln_mix_b[layer], ln_ffn_g[layer], ln_ffn_b[layer],
               w_ff_up[layer], w_ff_down[layer])
        if layer % 2 == 0:
            rw = (mu_b[i], w0_b[i], w_up_b[i], a0_b[i], a_up_b[i], g_up_b[i], k_k_b[i], k_a_b[i],
                  r_k_b[i].reshape(-1), lnx_g_b[i], lnx_b_b[i])
            xp, xs, st_p, st_s = _even_layer(
                xp, xs, cache_a_k[i], cache_a_v[i], state_b_shift[i], state_b_wkv[i],
                w_in_even[i], sink_a[i], rw, w_out_even[i], ffn, alpha, sizes)
            even_p.append(st_p)
            even_s.append(st_s)
        else:
            lam_init = 0.8 - 0.6 * math.exp(-0.3 * layer)
            xp, xs, st_p, st_s = _odd_layer(
                xp, xs, (cache_c_k[i], cache_c_v[i], cache_d_k[i], cache_d_v[i], cache_d_kidx[i]),
                w_in_odd[i], (lam_q1_c[i], lam_k1_c[i], lam_q2_c[i], lam_k2_c[i]), subln_g_c[i],
                w_out_odd[i], ffn, alpha, lam_init, sizes)
            odd_p.append(st_p)
            odd_s.append(st_s)
    stack = lambda states: [jnp.stack(z, axis=0) for z in zip(*states)]
    a_k_p, a_v_p, b_shift_p, b_wkv_p = stack(even_p)
    a_k_s, a_v_s, b_shift_s, b_wkv_s = stack(even_s)
    c_k_p, c_v_p, d_k_p, d_v_p, d_kidx_p = stack(odd_p)
    c_k_s, c_v_s, d_k_s, d_v_s, d_kidx_s = stack(odd_s)
    return (xp, xs, a_k_p, a_v_p, b_shift_p, b_wkv_p, c_k_p, c_v_p, d_k_p, d_v_p, d_kidx_p,
            a_k_s, a_v_s, b_shift_s, b_wkv_s, c_k_s, c_v_s, d_k_s, d_v_s, d_kidx_s)
```

```python
import functools
import math

import numpy as np
import jax
import jax.numpy as jnp
from jax import lax
from jax.experimental import pallas as pl
from jax.experimental.pallas import tpu as pltpu

F32 = jnp.float32
BF16 = jnp.bfloat16
I32 = jnp.int32

LANES = 128
HEAD_DIM = 64
CHUNK = 64
LN_EPS = 1e-5
RWKV_GN_EPS = 64e-5
WINDOW = 128
TOPK_MAX = 256
NEG_INF = float("-inf")
INT_MIN = -(2 ** 31)
VMEM_LIMIT = 56 * 1024 * 1024


def _cparams(sem):
    return pltpu.CompilerParams(dimension_semantics=sem, vmem_limit_bytes=VMEM_LIMIT)


LOG2E = math.log2(math.e)


def _alibi(n):
    return [float(2.0 ** (-8.0 * (i + 1) / n)) * LOG2E for i in range(n)]


def _bf16_parts(c):
    parts, rem = [], np.float64(c)
    for _ in range(3):
        piece = np.float64(np.asarray(rem, np.float32).astype(BF16).astype(np.float32))
        parts.append(float(piece))
        rem = rem - piece
    return parts


def _alibi_q_features(rows, slope):
    parts = _bf16_parts(slope)
    lane = lax.broadcasted_iota(I32, (rows, LANES), 1)
    out = jnp.zeros((rows, LANES), F32)
    for j, v in enumerate([16.0 * p for p in parts] + parts):
        out = jnp.where(lane == j, v, out)
    return out.astype(BF16)


def _alibi_k_features(kpos):
    lane = lax.broadcasted_iota(I32, (kpos.shape[0], LANES), 1)
    hi = (kpos // 16).astype(F32)
    lo = (kpos % 16).astype(F32)
    return jnp.where(lane < 3, hi, jnp.where(lane < 6, lo, 0.0)).astype(BF16)


def _dot(a, b):
    return jnp.dot(a, b, preferred_element_type=F32)


def _dot_nt(a, b):
    return lax.dot_general(a, b, (((1,), (1,)), ((), ())), preferred_element_type=F32)


def _lane_left(shape):
    return lax.broadcasted_iota(I32, shape, len(shape) - 1) < HEAD_DIM


def _segsum64(x):
    left = _lane_left(x.shape)
    sl = jnp.sum(jnp.where(left, x, 0.0), axis=-1, keepdims=True)
    sr = jnp.sum(jnp.where(left, 0.0, x), axis=-1, keepdims=True)
    return jnp.where(left, sl, sr)


def _proj_kernel(x_ref, w_ref, *out_refs, groups):
    xb = x_ref[...].astype(BF16)
    for off, width, outs in groups:
        r = _dot(xb, w_ref[:, off:off + width])
        for idx, ow, scale in outs:
            v = r if ow == width else r[:, :ow]
            if scale != 1.0:
                v = v * scale
            ref = out_refs[idx]
            if len(ref.shape) == 2:
                ref[...] = v.astype(ref.dtype)
            elif len(ref.shape) == 3:
                for j in range(ref.shape[1]):
                    ref[:, j, :] = v[:, j * ref.shape[2]:(j + 1) * ref.shape[2]].astype(ref.dtype)
            else:
                a, b, ln = ref.shape[1:]
                for j in range(a):
                    for m in range(b):
                        g = j * b + m
                        ref[:, j, m, :] = v[:, g * ln:(g + 1) * ln].astype(ref.dtype)


def _project(x, w_bf, groups, out_defs, tm):
    n, d = x.shape
    cols = w_bf.shape[1]
    out_shape, out_specs = [], []
    for od in out_defs:
        shp = od[2] if len(od) > 2 else (od[0],)
        out_shape.append(jax.ShapeDtypeStruct((n,) + shp, od[1]))
        out_specs.append(pl.BlockSpec((tm,) + shp, lambda i, _k=len(shp): (i,) + (0,) * _k))
    return pl.pallas_call(
        functools.partial(_proj_kernel, groups=groups),
        grid=(n // tm,),
        in_specs=[pl.BlockSpec((tm, d), lambda i: (i, 0)),
                  pl.BlockSpec((d, cols), lambda i: (0, 0))],
        out_specs=out_specs,
        out_shape=out_shape,
        compiler_params=_cparams(("parallel",)),
        name="in_proj",
    )(x, w_bf)


def _swa_kernel(sink_ref, q_ref, kp_ref, kc_ref, vp_ref, vc_ref, o_ref, *, qb, pos_base, slopes):
    i = pl.program_id(1)
    q0 = pos_base + i * qb
    kp = kp_ref[0].astype(BF16)
    kc = kc_ref[0].astype(BF16)
    vp = vp_ref[0].astype(BF16)
    vc = vc_ref[0].astype(BF16)
    qpos = q0 + lax.broadcasted_iota(I32, (qb, 1), 0)
    kpos_p = q0 - WINDOW + lax.broadcasted_iota(I32, (1, WINDOW), 1)
    kpos_c = q0 + lax.broadcasted_iota(I32, (1, qb), 1)
    qch = qpos // CHUNK

    def allowed(kpos):
        kch = kpos // CHUNK
        return (kpos >= 0) & (kch >= qch - WINDOW // CHUNK) & (kch <= qch)

    al_p, al_c = allowed(kpos_p), allowed(kpos_c)
    dist_p = jnp.abs(qpos - kpos_p).astype(F32)
    dist_c = jnp.abs(qpos - kpos_c).astype(F32)
    for s in range(4):
        slab = q_ref[0, :, s * LANES:(s + 1) * LANES]
        left = _lane_left(slab.shape)
        res = []
        for side in range(2):
            h = 4 * side + s
            qx = jnp.where(left if side == 0 else jnp.logical_not(left), slab, jnp.zeros_like(slab))
            lp = jnp.where(al_p, _dot_nt(qx, kp) - slopes[h] * dist_p, NEG_INF)
            lc = jnp.where(al_c, _dot_nt(qx, kc) - slopes[h] * dist_c, NEG_INF)
            sk = sink_ref[h] * LOG2E
            m = jnp.maximum(jnp.maximum(jnp.max(lp, -1, keepdims=True), jnp.max(lc, -1, keepdims=True)), sk)
            pp = jnp.exp2(lp - m)
            pc = jnp.exp2(lc - m)
            den = jnp.sum(pp, -1, keepdims=True) + jnp.sum(pc, -1, keepdims=True) + jnp.exp2(sk - m)
            o = _dot(pp.astype(BF16), vp) + _dot(pc.astype(BF16), vc)
            res.append(o / den)
        o_ref[0, :, s * LANES:(s + 1) * LANES] = jnp.where(left, res[0], res[1]).astype(o_ref.dtype)


def _swa(q, k_prev_src, k_cur, v_prev_src, v_cur, sink, *, qb, prev_is_cache, pos_base):
    b, t, _ = q.shape
    nq = t // qb
    if prev_is_cache:
        prev_map = lambda bi, i: (bi, 0, 0)
    else:
        r = qb // WINDOW
        prev_map = lambda bi, i: (bi, jnp.maximum(i * r - 1, 0), 0)
    cur_map = lambda bi, i: (bi, i, 0)
    return pl.pallas_call(
        functools.partial(_swa_kernel, qb=qb, pos_base=pos_base, slopes=_alibi(8)),
        grid=(b, nq),
        in_specs=[pl.BlockSpec(memory_space=pltpu.SMEM),
                  pl.BlockSpec((1, qb, 512), cur_map),
                  pl.BlockSpec((1, WINDOW, LANES), prev_map),
                  pl.BlockSpec((1, qb, LANES), cur_map),
                  pl.BlockSpec((1, WINDOW, LANES), prev_map),
                  pl.BlockSpec((1, qb, LANES), cur_map)],
        out_specs=pl.BlockSpec((1, qb, 512), cur_map),
        out_shape=jax.ShapeDtypeStruct((b, t, 512), BF16),
        compiler_params=_cparams(("parallel", "parallel")),
        name="swa_sink",
    )(sink, q, k_prev_src, k_cur, v_prev_src, v_cur)


def _swa_prompt_kernel(sink_ref, q_ref, kp_ref, kc_ref, vtp_ref, vtc_ref, o_ref, qall_scr, *, qb, slopes):
    i = pl.program_id(1)
    q0 = i * qb
    qpos = q0 + lax.broadcasted_iota(I32, (1, qb), 1)
    for s in range(4):
        slab = q_ref[0, :, s * LANES:(s + 1) * LANES]
        left = _lane_left(slab.shape)
        zero = jnp.zeros_like(slab)
        qall_scr[s * qb:(s + 1) * qb, :LANES] = jnp.where(left, slab, zero)
        qall_scr[(4 + s) * qb:(5 + s) * qb, :LANES] = jnp.where(left, zero, slab)
    for h in range(8):
        qall_scr[h * qb:(h + 1) * qb, LANES:] = _alibi_q_features(qb, slopes[h])
    kpos_p = q0 - WINDOW + lax.broadcasted_iota(I32, (WINDOW, 1), 0)
    kpos_c = q0 + lax.broadcasted_iota(I32, (qb, 1), 0)
    qch = qpos // CHUNK
    first = qch - WINDOW // CHUNK
    neg_p = jnp.where((kpos_p >= 0) & (kpos_p // CHUNK >= first), 0.0, NEG_INF)
    kch_c = kpos_c // CHUNK
    neg_c = jnp.where((kch_c >= first) & (kch_c <= qch), 0.0, NEG_INF)
    ahead = jnp.maximum(kpos_c - qpos, 0).astype(F32)
    qall = qall_scr[...]
    st_p = _dot_nt(jnp.concatenate([kp_ref[0], _alibi_k_features(jnp.maximum(kpos_p, 0))], axis=1), qall)
    st_c = _dot_nt(jnp.concatenate([kc_ref[0], _alibi_k_features(kpos_c)], axis=1), qall)
    qposf = qpos.astype(F32)
    pps, pcs, dens = [], [], []
    for h in range(8):
        sp = st_p[:, h * qb:(h + 1) * qb] + neg_p
        sc = st_c[:, h * qb:(h + 1) * qb] - (2.0 * slopes[h]) * ahead + neg_c
        sk = sink_ref[h] * LOG2E + slopes[h] * qposf
        m = jnp.maximum(jnp.maximum(jnp.max(sp, 0, keepdims=True), jnp.max(sc, 0, keepdims=True)), sk)
        pp = jnp.exp2(sp - m)
        pc = jnp.exp2(sc - m)
        dens.append(jnp.sum(pp, 0, keepdims=True) + jnp.sum(pc, 0, keepdims=True) + jnp.exp2(sk - m))
        pps.append(pp.astype(BF16))
        pcs.append(pc.astype(BF16))
    ot = (_dot(vtp_ref[0], jnp.concatenate(pps, axis=1))
          + _dot(vtc_ref[0], jnp.concatenate(pcs, axis=1)))
    top = lax.broadcasted_iota(I32, (LANES, qb), 0) < HEAD_DIM
    for s in range(4):
        o_l = ot[:, s * qb:(s + 1) * qb] / dens[s]
        o_r = ot[:, (4 + s) * qb:(5 + s) * qb] / dens[4 + s]
        o_ref[0, :, s * LANES:(s + 1) * LANES] = jnp.where(top, o_l, o_r).T.astype(o_ref.dtype)


def _swa_prompt(q, k, v, sink, qb):
    b, t, _ = q.shape
    r = qb // WINDOW
    vt = jnp.swapaxes(v, -1, -2)
    cur_map = lambda bi, i: (bi, i, 0)
    prev_map = lambda bi, i: (bi, jnp.maximum(i * r - 1, 0), 0)
    return pl.pallas_call(
        functools.partial(_swa_prompt_kernel, qb=qb, slopes=_alibi(8)),
        grid=(b, t // qb),
        in_specs=[pl.BlockSpec(memory_space=pltpu.SMEM),
                  pl.BlockSpec((1, qb, 512), cur_map),
                  pl.BlockSpec((1, WINDOW, LANES), prev_map),
                  pl.BlockSpec((1, qb, LANES), cur_map),
                  pl.BlockSpec((1, LANES, WINDOW), lambda bi, i: (bi, 0, jnp.maximum(i * r - 1, 0))),
                  pl.BlockSpec((1, LANES, qb), lambda bi, i: (bi, 0, i))],
        out_specs=pl.BlockSpec((1, qb, 512), cur_map),
        out_shape=jax.ShapeDtypeStruct((b, t, 512), BF16),
        scratch_shapes=[pltpu.VMEM((8 * qb, 2 * LANES), BF16)],
        compiler_params=_cparams(("parallel", "parallel")),
        name="swa_prompt",
    )(sink, q, k, k, vt, vt)


def _sigmoid(x):
    return 1.0 / (1.0 + jnp.exp(-x))


def _softplus(x):
    return jnp.maximum(x, 0.0) + jnp.log(1.0 + jnp.exp(-jnp.abs(x)))


def _rwkv_prep_kernel(pb_ref, pbprev_ref, shift_ref, mu_ref, w0_ref, wup_ref, a0_ref, aup_ref, gup_ref,
                      kk_ref, ka_ref, rk_ref,
                      r_out, w_out, k_out, v_out, kk_out, b_out, g_out, bonus_out):
    i = pl.program_id(1)
    pb = pb_ref[0]
    tm = pb.shape[0]
    prev_row = jnp.where(i == 0, shift_ref[0], pbprev_ref[0, 7:8, :])
    rolled = pltpu.roll(pb, 1, axis=0)
    row = lax.broadcasted_iota(I32, (tm, 1), 0)
    prev = jnp.where(row == 0, prev_row, rolled)
    xm = pb + (prev - pb) * mu_ref[...]
    r = xm[:, 0:512]
    k = xm[:, 512:1024]
    v = xm[:, 1024:1536]
    wa = xm[:, 1536:1664]
    gl = xm[:, 1664:1792]
    lw = _dot(jnp.tanh(wa).astype(BF16), wup_ref[...])
    la = _dot(wa.astype(BF16), aup_ref[...])
    w_log = -_softplus(-(w0_ref[...] + lw)) - 0.5
    log_decay = -jnp.exp(w_log)
    a = _sigmoid(a0_ref[...] + la)
    g = _dot(_sigmoid(gl).astype(BF16), gup_ref[...])
    kk = k * kk_ref[...]
    k2 = k * (1.0 + (a - 1.0) * ka_ref[...])
    rkk = r * k2 * rk_ref[...]
    for p in range(4):
        sl = slice(p * LANES, (p + 1) * LANES)
        kkp = kk[:, sl]
        kkn = kkp * lax.rsqrt(_segsum64(kkp * kkp) + 1e-12)
        kk_out[0, p] = kkn
        b_out[0, p] = kkn * a[:, sl]
        bonus_out[0, :, sl] = _segsum64(rkk[:, sl]) * v[:, sl]
        r_out[0, p] = r[:, sl]
        w_out[0, p] = log_decay[:, sl]
        k_out[0, p] = k2[:, sl]
        v_out[0, p] = v[:, sl]
    g_out[0] = g


RWKV_CHUNK = 16


def _split2(x):
    hi = x.astype(BF16)
    return hi, (x - hi.astype(F32)).astype(BF16)


def _dot3(a, b):
    return _dot(a[0], b[0]) + (_dot(a[0], b[1]) + _dot(a[1], b[0]))


def _rwkv_operator_scratch(tb):
    nc, c2 = tb // RWKV_CHUNK, 2 * RWKV_CHUNK
    return [pltpu.VMEM((4, nc, c2, LANES), BF16), pltpu.VMEM((4, nc, c2, LANES), BF16),
            pltpu.VMEM((4, nc, LANES, c2), BF16), pltpu.VMEM((4, nc, LANES, c2), BF16),
            pltpu.VMEM((4, nc, LANES, LANES), F32),
            pltpu.VMEM((4, tb, LANES), F32), pltpu.VMEM((4, tb, LANES), F32)]


def _rwkv_operator_stages(r_ref, lw_ref, k_ref, v_ref, kk_ref, b_ref,
                          l1hi_scr, l1lo_scr, l2hi_scr, l2lo_scr, pcol_scr, u0_scr, y0_scr, *, tb):
    c_sz = RWKV_CHUNK
    nc = tb // c_sz
    row = lax.broadcasted_iota(I32, (tb, tb), 0)
    col = lax.broadcasted_iota(I32, (tb, tb), 1)
    same = (row // c_sz) == (col // c_sz)
    strict = same & (col < row)
    incl = same & (col <= row)
    cum_lhs = jnp.concatenate([jnp.where(incl, 1.0, 0.0), jnp.where(same, 1.0, 0.0)], axis=0).astype(BF16)
    left = _lane_left((tb, LANES))
    left2 = jnp.concatenate([left, left], axis=1)

    def stages():
        qs = range(4)
        sides = range(2)

        def halves(fn):
            out = [None] * 4
            for group in ((0, 1), (2, 3)):
                for q in group:
                    out[q] = fn(q)
                yield
            return out

        def split3(x):
            x1 = x.astype(BF16)
            rem = x - x1.astype(F32)
            x2 = rem.astype(BF16)
            return x1, x2, (rem - x2.astype(F32)).astype(BF16)

        def cum_of(q):
            l = split3(lw_ref[0, q])
            return _dot(cum_lhs, l[0]) + (_dot(cum_lhs, l[1]) + _dot(cum_lhs, l[2]))
        cums = yield from halves(cum_of)
        al, rt, vs, rhs, lhs, bbt, kbt, pc_rows = [], [], [], [], [], [], [], []
        for q in qs:
            r, lw, k, v, kk, b = (ref[0, q] for ref in (r_ref, lw_ref, k_ref, v_ref, kk_ref, b_ref))
            cs, cse = cums[q][:tb], cums[q][tb:]
            p_inv = jnp.exp(-cs)
            p_end = jnp.exp(cse - cs)
            al.append(-kk * jnp.exp(cs - lw))
            rt.append(r * jnp.exp(cs))
            vs.append(_split2(v))
            rhs.append(_split2(jnp.concatenate([b * p_inv, k * p_inv], axis=0)))
            zero = jnp.zeros_like(v)
            half = lambda z, s: jnp.where(left, z, zero) if s == 0 else jnp.where(left, zero, z)
            lhs.append([_split2(jnp.concatenate([half(al[q], s), half(rt[q], s)], axis=0)) for s in sides])
            bbt.append(_split2((b * p_end).T))
            kbt.append(_split2((k * p_end).T))
            pc_rows.append(jnp.exp(cse))
        gram = yield from halves(lambda q: [
            _dot_nt(lhs[q][s][0], rhs[q][0]) + (_dot_nt(lhs[q][s][0], rhs[q][1]) + _dot_nt(lhs[q][s][1], rhs[q][0]))
            for s in sides])
        a1 =[[_split2(jnp.where(strict, gram[q][s][:tb, :tb], 0.0)) for s in sides] for q in qs]
        a2 = [[_split2(jnp.where(strict, gram[q][s][:tb, tb:], 0.0)) for s in sides] for q in qs]
        b1 = [[_split2(jnp.where(incl, gram[q][s][tb:, :tb], 0.0)) for s in sides] for q in qs]
        b2 = [[_split2(jnp.where(incl, gram[q][s][tb:, tb:], 0.0)) for s in sides] for q in qs]
        xv_pw = yield from halves(lambda q: (
            jnp.where(left, _dot3(a2[q][0], vs[q]), _dot3(a2[q][1], vs[q])),
            [_dot3(a1[q][s], a1[q][s]) for s in sides]))
        x = [jnp.concatenate([al[q], xv_pw[q][0]], axis=1) for q in qs]
        pw = [xv_pw[q][1] for q in qs]
        m = a1
        for step in range(4):
            def apply(q, m=m, pw=pw, x=x, step=step):
                m_new = [_split2(pw[q][s]) for s in sides] if step < 3 else None
                pw_new = [_dot3(m_new[s], m_new[s]) for s in sides] if step < 2 else None
                xq = _split2(x[q])
                x_new = x[q] + jnp.where(left2, _dot3(m[q][0], xq), _dot3(m[q][1], xq))
                return x_new, m_new, pw_new
            res = yield from halves(apply)
            x = [res[q][0] for q in qs]
            m = [res[q][1] for q in qs]
            pw = [res[q][2] for q in qs]
        xs = [_split2(x[q]) for q in qs]
        bxv = yield from halves(lambda q: (
            jnp.where(left2, _dot(b1[q][0][0], xs[q][0]), _dot(b1[q][1][0], xs[q][0])),
            jnp.where(left, _dot(b2[q][0][0], vs[q][0]), _dot(b2[q][1][0], vs[q][0]))))

        def store(q):
            bx, bv = bxv[q]
            ah = x[q][:, :LANES]
            rh = rt[q] + bx[:, :LANES]
            u0_scr[q] = x[q][:, LANES:]
            y0_scr[q] = bx[:, LANES:] + bv
            pct = pc_rows[q].T
            for c in range(nc):
                rows = slice(c * c_sz, (c + 1) * c_sz)
                l1 = _split2(jnp.concatenate([ah[rows], rh[rows]], axis=0))
                l1hi_scr[q, c], l1lo_scr[q, c] = l1
                l2hi_scr[q, c] = jnp.concatenate([bbt[q][0][:, rows], kbt[q][0][:, rows]], axis=1)
                l2lo_scr[q, c] = jnp.concatenate([bbt[q][1][:, rows], kbt[q][1][:, rows]], axis=1)
                pcol_scr[q, c] = jnp.broadcast_to(pct[:, c * c_sz:c * c_sz + 1], (LANES, LANES))
        yield from halves(store)

    yield from stages()


def _rwkv_sequential_steps(v_ref, y_ref, st, l1hi_scr, l1lo_scr, l2hi_scr, l2lo_scr, pcol_scr, u0_scr, y0_scr,
                           *, tb):
    c_sz = RWKV_CHUNK
    r128 = lax.broadcasted_iota(I32, (LANES, LANES), 0)
    c128 = lax.broadcasted_iota(I32, (LANES, LANES), 1)
    blockdiag = (r128 < HEAD_DIM) == (c128 < HEAD_DIM)
    for c in range(tb // c_sz):
        rows = slice(c * c_sz, (c + 1) * c_sz)
        res = [_dot3((l1hi_scr[p, c], l1lo_scr[p, c]), _split2(st[p])) for p in range(4)]
        yield
        for p in range(4):
            y_ref[0, p, rows, :] = res[p][c_sz:] + y0_scr[p, rows, :]
        w2 = [_split2(jnp.concatenate([res[p][:c_sz] + u0_scr[p, rows, :], v_ref[0, p, rows, :]], axis=0))
              for p in range(4)]
        upd = [_dot3((l2hi_scr[p, c], l2lo_scr[p, c]), w2[p]) for p in range(4)]
        for p in range(4):
            st[p] = st[p] * pcol_scr[p, c] + jnp.where(blockdiag, upd[p], 0.0)
        yield


def _interleave(*streams):
    live = list(streams)
    while live:
        for g in list(live):
            try:
                next(g)
            except StopIteration:
                live.remove(g)


def _state_to_blockdiag(s):
    b = s.shape[0]
    st = jnp.swapaxes(s, -1, -2).reshape(b, 4, 2, HEAD_DIM, HEAD_DIM)
    z = jnp.zeros_like(st[:, :, 0])
    top = jnp.concatenate([st[:, :, 0], z], axis=-1)
    bot = jnp.concatenate([z, st[:, :, 1]], axis=-1)
    return jnp.concatenate([top, bot], axis=-2)


def _state_from_blockdiag(s_bd):
    b = s_bd.shape[0]
    a = s_bd[:, :, :HEAD_DIM, :HEAD_DIM]
    d = s_bd[:, :, HEAD_DIM:, HEAD_DIM:]
    st = jnp.stack([a, d], axis=2).reshape(b, 8, HEAD_DIM, HEAD_DIM)
    return jnp.swapaxes(st, -1, -2)


def _rwkv_post_kernel(y_ref, bonus_ref, g_ref, lg_ref, lb_ref, o_ref):
    for p in range(4):
        sl = slice(p * LANES, (p + 1) * LANES)
        y = y_ref[0, p]
        d = y - _segsum64(y) * (1.0 / HEAD_DIM)
        var = _segsum64(d * d) * (1.0 / HEAD_DIM)
        yn = d * lax.rsqrt(var + RWKV_GN_EPS) * lg_ref[:, sl] + lb_ref[:, sl]
        o_ref[0, :, sl] = ((yn + bonus_ref[0, :, sl]) * g_ref[0, :, sl]).astype(o_ref.dtype)


def _rwkv_kernel(pb_ref, pbprev_ref, shift_ref, mu_ref, w0_ref, wup_ref, a0_ref, aup_ref, gup_ref,
                 kk_ref, ka_ref, rk_ref, lg_ref, lb_ref, s0_ref, ob_ref, sT_ref,
                 tok_s, g_s, bonus_s, y_s, st_scr, *op_scr, tb):
    i = pl.program_id(1)

    @pl.when(i == 0)
    def _():
        st_scr[...] = s0_ref[0]

    tok = [tok_s.at[j] for j in range(6)]
    _rwkv_prep_kernel(pb_ref, pbprev_ref, shift_ref, mu_ref, w0_ref, wup_ref, a0_ref, aup_ref, gup_ref,
                      kk_ref, ka_ref, rk_ref, *tok, g_s, bonus_s)
    _interleave(_rwkv_operator_stages(*tok, *op_scr, tb=tb))
    st = [st_scr[p] for p in range(4)]
    _interleave(_rwkv_sequential_steps(tok[3], y_s, st, *op_scr, tb=tb))
    for p in range(4):
        st_scr[p] = st[p]
    _rwkv_post_kernel(y_s, bonus_s, g_s, lg_ref, lb_ref, ob_ref)

    @pl.when(i == pl.num_programs(1) - 1)
    def _():
        sT_ref[0] = st_scr[...]


def _rwkv(pb, shift, mu, w0, wup, a0, aup, gup, k_k, k_a, r_k, lnx_g, lnx_b, s0_bd, tb):
    b, t, c = pb.shape
    cur = lambda bi, i: (bi, i, 0)
    vec = lambda n: pl.BlockSpec((1, n), lambda bi, i: (0, 0))
    lora = pl.BlockSpec((LANES, 512), lambda bi, i: (0, 0))
    st_spec = pl.BlockSpec((1, 4, LANES, LANES), lambda bi, i: (bi, 0, 0, 0))
    return pl.pallas_call(
        functools.partial(_rwkv_kernel, tb=tb),
        grid=(b, t // tb),
        in_specs=[pl.BlockSpec((1, tb, c), cur),
                  pl.BlockSpec((1, 8, c), lambda bi, i: (bi, jnp.maximum(i * (tb // 8) - 1, 0), 0)),
                  pl.BlockSpec((1, 1, c), lambda bi, i: (bi, 0, 0)),
                  vec(c), vec(512), lora, vec(512), lora, lora, vec(512), vec(512), vec(512),
                  vec(512), vec(512), st_spec],
        out_specs=[pl.BlockSpec((1, tb, 512), cur), st_spec],
        out_shape=[jax.ShapeDtypeStruct((b, t, 512), BF16),
                   jax.ShapeDtypeStruct((b, 4, LANES, LANES), F32)],
        scratch_shapes=([pltpu.VMEM((6, 1, 4, tb, LANES), F32)] + [pltpu.VMEM((1, tb, 512), F32)] * 2
                        + [pltpu.VMEM((1, 4, tb, LANES), F32), pltpu.VMEM((4, LANES, LANES), F32)]
                        + _rwkv_operator_scratch(tb)),
        compiler_params=_cparams(("parallel", "arbitrary")),
        name="rwkv",
    )(pb, pb, shift, mu, w0, wup, a0, aup, gup, k_k, k_a, r_k, lnx_g, lnx_b, s0_bd)


def _layer_norm(x, g, b):
    mu = jnp.mean(x, -1, keepdims=True)
    d = x - mu
    var = jnp.mean(d * d, -1, keepdims=True)
    return d * lax.rsqrt(var + LN_EPS) * g + b


def _ffn_kernel(x_ref, m1_ref, m2_ref, wo_ref, g1_ref, b1_ref, wu_ref, wd_ref, g2_ref, b2_ref, o_ref,
                *, alpha, ff_chunk):
    half = m1_ref.shape[1]
    mix = _dot(m1_ref[...], wo_ref[0:half, :]) + _dot(m2_ref[...], wo_ref[half:, :])
    x1 = _layer_norm(alpha * x_ref[...] + mix, g1_ref[...], b1_ref[...])
    x1b = x1.astype(BF16)
    d_ff = wu_ref.shape[1]
    h = jnp.zeros_like(x1)
    for c in range(d_ff // ff_chunk):
        u = jnp.maximum(_dot(x1b, wu_ref[:, c * ff_chunk:(c + 1) * ff_chunk]), 0.0)
        h = h + _dot((u * u).astype(BF16), wd_ref[c * ff_chunk:(c + 1) * ff_chunk, :])
    o_ref[...] = _layer_norm(alpha * x1 + h, g2_ref[...], b2_ref[...])


def _const_spec(shape):
    return pl.BlockSpec(shape, lambda i: (0,) * len(shape), pipeline_mode=pl.Buffered(1))


def _out_ffn(x, m1, m2, wo, g1, b1, wu, wd, g2, b2, alpha, tm):
    n, d = x.shape
    dm = m1.shape[1]
    d_ff = wu.shape[1]
    tile = lambda w: pl.BlockSpec((tm, w), lambda i: (i, 0))
    return pl.pallas_call(
        functools.partial(_ffn_kernel, alpha=alpha, ff_chunk=1024),
        grid=(n // tm,),
        in_specs=[tile(d), tile(dm), tile(dm),
                  _const_spec((2 * dm, d)), _const_spec((1, d)), _const_spec((1, d)),
                  _const_spec((d, d_ff)), _const_spec((d_ff, d)), _const_spec((1, d)), _const_spec((1, d))],
        out_specs=tile(d),
        out_shape=jax.ShapeDtypeStruct((n, d), F32),
        compiler_params=_cparams(("parallel",)),
        name="out_ffn",
    )(x, m1, m2, wo, g1, b1, wu, wd, g2, b2)


def _diff_lambda(lam_ref, lam_init):
    lv = lam_ref[...]
    s1 = jnp.sum(lv[0:1] * lv[1:2], axis=-1, keepdims=True)
    s2 = jnp.sum(lv[2:3] * lv[3:4], axis=-1, keepdims=True)
    return jnp.exp(s1) - jnp.exp(s2) + lam_init


def _split_halves(qh):
    left = _lane_left(qh.shape)
    zero = jnp.zeros_like(qh)
    return jnp.concatenate([jnp.where(left, qh, zero), jnp.where(left, zero, qh)], axis=0)


def _flash_update(carry, s3, vblk):
    m, l, acc = carry
    g, rws, kb = s3.shape
    m_new = jnp.maximum(m, jnp.max(s3, -1, keepdims=True))
    m_use = jnp.where(m_new == NEG_INF, 0.0, m_new)
    corr = jnp.exp2(m - m_use)
    p = jnp.exp2(s3 - m_use)
    l = l * corr + jnp.sum(p, -1, keepdims=True)
    pv = _dot(p.reshape(g * rws, kb).astype(BF16), vblk).reshape(g, rws, vblk.shape[1])
    return m_new, l, acc * corr + pv


def _flash_init(g, rws, e):
    return (jnp.full((g, rws, 1), NEG_INF, F32), jnp.zeros((g, rws, 1), F32), jnp.zeros((g, rws, e), F32))


def _diff_finish(carry, lam, gain, lam_init):
    m, l, acc = carry
    o = acc[0] / l[0] - lam * (acc[1] / l[1])
    o = o * lax.rsqrt(jnp.mean(o * o, -1, keepdims=True) + LN_EPS) * gain * (1.0 - lam_init)
    return o


def _diff_prompt_kernel(q_ref, k_ref, vt_ref, lam_ref, gain_ref, o_ref, q2_scr, *, qb, slopes, lam_init):
    i = pl.program_id(1)
    lam = _diff_lambda(lam_ref, lam_init)
    qpos = i * qb + lax.broadcasted_iota(I32, (1, qb), 1)
    for h in range(4):
        q2_scr[2 * h * qb:(2 * h + 2) * qb, :LANES] = _split_halves(q_ref[0, :, h * LANES:(h + 1) * LANES])
        q2_scr[2 * h * qb:(2 * h + 2) * qb, LANES:] = _alibi_q_features(2 * qb, slopes[h])

    def body(kb, carry, diagonal):
        m, l, acc = carry
        rows = pl.ds(pl.multiple_of(kb * qb, qb), qb)
        kpos = kb * qb + lax.broadcasted_iota(I32, (qb, 1), 0)
        kfeat = _alibi_k_features(kpos)
        st = [_dot_nt(jnp.concatenate([k_ref[0, rows, h * LANES:(h + 1) * LANES], kfeat], axis=1),
                      q2_scr[2 * h * qb:(2 * h + 2) * qb, :])
              for h in range(4)]
        part = lambda g: st[g // 2][:, (g % 2) * qb:(g % 2 + 1) * qb]
        if diagonal:
            ahead = jnp.maximum(kpos - qpos, 0).astype(F32)
            neg = jnp.where(kpos // CHUNK <= qpos // CHUNK, 0.0, NEG_INF)
            s3 = jnp.stack([part(g) - (2.0 * slopes[g // 2]) * ahead + neg for g in range(8)], axis=0)
        else:
            s3 = jnp.stack([part(g) for g in range(8)], axis=0)
        m_new = jnp.maximum(m, jnp.max(s3, axis=1, keepdims=True))
        corr = jnp.exp2(m - m_new)
        p = jnp.exp2(s3 - m_new)
        l = l * corr + jnp.sum(p, axis=1, keepdims=True)
        pv = [_dot(vt_ref[0, kb, h * LANES:(h + 1) * LANES, :],
                   jnp.concatenate([p[2 * h].astype(BF16), p[2 * h + 1].astype(BF16)], axis=1))
              for h in range(4)]
        acc = acc * corr + jnp.stack([pv[g // 2][:, (g % 2) * qb:(g % 2 + 1) * qb] for g in range(8)], axis=0)
        return m_new, l, acc

    init = (jnp.full((8, 1, qb), NEG_INF, F32), jnp.zeros((8, 1, qb), F32), jnp.zeros((8, LANES, qb), F32))
    carry = lax.fori_loop(0, i, functools.partial(body, diagonal=False), init)
    m, l, acc = body(i, carry, diagonal=True)
    for h in range(4):
        o = acc[2 * h] / l[2 * h] - lam * (acc[2 * h + 1] / l[2 * h + 1])
        o = o * lax.rsqrt(jnp.mean(o * o, axis=0, keepdims=True) + LN_EPS) * gain_ref[...] * (1.0 - lam_init)
        o_ref[0, :, h * LANES:(h + 1) * LANES] = o.T.astype(o_ref.dtype)


def _diff_prompt(q, k, v, lamv, gain, lam_init, qb):
    b, t, _ = q.shape
    nkb = t // qb
    vt = jnp.swapaxes(v.reshape(b, nkb, qb, 512), -1, -2)
    full = pl.BlockSpec((1, t, 512), lambda bi, i: (bi, 0, 0))
    tile = pl.BlockSpec((1, qb, 512), lambda bi, i: (bi, i, 0))
    return pl.pallas_call(
        functools.partial(_diff_prompt_kernel, qb=qb, slopes=_alibi(4), lam_init=lam_init),
        grid=(b, nkb),
        in_specs=[tile, full,
                  pl.BlockSpec((1, nkb, 512, qb), lambda bi, i: (bi, 0, 0, 0)),
                  pl.BlockSpec((4, HEAD_DIM), lambda bi, i: (0, 0)),
                  pl.BlockSpec((LANES, 1), lambda bi, i: (0, 0))],
        out_specs=tile,
        out_shape=jax.ShapeDtypeStruct((b, t, 512), BF16),
        scratch_shapes=[pltpu.VMEM((8 * qb, 2 * LANES), BF16)],
        compiler_params=_cparams(("parallel", "arbitrary")),
        name="diff_prompt",
    )(q, k, vt, lamv, gain.reshape(LANES, 1))


def _diff_sample_kernel(q_ref, kc_ref, vc_ref, kn_ref, vn_ref, lam_ref, gain_ref, o_ref,
                        m_scr, l_scr, acc_scr, *, t, kb_size, n_past, slopes, lam_init):
    j = pl.program_id(1)
    past_len = n_past * kb_size
    qpos = past_len + lax.broadcasted_iota(I32, (t, 1), 0)

    @pl.when(j == 0)
    def _():
        m_scr[...] = jnp.full(m_scr.shape, NEG_INF, F32)
        l_scr[...] = jnp.zeros(l_scr.shape, F32)
        acc_scr[...] = jnp.zeros(acc_scr.shape, F32)

    def run(kfull, v_of, kpos):
        dist = jnp.abs(qpos - kpos).astype(F32)
        nk = kfull.shape[0]
        s3 = jnp.stack(
            [_dot_nt(_split_halves(q_ref[0, :, h * LANES:(h + 1) * LANES]),
                     kfull[:, h * LANES:(h + 1) * LANES]).reshape(2, t, nk) - slopes[h] * dist[None]
             for h in range(4)], axis=0)
        m = m_scr[...]
        m_new = jnp.maximum(m, jnp.max(s3, -1, keepdims=True))
        corr = jnp.exp2(m - m_new)
        p = jnp.exp2(s3 - m_new)
        l_scr[...] = l_scr[...] * corr + jnp.sum(p, -1, keepdims=True)
        pv = jnp.stack([_dot(p[h].reshape(2 * t, nk).astype(BF16), v_of(h)).reshape(2, t, LANES)
                        for h in range(4)], axis=0)
        acc_scr[...] = acc_scr[...] * corr + pv
        m_scr[...] = m_new

    @pl.when(j < n_past)
    def _():
        kpos = j * kb_size + lax.broadcasted_iota(I32, (1, kb_size), 1)
        run(kc_ref[0].astype(BF16), lambda h: vc_ref[0, :, h * LANES:(h + 1) * LANES].astype(BF16), kpos)

    @pl.when(j == n_past)
    def _():
        kpos = past_len + lax.broadcasted_iota(I32, (1, t), 1)
        run(kn_ref[0].astype(BF16), lambda h: vn_ref[0, :, h * LANES:(h + 1) * LANES].astype(BF16), kpos)
        lam = _diff_lambda(lam_ref, lam_init)
        for h in range(4):
            sl = slice(h * LANES, (h + 1) * LANES)
            carry = (m_scr[h], l_scr[h], acc_scr[h])
            o_ref[0, :, sl] = _diff_finish(carry, lam, gain_ref[...], lam_init).astype(o_ref.dtype)


def _diff_sample(q, k_cache, v_cache, k_new, v_new, lamv, gain, lam_init, kb_size):
    b, t, _ = q.shape
    n_past = k_cache.shape[1] // kb_size
    cache = pl.BlockSpec((1, kb_size, 512), lambda bi, j: (bi, jnp.minimum(j, n_past - 1), 0))
    new = pl.BlockSpec((1, t, 512), lambda bi, j: (bi, 0, 0))
    return pl.pallas_call(
        functools.partial(_diff_sample_kernel, t=t, kb_size=kb_size, n_past=n_past,
                          slopes=_alibi(4), lam_init=lam_init),
        grid=(b, n_past + 1),
        in_specs=[new, cache, cache, new, new,
                  pl.BlockSpec((4, HEAD_DIM), lambda bi, j: (0, 0)),
                  pl.BlockSpec((1, LANES), lambda bi, j: (0, 0))],
        out_specs=new,
        out_shape=jax.ShapeDtypeStruct((b, t, 512), BF16),
        scratch_shapes=[pltpu.VMEM((4, 2, t, 1), F32), pltpu.VMEM((4, 2, t, 1), F32),
                        pltpu.VMEM((4, 2, t, LANES), F32)],
        compiler_params=_cparams(("parallel", "arbitrary")),
        name="diff_sample",
    )(q, k_cache, v_cache, k_new, v_new, lamv, gain)


def _index_keys(qi, wi, ki_blk, allowed):
    acc = None
    for h in range(8):
        sc = _dot_nt(qi[:, h * HEAD_DIM:(h + 1) * HEAD_DIM], ki_blk)
        term = jnp.maximum(sc, 0.0) * wi[:, h:h + 1]
        acc = term if acc is None else acc + term
    idx = acc * (8.0 ** -0.5) + 0.0
    bits = pltpu.bitcast(idx, I32)
    key = jnp.where(bits < 0, bits ^ 0x7FFFFFFF, bits)
    return jnp.where(allowed, key, INT_MIN)


def _lane_fold(x):
    acc = x[:, :LANES]
    for j in range(1, x.shape[1] // LANES):
        acc = acc + x[:, j * LANES:(j + 1) * LANES]
    return acc


def _ones_where(pred):
    return jnp.where(pred, 1, 0).astype(I32)


def _kth_largest(count_ge, shape, n_sel):
    def body(it, lo):
        bit = 31 - it
        cand = lo + jnp.left_shift(jnp.int32(1), bit)
        return jnp.where(count_ge(cand) >= n_sel, cand, lo)
    return lax.fori_loop(0, 32, body, jnp.full(shape, INT_MIN, I32))


def _tie_position(count_eq_le, need, shape, nbits):
    def body(it, lo):
        bit = nbits - 1 - it
        cand = lo + jnp.left_shift(jnp.int32(1), bit)
        return jnp.where(count_eq_le(cand - 1) < need, cand, lo)
    return lax.fori_loop(0, nbits, body, jnp.zeros(shape, I32))


def _select_threshold(count_over, shape, n_sel, pos_bits):
    thr = _kth_largest(lambda cand: count_over(lambda key, kpos: key >= cand), shape, n_sel)
    need = n_sel - count_over(lambda key, kpos: key > thr)
    n_eq = count_over(lambda key, kpos: key == thr)
    has_tie = jnp.max(jnp.where((n_eq > need) & (thr > INT_MIN), 1, 0)) > 0
    pos_thr = lax.cond(
        has_tie,
        lambda: _tie_position(
            lambda p: count_over(lambda key, kpos: (key == thr) & (kpos <= p)), need, shape, pos_bits),
        lambda: jnp.full(shape, 2 ** 30, I32))
    return thr, pos_thr


def _dsa_attend(carry, q_ref, kblk, vblk, dist, sel, slopes):
    neg = jnp.where(sel, 0.0, NEG_INF)
    qs = []
    for s in range(4):
        slab = q_ref[0, :, s * LANES:(s + 1) * LANES]
        left = _lane_left(slab.shape)
        zero = jnp.zeros_like(slab)
        qs.append((jnp.where(left, slab, zero), jnp.where(left, zero, slab)))
    q_all = jnp.concatenate([qs[h % 4][h // 4] for h in range(8)], axis=0)
    r, kb = dist.shape
    s3 = _dot_nt(q_all, kblk).reshape(8, r, kb)
    s3 = jnp.stack([s3[h] - slopes[h] * dist + neg for h in range(8)], axis=0)
    return _flash_update(carry, s3, vblk)


def _dsa_finish(carry, o_ref):
    m, l, acc = carry
    o = acc / l
    for s in range(4):
        left = _lane_left(o[s].shape)
        o_ref[0, :, s * LANES:(s + 1) * LANES] = jnp.where(left, o[s], o[4 + s]).astype(o_ref.dtype)


def _dsa_prompt_kernel(q_ref, qi_ref, wi_ref, k_ref, vt_ref, ki_ref, o_ref, key_scr, qall_scr, qiall_scr,
                       *, qb, n_sel, slopes):
    i = pl.program_id(1)
    nkb = i + 1
    qpos = i * qb + lax.broadcasted_iota(I32, (1, qb), 1)

    for s in range(4):
        slab = q_ref[0, :, s * LANES:(s + 1) * LANES]
        left = _lane_left(slab.shape)
        zero = jnp.zeros_like(slab)
        qall_scr[s * qb:(s + 1) * qb, :LANES] = jnp.where(left, slab, zero)
        qall_scr[(4 + s) * qb:(5 + s) * qb, :LANES] = jnp.where(left, zero, slab)
    for h in range(8):
        qall_scr[h * qb:(h + 1) * qb, LANES:] = _alibi_q_features(qb, slopes[h])
        qiall_scr[h * qb:(h + 1) * qb, :] = qi_ref[0, :, h * HEAD_DIM:(h + 1) * HEAD_DIM]
    wit = wi_ref[0].T

    def kpos_of(kb):
        return kb * qb + lax.broadcasted_iota(I32, (qb, 1), 0)

    def rows_of(kb):
        return pl.ds(pl.multiple_of(kb * qb, qb), qb)

    def fill(kb, c):
        allowed = kpos_of(kb) // CHUNK <= qpos // CHUNK
        sc = _dot_nt(ki_ref[0, rows_of(kb), :], qiall_scr[...])
        acc = None
        for h in range(8):
            term = jnp.maximum(sc[:, h * qb:(h + 1) * qb], 0.0) * wit[h:h + 1, :]
            acc = term if acc is None else acc + term
        idx = acc * (8.0 ** -0.5) + 0.0
        bits = pltpu.bitcast(idx, I32)
        key = jnp.where(bits < 0, bits ^ 0x7FFFFFFF, bits)
        key_scr[kb] = jnp.where(allowed, key, INT_MIN)
        return c
    lax.fori_loop(0, nkb, fill, 0)

    def count_over(pred_fn):
        def body(kb, c):
            ones = _ones_where(pred_fn(key_scr[kb], kpos_of(kb)))
            return c + jnp.sum(ones.reshape(qb // 8, 8, qb), axis=0)
        part = lax.fori_loop(0, nkb, body, jnp.zeros((8, qb), I32))
        return jnp.sum(part, axis=0, keepdims=True)

    thr, pos_thr = _select_threshold(count_over, (1, qb), n_sel, 12)

    def attend(kb, carry, diagonal):
        m, l, acc = carry
        kpos = kpos_of(kb)
        key = key_scr[kb]
        sel = (key > thr) | ((key == thr) & (kpos <= pos_thr))
        if diagonal:
            sel = sel & (kpos // CHUNK <= qpos // CHUNK)
        neg = jnp.where(sel, 0.0, NEG_INF)
        k_aug = jnp.concatenate([k_ref[0, rows_of(kb), :], _alibi_k_features(kpos)], axis=1)
        st = _dot_nt(k_aug, qall_scr[...])
        if diagonal:
            ahead = jnp.maximum(kpos - qpos, 0).astype(F32)
            s3 = jnp.stack([st[:, h * qb:(h + 1) * qb] - (2.0 * slopes[h]) * ahead + neg for h in range(8)],
                           axis=0)
        else:
            s3 = jnp.stack([st[:, h * qb:(h + 1) * qb] + neg for h in range(8)], axis=0)
        m_new = jnp.maximum(m, jnp.max(s3, axis=1, keepdims=True))
        m_use = jnp.where(m_new == NEG_INF, 0.0, m_new)
        corr = jnp.exp2(m - m_use)
        p = jnp.exp2(s3 - m_use)
        l = l * corr + jnp.sum(p, axis=1, keepdims=True)
        p_all = jnp.concatenate([p[h].astype(BF16) for h in range(8)], axis=1)
        pv = _dot(vt_ref[0, kb], p_all)
        acc = acc * corr + jnp.stack([pv[:, h * qb:(h + 1) * qb] for h in range(8)], axis=0)
        return m_new, l, acc

    init = (jnp.full((8, 1, qb), NEG_INF, F32), jnp.zeros((8, 1, qb), F32), jnp.zeros((8, LANES, qb), F32))
    carry = lax.fori_loop(0, i, functools.partial(attend, diagonal=False), init)
    m, l, acc = attend(i, carry, diagonal=True)
    ot = acc / l
    top = lax.broadcasted_iota(I32, (LANES, qb), 0) < HEAD_DIM
    for s in range(4):
        o_ref[0, :, s * LANES:(s + 1) * LANES] = jnp.where(top, ot[s], ot[4 + s]).T.astype(o_ref.dtype)


def _dsa_prompt(q, qi, wi, k, v, ki, qb):
    b, t, _ = q.shape
    n_sel = min(TOPK_MAX, t // 4)
    nkb = t // qb
    vt = jnp.swapaxes(v.reshape(b, nkb, qb, LANES), -1, -2)
    tile = lambda w: pl.BlockSpec((1, qb, w), lambda bi, i: (bi, i, 0))
    full = lambda w: pl.BlockSpec((1, t, w), lambda bi, i: (bi, 0, 0))
    return pl.pallas_call(
        functools.partial(_dsa_prompt_kernel, qb=qb, n_sel=n_sel, slopes=_alibi(8)),
        grid=(b, nkb),
        in_specs=[tile(512), tile(512), tile(LANES), full(LANES),
                  pl.BlockSpec((1, nkb, LANES, qb), lambda bi, i: (bi, 0, 0, 0)), full(HEAD_DIM)],
        out_specs=tile(512),
        out_shape=jax.ShapeDtypeStruct((b, t, 512), BF16),
        scratch_shapes=[pltpu.VMEM((nkb, qb, qb), I32), pltpu.VMEM((8 * qb, 2 * LANES), BF16),
                        pltpu.VMEM((8 * qb, HEAD_DIM), BF16)],
        compiler_params=_cparams(("parallel", "arbitrary")),
        name="dsa_prompt",
    )(q, qi, wi, k, vt, ki)


def _dsa_sample_kernel(q_ref, qi_ref, wi_ref, kc_ref, vc_ref, kic_ref, kn_ref, vn_ref, kin_ref, o_ref,
                       key_scr, keyn_scr, *, t, kb_size, n_past, n_sel, slopes):
    past_len = n_past * kb_size
    qpos = past_len + lax.broadcasted_iota(I32, (t, 1), 0)
    qi = qi_ref[0]
    wi = wi_ref[0]
    kpos_n = past_len + lax.broadcasted_iota(I32, (1, t), 1)

    def kpos_of(kb):
        return kb * kb_size + lax.broadcasted_iota(I32, (1, kb_size), 1)

    for kb in range(n_past):
        ki_blk = kic_ref[0, kb * kb_size:(kb + 1) * kb_size, :].astype(BF16)
        key_scr[kb] = _index_keys(qi, wi, ki_blk, True)
    keyn_scr[...] = _index_keys(qi, wi, kin_ref[0].astype(BF16), True)

    def count_over(pred_fn):
        part = _lane_fold(_ones_where(pred_fn(key_scr[0], kpos_of(0))))
        for kb in range(1, n_past):
            part = part + _lane_fold(_ones_where(pred_fn(key_scr[kb], kpos_of(kb))))
        new = jnp.sum(_ones_where(pred_fn(keyn_scr[...], kpos_n)), axis=-1, keepdims=True)
        return jnp.sum(part, axis=-1, keepdims=True) + new

    thr, pos_thr = _select_threshold(count_over, (t, 1), n_sel, 13)
    carry = _flash_init(8, t, LANES)

    def sel_of(key, kpos):
        return (key > thr) | ((key == thr) & (kpos <= pos_thr))

    for kb in range(n_past):
        kpos = kpos_of(kb)
        dist = jnp.abs(qpos - kpos).astype(F32)
        kblk = kc_ref[0, kb * kb_size:(kb + 1) * kb_size, :].astype(BF16)
        vblk = vc_ref[0, kb * kb_size:(kb + 1) * kb_size, :].astype(BF16)
        carry = _dsa_attend(carry, q_ref, kblk, vblk, dist, sel_of(key_scr[kb], kpos), slopes)
    dist = jnp.abs(qpos - kpos_n).astype(F32)
    carry = _dsa_attend(carry, q_ref, kn_ref[0].astype(BF16), vn_ref[0].astype(BF16), dist,
                        sel_of(keyn_scr[...], kpos_n), slopes)
    _dsa_finish(carry, o_ref)


def _dsa_sample(q, qi, wi, k_cache, v_cache, ki_cache, k_new, v_new, ki_new, kb_size):
    b, t, _ = q.shape
    past = k_cache.shape[1]
    n_past = past // kb_size
    n_sel = min(TOPK_MAX, (past + t) // 4)
    per_b = lambda rows, w: pl.BlockSpec((1, rows, w), lambda bi: (bi, 0, 0))
    return pl.pallas_call(
        functools.partial(_dsa_sample_kernel, t=t, kb_size=kb_size, n_past=n_past, n_sel=n_sel,
                          slopes=_alibi(8)),
        grid=(b,),
        in_specs=[per_b(t, 512), per_b(t, 512), per_b(t, LANES),
                  per_b(past, LANES), per_b(past, LANES), per_b(past, HEAD_DIM),
                  per_b(t, LANES), per_b(t, LANES), per_b(t, HEAD_DIM)],
        out_specs=per_b(t, 512),
        out_shape=jax.ShapeDtypeStruct((b, t, 512), BF16),
        scratch_shapes=[pltpu.VMEM((n_past, t, kb_size), I32), pltpu.VMEM((t, t), I32)],
        compiler_params=_cparams(("parallel",)),
        name="dsa_sample",
    )(q, qi, wi, k_cache, v_cache, ki_cache, k_new, v_new, ki_new)


def _pad_cols(w, total):
    return jnp.pad(w, ((0, 0), (0, total - w.shape[1])))


def _q_perm():
    idx = []
    for s in range(4):
        idx.extend(range(s * HEAD_DIM, (s + 1) * HEAD_DIM))
        idx.extend(range((4 + s) * HEAD_DIM, (5 + s) * HEAD_DIM))
    return np.asarray(idx, np.int32)


def _even_layer(xp, xs, cache_k, cache_v, shift_s, wkv_s, w_in, sink, rw, w_out, ffn, alpha, sizes):
    (mu, w0, w_up, a0, a_up, g_up, k_k, k_a, r_k, lnx_g, lnx_b) = rw
    perm = _q_perm()
    w_in_p = jnp.concatenate([w_in[:, :512][:, perm], w_in[:, 512:]], axis=1).astype(BF16)
    w_out_p = jnp.concatenate([w_out[:512][perm], w_out[512:]], axis=0).astype(BF16)
    groups = ((0, 512, ((0, 512, HEAD_DIM ** -0.5 * LOG2E),)),
              (512, 128, ((1, 128, 1.0), (2, 128, 1.0))),
              (640, 128, ((3, 128, 1.0), (4, 128, 1.0))),
              (768, 1792, ((5, 1792, 1.0),)))
    out_defs = ((512, BF16), (128, F32), (128, BF16), (128, F32), (128, BF16), (1792, F32))
    wup_p = jnp.concatenate([w_up, jnp.zeros_like(w_up)], axis=0).astype(BF16)
    aup_p = jnp.concatenate([jnp.zeros_like(a_up), a_up], axis=0).astype(BF16)
    gup = g_up.astype(BF16)
    row = lambda v: v.reshape(1, -1)
    g1, b1, g2, b2, wu, wd = ffn

    def run(x, state, tm, qb, scan_tb):
        b, t, d = x.shape
        n = b * t
        qa, ka, ka_bf, va, va_bf, pb = _project(x.reshape(n, d), w_in_p, groups, out_defs, tm)
        r3 = lambda a: a.reshape(b, t, a.shape[-1])
        if state is None:
            oa = _swa_prompt(r3(qa), r3(ka_bf), r3(va_bf), sink, qb)
            k_buf = r3(ka)[:, -WINDOW:]
            v_buf = r3(va)[:, -WINDOW:]
            shift = jnp.zeros((b, 1, pb.shape[-1]), F32)
            s0 = jnp.zeros((b, 8, HEAD_DIM, HEAD_DIM), F32)
        else:
            ck, cv, shift, s0 = state
            ck2 = ck.reshape(b, WINDOW, LANES)
            cv2 = cv.reshape(b, WINDOW, LANES)
            oa = _swa(r3(qa), ck2, r3(ka_bf), cv2, r3(va_bf), sink,
                      qb=qb, prev_is_cache=True, pos_base=sizes["past"])
            k_buf = jnp.concatenate([ck2, r3(ka)], axis=1)[:, -WINDOW:]
            v_buf = jnp.concatenate([cv2, r3(va)], axis=1)[:, -WINDOW:]
        pb3 = r3(pb)
        ob, s_bd = _rwkv(pb3, shift, row(mu), row(w0), wup_p, row(a0), aup_p, gup, row(k_k), row(k_a),
                         row(r_k), row(lnx_g), row(lnx_b), _state_to_blockdiag(s0), scan_tb)
        s_t = _state_from_blockdiag(s_bd)
        xo = _out_ffn(x.reshape(n, d), oa.reshape(n, 512), ob.reshape(n, 512), w_out_p, row(g1), row(b1),
                      wu.astype(BF16), wd.astype(BF16), row(g2), row(b2), alpha, tm)
        st = (k_buf.reshape(b, WINDOW, 2, HEAD_DIM), v_buf.reshape(b, WINDOW, 2, HEAD_DIM),
              pb3[:, -1:], s_t)
        return xo.reshape(b, t, d), st

    xp2, st_p = run(xp, None, sizes["tm_p"], sizes["swa_qb"], sizes["scan_tb_p"])
    xs2, st_s = run(xs, (cache_k, cache_v, shift_s, wkv_s), sizes["tm_s"], xs.shape[1], xs.shape[1])
    return xp2, xs2, st_p, st_s


def _odd_layer(xp, xs, caches, w_in, lam_vecs, subln_g, w_out, ffn, alpha, lam_init, sizes):
    perm = _q_perm()
    scale = HEAD_DIM ** -0.5 * LOG2E
    w_p = jnp.concatenate([
        w_in[:, :1536], w_in[:, 1536:2048][:, perm], w_in[:, 2048:2816],
        _pad_cols(w_in[:, 2816:2880], LANES), _pad_cols(w_in[:, 2880:2888], LANES)], axis=1).astype(BF16)
    w_out_p = jnp.concatenate([w_out[:512], w_out[512:][perm]], axis=0).astype(BF16)
    groups = ((0, 512, ((0, 512, scale),)),
              (512, 512, ((1, 512, 1.0), (2, 512, 1.0))),
              (1024, 512, ((3, 512, 1.0), (4, 512, 1.0))),
              (1536, 512, ((5, 512, scale),)),
              (2048, 128, ((6, 128, 1.0), (7, 128, 1.0))),
              (2176, 128, ((8, 128, 1.0), (9, 128, 1.0))),
              (2304, 512, ((10, 512, HEAD_DIM ** -0.5),)),
              (2816, 128, ((11, 64, 1.0), (12, 64, 1.0))),
              (2944, 128, ((13, 128, 1.0),)))
    out_defs = ((512, BF16), (512, F32, (4, 2, HEAD_DIM)), (512, BF16), (512, F32, (4, LANES)), (512, BF16),
                (512, BF16),
                (128, F32, (2, HEAD_DIM)), (128, BF16), (128, F32, (2, HEAD_DIM)), (128, BF16), (512, BF16),
                (64, F32), (64, BF16), (128, F32))
    row = lambda v: v.reshape(1, -1)
    g1, b1, g2, b2, wu, wd = ffn
    lamv = jnp.stack(lam_vecs, axis=0)
    gain = row(subln_g)

    def run(x, state, tm):
        b, t, d = x.shape
        n = b * t
        (qc, kc, kc_bf, vc, vc_bf, qd, kd, kd_bf, vd, vd_bf, qi, ki, ki_bf, wi) = _project(
            x.reshape(n, d), w_p, groups, out_defs, tm)
        r3 = lambda a: a.reshape(b, t, a.shape[-1])
        if state is None:
            oc = _diff_prompt(r3(qc), r3(kc_bf), r3(vc_bf), lamv, gain, lam_init, sizes["diff_qb"])
            od = _dsa_prompt(r3(qd), r3(qi), r3(wi), r3(kd_bf), r3(vd_bf), r3(ki_bf), sizes["dsa_qb"])
        else:
            c_k, c_v, d_k, d_v, d_ki = state
            past = c_k.shape[1]
            oc = _diff_sample(r3(qc), c_k.reshape(b, past, 512), c_v.reshape(b, past, 512),
                              r3(kc_bf), r3(vc_bf), lamv, gain, lam_init, sizes["diff_kb_s"])
            od = _dsa_sample(r3(qd), r3(qi), r3(wi), d_k.reshape(b, past, LANES), d_v.reshape(b, past, LANES),
                             d_ki, r3(kd_bf), r3(vd_bf), r3(ki_bf), sizes["dsa_kb_s"])
        xo = _out_ffn(x.reshape(n, d), oc.reshape(n, 512), od.reshape(n, 512), w_out_p, row(g1), row(b1),
                      wu.astype(BF16), wd.astype(BF16), row(g2), row(b2), alpha, tm)
        rows = (kc.reshape(b, t, 4, 2, HEAD_DIM), vc.reshape(b, t, 4, 2 * HEAD_DIM),
                kd.reshape(b, t, 2, HEAD_DIM), vd.reshape(b, t, 2, HEAD_DIM), ki.reshape(b, t, HEAD_DIM))
        return xo.reshape(b, t, d), rows

    xp2, st_p = run(xp, None, sizes["tm_p"])
    xs2, st_s = run(xs, caches, sizes["tm_s"])
    return xp2, xs2, st_p, st_s


def _sizes(xp, xs, past):
    return dict(tm_p=min(512, xp.shape[0] * xp.shape[1]), tm_s=min(512, xs.shape[0] * xs.shape[1]),
                swa_qb=min(256, xp.shape[1]), scan_tb_p=min(128, xp.shape[1]),
                diff_qb=min(256, xp.shape[1]),
                dsa_qb=min(256, xp.shape[1]), diff_kb_s=min(1024, past), dsa_kb_s=min(512, past),
                past=past)


def kernel(x_prompt, x_sample, cache_a_k, cache_a_v, state_b_shift, state_b_wkv, cache_c_k, cache_c_v, cache_d_k, cache_d_v, cache_d_kidx, w_in_even, sink_a, mu_b, w0_b, w_up_b, a0_b, a_up_b, g_up_b, k_k_b, k_a_b, r_k_b, lnx_g_b, lnx_b_b, w_out_even, w_in_odd, lam_q1_c, lam_k1_c, lam_q2_c, lam_k2_c, subln_g_c, w_out_odd, ln_mix_g, ln_mix_b, ln_ffn_g, ln_ffn_b, w_ff_up, w_ff_down):
    depth = ln_mix_g.shape[0]
    alpha = (2 * depth) ** 0.25
    past = cache_c_k.shape[2]
    sizes = _sizes(x_prompt, x_sample, past)
    xp, xs = x_prompt, x_sample
    even_p, even_s, odd_p, odd_s = [], [], [], []
    for layer in range(depth):
        i = layer // 2
        ffn = (ln_mix_g[layer], ln_mix_b[layer], ln_ffn_g[layer], ln_ffn_b[layer],
               w_ff_up[layer], w_ff_down[layer])
        if layer % 2 == 0:
            rw = (mu_b[i], w0_b[i], w_up_b[i], a0_b[i], a_up_b[i], g_up_b[i], k_k_b[i], k_a_b[i],
                  r_k_b[i].reshape(-1), lnx_g_b[i], lnx_b_b[i])
            xp, xs, st_p, st_s = _even_layer(
                xp, xs, cache_a_k[i], cache_a_v[i], state_b_shift[i], state_b_wkv[i],
                w_in_even[i], sink_a[i], rw, w_out_even[i], ffn, alpha, sizes)
            even_p.append(st_p)
            even_s.append(st_s)
        else:
            lam_init = 0.8 - 0.6 * math.exp(-0.3 * layer)
            xp, xs, st_p, st_s = _odd_layer(
                xp, xs, (cache_c_k[i], cache_c_v[i], cache_d_k[i], cache_d_v[i], cache_d_kidx[i]),
                w_in_odd[i], (lam_q1_c[i], lam_k1_c[i], lam_q2_c[i], lam_k2_c[i]), subln_g_c[i],
                w_out_odd[i], ffn, alpha, lam_init, sizes)
            odd_p.append(st_p)
            odd_s.append(st_s)
    stack = lambda states: [jnp.stack(z, axis=0) for z in zip(*states)]
    a_k_p, a_v_p, b_shift_p, b_wkv_p = stack(even_p)
    a_k_s, a_v_s, b_shift_s, b_wkv_s = stack(even_s)
    c_k_p, c_v_p, d_k_p, d_v_p, d_kidx_p = stack(odd_p)
    c_k_s, c_v_s, d_k_s, d_v_s, d_kidx_s = stack(odd_s)
    return (xp, xs, a_k_p, a_v_p, b_shift_p, b_wkv_p, c_k_p, c_v_p, d_k_p, d_v_p, d_kidx_p,
            a_k_s, a_v_s, b_shift_s, b_wkv_s, c_k_s, c_v_s, d_k_s, d_v_s, d_kidx_s)
```

```python
import functools
import math

import numpy as np
import jax
import jax.numpy as jnp
from jax import lax
from jax.experimental import pallas as pl
from jax.experimental.pallas import tpu as pltpu

F32 = jnp.float32
BF16 = jnp.bfloat16
I32 = jnp.int32

LANES = 128
HEAD_DIM = 64
CHUNK = 64
LN_EPS = 1e-5
RWKV_GN_EPS = 64e-5
WINDOW = 128
TOPK_MAX = 256
NEG_INF = float("-inf")
INT_MIN = -(2 ** 31)
VMEM_LIMIT = 56 * 1024 * 1024


def _cparams(sem):
    return pltpu.CompilerParams(dimension_semantics=sem, vmem_limit_bytes=VMEM_LIMIT)


LOG2E = math.log2(math.e)


def _alibi(n):
    return [float(2.0 ** (-8.0 * (i + 1) / n)) * LOG2E for i in range(n)]


def _bf16_parts(c):
    parts, rem = [], np.float64(c)
    for _ in range(3):
        piece = np.float64(np.asarray(rem, np.float32).astype(BF16).astype(np.float32))
        parts.append(float(piece))
        rem = rem - piece
    return parts


def _alibi_q_features(rows, slope):
    parts = _bf16_parts(slope)
    lane = lax.broadcasted_iota(I32, (rows, LANES), 1)
    out = jnp.zeros((rows, LANES), F32)
    for j, v in enumerate([16.0 * p for p in parts] + parts):
        out = jnp.where(lane == j, v, out)
    return out.astype(BF16)


def _alibi_k_features(kpos):
    lane = lax.broadcasted_iota(I32, (kpos.shape[0], LANES), 1)
    hi = (kpos // 16).astype(F32)
    lo = (kpos % 16).astype(F32)
    return jnp.where(lane < 3, hi, jnp.where(lane < 6, lo, 0.0)).astype(BF16)


def _dot(a, b):
    return jnp.dot(a, b, preferred_element_type=F32)


def _dot_nt(a, b):
    return lax.dot_general(a, b, (((1,), (1,)), ((), ())), preferred_element_type=F32)


def _lane_left(shape):
    return lax.broadcasted_iota(I32, shape, len(shape) - 1) < HEAD_DIM


def _segsum64(x):
    left = _lane_left(x.shape)
    sl = jnp.sum(jnp.where(left, x, 0.0), axis=-1, keepdims=True)
    sr = jnp.sum(jnp.where(left, 0.0, x), axis=-1, keepdims=True)
    return jnp.where(left, sl, sr)


def _proj_kernel(x_ref, w_ref, *out_refs, groups):
    xb = x_ref[...].astype(BF16)
    for off, width, outs in groups:
        r = _dot(xb, w_ref[:, off:off + width])
        for idx, ow, scale in outs:
            v = r if ow == width else r[:, :ow]
            if scale != 1.0:
                v = v * scale
            ref = out_refs[idx]
            if len(ref.shape) == 2:
                ref[...] = v.astype(ref.dtype)
            elif len(ref.shape) == 3:
                for j in range(ref.shape[1]):
                    ref[:, j, :] = v[:, j * ref.shape[2]:(j + 1) * ref.shape[2]].astype(ref.dtype)
            else:
                a, b, ln = ref.shape[1:]
                for j in range(a):
                    for m in range(b):
                        g = j * b + m
                        ref[:, j, m, :] = v[:, g * ln:(g + 1) * ln].astype(ref.dtype)


def _project(x, w_bf, groups, out_defs, tm):
    n, d = x.shape
    cols = w_bf.shape[1]
    out_shape, out_specs = [], []
    for od in out_defs:
        shp = od[2] if len(od) > 2 else (od[0],)
        out_shape.append(jax.ShapeDtypeStruct((n,) + shp, od[1]))
        out_specs.append(pl.BlockSpec((tm,) + shp, lambda i, _k=len(shp): (i,) + (0,) * _k))
    return pl.pallas_call(
        functools.partial(_proj_kernel, groups=groups),
        grid=(n // tm,),
        in_specs=[pl.BlockSpec((tm, d), lambda i: (i, 0)),
                  pl.BlockSpec((d, cols), lambda i: (0, 0))],
        out_specs=out_specs,
        out_shape=out_shape,
        compiler_params=_cparams(("parallel",)),
        name="in_proj",
    )(x, w_bf)


def _swa_kernel(sink_ref, q_ref, kp_ref, kc_ref, vp_ref, vc_ref, o_ref, *, qb, pos_base, slopes):
    i = pl.program_id(1)
    q0 = pos_base + i * qb
    kp = kp_ref[0].astype(BF16)
    kc = kc_ref[0].astype(BF16)
    vp = vp_ref[0].astype(BF16)
    vc = vc_ref[0].astype(BF16)
    qpos = q0 + lax.broadcasted_iota(I32, (qb, 1), 0)
    kpos_p = q0 - WINDOW + lax.broadcasted_iota(I32, (1, WINDOW), 1)
    kpos_c = q0 + lax.broadcasted_iota(I32, (1, qb), 1)
    qch = qpos // CHUNK

    def allowed(kpos):
        kch = kpos // CHUNK
        return (kpos >= 0) & (kch >= qch - WINDOW // CHUNK) & (kch <= qch)

    al_p, al_c = allowed(kpos_p), allowed(kpos_c)
    dist_p = jnp.abs(qpos - kpos_p).astype(F32)
    dist_c = jnp.abs(qpos - kpos_c).astype(F32)
    for s in range(4):
        slab = q_ref[0, :, s * LANES:(s + 1) * LANES]
        left = _lane_left(slab.shape)
        res = []
        for side in range(2):
            h = 4 * side + s
            qx = jnp.where(left if side == 0 else jnp.logical_not(left), slab, jnp.zeros_like(slab))
            lp = jnp.where(al_p, _dot_nt(qx, kp) - slopes[h] * dist_p, NEG_INF)
            lc = jnp.where(al_c, _dot_nt(qx, kc) - slopes[h] * dist_c, NEG_INF)
            sk = sink_ref[h] * LOG2E
            m = jnp.maximum(jnp.maximum(jnp.max(lp, -1, keepdims=True), jnp.max(lc, -1, keepdims=True)), sk)
            pp = jnp.exp2(lp - m)
            pc = jnp.exp2(lc - m)
            den = jnp.sum(pp, -1, keepdims=True) + jnp.sum(pc, -1, keepdims=True) + jnp.exp2(sk - m)
            o = _dot(pp.astype(BF16), vp) + _dot(pc.astype(BF16), vc)
            res.append(o / den)
        o_ref[0, :, s * LANES:(s + 1) * LANES] = jnp.where(left, res[0], res[1]).astype(o_ref.dtype)


def _swa(q, k_prev_src, k_cur, v_prev_src, v_cur, sink, *, qb, prev_is_cache, pos_base):
    b, t, _ = q.shape
    nq = t // qb
    if prev_is_cache:
        prev_map = lambda bi, i: (bi, 0, 0)
    else:
        r = qb // WINDOW
        prev_map = lambda bi, i: (bi, jnp.maximum(i * r - 1, 0), 0)
    cur_map = lambda bi, i: (bi, i, 0)
    return pl.pallas_call(
        functools.partial(_swa_kernel, qb=qb, pos_base=pos_base, slopes=_alibi(8)),
        grid=(b, nq),
        in_specs=[pl.BlockSpec(memory_space=pltpu.SMEM),
                  pl.BlockSpec((1, qb, 512), cur_map),
                  pl.BlockSpec((1, WINDOW, LANES), prev_map),
                  pl.BlockSpec((1, qb, LANES), cur_map),
                  pl.BlockSpec((1, WINDOW, LANES), prev_map),
                  pl.BlockSpec((1, qb, LANES), cur_map)],
        out_specs=pl.BlockSpec((1, qb, 512), cur_map),
        out_shape=jax.ShapeDtypeStruct((b, t, 512), BF16),
        compiler_params=_cparams(("parallel", "parallel")),
        name="swa_sink",
    )(sink, q, k_prev_src, k_cur, v_prev_src, v_cur)


def _swa_prompt_kernel(sink_ref, q_ref, kp_ref, kc_ref, vtp_ref, vtc_ref, o_ref, qall_scr, *, qb, slopes):
    i = pl.program_id(1)
    q0 = i * qb
    qpos = q0 + lax.broadcasted_iota(I32, (1, qb), 1)
    for s in range(4):
        slab = q_ref[0, :, s * LANES:(s + 1) * LANES]
        left = _lane_left(slab.shape)
        zero = jnp.zeros_like(slab)
        qall_scr[s * qb:(s + 1) * qb, :LANES] = jnp.where(left, slab, zero)
        qall_scr[(4 + s) * qb:(5 + s) * qb, :LANES] = jnp.where(left, zero, slab)
    for h in range(8):
        qall_scr[h * qb:(h + 1) * qb, LANES:] = _alibi_q_features(qb, slopes[h])
    kpos_p = q0 - WINDOW + lax.broadcasted_iota(I32, (WINDOW, 1), 0)
    kpos_c = q0 + lax.broadcasted_iota(I32, (qb, 1), 0)
    qch = qpos // CHUNK
    first = qch - WINDOW // CHUNK
    neg_p = jnp.where((kpos_p >= 0) & (kpos_p // CHUNK >= first), 0.0, NEG_INF)
    kch_c = kpos_c // CHUNK
    neg_c = jnp.where((kch_c >= first) & (kch_c <= qch), 0.0, NEG_INF)
    ahead = jnp.maximum(kpos_c - qpos, 0).astype(F32)
    qall = qall_scr[...]
    st_p = _dot_nt(jnp.concatenate([kp_ref[0], _alibi_k_features(jnp.maximum(kpos_p, 0))], axis=1), qall)
    st_c = _dot_nt(jnp.concatenate([kc_ref[0], _alibi_k_features(kpos_c)], axis=1), qall)
    qposf = qpos.astype(F32)
    pps, pcs, dens = [], [], []
    for h in range(8):
        sp = st_p[:, h * qb:(h + 1) * qb] + neg_p
        sc = st_c[:, h * qb:(h + 1) * qb] - (2.0 * slopes[h]) * ahead + neg_c
        sk = sink_ref[h] * LOG2E + slopes[h] * qposf
        m = jnp.maximum(jnp.maximum(jnp.max(sp, 0, keepdims=True), jnp.max(sc, 0, keepdims=True)), sk)
        pp = jnp.exp2(sp - m)
        pc = jnp.exp2(sc - m)
        dens.append(jnp.sum(pp, 0, keepdims=True) + jnp.sum(pc, 0, keepdims=True) + jnp.exp2(sk - m))
        pps.append(pp.astype(BF16))
        pcs.append(pc.astype(BF16))
    ot = (_dot(vtp_ref[0], jnp.concatenate(pps, axis=1))
          + _dot(vtc_ref[0], jnp.concatenate(pcs, axis=1)))
    top = lax.broadcasted_iota(I32, (LANES, qb), 0) < HEAD_DIM
    for s in range(4):
        o_l = ot[:, s * qb:(s + 1) * qb] / dens[s]
        o_r = ot[:, (4 + s) * qb:(5 + s) * qb] / dens[4 + s]
        o_ref[0, :, s * LANES:(s + 1) * LANES] = jnp.where(top, o_l, o_r).T.astype(o_ref.dtype)


def _swa_prompt(q, k, v, sink, qb):
    b, t, _ = q.shape
    r = qb // WINDOW
    vt = jnp.swapaxes(v, -1, -2)
    cur_map = lambda bi, i: (bi, i, 0)
    prev_map = lambda bi, i: (bi, jnp.maximum(i * r - 1, 0), 0)
    return pl.pallas_call(
        functools.partial(_swa_prompt_kernel, qb=qb, slopes=_alibi(8)),
        grid=(b, t // qb),
        in_specs=[pl.BlockSpec(memory_space=pltpu.SMEM),
                  pl.BlockSpec((1, qb, 512), cur_map),
                  pl.BlockSpec((1, WINDOW, LANES), prev_map),
                  pl.BlockSpec((1, qb, LANES), cur_map),
                  pl.BlockSpec((1, LANES, WINDOW), lambda bi, i: (bi, 0, jnp.maximum(i * r - 1, 0))),
                  pl.BlockSpec((1, LANES, qb), lambda bi, i: (bi, 0, i))],
        out_specs=pl.BlockSpec((1, qb, 512), cur_map),
        out_shape=jax.ShapeDtypeStruct((b, t, 512), BF16),
        scratch_shapes=[pltpu.VMEM((8 * qb, 2 * LANES), BF16)],
        compiler_params=_cparams(("parallel", "parallel")),
        name="swa_prompt",
    )(sink, q, k, k, vt, vt)


def _sigmoid(x):
    return 1.0 / (1.0 + jnp.exp(-x))


def _softplus(x):
    return jnp.maximum(x, 0.0) + jnp.log(1.0 + jnp.exp(-jnp.abs(x)))


def _rwkv_prep_kernel(pb_ref, pbprev_ref, shift_ref, mu_ref, w0_ref, wup_ref, a0_ref, aup_ref, gup_ref,
                      kk_ref, ka_ref, rk_ref,
                      r_out, w_out, k_out, v_out, kk_out, b_out, g_out, bonus_out):
    i = pl.program_id(1)
    pb = pb_ref[0]
    tm = pb.shape[0]
    prev_row = jnp.where(i == 0, shift_ref[0], pbprev_ref[0, 7:8, :])
    rolled = pltpu.roll(pb, 1, axis=0)
    row = lax.broadcasted_iota(I32, (tm, 1), 0)
    prev = jnp.where(row == 0, prev_row, rolled)
    xm = pb + (prev - pb) * mu_ref[...]
    r = xm[:, 0:512]
    k = xm[:, 512:1024]
    v = xm[:, 1024:1536]
    wa = xm[:, 1536:1664]
    gl = xm[:, 1664:1792]
    lw = _dot(jnp.tanh(wa).astype(BF16), wup_ref[...])
    la = _dot(wa.astype(BF16), aup_ref[...])
    w_log = -_softplus(-(w0_ref[...] + lw)) - 0.5
    log_decay = -jnp.exp(w_log)
    a = _sigmoid(a0_ref[...] + la)
    g = _dot(_sigmoid(gl).astype(BF16), gup_ref[...])
    kk = k * kk_ref[...]
    k2 = k * (1.0 + (a - 1.0) * ka_ref[...])
    rkk = r * k2 * rk_ref[...]
    for p in range(4):
        sl = slice(p * LANES, (p + 1) * LANES)
        kkp = kk[:, sl]
        kkn = kkp * lax.rsqrt(_segsum64(kkp * kkp) + 1e-12)
        kk_out[0, p] = kkn
        b_out[0, p] = kkn * a[:, sl]
        bonus_out[0, :, sl] = _segsum64(rkk[:, sl]) * v[:, sl]
        r_out[0, p] = r[:, sl]
        w_out[0, p] = log_decay[:, sl]
        k_out[0, p] = k2[:, sl]
        v_out[0, p] = v[:, sl]
    g_out[0] = g


RWKV_CHUNK = 16


def _split2(x):
    hi = x.astype(BF16)
    return hi, (x - hi.astype(F32)).astype(BF16)


def _dot3(a, b):
    m = a[0].shape[0]
    both = _dot(jnp.concatenate([a[0], a[1]], axis=0), b[0])
    return both[:m] + (_dot(a[0], b[1]) + both[m:])


def _rwkv_operator_scratch(tb):
    nc, c2 = tb // RWKV_CHUNK, 2 * RWKV_CHUNK
    return [pltpu.VMEM((4, nc, c2, LANES), BF16), pltpu.VMEM((4, nc, c2, LANES), BF16),
            pltpu.VMEM((4, nc, LANES, c2), BF16), pltpu.VMEM((4, nc, LANES, c2), BF16),
            pltpu.VMEM((4, nc, LANES, LANES), F32),
            pltpu.VMEM((4, tb, LANES), F32), pltpu.VMEM((4, tb, LANES), F32)]


def _rwkv_operator_stages(r_ref, lw_ref, k_ref, v_ref, kk_ref, b_ref,
                          l1hi_scr, l1lo_scr, l2hi_scr, l2lo_scr, pcol_scr, u0_scr, y0_scr, *, tb):
    c_sz = RWKV_CHUNK
    nc = tb // c_sz
    row = lax.broadcasted_iota(I32, (tb, tb), 0)
    col = lax.broadcasted_iota(I32, (tb, tb), 1)
    same = (row // c_sz) == (col // c_sz)
    strict = same & (col < row)
    incl = same & (col <= row)
    cum_lhs = jnp.concatenate([jnp.where(incl, 1.0, 0.0), jnp.where(same, 1.0, 0.0)], axis=0).astype(BF16)
    left = _lane_left((tb, LANES))
    left2 = jnp.concatenate([left, left], axis=1)

    def stages():
        qs = range(4)
        sides = range(2)

        def halves(fn):
            out = [None] * 4
            for group in ((0, 1), (2, 3)):
                for q in group:
                    out[q] = fn(q)
                yield
            return out

        def split3(x):
            x1 = x.astype(BF16)
            rem = x - x1.astype(F32)
            x2 = rem.astype(BF16)
            return x1, x2, (rem - x2.astype(F32)).astype(BF16)

        def cum_of(q):
            l = split3(lw_ref[0, q])
            return _dot(cum_lhs, l[0]) + (_dot(cum_lhs, l[1]) + _dot(cum_lhs, l[2]))
        cums = yield from halves(cum_of)
        al, rt, vs, rhs, lhs, bbt, kbt, pc_rows = [], [], [], [], [], [], [], []
        for q in qs:
            r, lw, k, v, kk, b = (ref[0, q] for ref in (r_ref, lw_ref, k_ref, v_ref, kk_ref, b_ref))
            cs, cse = cums[q][:tb], cums[q][tb:]
            p_inv = jnp.exp(-cs)
            p_end = jnp.exp(cse - cs)
            al.append(-kk * jnp.exp(cs - lw))
            rt.append(r * jnp.exp(cs))
            vs.append(_split2(v))
            rhs.append(_split2(jnp.concatenate([b * p_inv, k * p_inv], axis=0)))
            zero = jnp.zeros_like(v)
            half = lambda z, s: jnp.where(left, z, zero) if s == 0 else jnp.where(left, zero, z)
            lhs.append([_split2(jnp.concatenate([half(al[q], s), half(rt[q], s)], axis=0)) for s in sides])
            bbt.append(_split2((b * p_end).T))
            kbt.append(_split2((k * p_end).T))
            pc_rows.append(jnp.exp(cse))
        gram = yield from halves(lambda q: [
            _dot_nt(lhs[q][s][0], rhs[q][0]) + (_dot_nt(lhs[q][s][0], rhs[q][1]) + _dot_nt(lhs[q][s][1], rhs[q][0]))
            for s in sides])
        a1 =[[_split2(jnp.where(strict, gram[q][s][:tb, :tb], 0.0)) for s in sides] for q in qs]
        a2 = [[_split2(jnp.where(strict, gram[q][s][:tb, tb:], 0.0)) for s in sides] for q in qs]
        b1 = [[_split2(jnp.where(incl, gram[q][s][tb:, :tb], 0.0)) for s in sides] for q in qs]
        b2 = [[_split2(jnp.where(incl, gram[q][s][tb:, tb:], 0.0)) for s in sides] for q in qs]
        xv_pw = yield from halves(lambda q: (
            jnp.where(left, _dot3(a2[q][0], vs[q]), _dot3(a2[q][1], vs[q])),
            [_dot3(a1[q][s], a1[q][s]) for s in sides]))
        x = [jnp.concatenate([al[q], xv_pw[q][0]], axis=1) for q in qs]
        pw = [xv_pw[q][1] for q in qs]
        m = a1
        for step in range(4):
            def apply(q, m=m, pw=pw, x=x, step=step):
                m_new = [_split2(pw[q][s]) for s in sides] if step < 3 else None
                pw_new = [_dot3(m_new[s], m_new[s]) for s in sides] if step < 2 else None
                xq = _split2(x[q])
                x_new = x[q] + jnp.where(left2, _dot3(m[q][0], xq), _dot3(m[q][1], xq))
                return x_new, m_new, pw_new
            res = yield from halves(apply)
            x = [res[q][0] for q in qs]
            m = [res[q][1] for q in qs]
            pw = [res[q][2] for q in qs]
        xs = [_split2(x[q]) for q in qs]
        bxv = yield from halves(lambda q: (
            jnp.where(left2, _dot(b1[q][0][0], xs[q][0]), _dot(b1[q][1][0], xs[q][0])),
            jnp.where(left, _dot(b2[q][0][0], vs[q][0]), _dot(b2[q][1][0], vs[q][0]))))

        def store(q):
            bx, bv = bxv[q]
            ah = x[q][:, :LANES]
            rh = rt[q] + bx[:, :LANES]
            u0_scr[q] = x[q][:, LANES:]
            y0_scr[q] = bx[:, LANES:] + bv
            pct = pc_rows[q].T
            for c in range(nc):
                rows = slice(c * c_sz, (c + 1) * c_sz)
                l1 = _split2(jnp.concatenate([ah[rows], rh[rows]], axis=0))
                l1hi_scr[q, c], l1lo_scr[q, c] = l1
                l2hi_scr[q, c] = jnp.concatenate([bbt[q][0][:, rows], kbt[q][0][:, rows]], axis=1)
                l2lo_scr[q, c] = jnp.concatenate([bbt[q][1][:, rows], kbt[q][1][:, rows]], axis=1)
                pcol_scr[q, c] = jnp.broadcast_to(pct[:, c * c_sz:c * c_sz + 1], (LANES, LANES))
        yield from halves(store)

    yield from stages()


def _rwkv_sequential_steps(v_ref, y_ref, st, l1hi_scr, l1lo_scr, l2hi_scr, l2lo_scr, pcol_scr, u0_scr, y0_scr,
                           *, tb):
    c_sz = RWKV_CHUNK
    r128 = lax.broadcasted_iota(I32, (LANES, LANES), 0)
    c128 = lax.broadcasted_iota(I32, (LANES, LANES), 1)
    blockdiag = (r128 < HEAD_DIM) == (c128 < HEAD_DIM)
    for c in range(tb // c_sz):
        rows = slice(c * c_sz, (c + 1) * c_sz)
        res = [_dot3((l1hi_scr[p, c], l1lo_scr[p, c]), _split2(st[p])) for p in range(4)]
        yield
        for p in range(4):
            y_ref[0, p, rows, :] = res[p][c_sz:] + y0_scr[p, rows, :]
        w2 = [_split2(jnp.concatenate([res[p][:c_sz] + u0_scr[p, rows, :], v_ref[0, p, rows, :]], axis=0))
              for p in range(4)]
        upd = [_dot3((l2hi_scr[p, c], l2lo_scr[p, c]), w2[p]) for p in range(4)]
        for p in range(4):
            st[p] = st[p] * pcol_scr[p, c] + jnp.where(blockdiag, upd[p], 0.0)
        yield


def _interleave(*streams):
    live = list(streams)
    while live:
        for g in list(live):
            try:
                next(g)
            except StopIteration:
                live.remove(g)


def _state_to_blockdiag(s):
    b = s.shape[0]
    st = jnp.swapaxes(s, -1, -2).reshape(b, 4, 2, HEAD_DIM, HEAD_DIM)
    z = jnp.zeros_like(st[:, :, 0])
    top = jnp.concatenate([st[:, :, 0], z], axis=-1)
    bot = jnp.concatenate([z, st[:, :, 1]], axis=-1)
    return jnp.concatenate([top, bot], axis=-2)


def _state_from_blockdiag(s_bd):
    b = s_bd.shape[0]
    a = s_bd[:, :, :HEAD_DIM, :HEAD_DIM]
    d = s_bd[:, :, HEAD_DIM:, HEAD_DIM:]
    st = jnp.stack([a, d], axis=2).reshape(b, 8, HEAD_DIM, HEAD_DIM)
    return jnp.swapaxes(st, -1, -2)


def _rwkv_post_kernel(y_ref, bonus_ref, g_ref, lg_ref, lb_ref, o_ref):
    for p in range(4):
        sl = slice(p * LANES, (p + 1) * LANES)
        y = y_ref[0, p]
        d = y - _segsum64(y) * (1.0 / HEAD_DIM)
        var = _segsum64(d * d) * (1.0 / HEAD_DIM)
        yn = d * lax.rsqrt(var + RWKV_GN_EPS) * lg_ref[:, sl] + lb_ref[:, sl]
        o_ref[0, :, sl] = ((yn + bonus_ref[0, :, sl]) * g_ref[0, :, sl]).astype(o_ref.dtype)


def _rwkv_kernel(pb_ref, pbprev_ref, shift_ref, mu_ref, w0_ref, wup_ref, a0_ref, aup_ref, gup_ref,
                 kk_ref, ka_ref, rk_ref, lg_ref, lb_ref, s0_ref, ob_ref, sT_ref,
                 tok_s, g_s, bonus_s, y_s, st_scr, *op_scr, tb):
    i = pl.program_id(1)

    @pl.when(i == 0)
    def _():
        st_scr[...] = s0_ref[0]

    tok = [tok_s.at[j] for j in range(6)]
    _rwkv_prep_kernel(pb_ref, pbprev_ref, shift_ref, mu_ref, w0_ref, wup_ref, a0_ref, aup_ref, gup_ref,
                      kk_ref, ka_ref, rk_ref, *tok, g_s, bonus_s)
    _interleave(_rwkv_operator_stages(*tok, *op_scr, tb=tb))
    st = [st_scr[p] for p in range(4)]
    _interleave(_rwkv_sequential_steps(tok[3], y_s, st, *op_scr, tb=tb))
    for p in range(4):
        st_scr[p] = st[p]
    _rwkv_post_kernel(y_s, bonus_s, g_s, lg_ref, lb_ref, ob_ref)

    @pl.when(i == pl.num_programs(1) - 1)
    def _():
        sT_ref[0] = st_scr[...]


def _rwkv(pb, shift, mu, w0, wup, a0, aup, gup, k_k, k_a, r_k, lnx_g, lnx_b, s0_bd, tb):
    b, t, c = pb.shape
    cur = lambda bi, i: (bi, i, 0)
    vec = lambda n: pl.BlockSpec((1, n), lambda bi, i: (0, 0))
    lora = pl.BlockSpec((LANES, 512), lambda bi, i: (0, 0))
    st_spec = pl.BlockSpec((1, 4, LANES, LANES), lambda bi, i: (bi, 0, 0, 0))
    return pl.pallas_call(
        functools.partial(_rwkv_kernel, tb=tb),
        grid=(b, t // tb),
        in_specs=[pl.BlockSpec((1, tb, c), cur),
                  pl.BlockSpec((1, 8, c), lambda bi, i: (bi, jnp.maximum(i * (tb // 8) - 1, 0), 0)),
                  pl.BlockSpec((1, 1, c), lambda bi, i: (bi, 0, 0)),
                  vec(c), vec(512), lora, vec(512), lora, lora, vec(512), vec(512), vec(512),
                  vec(512), vec(512), st_spec],
        out_specs=[pl.BlockSpec((1, tb, 512), cur), st_spec],
        out_shape=[jax.ShapeDtypeStruct((b, t, 512), BF16),
                   jax.ShapeDtypeStruct((b, 4, LANES, LANES), F32)],
        scratch_shapes=([pltpu.VMEM((6, 1, 4, tb, LANES), F32)] + [pltpu.VMEM((1, tb, 512), F32)] * 2
                        + [pltpu.VMEM((1, 4, tb, LANES), F32), pltpu.VMEM((4, LANES, LANES), F32)]
                        + _rwkv_operator_scratch(tb)),
        compiler_params=_cparams(("parallel", "arbitrary")),
        name="rwkv",
    )(pb, pb, shift, mu, w0, wup, a0, aup, gup, k_k, k_a, r_k, lnx_g, lnx_b, s0_bd)


def _layer_norm(x, g, b):
    mu = jnp.mean(x, -1, keepdims=True)
    d = x - mu
    var = jnp.mean(d * d, -1, keepdims=True)
    return d * lax.rsqrt(var + LN_EPS) * g + b


def _ffn_kernel(x_ref, m1_ref, m2_ref, wo_ref, g1_ref, b1_ref, wu_ref, wd_ref, g2_ref, b2_ref, o_ref,
                *, alpha, ff_chunk):
    half = m1_ref.shape[1]
    mix = _dot(m1_ref[...], wo_ref[0:half, :]) + _dot(m2_ref[...], wo_ref[half:, :])
    x1 = _layer_norm(alpha * x_ref[...] + mix, g1_ref[...], b1_ref[...])
    x1b = x1.astype(BF16)
    d_ff = wu_ref.shape[1]
    h = jnp.zeros_like(x1)
    for c in range(d_ff // ff_chunk):
        u = jnp.maximum(_dot(x1b, wu_ref[:, c * ff_chunk:(c + 1) * ff_chunk]), 0.0)
        h = h + _dot((u * u).astype(BF16), wd_ref[c * ff_chunk:(c + 1) * ff_chunk, :])
    o_ref[...] = _layer_norm(alpha * x1 + h, g2_ref[...], b2_ref[...])


def _const_spec(shape):
    return pl.BlockSpec(shape, lambda i: (0,) * len(shape), pipeline_mode=pl.Buffered(1))


def _out_ffn(x, m1, m2, wo, g1, b1, wu, wd, g2, b2, alpha, tm):
    n, d = x.shape
    dm = m1.shape[1]
    d_ff = wu.shape[1]
    tile = lambda w: pl.BlockSpec((tm, w), lambda i: (i, 0))
    return pl.pallas_call(
        functools.partial(_ffn_kernel, alpha=alpha, ff_chunk=1024),
        grid=(n // tm,),
        in_specs=[tile(d), tile(dm), tile(dm),
                  _const_spec((2 * dm, d)), _const_spec((1, d)), _const_spec((1, d)),
                  _const_spec((d, d_ff)), _const_spec((d_ff, d)), _const_spec((1, d)), _const_spec((1, d))],
        out_specs=tile(d),
        out_shape=jax.ShapeDtypeStruct((n, d), F32),
        compiler_params=_cparams(("parallel",)),
        name="out_ffn",
    )(x, m1, m2, wo, g1, b1, wu, wd, g2, b2)


def _diff_lambda(lam_ref, lam_init):
    lv = lam_ref[...]
    s1 = jnp.sum(lv[0:1] * lv[1:2], axis=-1, keepdims=True)
    s2 = jnp.sum(lv[2:3] * lv[3:4], axis=-1, keepdims=True)
    return jnp.exp(s1) - jnp.exp(s2) + lam_init


def _split_halves(qh):
    left = _lane_left(qh.shape)
    zero = jnp.zeros_like(qh)
    return jnp.concatenate([jnp.where(left, qh, zero), jnp.where(left, zero, qh)], axis=0)


def _flash_update(carry, s3, vblk):
    m, l, acc = carry
    g, rws, kb = s3.shape
    m_new = jnp.maximum(m, jnp.max(s3, -1, keepdims=True))
    m_use = jnp.where(m_new == NEG_INF, 0.0, m_new)
    corr = jnp.exp2(m - m_use)
    p = jnp.exp2(s3 - m_use)
    l = l * corr + jnp.sum(p, -1, keepdims=True)
    pv = _dot(p.reshape(g * rws, kb).astype(BF16), vblk).reshape(g, rws, vblk.shape[1])
    return m_new, l, acc * corr + pv


def _flash_init(g, rws, e):
    return (jnp.full((g, rws, 1), NEG_INF, F32), jnp.zeros((g, rws, 1), F32), jnp.zeros((g, rws, e), F32))


def _diff_finish(carry, lam, gain, lam_init):
    m, l, acc = carry
    o = acc[0] / l[0] - lam * (acc[1] / l[1])
    o = o * lax.rsqrt(jnp.mean(o * o, -1, keepdims=True) + LN_EPS) * gain * (1.0 - lam_init)
    return o


def _diff_prompt_kernel(q_ref, k_ref, vt_ref, lam_ref, gain_ref, o_ref, q2_scr, *, qb, slopes, lam_init):
    i = pl.program_id(1)
    lam = _diff_lambda(lam_ref, lam_init)
    qpos = i * qb + lax.broadcasted_iota(I32, (1, qb), 1)
    @pl.when(i == 0)
    def _():
        for h in range(4):
            q2_scr[2 * h * qb:(2 * h + 2) * qb, LANES:] = _alibi_q_features(2 * qb, slopes[h])

    for h in range(4):
        q2_scr[2 * h * qb:(2 * h + 2) * qb, :LANES] = _split_halves(q_ref[0, :, h * LANES:(h + 1) * LANES])

    def body(kb, carry, diagonal):
        m, l, acc = carry
        rows = pl.ds(pl.multiple_of(kb * qb, qb), qb)
        kpos = kb * qb + lax.broadcasted_iota(I32, (qb, 1), 0)
        kfeat = _alibi_k_features(kpos)
        st = [_dot_nt(jnp.concatenate([k_ref[0, rows, h * LANES:(h + 1) * LANES], kfeat], axis=1),
                      q2_scr[2 * h * qb:(2 * h + 2) * qb, :])
              for h in range(4)]
        part = lambda g: st[g // 2][:, (g % 2) * qb:(g % 2 + 1) * qb]
        if diagonal:
            ahead = jnp.maximum(kpos - qpos, 0).astype(F32)
            neg = jnp.where(kpos // CHUNK <= qpos // CHUNK, 0.0, NEG_INF)
            s3 = jnp.stack([part(g) - (2.0 * slopes[g // 2]) * ahead + neg for g in range(8)], axis=0)
        else:
            s3 = jnp.stack([part(g) for g in range(8)], axis=0)
        m_new = jnp.maximum(m, jnp.max(s3, axis=1, keepdims=True))
        corr = jnp.exp2(m - m_new)
        p = jnp.exp2(s3 - m_new)
        l = l * corr + jnp.sum(p, axis=1, keepdims=True)
        pv = [_dot(vt_ref[0, kb, h * LANES:(h + 1) * LANES, :],
                   jnp.concatenate([p[2 * h].astype(BF16), p[2 * h + 1].astype(BF16)], axis=1))
              for h in range(4)]
        acc = acc * corr + jnp.stack([pv[g // 2][:, (g % 2) * qb:(g % 2 + 1) * qb] for g in range(8)], axis=0)
        return m_new, l, acc

    init = (jnp.full((8, 1, qb), NEG_INF, F32), jnp.zeros((8, 1, qb), F32), jnp.zeros((8, LANES, qb), F32))
    carry = lax.fori_loop(0, i, functools.partial(body, diagonal=False), init)
    m, l, acc = body(i, carry, diagonal=True)
    for h in range(4):
        o = acc[2 * h] / l[2 * h] - lam * (acc[2 * h + 1] / l[2 * h + 1])
        o = o * lax.rsqrt(jnp.mean(o * o, axis=0, keepdims=True) + LN_EPS) * gain_ref[...] * (1.0 - lam_init)
        o_ref[0, :, h * LANES:(h + 1) * LANES] = o.T.astype(o_ref.dtype)


def _diff_prompt(q, k, v, lamv, gain, lam_init, qb):
    b, t, _ = q.shape
    nkb = t // qb
    vt = jnp.swapaxes(v.reshape(b, nkb, qb, 512), -1, -2)
    full = pl.BlockSpec((1, t, 512), lambda bi, i: (bi, 0, 0))
    tile = pl.BlockSpec((1, qb, 512), lambda bi, i: (bi, i, 0))
    return pl.pallas_call(
        functools.partial(_diff_prompt_kernel, qb=qb, slopes=_alibi(4), lam_init=lam_init),
        grid=(b, nkb),
        in_specs=[tile, full,
                  pl.BlockSpec((1, nkb, 512, qb), lambda bi, i: (bi, 0, 0, 0)),
                  pl.BlockSpec((4, HEAD_DIM), lambda bi, i: (0, 0)),
                  pl.BlockSpec((LANES, 1), lambda bi, i: (0, 0))],
        out_specs=tile,
        out_shape=jax.ShapeDtypeStruct((b, t, 512), BF16),
        scratch_shapes=[pltpu.VMEM((8 * qb, 2 * LANES), BF16)],
        compiler_params=_cparams(("parallel", "arbitrary")),
        name="diff_prompt",
    )(q, k, vt, lamv, gain.reshape(LANES, 1))


def _diff_sample_kernel(q_ref, kc_ref, vc_ref, kn_ref, vn_ref, lam_ref, gain_ref, o_ref,
                        m_scr, l_scr, acc_scr, *, t, kb_size, n_past, slopes, lam_init):
    j = pl.program_id(1)
    past_len = n_past * kb_size
    qpos = past_len + lax.broadcasted_iota(I32, (t, 1), 0)

    @pl.when(j == 0)
    def _():
        m_scr[...] = jnp.full(m_scr.shape, NEG_INF, F32)
        l_scr[...] = jnp.zeros(l_scr.shape, F32)
        acc_scr[...] = jnp.zeros(acc_scr.shape, F32)

    def run(kfull, v_of, kpos):
        dist = jnp.abs(qpos - kpos).astype(F32)
        nk = kfull.shape[0]
        s3 = jnp.stack(
            [_dot_nt(_split_halves(q_ref[0, :, h * LANES:(h + 1) * LANES]),
                     kfull[:, h * LANES:(h + 1) * LANES]).reshape(2, t, nk) - slopes[h] * dist[None]
             for h in range(4)], axis=0)
        m = m_scr[...]
        m_new = jnp.maximum(m, jnp.max(s3, -1, keepdims=True))
        corr = jnp.exp2(m - m_new)
        p = jnp.exp2(s3 - m_new)
        l_scr[...] = l_scr[...] * corr + jnp.sum(p, -1, keepdims=True)
        pv = jnp.stack([_dot(p[h].reshape(2 * t, nk).astype(BF16), v_of(h)).reshape(2, t, LANES)
                        for h in range(4)], axis=0)
        acc_scr[...] = acc_scr[...] * corr + pv
        m_scr[...] = m_new

    @pl.when(j < n_past)
    def _():
        kpos = j * kb_size + lax.broadcasted_iota(I32, (1, kb_size), 1)
        run(kc_ref[0].astype(BF16), lambda h: vc_ref[0, :, h * LANES:(h + 1) * LANES].astype(BF16), kpos)

    @pl.when(j == n_past)
    def _():
        kpos = past_len + lax.broadcasted_iota(I32, (1, t), 1)
        run(kn_ref[0].astype(BF16), lambda h: vn_ref[0, :, h * LANES:(h + 1) * LANES].astype(BF16), kpos)
        lam = _diff_lambda(lam_ref, lam_init)
        for h in range(4):
            sl = slice(h * LANES, (h + 1) * LANES)
            carry = (m_scr[h], l_scr[h], acc_scr[h])
            o_ref[0, :, sl] = _diff_finish(carry, lam, gain_ref[...], lam_init).astype(o_ref.dtype)


def _diff_sample(q, k_cache, v_cache, k_new, v_new, lamv, gain, lam_init, kb_size):
    b, t, _ = q.shape
    n_past = k_cache.shape[1] // kb_size
    cache = pl.BlockSpec((1, kb_size, 512), lambda bi, j: (bi, jnp.minimum(j, n_past - 1), 0))
    new = pl.BlockSpec((1, t, 512), lambda bi, j: (bi, 0, 0))
    return pl.pallas_call(
        functools.partial(_diff_sample_kernel, t=t, kb_size=kb_size, n_past=n_past,
                          slopes=_alibi(4), lam_init=lam_init),
        grid=(b, n_past + 1),
        in_specs=[new, cache, cache, new, new,
                  pl.BlockSpec((4, HEAD_DIM), lambda bi, j: (0, 0)),
                  pl.BlockSpec((1, LANES), lambda bi, j: (0, 0))],
        out_specs=new,
        out_shape=jax.ShapeDtypeStruct((b, t, 512), BF16),
        scratch_shapes=[pltpu.VMEM((4, 2, t, 1), F32), pltpu.VMEM((4, 2, t, 1), F32),
                        pltpu.VMEM((4, 2, t, LANES), F32)],
        compiler_params=_cparams(("parallel", "arbitrary")),
        name="diff_sample",
    )(q, k_cache, v_cache, k_new, v_new, lamv, gain)


def _index_keys(qi, wi, ki_blk, allowed):
    acc = None
    for h in range(8):
        sc = _dot_nt(qi[:, h * HEAD_DIM:(h + 1) * HEAD_DIM], ki_blk)
        term = jnp.maximum(sc, 0.0) * wi[:, h:h + 1]
        acc = term if acc is None else acc + term
    idx = acc * (8.0 ** -0.5) + 0.0
    bits = pltpu.bitcast(idx, I32)
    key = jnp.where(bits < 0, bits ^ 0x7FFFFFFF, bits)
    return jnp.where(allowed, key, INT_MIN)


def _lane_fold(x):
    acc = x[:, :LANES]
    for j in range(1, x.shape[1] // LANES):
        acc = acc + x[:, j * LANES:(j + 1) * LANES]
    return acc


def _ones_where(pred):
    return jnp.where(pred, 1, 0).astype(I32)


def _kth_largest(count_ge, shape, n_sel):
    def body(it, lo):
        bit = 31 - it
        cand = lo + jnp.left_shift(jnp.int32(1), bit)
        return jnp.where(count_ge(cand) >= n_sel, cand, lo)
    return lax.fori_loop(0, 32, body, jnp.full(shape, INT_MIN, I32))


def _tie_position(count_eq_le, need, shape, nbits):
    def body(it, lo):
        bit = nbits - 1 - it
        cand = lo + jnp.left_shift(jnp.int32(1), bit)
        return jnp.where(count_eq_le(cand - 1) < need, cand, lo)
    return lax.fori_loop(0, nbits, body, jnp.zeros(shape, I32))


def _select_threshold(count_over, shape, n_sel, pos_bits):
    thr = _kth_largest(lambda cand: count_over(lambda key, kpos: key >= cand), shape, n_sel)
    need = n_sel - count_over(lambda key, kpos: key > thr)
    n_eq = count_over(lambda key, kpos: key == thr)
    has_tie = jnp.max(jnp.where((n_eq > need) & (thr > INT_MIN), 1, 0)) > 0
    pos_thr = lax.cond(
        has_tie,
        lambda: _tie_position(
            lambda p: count_over(lambda key, kpos: (key == thr) & (kpos <= p)), need, shape, pos_bits),
        lambda: jnp.full(shape, 2 ** 30, I32))
    return thr, pos_thr


def _dsa_attend(carry, q_ref, kblk, vblk, dist, sel, slopes):
    neg = jnp.where(sel, 0.0, NEG_INF)
    qs = []
    for s in range(4):
        slab = q_ref[0, :, s * LANES:(s + 1) * LANES]
        left = _lane_left(slab.shape)
        zero = jnp.zeros_like(slab)
        qs.append((jnp.where(left, slab, zero), jnp.where(left, zero, slab)))
    q_all = jnp.concatenate([qs[h % 4][h // 4] for h in range(8)], axis=0)
    r, kb = dist.shape
    s3 = _dot_nt(q_all, kblk).reshape(8, r, kb)
    s3 = jnp.stack([s3[h] - slopes[h] * dist + neg for h in range(8)], axis=0)
    return _flash_update(carry, s3, vblk)


def _dsa_finish(carry, o_ref):
    m, l, acc = carry
    o = acc / l
    for s in range(4):
        left = _lane_left(o[s].shape)
        o_ref[0, :, s * LANES:(s + 1) * LANES] = jnp.where(left, o[s], o[4 + s]).astype(o_ref.dtype)


def _dsa_prompt_kernel(q_ref, qi_ref, wi_ref, k_ref, vt_ref, ki_ref, o_ref, key_scr, qall_scr, qiall_scr,
                       *, qb, n_sel, slopes):
    i = pl.program_id(1)
    nkb = i + 1
    qpos = i * qb + lax.broadcasted_iota(I32, (1, qb), 1)

    for s in range(4):
        slab = q_ref[0, :, s * LANES:(s + 1) * LANES]
        left = _lane_left(slab.shape)
        zero = jnp.zeros_like(slab)
        qall_scr[s * qb:(s + 1) * qb, :LANES] = jnp.where(left, slab, zero)
        qall_scr[(4 + s) * qb:(5 + s) * qb, :LANES] = jnp.where(left, zero, slab)
    @pl.when(i == 0)
    def _():
        for h in range(8):
            qall_scr[h * qb:(h + 1) * qb, LANES:] = _alibi_q_features(qb, slopes[h])

    for h in range(8):
        qiall_scr[h * qb:(h + 1) * qb, :] = qi_ref[0, :, h * HEAD_DIM:(h + 1) * HEAD_DIM]
    wit = wi_ref[0].T

    def kpos_of(kb):
        return kb * qb + lax.broadcasted_iota(I32, (qb, 1), 0)

    def rows_of(kb):
        return pl.ds(pl.multiple_of(kb * qb, qb), qb)

    def fill(kb, c):
        allowed = kpos_of(kb) // CHUNK <= qpos // CHUNK
        sc = _dot_nt(ki_ref[0, rows_of(kb), :], qiall_scr[...])
        acc = None
        for h in range(8):
            term = jnp.maximum(sc[:, h * qb:(h + 1) * qb], 0.0) * wit[h:h + 1, :]
            acc = term if acc is None else acc + term
        idx = acc * (8.0 ** -0.5) + 0.0
        bits = pltpu.bitcast(idx, I32)
        key = jnp.where(bits < 0, bits ^ 0x7FFFFFFF, bits)
        key_scr[kb] = jnp.where(allowed, key, INT_MIN)
        return c
    lax.fori_loop(0, nkb, fill, 0)

    def count_over(pred_fn):
        def body(kb, c):
            ones = _ones_where(pred_fn(key_scr[kb], kpos_of(kb)))
            return c + jnp.sum(ones.reshape(qb // 8, 8, qb), axis=0)
        part = lax.fori_loop(0, nkb, body, jnp.zeros((8, qb), I32))
        return jnp.sum(part, axis=0, keepdims=True)

    thr, pos_thr = _select_threshold(count_over, (1, qb), n_sel, 12)

    def attend(kb, carry, diagonal):
        m, l, acc = carry
        kpos = kpos_of(kb)
        key = key_scr[kb]
        sel = (key > thr) | ((key == thr) & (kpos <= pos_thr))
        if diagonal:
            sel = sel & (kpos // CHUNK <= qpos // CHUNK)
        neg = jnp.where(sel, 0.0, NEG_INF)
        k_aug = jnp.concatenate([k_ref[0, rows_of(kb), :], _alibi_k_features(kpos)], axis=1)
        st = _dot_nt(k_aug, qall_scr[...])
        if diagonal:
            ahead = jnp.maximum(kpos - qpos, 0).astype(F32)
            s3 = jnp.stack([st[:, h * qb:(h + 1) * qb] - (2.0 * slopes[h]) * ahead + neg for h in range(8)],
                           axis=0)
        else:
            s3 = jnp.stack([st[:, h * qb:(h + 1) * qb] + neg for h in range(8)], axis=0)
        m_new = jnp.maximum(m, jnp.max(s3, axis=1, keepdims=True))
        m_use = jnp.where(m_new == NEG_INF, 0.0, m_new)
        corr = jnp.exp2(m - m_use)
        p = jnp.exp2(s3 - m_use)
        l = l * corr + jnp.sum(p, axis=1, keepdims=True)
        p_all = jnp.concatenate([p[h].astype(BF16) for h in range(8)], axis=1)
        pv = _dot(vt_ref[0, kb], p_all)
        acc = acc * corr + jnp.stack([pv[:, h * qb:(h + 1) * qb] for h in range(8)], axis=0)
        return m_new, l, acc

    init = (jnp.full((8, 1, qb), NEG_INF, F32), jnp.zeros((8, 1, qb), F32), jnp.zeros((8, LANES, qb), F32))
    carry = lax.fori_loop(0, i, functools.partial(attend, diagonal=False), init)
    m, l, acc = attend(i, carry, diagonal=True)
    ot = acc / l
    top = lax.broadcasted_iota(I32, (LANES, qb), 0) < HEAD_DIM
    for s in range(4):
        o_ref[0, :, s * LANES:(s + 1) * LANES] = jnp.where(top, ot[s], ot[4 + s]).T.astype(o_ref.dtype)


def _dsa_prompt(q, qi, wi, k, v, ki, qb):
    b, t, _ = q.shape
    n_sel = min(TOPK_MAX, t // 4)
    nkb = t // qb
    vt = jnp.swapaxes(v.reshape(b, nkb, qb, LANES), -1, -2)
    tile = lambda w: pl.BlockSpec((1, qb, w), lambda bi, i: (bi, i, 0))
    full = lambda w: pl.BlockSpec((1, t, w), lambda bi, i: (bi, 0, 0))
    return pl.pallas_call(
        functools.partial(_dsa_prompt_kernel, qb=qb, n_sel=n_sel, slopes=_alibi(8)),
        grid=(b, nkb),
        in_specs=[tile(512), tile(512), tile(LANES), full(LANES),
                  pl.BlockSpec((1, nkb, LANES, qb), lambda bi, i: (bi, 0, 0, 0)), full(HEAD_DIM)],
        out_specs=tile(512),
        out_shape=jax.ShapeDtypeStruct((b, t, 512), BF16),
        scratch_shapes=[pltpu.VMEM((nkb, qb, qb), I32), pltpu.VMEM((8 * qb, 2 * LANES), BF16),
                        pltpu.VMEM((8 * qb, HEAD_DIM), BF16)],
        compiler_params=_cparams(("parallel", "arbitrary")),
        name="dsa_prompt",
    )(q, qi, wi, k, vt, ki)


def _dsa_sample_kernel(q_ref, qi_ref, wi_ref, kc_ref, vc_ref, kic_ref, kn_ref, vn_ref, kin_ref, o_ref,
                       key_scr, keyn_scr, *, t, kb_size, n_past, n_sel, slopes):
    past_len = n_past * kb_size
    qpos = past_len + lax.broadcasted_iota(I32, (t, 1), 0)
    qi = qi_ref[0]
    wi = wi_ref[0]
    kpos_n = past_len + lax.broadcasted_iota(I32, (1, t), 1)

    def kpos_of(kb):
        return kb * kb_size + lax.broadcasted_iota(I32, (1, kb_size), 1)

    for kb in range(n_past):
        ki_blk = kic_ref[0, kb * kb_size:(kb + 1) * kb_size, :].astype(BF16)
        key_scr[kb] = _index_keys(qi, wi, ki_blk, True)
    keyn_scr[...] = _index_keys(qi, wi, kin_ref[0].astype(BF16), True)

    def count_over(pred_fn):
        part = _lane_fold(_ones_where(pred_fn(key_scr[0], kpos_of(0))))
        for kb in range(1, n_past):
            part = part + _lane_fold(_ones_where(pred_fn(key_scr[kb], kpos_of(kb))))
        new = jnp.sum(_ones_where(pred_fn(keyn_scr[...], kpos_n)), axis=-1, keepdims=True)
        return jnp.sum(part, axis=-1, keepdims=True) + new

    thr, pos_thr = _select_threshold(count_over, (t, 1), n_sel, 13)
    carry = _flash_init(8, t, LANES)

    def sel_of(key, kpos):
        return (key > thr) | ((key == thr) & (kpos <= pos_thr))

    for kb in range(n_past):
        kpos = kpos_of(kb)
        dist = jnp.abs(qpos - kpos).astype(F32)
        kblk = kc_ref[0, kb * kb_size:(kb + 1) * kb_size, :].astype(BF16)
        vblk = vc_ref[0, kb * kb_size:(kb + 1) * kb_size, :].astype(BF16)
        carry = _dsa_attend(carry, q_ref, kblk, vblk, dist, sel_of(key_scr[kb], kpos), slopes)
    dist = jnp.abs(qpos - kpos_n).astype(F32)
    carry = _dsa_attend(carry, q_ref, kn_ref[0].astype(BF16), vn_ref[0].astype(BF16), dist,
                        sel_of(keyn_scr[...], kpos_n), slopes)
    _dsa_finish(carry, o_ref)


def _dsa_sample(q, qi, wi, k_cache, v_cache, ki_cache, k_new, v_new, ki_new, kb_size):
    b, t, _ = q.shape
    past = k_cache.shape[1]
    n_past = past // kb_size
    n_sel = min(TOPK_MAX, (past + t) // 4)
    per_b = lambda rows, w: pl.BlockSpec((1, rows, w), lambda bi: (bi, 0, 0))
    return pl.pallas_call(
        functools.partial(_dsa_sample_kernel, t=t, kb_size=kb_size, n_past=n_past, n_sel=n_sel,
                          slopes=_alibi(8)),
        grid=(b,),
        in_specs=[per_b(t, 512), per_b(t, 512), per_b(t, LANES),
                  per_b(past, LANES), per_b(past, LANES), per_b(past, HEAD_DIM),
                  per_b(t, LANES), per_b(t, LANES), per_b(t, HEAD_DIM)],
        out_specs=per_b(t, 512),
        out_shape=jax.ShapeDtypeStruct((b, t, 512), BF16),
        scratch_shapes=[pltpu.VMEM((n_past, t, kb_size), I32), pltpu.VMEM((t, t), I32)],
        compiler_params=_cparams(("parallel",)),
        name="dsa_sample",
    )(q, qi, wi, k_cache, v_cache, ki_cache, k_new, v_new, ki_new)


def _pad_cols(w, total):
    return jnp.pad(w, ((0, 0), (0, total - w.shape[1])))


def _q_perm():
    idx = []
    for s in range(4):
        idx.extend(range(s * HEAD_DIM, (s + 1) * HEAD_DIM))
        idx.extend(range((4 + s) * HEAD_DIM, (5 + s) * HEAD_DIM))
    return np.asarray(idx, np.int32)


def _even_layer(xp, xs, cache_k, cache_v, shift_s, wkv_s, w_in, sink, rw, w_out, ffn, alpha, sizes):
    (mu, w0, w_up, a0, a_up, g_up, k_k, k_a, r_k, lnx_g, lnx_b) = rw
    perm = _q_perm()
    w_in_p = jnp.concatenate([w_in[:, :512][:, perm], w_in[:, 512:]], axis=1).astype(BF16)
    w_out_p = jnp.concatenate([w_out[:512][perm], w_out[512:]], axis=0).astype(BF16)
    groups = ((0, 512, ((0, 512, HEAD_DIM ** -0.5 * LOG2E),)),
              (512, 128, ((1, 128, 1.0), (2, 128, 1.0))),
              (640, 128, ((3, 128, 1.0), (4, 128, 1.0))),
              (768, 1792, ((5, 1792, 1.0),)))
    out_defs = ((512, BF16), (128, F32), (128, BF16), (128, F32), (128, BF16), (1792, F32))
    wup_p = jnp.concatenate([w_up, jnp.zeros_like(w_up)], axis=0).astype(BF16)
    aup_p = jnp.concatenate([jnp.zeros_like(a_up), a_up], axis=0).astype(BF16)
    gup = g_up.astype(BF16)
    row = lambda v: v.reshape(1, -1)
    g1, b1, g2, b2, wu, wd = ffn

    def run(x, state, tm, qb, scan_tb):
        b, t, d = x.shape
        n = b * t
        qa, ka, ka_bf, va, va_bf, pb = _project(x.reshape(n, d), w_in_p, groups, out_defs, tm)
        r3 = lambda a: a.reshape(b, t, a.shape[-1])
        if state is None:
            oa = _swa_prompt(r3(qa), r3(ka_bf), r3(va_bf), sink, qb)
            k_buf = r3(ka)[:, -WINDOW:]
            v_buf = r3(va)[:, -WINDOW:]
            shift = jnp.zeros((b, 1, pb.shape[-1]), F32)
            s0 = jnp.zeros((b, 8, HEAD_DIM, HEAD_DIM), F32)
        else:
            ck, cv, shift, s0 = state
            ck2 = ck.reshape(b, WINDOW, LANES)
            cv2 = cv.reshape(b, WINDOW, LANES)
            oa = _swa(r3(qa), ck2, r3(ka_bf), cv2, r3(va_bf), sink,
                      qb=qb, prev_is_cache=True, pos_base=sizes["past"])
            k_buf = jnp.concatenate([ck2, r3(ka)], axis=1)[:, -WINDOW:]
            v_buf = jnp.concatenate([cv2, r3(va)], axis=1)[:, -WINDOW:]
        pb3 = r3(pb)
        ob, s_bd = _rwkv(pb3, shift, row(mu), row(w0), wup_p, row(a0), aup_p, gup, row(k_k), row(k_a),
                         row(r_k), row(lnx_g), row(lnx_b), _state_to_blockdiag(s0), scan_tb)
        s_t = _state_from_blockdiag(s_bd)
        xo = _out_ffn(x.reshape(n, d), oa.reshape(n, 512), ob.reshape(n, 512), w_out_p, row(g1), row(b1),
                      wu.astype(BF16), wd.astype(BF16), row(g2), row(b2), alpha, tm)
        st = (k_buf.reshape(b, WINDOW, 2, HEAD_DIM), v_buf.reshape(b, WINDOW, 2, HEAD_DIM),
              pb3[:, -1:], s_t)
        return xo.reshape(b, t, d), st

    xp2, st_p = run(xp, None, sizes["tm_p"], sizes["swa_qb"], sizes["scan_tb_p"])
    xs2, st_s = run(xs, (cache_k, cache_v, shift_s, wkv_s), sizes["tm_s"], xs.shape[1], xs.shape[1])
    return xp2, xs2, st_p, st_s


def _odd_layer(xp, xs, caches, w_in, lam_vecs, subln_g, w_out, ffn, alpha, lam_init, sizes):
    perm = _q_perm()
    scale = HEAD_DIM ** -0.5 * LOG2E
    w_p = jnp.concatenate([
        w_in[:, :1536], w_in[:, 1536:2048][:, perm], w_in[:, 2048:2816],
        _pad_cols(w_in[:, 2816:2880], LANES), _pad_cols(w_in[:, 2880:2888], LANES)], axis=1).astype(BF16)
    w_out_p = jnp.concatenate([w_out[:512], w_out[512:][perm]], axis=0).astype(BF16)
    groups = ((0, 512, ((0, 512, scale),)),
              (512, 512, ((1, 512, 1.0), (2, 512, 1.0))),
              (1024, 512, ((3, 512, 1.0), (4, 512, 1.0))),
              (1536, 512, ((5, 512, scale),)),
              (2048, 128, ((6, 128, 1.0), (7, 128, 1.0))),
              (2176, 128, ((8, 128, 1.0), (9, 128, 1.0))),
              (2304, 512, ((10, 512, HEAD_DIM ** -0.5),)),
              (2816, 128, ((11, 64, 1.0), (12, 64, 1.0))),
              (2944, 128, ((13, 128, 1.0),)))
    out_defs = ((512, BF16), (512, F32, (4, 2, HEAD_DIM)), (512, BF16), (512, F32, (4, LANES)), (512, BF16),
                (512, BF16),
                (128, F32, (2, HEAD_DIM)), (128, BF16), (128, F32, (2, HEAD_DIM)), (128, BF16), (512, BF16),
                (64, F32), (64, BF16), (128, F32))
    row = lambda v: v.reshape(1, -1)
    g1, b1, g2, b2, wu, wd = ffn
    lamv = jnp.stack(lam_vecs, axis=0)
    gain = row(subln_g)

    def run(x, state, tm):
        b, t, d = x.shape
        n = b * t
        (qc, kc, kc_bf, vc, vc_bf, qd, kd, kd_bf, vd, vd_bf, qi, ki, ki_bf, wi) = _project(
            x.reshape(n, d), w_p, groups, out_defs, tm)
        r3 = lambda a: a.reshape(b, t, a.shape[-1])
        if state is None:
            oc = _diff_prompt(r3(qc), r3(kc_bf), r3(vc_bf), lamv, gain, lam_init, sizes["diff_qb"])
            od = _dsa_prompt(r3(qd), r3(qi), r3(wi), r3(kd_bf), r3(vd_bf), r3(ki_bf), sizes["dsa_qb"])
        else:
            c_k, c_v, d_k, d_v, d_ki = state
            past = c_k.shape[1]
            oc = _diff_sample(r3(qc), c_k.reshape(b, past, 512), c_v.reshape(b, past, 512),
                              r3(kc_bf), r3(vc_bf), lamv, gain, lam_init, sizes["diff_kb_s"])
            od = _dsa_sample(r3(qd), r3(qi), r3(wi), d_k.reshape(b, past, LANES), d_v.reshape(b, past, LANES),
                             d_ki, r3(kd_bf), r3(vd_bf), r3(ki_bf), sizes["dsa_kb_s"])
        xo = _out_ffn(x.reshape(n, d), oc.reshape(n, 512), od.reshape(n, 512), w_out_p, row(g1), row(b1),
                      wu.astype(BF16), wd.astype(BF16), row(g2), row(b2), alpha, tm)
        rows = (kc.reshape(b, t, 4, 2, HEAD_DIM), vc.reshape(b, t, 4, 2 * HEAD_DIM),
                kd.reshape(b, t, 2, HEAD_DIM), vd.reshape(b, t, 2, HEAD_DIM), ki.reshape(b, t, HEAD_DIM))
        return xo.reshape(b, t, d), rows

    xp2, st_p = run(xp, None, sizes["tm_p"])
    xs2, st_s = run(xs, caches, sizes["tm_s"])
    return xp2, xs2, st_p, st_s


def _sizes(xp, xs, past):
    return dict(tm_p=min(512, xp.shape[0] * xp.shape[1]), tm_s=min(512, xs.shape[0] * xs.shape[1]),
                swa_qb=min(256, xp.shape[1]), scan_tb_p=min(128, xp.shape[1]),
                diff_qb=min(256, xp.shape[1]),
                dsa_qb=min(256, xp.shape[1]), diff_kb_s=min(1024, past), dsa_kb_s=min(512, past),
                past=past)


def kernel(x_prompt, x_sample, cache_a_k, cache_a_v, state_b_shift, state_b_wkv, cache_c_k, cache_c_v, cache_d_k, cache_d_v, cache_d_kidx, w_in_even, sink_a, mu_b, w0_b, w_up_b, a0_b, a_up_b, g_up_b, k_k_b, k_a_b, r_k_b, lnx_g_b, lnx_b_b, w_out_even, w_in_odd, lam_q1_c, lam_k1_c, lam_q2_c, lam_k2_c, subln_g_c, w_out_odd, ln_mix_g, ln_mix_b, ln_ffn_g, ln_ffn_b, w_ff_up, w_ff_down):
    depth = ln_mix_g.shape[0]
    alpha = (2 * depth) ** 0.25
    past = cache_c_k.shape[2]
    sizes = _sizes(x_prompt, x_sample, past)
    xp, xs = x_prompt, x_sample
    even_p, even_s, odd_p, odd_s = [], [], [], []
    for layer in range(depth):
        i = layer // 2
        ffn = (ln_mix_g[layer], ln_mix_b[layer], ln_ffn_g[layer], ln_ffn_b[layer],
               w_ff_up[layer], w_ff_down[layer])
        if layer % 2 == 0:
            rw = (mu_b[i], w0_b[i], w_up_b[i], a0_b[i], a_up_b[i], g_up_b[i], k_k_b[i], k_a_b[i],
                  r_k_b[i].reshape(-1), lnx_g_b[i], lnx_b_b[i])
            xp, xs, st_p, st_s = _even_layer(
                xp, xs, cache_a_k[i], cache_a_v[i], state_b_shift[i], state_b_wkv[i],
                w_in_even[i], sink_a[i], rw, w_out_even[i], ffn, alpha, sizes)
            even_p.append(st_p)
            even_s.append(st_s)
        else:
            lam_init = 0.8 - 0.6 * math.exp(-0.3 * layer)
            xp, xs, st_p, st_s = _odd_layer(
                xp, xs, (cache_c_k[i], cache_c_v[i], cache_d_k[i], cache_d_v[i], cache_d_kidx[i]),
                w_in_odd[i], (lam_q1_c[i], lam_k1_c[i], lam_q2_c[i], lam_k2_c[i]), subln_g_c[i],
                w_out_odd[i], ffn, alpha, lam_init, sizes)
            odd_p.append(st_p)
            odd_s.append(st_s)
    stack = lambda states: [jnp.stack(z, axis=0) for z in zip(*states)]
    a_k_p, a_v_p, b_shift_p, b_wkv_p = stack(even_p)
    a_k_s, a_v_s, b_shift_s, b_wkv_s = stack(even_s)
    c_k_p, c_v_p, d_k_p, d_v_p, d_kidx_p = stack(odd_p)
    c_k_s, c_v_s, d_k_s, d_v_s, d_kidx_s = stack(odd_s)
    return (xp, xs, a_k_p, a_v_p, b_shift_p, b_wkv_p, c_k_p, c_v_p, d_k_p, d_v_p, d_kidx_p,
            a_k_s, a_v_s, b_shift_s, b_wkv_s, c_k_s, c_v_s, d_k_s, d_v_s, d_kidx_s)
```
